```python
import math
import jax, jax.numpy as jnp
from jax import lax
import numpy as np

D_MODEL = 1024
BATCH = 8
SEQ = 8192
DEPTH = 1

HEAD_DIM = 64
SB_HEADS = D_MODEL // 128
NSA_Q_HEADS = D_MODEL // 128
NSA_KV_GROUPS = 2
SB_W = SB_HEADS * HEAD_DIM
NSA_Q_W = NSA_Q_HEADS * HEAD_DIM
NSA_KV_W = NSA_KV_GROUPS * HEAD_DIM
N_BRANCHES = 2
IN_W = 3 * SB_W + NSA_Q_W + 6 * NSA_KV_W + 3 * NSA_Q_HEADS + N_BRANCHES * D_MODEL
Q_BLOCK = 128
CMP_BLOCK = 32
CMP_STRIDE = 16
SEL_BLOCK = 64
SEL_TOPK = 16
WINDOW = 512
ROPE_THETA = 10000.0
N_EXPERTS = 32
TOP_K = 4
D_FF = D_MODEL
SWIGLU_LIMIT = 7.0
SWIGLU_ALPHA = 1.702
MOE_CHUNK = 256
LN_EPS = 1e-5
NEG_INF = -1e30
FORCE_SCORE = 1e30
DEEPNORM_ALPHA = (2.0 * DEPTH) ** 0.25
DEEPNORM_BETA = (8.0 * DEPTH) ** -0.25

kernel_name = "hybrid_stickbreak_nsa_moe_block"


def layer_norm(x, g, b):
    xf = x.astype(jnp.float32)
    mu = xf.mean(-1, keepdims=True)
    var = jnp.square(xf - mu).mean(-1, keepdims=True)
    return ((xf - mu) * lax.rsqrt(var + LN_EPS) * g.astype(jnp.float32) + b.astype(jnp.float32)).astype(x.dtype)


def rope(x, pos):
    half = HEAD_DIM // 2
    inv_freq = ROPE_THETA ** (-jnp.arange(half, dtype=jnp.float32) / half)
    ang = pos.astype(jnp.float32)[:, None] * inv_freq[None, :]
    cos = jnp.cos(ang).astype(x.dtype)
    sin = jnp.sin(ang).astype(x.dtype)
    x1, x2 = x[..., :half], x[..., half:]
    return jnp.concatenate([x1 * cos - x2 * sin, x2 * cos + x1 * sin], axis=-1)


def masked_softmax(s, mask):
    p = jax.nn.softmax(jnp.where(mask, s, NEG_INF), axis=-1)
    return jnp.where(mask, p, 0.0)


def stick_breaking_attention(q, k, v):
    B, H, S, dh = q.shape
    nblk = S // Q_BLOCK
    qb = q.reshape(B, H, nblk, Q_BLOCK, dh).transpose(2, 0, 1, 3, 4)
    kpos = jnp.arange(S)
    scale = dh ** -0.5

    def block(args):
        qi, i = args
        qpos = i * Q_BLOCK + jnp.arange(Q_BLOCK)
        z = jnp.einsum('bhqd,bhkd->bhqk', qi, k).astype(jnp.float32) * scale
        past = kpos[None, :] < qpos[:, None]
        log_keep = jnp.where(past, jax.nn.log_sigmoid(-z), 0.0)
        tail = lax.cumsum(log_keep, axis=3, reverse=True) - log_keep
        a = jnp.where(past, jnp.exp(jax.nn.log_sigmoid(z) + tail), 0.0)
        return jnp.einsum('bhqk,bhkd->bhqd', a.astype(v.dtype), v)

    out = lax.map(block, (qb, jnp.arange(nblk)))
    return out.transpose(1, 2, 0, 3, 4).reshape(B, H, S, dh)


def compress_tokens(x, pe, w1, w2):
    B, G, S, dh = x.shape
    n_sub = CMP_BLOCK // CMP_STRIDE
    n_stride = S // CMP_STRIDE
    xs = x.reshape(B, G, n_stride, CMP_STRIDE, dh)
    blocks = jnp.concatenate([xs[:, :, j:n_stride - n_sub + 1 + j] for j in range(n_sub)], axis=3)
    blocks = blocks + pe
    flat = blocks.reshape(B, G, blocks.shape[2], CMP_BLOCK * dh)
    return jax.nn.gelu(flat @ w1) @ w2


def nsa_attention(q, kc, vc, ks, vs, kw, vw, gates):
    B, Hq, S, dh = q.shape
    G = ks.shape[1]
    R = Hq // G
    nblk = S // Q_BLOCK
    n_sel = S // SEL_BLOCK
    ratio = SEL_BLOCK // CMP_STRIDE
    n_sub = CMP_BLOCK // CMP_STRIDE
    topk = min(SEL_TOPK, n_sel)
    scale = dh ** -0.5
    cmp_end = jnp.arange(kc.shape[2]) * CMP_STRIDE + CMP_BLOCK - 1
    ks_blk = ks.reshape(B, G, n_sel, SEL_BLOCK, dh)
    vs_blk = vs.reshape(B, G, n_sel, SEL_BLOCK, dh)
    kw_pad = jnp.pad(kw, ((0, 0), (0, 0), (WINDOW, 0), (0, 0)))
    vw_pad = jnp.pad(vw, ((0, 0), (0, 0), (WINDOW, 0), (0, 0)))
    qb = q.reshape(B, G, R, nblk, Q_BLOCK, dh).transpose(3, 0, 1, 2, 4, 5)
    gather = jax.vmap(jax.vmap(lambda blk, ix: blk[ix]))
    sel_j = jnp.arange(n_sel)

    def block(args):
        qi, i = args
        qpos = i * Q_BLOCK + jnp.arange(Q_BLOCK)
        sc = jnp.einsum('bgrqd,bgnd->bgrqn', qi, kc).astype(jnp.float32) * scale
        p_cmp = masked_softmax(sc, cmp_end[None, :] <= qpos[:, None])
        o_cmp = jnp.einsum('bgrqn,bgnd->bgrqd', p_cmp.astype(vc.dtype), vc)
        imp = jnp.pad(p_cmp.sum(axis=2), ((0, 0), (0, 0), (0, 0), (n_sub - 1, n_sub - 1)))
        p_slc = sum(imp[..., m:m + n_sel * ratio:ratio] for m in range(ratio + n_sub - 1))
        blk_t = qpos // SEL_BLOCK
        visible = sel_j[None, :] <= blk_t[:, None]
        forced = (sel_j[None, :] == 0) | (sel_j[None, :] == blk_t[:, None]) | (sel_j[None, :] == blk_t[:, None] - 1)
        score = jnp.where(forced, FORCE_SCORE, jnp.where(visible, p_slc, NEG_INF))
        _, idx = lax.top_k(score, topk)
        k_sel = gather(ks_blk, idx).reshape(B, G, Q_BLOCK, topk * SEL_BLOCK, dh)
        v_sel = gather(vs_blk, idx).reshape(B, G, Q_BLOCK, topk * SEL_BLOCK, dh)
        sel_pos = (idx[..., None] * SEL_BLOCK + jnp.arange(SEL_BLOCK)).reshape(B, G, Q_BLOCK, topk * SEL_BLOCK)
        smask = (sel_pos <= qpos[None, None, :, None])[:, :, None]
        ss = jnp.einsum('bgrqd,bgqnd->bgrqn', qi, k_sel).astype(jnp.float32) * scale
        o_sel = jnp.einsum('bgrqn,bgqnd->bgrqd', masked_softmax(ss, smask).astype(v_sel.dtype), v_sel)
        kwin = lax.dynamic_slice_in_dim(kw_pad, i * Q_BLOCK, WINDOW + Q_BLOCK, axis=2)
        vwin = lax.dynamic_slice_in_dim(vw_pad, i * Q_BLOCK, WINDOW + Q_BLOCK, axis=2)
        wpos = i * Q_BLOCK - WINDOW + jnp.arange(WINDOW + Q_BLOCK)
        wmask = (wpos[None, :] >= 0) & (wpos[None, :] <= qpos[:, None]) & (qpos[:, None] - wpos[None, :] < WINDOW)
        sw = jnp.einsum('bgrqd,bgkd->bgrqk', qi, kwin).astype(jnp.float32) * scale
        o_win = jnp.einsum('bgrqk,bgkd->bgrqd', masked_softmax(sw, wmask).astype(vwin.dtype), vwin)
        return jnp.stack([o_cmp, o_sel, o_win])

    out = lax.map(block, (qb, jnp.arange(nblk)))
    out = out.transpose(1, 2, 3, 4, 0, 5, 6).reshape(3, B, Hq, S, dh)
    g = gates.transpose(3, 0, 2, 1)[..., None].astype(out.dtype)
    return (g * out).sum(axis=0)


def clamped_swiglu(gu):
    gate, up = jnp.split(gu, 2, axis=-1)
    gate = jnp.minimum(gate, SWIGLU_LIMIT)
    up = jnp.clip(up, -SWIGLU_LIMIT, SWIGLU_LIMIT)
    return gate * jax.nn.sigmoid(SWIGLU_ALPHA * gate) * (up + 1.0)


def moe_ffn(x, w_router, b_router, w_gate_up, b_gate_up, w_down, b_down):
    B, S, D = x.shape
    xt = x.reshape(-1, D)
    N = xt.shape[0]
    M = N * TOP_K
    logits = (xt @ w_router + b_router).astype(jnp.float32)
    top_vals, top_idx = lax.top_k(logits, TOP_K)
    gate_w = jax.nn.softmax(top_vals, axis=-1)
    e_flat = top_idx.reshape(M)
    order = jnp.argsort(e_flat)
    e_sorted = e_flat[order]
    tok_sorted = order // TOP_K
    w_sorted = gate_w.reshape(M)[order]
    counts = jax.ops.segment_sum(jnp.ones((M,), jnp.int32), e_flat, num_segments=N_EXPERTS)
    padded = (counts + MOE_CHUNK - 1) // MOE_CHUNK * MOE_CHUNK
    starts = jnp.cumsum(counts) - counts
    pends = jnp.cumsum(padded)
    pstarts = pends - padded
    dest = pstarts[e_sorted] + (jnp.arange(M) - starts[e_sorted])
    n_chunks = -(-M // MOE_CHUNK) + N_EXPERTS
    buf = jnp.zeros((n_chunks * MOE_CHUNK, D), x.dtype).at[dest].set(xt[tok_sorted])
    chunk_expert = jnp.clip(jnp.searchsorted(pends, jnp.arange(n_chunks) * MOE_CHUNK, side='right'), 0, N_EXPERTS - 1)

    def expert_block(args):
        xc, e = args
        h = clamped_swiglu(xc @ w_gate_up[e] + b_gate_up[e])
        return h @ w_down[e] + b_down[e]

    out = lax.map(expert_block, (buf.reshape(n_chunks, MOE_CHUNK, D), chunk_expert)).reshape(-1, D)
    y = jax.ops.segment_sum(out[dest] * w_sorted[:, None].astype(out.dtype), tok_sorted, num_segments=N)
    return y.reshape(B, S, D)


def setup_inputs(seed: int = 0) -> dict:
    key = jax.random.key(seed)
    ks = jax.random.split(key, 24)
    L = DEPTH
    flat = CMP_BLOCK * HEAD_DIM

    def nrm(k, shape, scale):
        return jax.random.normal(k, shape, jnp.float32) * scale

    return {
        "x": nrm(ks[0], (BATCH, SEQ, D_MODEL), 1.0),
        "w_in": nrm(ks[1], (L, D_MODEL, IN_W), D_MODEL ** -0.5),
        "cmp_pe_k": nrm(ks[2], (L, CMP_BLOCK, HEAD_DIM), 0.1),
        "cmp_w1_k": nrm(ks[3], (L, flat, HEAD_DIM), flat ** -0.5),
        "cmp_w2_k": nrm(ks[4], (L, HEAD_DIM, HEAD_DIM), HEAD_DIM ** -0.5),
        "cmp_pe_v": nrm(ks[5], (L, CMP_BLOCK, HEAD_DIM), 0.1),
        "cmp_w1_v": nrm(ks[6], (L, flat, HEAD_DIM), flat ** -0.5),
        "cmp_w2_v": nrm(ks[7], (L, HEAD_DIM, HEAD_DIM), HEAD_DIM ** -0.5),
        "w_proj_sb": nrm(ks[8], (L, SB_W, D_MODEL), SB_W ** -0.5),
        "w_proj_nsa": nrm(ks[9], (L, NSA_Q_W, D_MODEL), NSA_Q_W ** -0.5),
        "w_out": nrm(ks[10], (L, D_MODEL, D_MODEL), D_MODEL ** -0.5 * DEEPNORM_BETA),
        "ln1_g": 1.0 + nrm(ks[11], (L, D_MODEL), 0.01),
        "ln1_b": nrm(ks[12], (L, D_MODEL), 0.01),
        "w_router": nrm(ks[13], (L, D_MODEL, N_EXPERTS), D_MODEL ** -0.5),
        "b_router": nrm(ks[14], (L, N_EXPERTS), 0.01),
        "w_gate_up": nrm(ks[15], (L, N_EXPERTS, D_MODEL, 2 * D_FF), D_MODEL ** -0.5),
        "b_gate_up": nrm(ks[16], (L, N_EXPERTS, 2 * D_FF), 0.01),
        "w_down": nrm(ks[17], (L, N_EXPERTS, D_FF, D_MODEL), D_FF ** -0.5 * DEEPNORM_BETA),
        "b_down": nrm(ks[18], (L, N_EXPERTS, D_MODEL), 0.01),
        "ln2_g": 1.0 + nrm(ks[19], (L, D_MODEL), 0.01),
        "ln2_b": nrm(ks[20], (L, D_MODEL), 0.01),
    }


def reference(x, w_in, cmp_pe_k, cmp_w1_k, cmp_w2_k, cmp_pe_v, cmp_w1_v, cmp_w2_v,
              w_proj_sb, w_proj_nsa, w_out, ln1_g, ln1_b, w_router, b_router,
              w_gate_up, b_gate_up, w_down, b_down, ln2_g, ln2_b):
    B, S, D = x.shape
    pos = jnp.arange(S, dtype=jnp.int32)
    sizes = [SB_W, SB_W, SB_W, NSA_Q_W] + [NSA_KV_W] * 6 + [3 * NSA_Q_HEADS, N_BRANCHES * D_MODEL]
    split_at = [int(c) for c in np.cumsum(sizes)[:-1]]

    def heads(t, n):
        return t.reshape(B, S, n, HEAD_DIM).transpose(0, 2, 1, 3)

    def merge_heads(t):
        return t.transpose(0, 2, 1, 3).reshape(B, S, -1)

    h = x
    for l in range(DEPTH):
        proj = h @ w_in[l]
        sb_q, sb_k, sb_v, n_q, k_c, v_c, k_s, v_s, k_w, v_w, n_g, m_g = jnp.split(proj, split_at, axis=-1)
        y_sb = stick_breaking_attention(heads(sb_q, SB_HEADS), heads(sb_k, SB_HEADS), heads(sb_v, SB_HEADS))
        q_n = rope(heads(n_q, NSA_Q_HEADS), pos)
        kc = compress_tokens(rope(heads(k_c, NSA_KV_GROUPS), pos), cmp_pe_k[l], cmp_w1_k[l], cmp_w2_k[l])
        vc = compress_tokens(heads(v_c, NSA_KV_GROUPS), cmp_pe_v[l], cmp_w1_v[l], cmp_w2_v[l])
        ks_ = rope(heads(k_s, NSA_KV_GROUPS), pos)
        kw_ = rope(heads(k_w, NSA_KV_GROUPS), pos)
        nsa_gates = jax.nn.sigmoid(n_g).reshape(B, S, NSA_Q_HEADS, 3)
        y_nsa = nsa_attention(q_n, kc, vc, ks_, heads(v_s, NSA_KV_GROUPS), kw_, heads(v_w, NSA_KV_GROUPS), nsa_gates)
        mg = jax.nn.sigmoid(m_g).reshape(B, S, N_BRANCHES, D)
        merged = mg[:, :, 0] * (merge_heads(y_sb) @ w_proj_sb[l]) + mg[:, :, 1] * (merge_heads(y_nsa) @ w_proj_nsa[l])
        h = layer_norm(DEEPNORM_ALPHA * h + merged @ w_out[l], ln1_g[l], ln1_b[l])
        y_moe = moe_ffn(h, w_router[l], b_router[l], w_gate_up[l], b_gate_up[l], w_down[l], b_down[l])
        h = layer_norm(DEEPNORM_ALPHA * h + y_moe, ln2_g[l], ln2_b[l])
    return h
```

```python
import functools

import numpy as np
import jax
import jax.numpy as jnp
from jax import lax
from jax.experimental import pallas as pl
from jax.experimental.pallas import tpu as pltpu

D_MODEL = 1024
HEAD_DIM = 64
LANES = 128
Q_BLOCK = 128
CMP_BLOCK = 32
CMP_STRIDE = 16
SEL_BLOCK = 64
SEL_TOPK = 16
WINDOW = 512
ROPE_THETA = 10000.0
N_EXPERTS = 32
TOP_K = 4
SWIGLU_LIMIT = 7.0
SWIGLU_ALPHA = 1.702
LN_EPS = 1e-5
NEG_INF = -1e30
FORCE_SCORE = 1e30
TAKEN = -3e38
DEEPNORM_ALPHA = 2.0 ** 0.25
QK_SCALE = HEAD_DIM ** -0.5

CB_SBQ, CB_SBK, CB_SBV, CB_NQ = 0, 4, 8, 12
CB_KC, CB_VC, CB_KS, CB_VS, CB_KW, CB_VW = 16, 17, 18, 19, 20, 21
CB_NG = 22
CB_MG = 24
PROJ_W = 40 * LANES

SB_TAIL_CUTOFF = -110.0

MOE_ROWS = 256
SLAB = D_MODEL // LANES
VMEM_LIMIT = 56 * 1024 * 1024

_bf16 = jnp.bfloat16
_f32 = jnp.float32


def _cparams(sem):
    return pltpu.CompilerParams(dimension_semantics=sem, vmem_limit_bytes=VMEM_LIMIT)


def _dot_t(a, b):
    return lax.dot_general(a, b, (((1,), (1,)), ((), ())), preferred_element_type=_f32)


def _dot(a, b):
    return jnp.dot(a, b, preferred_element_type=_f32)


def _lane_iota(shape):
    return lax.broadcasted_iota(jnp.int32, shape, len(shape) - 1)


def _half0(shape=(1, LANES)):
    return _lane_iota(shape) < HEAD_DIM


def _in_proj_kernel(x_ref, w_ref, o_ref):
    o_ref[...] = _dot(x_ref[...].astype(_bf16), w_ref[...]).astype(o_ref.dtype)


def _in_proj(x2, w):
    n = x2.shape[0]
    tm, tn = 512, 1280
    return pl.pallas_call(
        _in_proj_kernel,
        grid=(PROJ_W // tn, n // tm),
        in_specs=[pl.BlockSpec((tm, D_MODEL), lambda j, i: (i, 0)),
                  pl.BlockSpec((D_MODEL, tn), lambda j, i: (0, j))],
        out_specs=pl.BlockSpec((tm, tn), lambda j, i: (i, j)),
        out_shape=jax.ShapeDtypeStruct((n, PROJ_W), _bf16),
        compiler_params=_cparams(("arbitrary", "arbitrary")),
        name="in_proj",
    )(x2, w)


def _rope(x, cos, sin_signed):
    first = (_lane_iota((1, LANES)) % HEAD_DIM) < (HEAD_DIM // 2)
    swapped = jnp.where(first, pltpu.roll(x, LANES - HEAD_DIM // 2, 1), pltpu.roll(x, HEAD_DIM // 2, 1))
    return x * cos + swapped * sin_signed


def _dup(x, g):
    other = pltpu.roll(x, HEAD_DIM, 1)
    h0 = _half0()
    return jnp.where(h0, x, other) if g == 0 else jnp.where(h0, other, x)


def _rope_kernel(nq_ref, kc_ref, vc_ref, ks_ref, vs_ref, kw_ref, vw_ref, cos_ref, sin_ref,
                 nq_o, kc_o, vc_o, ka_o, vs_o, kw_o, vw_o, *, blocks_per_seq):
    ts = cos_ref.shape[0]
    cos = cos_ref[...]
    sin = sin_ref[...]
    for c in range(4):
        sl = slice(c * LANES, (c + 1) * LANES)
        nq_o[:, sl] = (_rope(nq_ref[:, sl].astype(_f32), cos, sin) * QK_SCALE).astype(_bf16)
    kc_o[...] = _rope(kc_ref[...].astype(_f32), cos, sin).astype(_bf16)
    vc_o[...] = vc_ref[...]
    ks = _rope(ks_ref[...].astype(_f32), cos, sin)
    kw = _rope(kw_ref[...].astype(_f32), cos, sin)
    vs = vs_ref[...].astype(_f32)
    vw = vw_ref[...].astype(_f32)
    pos = (pl.program_id(0) % blocks_per_seq) * ts + lax.broadcasted_iota(jnp.int32, (ts, LANES), 0)
    onehot = jnp.where(pos // SEL_BLOCK == _lane_iota((ts, LANES)), 1.0, 0.0).astype(_bf16)
    for g in range(2):
        ka_o[g, :, 0:LANES] = _dup(ks, g).astype(_bf16)
        ka_o[g, :, LANES:2 * LANES] = onehot
        vs_o[g] = _dup(vs, g).astype(_bf16)
        kw_o[g] = _dup(kw, g).astype(_bf16)
        vw_o[g] = _dup(vw, g).astype(_bf16)


def _rope_stage(proj, cos, sin_signed, seq):
    n = proj.shape[0]
    ts = 512
    bps = seq // ts
    col = lambda cb: pl.BlockSpec((ts, LANES), lambda i, cb=cb: (i, cb))
    tab = pl.BlockSpec((ts, LANES), lambda i: (i % bps, 0))
    grp = lambda w: pl.BlockSpec((2, ts, w), lambda i: (0, i, 0))
    return pl.pallas_call(
        functools.partial(_rope_kernel, blocks_per_seq=bps),
        grid=(n // ts,),
        in_specs=[pl.BlockSpec((ts, 4 * LANES), lambda i: (i, CB_NQ // 4)),
                  col(CB_KC), col(CB_VC), col(CB_KS), col(CB_VS), col(CB_KW), col(CB_VW), tab, tab],
        out_specs=[pl.BlockSpec((ts, 4 * LANES), lambda i: (i, 0)),
                   pl.BlockSpec((ts, LANES), lambda i: (i, 0)),
                   pl.BlockSpec((ts, LANES), lambda i: (i, 0)),
                   grp(2 * LANES), grp(LANES), grp(LANES), grp(LANES)],
        out_shape=[jax.ShapeDtypeStruct((n, 4 * LANES), _bf16),
                   jax.ShapeDtypeStruct((n, LANES), _bf16),
                   jax.ShapeDtypeStruct((n, LANES), _bf16),
                   jax.ShapeDtypeStruct((2, n, 2 * LANES), _bf16),
                   jax.ShapeDtypeStruct((2, n, LANES), _bf16),
                   jax.ShapeDtypeStruct((2, n, LANES), _bf16),
                   jax.ShapeDtypeStruct((2, n, LANES), _bf16)],
        compiler_params=_cparams(("arbitrary",)),
        name="rope_layout",
    )(proj, proj, proj, proj, proj, proj, proj, cos, sin_signed)


def _gelu_tanh(x):
    return 0.5 * x * (1.0 + jnp.tanh(0.7978845608028654 * (x + 0.044715 * (x * x * x))))


def _compress_one(x_ref, pe_t, pe_b, w_t, w_b, w2, out_ref):
    x = x_ref[0].astype(_f32)
    a = _dot((x + pe_t[...]).astype(_bf16), w_t[...])
    b = _dot((x + pe_b[...]).astype(_bf16), w_b[...])
    nc = a.shape[0]
    pre = a + pltpu.roll(b, nc - 1, 0)
    y = _dot(_gelu_tanh(pre).astype(_bf16), w2[...])
    for g in range(2):
        out_ref[0, g] = _dup(y, g).astype(_bf16)


def _compress_kernel(k_ref, v_ref, kpt, kpb, kwt, kwb, kw2, vpt, vpb, vwt, vwb, vw2, ko_ref, vo_ref):
    _compress_one(k_ref, kpt, kpb, kwt, kwb, kw2, ko_ref)
    _compress_one(v_ref, vpt, vpb, vwt, vwb, vw2, vo_ref)


def _compress_weights(pe, w1, w2):
    half = CMP_BLOCK // 2
    eye = jnp.eye(2, dtype=_f32)
    outs = []
    for part in range(2):
        w = w1[part * half * HEAD_DIM:(part + 1) * half * HEAD_DIM].reshape(half, HEAD_DIM, HEAD_DIM)
        wbd = (w[:, None, :, None, :] * eye[None, :, None, :, None]).reshape(half * 2 * HEAD_DIM, 2 * HEAD_DIM)
        p = jnp.broadcast_to(pe[part * half:(part + 1) * half, None, :], (half, 2, HEAD_DIM)).reshape(1, -1)
        outs.append((p.astype(_f32), wbd.astype(_bf16)))
    w2bd = (w2[None, :, None, :] * eye[:, None, :, None]).reshape(2 * HEAD_DIM, 2 * HEAD_DIM).astype(_bf16)
    (pt, wt), (pb, wb) = outs
    return pt, pb, wt, wb, w2bd


def _compress_stage(kc_r, vc_r, kparams, vparams, batch, seq):
    nc = seq // CMP_STRIDE
    width = CMP_STRIDE * LANES
    xs = pl.BlockSpec((1, nc, width), lambda b: (b, 0, 0))
    full = lambda a: pl.BlockSpec(a.shape, lambda b: (0,) * a.ndim)
    out = pl.BlockSpec((1, 2, nc, LANES), lambda b: (b, 0, 0, 0))
    weights = list(kparams) + list(vparams)
    return pl.pallas_call(
        _compress_kernel,
        grid=(batch,),
        in_specs=[xs, xs] + [full(a) for a in weights],
        out_specs=[out, out],
        out_shape=[jax.ShapeDtypeStruct((batch, 2, nc, LANES), _bf16)] * 2,
        compiler_params=_cparams(("arbitrary",)),
        name="compress",
    )(kc_r.reshape(batch, nc, width), vc_r.reshape(batch, nc, width), *weights)


def _head_q(q_ref, r):
    q2 = q_ref[:, (r // 2) * LANES:(r // 2 + 1) * LANES]
    keep = _half0() if r % 2 == 0 else jnp.logical_not(_half0())
    return jnp.where(keep, q2, jnp.zeros_like(q2))


def _masked_softmax(s, mask):
    s = jnp.where(mask, s, NEG_INF)
    e = jnp.where(mask, jnp.exp(s - jnp.max(s, axis=-1, keepdims=True)), 0.0)
    l = jnp.sum(e, axis=-1, keepdims=True)
    return e * (1.0 / jnp.where(l > 0.0, l, 1.0))


def _pair(even, odd):
    return jnp.where(_half0(), even, odd)


def _nsa_cw_kernel(q_ref, kc_ref, vc_ref, kw_ref, vw_ref, ng_ref, st_ref, yp_ref, mb_ref):
    t0 = pl.program_id(2) * Q_BLOCK
    rows = t0 + lax.broadcasted_iota(jnp.int32, (Q_BLOCK, 1), 0)
    gates = jax.nn.sigmoid(ng_ref[...].astype(_f32))
    kc = kc_ref[0, 0]
    vc = vc_ref[0, 0]
    nc = kc.shape[0]
    cmask = (_lane_iota((1, nc)) * CMP_STRIDE + (CMP_BLOCK - 1)) <= rows
    start = pl.multiple_of(jnp.maximum(t0 - WINDOW, 0), Q_BLOCK)
    wlen = WINDOW + Q_BLOCK
    kwin = kw_ref[0, 0, pl.ds(start, wlen), :]
    vwin = vw_ref[0, 0, pl.ds(start, wlen), :]
    kpos = start + _lane_iota((1, wlen))
    wmask = (kpos <= rows) & (rows - kpos < WINDOW)

    imp = jnp.zeros((Q_BLOCK, nc), _f32)
    ys = []
    for r in range(4):
        q = _head_q(q_ref, r)
        p = _masked_softmax(_dot_t(q, kc), cmask)
        imp = imp + p
        o_cmp = _dot(p.astype(_bf16), vc)
        pw = _masked_softmax(_dot_t(q, kwin), wmask)
        o_win = _dot(pw.astype(_bf16), vwin)
        ys.append(gates[:, 3 * r:3 * r + 1] * o_cmp + gates[:, 3 * r + 2:3 * r + 3] * o_win)
    yp_ref[:, 0:LANES] = _pair(ys[0], ys[1])
    yp_ref[:, LANES:2 * LANES] = _pair(ys[2], ys[3])

    p_slc = jnp.dot(imp, st_ref[...], preferred_element_type=_f32, precision=lax.Precision.HIGHEST)
    selj = _lane_iota((1, LANES))
    blk_t = rows // SEL_BLOCK
    forced = (selj == 0) | (selj == blk_t) | (selj == blk_t - 1)
    score = jnp.where(forced, FORCE_SCORE, jnp.where(selj <= blk_t, p_slc, NEG_INF))
    lanef = selj.astype(_f32)
    picked = jnp.zeros((Q_BLOCK, LANES), jnp.bool_)
    for _ in range(SEL_TOPK):
        m = jnp.max(score, axis=-1, keepdims=True)
        first = jnp.min(jnp.where(score == m, lanef, float(LANES)), axis=-1, keepdims=True)
        hit = lanef == first
        picked = picked | hit
        score = jnp.where(hit, TAKEN, score)
    mb_ref[0] = jnp.where(picked, 0.0, NEG_INF).astype(_bf16)


def _nsa_cw_stage(nq_r, kc_d, vc_d, kw_d, vw_d, proj, stencil, batch, seq):
    n = nq_r.shape[0]
    nblk = seq // Q_BLOCK
    nc = seq // CMP_STRIDE
    qrow = lambda b, g, i: b * nblk + i
    return pl.pallas_call(
        _nsa_cw_kernel,
        grid=(batch, 2, nblk),
        in_specs=[pl.BlockSpec((Q_BLOCK, 2 * LANES), lambda b, g, i: (qrow(b, g, i), g)),
                  pl.BlockSpec((1, 1, nc, LANES), lambda b, g, i: (b, g, 0, 0)),
                  pl.BlockSpec((1, 1, nc, LANES), lambda b, g, i: (b, g, 0, 0)),
                  pl.BlockSpec((1, 1, seq, LANES), lambda b, g, i: (g, b, 0, 0)),
                  pl.BlockSpec((1, 1, seq, LANES), lambda b, g, i: (g, b, 0, 0)),
                  pl.BlockSpec((Q_BLOCK, LANES), lambda b, g, i: (qrow(b, g, i), CB_NG + g)),
                  pl.BlockSpec((nc, LANES), lambda b, g, i: (0, 0))],
        out_specs=[pl.BlockSpec((Q_BLOCK, 2 * LANES), lambda b, g, i: (qrow(b, g, i), g)),
                   pl.BlockSpec((1, Q_BLOCK, LANES), lambda b, g, i: (g, qrow(b, g, i), 0))],
        out_shape=[jax.ShapeDtypeStruct((n, 4 * LANES), _f32),
                   jax.ShapeDtypeStruct((2, n, LANES), _bf16)],
        compiler_params=_cparams(("arbitrary", "arbitrary", "arbitrary")),
        name="nsa_cmp_win_select",
    )(nq_r, kc_d, vc_d, kw_d.reshape(2, batch, seq, LANES), vw_d.reshape(2, batch, seq, LANES), proj, stencil)


def _nsa_sel_kernel(q_ref, mb_ref, ka_ref, vs_ref, ng_ref, yp_ref, o_ref, qs_ref, m_ref, l_ref, acc_ref, *, tk):
    t0 = pl.program_id(2) * Q_BLOCK
    mb = mb_ref[0]
    for r in range(4):
        qs_ref[r * Q_BLOCK:(r + 1) * Q_BLOCK, 0:LANES] = _head_q(q_ref, r)
        qs_ref[r * Q_BLOCK:(r + 1) * Q_BLOCK, LANES:2 * LANES] = mb
    m_ref[...] = jnp.full(m_ref.shape, NEG_INF, _f32)
    l_ref[...] = jnp.zeros(l_ref.shape, _f32)
    acc_ref[...] = jnp.zeros(acc_ref.shape, _f32)
    rows = t0 + lax.broadcasted_iota(jnp.int32, (4 * Q_BLOCK, 1), 0) % Q_BLOCK

    def body(kt, carry):
        k0 = pl.multiple_of(kt * tk, tk)
        s = _dot_t(qs_ref[...], ka_ref[0, 0, pl.ds(k0, tk), :])
        s = jnp.where(k0 + _lane_iota((1, tk)) <= rows, s, NEG_INF)
        m_old = m_ref[...]
        m_new = jnp.maximum(m_old, jnp.max(s, axis=-1, keepdims=True))
        alpha = jnp.exp(m_old - m_new)
        p = jnp.exp(s - m_new)
        l_ref[...] = alpha * l_ref[...] + jnp.sum(p, axis=-1, keepdims=True)
        acc_ref[...] = alpha * acc_ref[...] + _dot(p.astype(_bf16), vs_ref[0, 0, pl.ds(k0, tk), :])
        m_ref[...] = m_new
        return carry

    lax.fori_loop(0, (t0 + Q_BLOCK - 1) // tk + 1, body, 0)
    o = acc_ref[...] * (1.0 / l_ref[...])
    gates = jax.nn.sigmoid(ng_ref[...].astype(_f32))
    ys = [gates[:, 3 * r + 1:3 * r + 2] * o[r * Q_BLOCK:(r + 1) * Q_BLOCK] for r in range(4)]
    o_ref[:, 0:LANES] = (yp_ref[:, 0:LANES] + _pair(ys[0], ys[1])).astype(o_ref.dtype)
    o_ref[:, LANES:2 * LANES] = (yp_ref[:, LANES:2 * LANES] + _pair(ys[2], ys[3])).astype(o_ref.dtype)


def _nsa_sel_stage(nq_r, mbias, k_aug, vs_d, proj, ypart, batch, seq):
    n = nq_r.shape[0]
    nblk = seq // Q_BLOCK
    tk = min(512, seq)
    qrow = lambda b, g, i: b * nblk + i
    return pl.pallas_call(
        functools.partial(_nsa_sel_kernel, tk=tk),
        grid=(batch, 2, nblk),
        in_specs=[pl.BlockSpec((Q_BLOCK, 2 * LANES), lambda b, g, i: (qrow(b, g, i), g)),
                  pl.BlockSpec((1, Q_BLOCK, LANES), lambda b, g, i: (g, qrow(b, g, i), 0)),
                  pl.BlockSpec((1, 1, seq, 2 * LANES), lambda b, g, i: (g, b, 0, 0)),
                  pl.BlockSpec((1, 1, seq, LANES), lambda b, g, i: (g, b, 0, 0)),
                  pl.BlockSpec((Q_BLOCK, LANES), lambda b, g, i: (qrow(b, g, i), CB_NG + g)),
                  pl.BlockSpec((Q_BLOCK, 2 * LANES), lambda b, g, i: (qrow(b, g, i), g))],
        out_specs=pl.BlockSpec((Q_BLOCK, 2 * LANES), lambda b, g, i: (qrow(b, g, i), g)),
        out_shape=jax.ShapeDtypeStruct((n, 4 * LANES), _bf16),
        scratch_shapes=[pltpu.VMEM((4 * Q_BLOCK, 2 * LANES), _bf16),
                        pltpu.VMEM((4 * Q_BLOCK, 1), _f32),
                        pltpu.VMEM((4 * Q_BLOCK, 1), _f32),
                        pltpu.VMEM((4 * Q_BLOCK, LANES), _f32)],
        compiler_params=_cparams(("arbitrary", "arbitrary", "arbitrary")),
        name="nsa_selected",
    )(nq_r, mbias, k_aug.reshape(2, batch, seq, 2 * LANES), vs_d.reshape(2, batch, seq, LANES), proj, ypart)


def _sb_kernel(q_ref, k_ref, v_ref, o_ref):
    i = pl.program_id(2)
    h0 = _half0()
    q = q_ref[...]
    zero = jnp.zeros_like(q)
    qh = [jnp.where(h0, q, zero) * QK_SCALE, jnp.where(h0, zero, q) * QK_SCALE]
    rloc = lax.broadcasted_iota(jnp.int32, (Q_BLOCK, Q_BLOCK), 0)
    cloc = lax.broadcasted_iota(jnp.int32, (Q_BLOCK, Q_BLOCK), 1)
    later = jnp.where(rloc > cloc, 1.0, 0.0).astype(_bf16)

    def cond(c):
        j, t0_, t1_, _, _ = c
        return (j >= 0) & (jnp.maximum(jnp.max(t0_), jnp.max(t1_)) > SB_TAIL_CUTOFF)

    def body(c):
        j, t0_, t1_, a0, a1 = c
        k0 = pl.multiple_of(j * Q_BLOCK, Q_BLOCK)
        kj = k_ref[0, pl.ds(k0, Q_BLOCK), :]
        vj = v_ref[0, pl.ds(k0, Q_BLOCK), :]
        past = (cloc + (j - i) * Q_BLOCK) < rloc
        tails, accs = [], []
        for qx, tail, acc in ((qh[0], t0_, a0), (qh[1], t1_, a1)):
            z = _dot_t(qx, kj)
            sp = jnp.maximum(z, 0.0) + jnp.log1p(jnp.exp(-jnp.abs(z)))
            log_keep = jnp.where(past, -sp, 0.0)
            hi = log_keep.astype(_bf16)
            lo = (log_keep - hi.astype(_f32)).astype(_bf16)
            inner = _dot(hi, later) + _dot(lo, later)
            a = jnp.where(past, jnp.exp((z - sp) + inner + tail), 0.0)
            accs.append(acc + _dot(a.astype(_bf16), vj))
            tails.append(tail + jnp.sum(log_keep, axis=-1, keepdims=True))
        return j - 1, tails[0], tails[1], accs[0], accs[1]

    zt = jnp.zeros((Q_BLOCK, 1), _f32)
    za = jnp.zeros((Q_BLOCK, LANES), _f32)
    _, _, _, a0, a1 = lax.while_loop(cond, body, (i, zt, zt, za, za))
    o_ref[...] = jnp.where(h0, a0, a1).astype(o_ref.dtype)


def _sb_stage(proj, batch, seq):
    n = proj.shape[0]
    nblk = seq // Q_BLOCK
    proj3 = proj.reshape(batch, seq, PROJ_W)
    return pl.pallas_call(
        _sb_kernel,
        grid=(batch, 4, nblk),
        in_specs=[pl.BlockSpec((Q_BLOCK, LANES), lambda b, p, i: (b * nblk + i, CB_SBQ + p)),
                  pl.BlockSpec((1, seq, LANES), lambda b, p, i: (b, 0, CB_SBK + p)),
                  pl.BlockSpec((1, seq, LANES), lambda b, p, i: (b, 0, CB_SBV + p))],
        out_specs=pl.BlockSpec((Q_BLOCK, LANES), lambda b, p, i: (b * nblk + i, p)),
        out_shape=jax.ShapeDtypeStruct((n, 4 * LANES), _bf16),
        compiler_params=_cparams(("arbitrary", "arbitrary", "arbitrary")),
        name="stick_breaking",
    )(proj, proj3, proj3)


def _layer_norm(x, g, b):
    mu = jnp.mean(x, axis=-1, keepdims=True)
    xc = x - mu
    var = jnp.mean(xc * xc, axis=-1, keepdims=True)
    return xc * lax.rsqrt(var + LN_EPS) * g + b


def _merge_kernel(x_ref, ysb_ref, yns_ref, mg0_ref, mg1_ref, wsb_ref, wns_ref, wo_ref, g_ref, b_ref,
                  wr_ref, br_ref, h_ref, idx_ref, gw_ref):
    m0 = jax.nn.sigmoid(mg0_ref[...].astype(_f32))
    m1 = jax.nn.sigmoid(mg1_ref[...].astype(_f32))
    merged = m0 * _dot(ysb_ref[...], wsb_ref[...]) + m1 * _dot(yns_ref[...], wns_ref[...])
    pre = DEEPNORM_ALPHA * x_ref[...] + _dot(merged.astype(_bf16), wo_ref[...])
    h = _layer_norm(pre, g_ref[...], b_ref[...])
    tm = h.shape[0]
    for s in range(SLAB):
        h_ref[pl.ds(s, tm, stride=SLAB), :] = h[:, s * LANES:(s + 1) * LANES]
    logits = jnp.dot(h, wr_ref[...], preferred_element_type=_f32, precision=lax.Precision.HIGHEST) + br_ref[...]
    lane = _lane_iota((1, LANES))
    lanef = lane.astype(_f32)
    lg = jnp.where(lane < N_EXPERTS, logits, TAKEN)
    vals, idxs = [], []
    for _ in range(TOP_K):
        m = jnp.max(lg, axis=-1, keepdims=True)
        first = jnp.min(jnp.where(lg == m, lanef, float(LANES)), axis=-1, keepdims=True)
        vals.append(m)
        idxs.append(first)
        lg = jnp.where(lanef == first, TAKEN, lg)
    es = [jnp.exp(v - vals[0]) for v in vals]
    inv = 1.0 / (es[0] + es[1] + es[2] + es[3])
    idx_t = jnp.zeros(lg.shape, _f32)
    gw_t = jnp.zeros(lg.shape, _f32)
    for k in range(TOP_K):
        idx_t = jnp.where(lane == k, idxs[k], idx_t)
        gw_t = jnp.where(lane == k, es[k] * inv, gw_t)
    idx_ref[...] = idx_t[:, :TOP_K].astype(jnp.int32)
    gw_ref[...] = gw_t[:, :TOP_K]


def _merge_stage(x2, y_sb, y_nsa, proj, wsb, wns, wo, g1, b1, wr, br):
    n = x2.shape[0]
    tm = 256
    row = lambda w: pl.BlockSpec((tm, w), lambda i: (i, 0))
    full = lambda a: pl.BlockSpec(a.shape, lambda i: (0,) * a.ndim)
    return pl.pallas_call(
        _merge_kernel,
        grid=(n // tm,),
        in_specs=[row(D_MODEL), row(4 * LANES), row(4 * LANES),
                  pl.BlockSpec((tm, D_MODEL), lambda i: (i, CB_MG // 8)),
                  pl.BlockSpec((tm, D_MODEL), lambda i: (i, CB_MG // 8 + 1)),
                  full(wsb), full(wns), full(wo), full(g1), full(b1), full(wr), full(br)],
        out_specs=[pl.BlockSpec((tm * SLAB, LANES), lambda i: (i, 0)), row(TOP_K), row(TOP_K)],
        out_shape=[jax.ShapeDtypeStruct((n * SLAB, LANES), _f32),
                   jax.ShapeDtypeStruct((n, TOP_K), jnp.int32),
                   jax.ShapeDtypeStruct((n, TOP_K), _f32)],
        compiler_params=_cparams(("arbitrary",)),
        name="merge_ln1_router",
    )(x2, y_sb, y_nsa, proj, proj, wsb, wns, wo, g1, b1, wr, br)


def _prep_w_in(w):
    main = w[:, :CB_NG * LANES]
    ng = w[:, CB_NG * LANES:CB_NG * LANES + 24]
    mg = w[:, CB_NG * LANES + 24:]
    pad = jnp.zeros((w.shape[0], LANES - 12), w.dtype)
    return jnp.concatenate([main, ng[:, :12], pad, ng[:, 12:], pad, mg], axis=1).astype(_bf16)


def _rope_tables(seq):
    half = HEAD_DIM // 2
    inv_freq = ROPE_THETA ** (-jnp.arange(half, dtype=_f32) / half)
    ang = jnp.arange(seq, dtype=_f32)[:, None] * inv_freq[None, :]
    cos = jnp.cos(ang)
    sin = jnp.sin(ang)
    cos128 = jnp.concatenate([cos, cos, cos, cos], axis=1)
    sin128 = jnp.concatenate([-sin, sin, -sin, sin], axis=1)
    return cos128, sin128


def _stencil(nc):
    n = np.arange(nc)[:, None]
    j = np.arange(LANES)[None, :]
    ratio = SEL_BLOCK // CMP_STRIDE
    ok = (n >= ratio * j - 1) & (n <= ratio * j + ratio - 1) & (n < nc - 1)
    return jnp.asarray(ok.astype(np.float32))


def _attention_half(x, w_in, cmp_pe_k, cmp_w1_k, cmp_w2_k, cmp_pe_v, cmp_w1_v, cmp_w2_v,
                    w_proj_sb, w_proj_nsa, w_out, ln1_g, ln1_b, w_router, b_router):
    batch, seq, _ = x.shape
    n = batch * seq
    x2 = x.reshape(n, D_MODEL)
    proj = _in_proj(x2, _prep_w_in(w_in))
    cos, sin_signed = _rope_tables(seq)
    nq_r, kc_r, vc_r, k_aug, vs_d, kw_d, vw_d = _rope_stage(proj, cos, sin_signed, seq)
    kc_d, vc_d = _compress_stage(kc_r, vc_r, _compress_weights(cmp_pe_k, cmp_w1_k, cmp_w2_k),
                                 _compress_weights(cmp_pe_v, cmp_w1_v, cmp_w2_v), batch, seq)
    ypart, mbias = _nsa_cw_stage(nq_r, kc_d, vc_d, kw_d, vw_d, proj, _stencil(seq // CMP_STRIDE), batch, seq)
    y_nsa = _nsa_sel_stage(nq_r, mbias, k_aug, vs_d, proj, ypart, batch, seq)
    y_sb = _sb_stage(proj, batch, seq)
    wr = jnp.pad(w_router.astype(_f32), ((0, 0), (0, LANES - N_EXPERTS)))
    br = jnp.pad(b_router.astype(_f32), (0, LANES - N_EXPERTS)).reshape(1, LANES)
    return _merge_stage(x2, y_sb, y_nsa, proj, w_proj_sb.astype(_bf16), w_proj_nsa.astype(_bf16),
                        w_out.astype(_bf16), ln1_g.reshape(1, -1), ln1_b.reshape(1, -1), wr, br)


def _row_copy(src, src_row, dst, dst_row, sem):
    return pltpu.make_async_copy(src.at[pl.ds(src_row * SLAB, SLAB)], dst.at[pl.ds(dst_row * SLAB, SLAB)], sem)


def _dispatch_kernel(dest_ref, h_ref, zeros_ref, buf_ref, sem, *, tokens):
    del zeros_ref
    base = pl.program_id(0) * tokens
    count = tokens * TOP_K

    def issue(j, c):
        _row_copy(h_ref, base + j // TOP_K, buf_ref, dest_ref[j], sem).start()
        return c

    def drain(j, c):
        _row_copy(h_ref, 0, buf_ref, 0, sem).wait()
        return c

    lax.fori_loop(0, count, issue, 0)
    lax.fori_loop(0, count, drain, 0)


def _dispatch_stage(dest, h_slab, buf_rows):
    n = h_slab.shape[0] // SLAB
    tokens = 256
    zeros = jnp.zeros((buf_rows * SLAB, LANES), _f32)
    return pl.pallas_call(
        functools.partial(_dispatch_kernel, tokens=tokens),
        grid=(n // tokens,),
        in_specs=[pl.BlockSpec((tokens * TOP_K,), lambda i: (i,), memory_space=pltpu.SMEM),
                  pl.BlockSpec(memory_space=pl.ANY),
                  pl.BlockSpec(memory_space=pl.ANY)],
        out_specs=pl.BlockSpec(memory_space=pl.ANY),
        out_shape=jax.ShapeDtypeStruct(zeros.shape, _f32),
        scratch_shapes=[pltpu.SemaphoreType.DMA(())],
        input_output_aliases={2: 0},
        compiler_params=pltpu.CompilerParams(dimension_semantics=("arbitrary",), has_side_effects=True),
        name="moe_dispatch",
    )(dest, h_slab, zeros)


def _slab_load(ref, rows):
    return jnp.concatenate([ref[pl.ds(s, rows, stride=SLAB), :] for s in range(SLAB)], axis=1)


def _expert_kernel(ce_ref, used_ref, x_ref, wgu_ref, bgu_ref, wd_ref, bd_ref, o_ref):
    del ce_ref
    c = pl.program_id(0)

    @pl.when(c < used_ref[0])
    def _():
        x = _slab_load(x_ref, MOE_ROWS).astype(_bf16)
        gu = _dot(x, wgu_ref[0]) + bgu_ref[0]
        gate = jnp.minimum(gu[:, :D_MODEL], SWIGLU_LIMIT)
        up = jnp.clip(gu[:, D_MODEL:], -SWIGLU_LIMIT, SWIGLU_LIMIT)
        h = gate * jax.nn.sigmoid(SWIGLU_ALPHA * gate) * (up + 1.0)
        y = _dot(h.astype(_bf16), wd_ref[0]) + bd_ref[0]
        for s in range(SLAB):
            o_ref[pl.ds(s, MOE_ROWS, stride=SLAB), :] = y[:, s * LANES:(s + 1) * LANES]

    @pl.when(c >= used_ref[0])
    def _():
        o_ref[...] = jnp.zeros(o_ref.shape, o_ref.dtype)


def _expert_stage(chunk_expert, n_used, buf, wgu, bgu, wd, bd):
    n_chunks = chunk_expert.shape[0]
    rows = MOE_ROWS * SLAB
    grid_spec = pltpu.PrefetchScalarGridSpec(
        num_scalar_prefetch=2,
        grid=(n_chunks,),
        in_specs=[pl.BlockSpec((rows, LANES), lambda c, ce, nu: (c, 0)),
                  pl.BlockSpec((1, D_MODEL, 2 * D_MODEL), lambda c, ce, nu: (ce[c], 0, 0)),
                  pl.BlockSpec((1, 1, 2 * D_MODEL), lambda c, ce, nu: (ce[c], 0, 0)),
                  pl.BlockSpec((1, D_MODEL, D_MODEL), lambda c, ce, nu: (ce[c], 0, 0)),
                  pl.BlockSpec((1, 1, D_MODEL), lambda c, ce, nu: (ce[c], 0, 0))],
        out_specs=pl.BlockSpec((rows, LANES), lambda c, ce, nu: (c, 0)))
    return pl.pallas_call(
        _expert_kernel,
        grid_spec=grid_spec,
        out_shape=jax.ShapeDtypeStruct(buf.shape, _f32),
        compiler_params=_cparams(("arbitrary",)),
        name="moe_experts",
    )(chunk_expert, n_used, buf, wgu, bgu, wd, bd)


def _combine_kernel(dest_ref, gw_ref, h_ref, eo_ref, g_ref, b_ref, o_ref, gbuf, ysl, sem, *, tokens):
    count = tokens * TOP_K

    def issue(j, c):
        _row_copy(eo_ref, dest_ref[j], gbuf, j, sem).start()
        return c

    def drain(j, c):
        _row_copy(eo_ref, 0, gbuf, 0, sem).wait()
        return c

    lax.fori_loop(0, count, issue, 0)
    lax.fori_loop(0, count, drain, 0)

    def token(t, c):
        acc = DEEPNORM_ALPHA * h_ref[pl.ds(pl.multiple_of(t * SLAB, SLAB), SLAB), :]
        for k in range(TOP_K):
            j = t * TOP_K + k
            acc = acc + gw_ref[j] * gbuf[pl.ds(pl.multiple_of(j * SLAB, SLAB), SLAB), :]
        ysl[pl.ds(pl.multiple_of(t * SLAB, SLAB), SLAB), :] = acc
        return c

    lax.fori_loop(0, tokens, token, 0)
    o_ref[...] = _layer_norm(_slab_load(ysl, tokens), g_ref[...], b_ref[...])


def _combine_stage(dest, gate_w, h_slab, expert_out, g2, b2):
    n = h_slab.shape[0] // SLAB
    tokens = 256
    smem = lambda: pl.BlockSpec((tokens * TOP_K,), lambda i: (i,), memory_space=pltpu.SMEM)
    return pl.pallas_call(
        functools.partial(_combine_kernel, tokens=tokens),
        grid=(n // tokens,),
        in_specs=[smem(), smem(),
                  pl.BlockSpec((tokens * SLAB, LANES), lambda i: (i, 0)),
                  pl.BlockSpec(memory_space=pl.ANY),
                  pl.BlockSpec((1, D_MODEL), lambda i: (0, 0)),
                  pl.BlockSpec((1, D_MODEL), lambda i: (0, 0))],
        out_specs=pl.BlockSpec((tokens, D_MODEL), lambda i: (i, 0)),
        out_shape=jax.ShapeDtypeStruct((n, D_MODEL), _f32),
        scratch_shapes=[pltpu.VMEM((tokens * TOP_K * SLAB, LANES), _f32),
                        pltpu.VMEM((tokens * SLAB, LANES), _f32),
                        pltpu.SemaphoreType.DMA(())],
        compiler_params=_cparams(("arbitrary",)),
        name="moe_combine_ln2",
    )(dest, gate_w, h_slab, expert_out, g2, b2)


def _dispatch_plan(top_idx):
    m = top_idx.size
    e_flat = top_idx.reshape(m)
    onehot = (e_flat[:, None] == jnp.arange(N_EXPERTS, dtype=jnp.int32)[None, :]).astype(jnp.int32)
    csum = jnp.cumsum(onehot, axis=0)
    counts = csum[-1]
    padded = (counts + MOE_ROWS - 1) // MOE_ROWS * MOE_ROWS
    pends = jnp.cumsum(padded)
    pstarts = pends - padded
    dest = jnp.sum(onehot * (csum - 1 + pstarts[None, :]), axis=1).astype(jnp.int32)
    n_chunks = m // MOE_ROWS + N_EXPERTS
    chunk_start = jnp.arange(n_chunks, dtype=jnp.int32) * MOE_ROWS
    chunk_expert = jnp.clip(jnp.searchsorted(pends, chunk_start, side="right"), 0, N_EXPERTS - 1).astype(jnp.int32)
    n_used = (pends[-1] // MOE_ROWS).astype(jnp.int32).reshape(1)
    return dest, chunk_expert, n_used, n_chunks * MOE_ROWS


def _moe_half(h_slab, top_idx, gate_w, w_gate_up, b_gate_up, w_down, b_down, ln2_g, ln2_b):
    dest, chunk_expert, n_used, buf_rows = _dispatch_plan(top_idx)
    buf = _dispatch_stage(dest, h_slab, buf_rows)
    expert_out = _expert_stage(chunk_expert, n_used, buf, w_gate_up.astype(_bf16),
                               b_gate_up.reshape(N_EXPERTS, 1, -1), w_down.astype(_bf16),
                               b_down.reshape(N_EXPERTS, 1, -1))
    return _combine_stage(dest, gate_w.reshape(-1), h_slab, expert_out, ln2_g.reshape(1, -1), ln2_b.reshape(1, -1))


def kernel(x, w_in, cmp_pe_k, cmp_w1_k, cmp_w2_k, cmp_pe_v, cmp_w1_v, cmp_w2_v, w_proj_sb, w_proj_nsa, w_out,
           ln1_g, ln1_b, w_router, b_router, w_gate_up, b_gate_up, w_down, b_down, ln2_g, ln2_b):
    assert w_in.shape[0] == 1, "single-layer block"
    batch, seq, _ = x.shape
    assert seq % 512 == 0 and seq // SEL_BLOCK <= LANES and seq >= WINDOW + Q_BLOCK
    h_slab, top_idx, gate_w = _attention_half(
        x, w_in[0], cmp_pe_k[0], cmp_w1_k[0], cmp_w2_k[0], cmp_pe_v[0], cmp_w1_v[0], cmp_w2_v[0],
        w_proj_sb[0], w_proj_nsa[0], w_out[0], ln1_g[0], ln1_b[0], w_router[0], b_router[0])
    out = _moe_half(h_slab, top_idx, gate_w, w_gate_up[0], b_gate_up[0], w_down[0], b_down[0], ln2_g[0], ln2_b[0])
    return out.reshape(batch, seq, D_MODEL)
```

```python
import functools

import numpy as np
import jax
import jax.numpy as jnp
from jax import lax
from jax.experimental import pallas as pl
from jax.experimental.pallas import tpu as pltpu

D_MODEL = 1024
HEAD_DIM = 64
LANES = 128
Q_BLOCK = 128
CMP_BLOCK = 32
CMP_STRIDE = 16
SEL_BLOCK = 64
SEL_TOPK = 16
WINDOW = 512
ROPE_THETA = 10000.0
N_EXPERTS = 32
TOP_K = 4
SWIGLU_LIMIT = 7.0
SWIGLU_ALPHA = 1.702
LN_EPS = 1e-5
NEG_INF = -1e30
FORCE_SCORE = 1e30
TAKEN = -3e38
DEEPNORM_ALPHA = 2.0 ** 0.25
QK_SCALE = HEAD_DIM ** -0.5

CB_SBQ, CB_SBK, CB_SBV, CB_NQ = 0, 4, 8, 12
CB_KC, CB_VC, CB_KS, CB_VS, CB_KW, CB_VW = 16, 17, 18, 19, 20, 21
CB_NG = 22
CB_MG = 24
PROJ_W = 40 * LANES

SB_TAIL_CUTOFF = -110.0

MOE_ROWS = 256
SLAB = D_MODEL // LANES
VMEM_LIMIT = 56 * 1024 * 1024

_bf16 = jnp.bfloat16
_f32 = jnp.float32


def _cparams(sem):
    return pltpu.CompilerParams(dimension_semantics=sem, vmem_limit_bytes=VMEM_LIMIT)


def _dot_t(a, b):
    return lax.dot_general(a, b, (((1,), (1,)), ((), ())), preferred_element_type=_f32)


def _dot(a, b):
    return jnp.dot(a, b, preferred_element_type=_f32)


def _lane_iota(shape):
    return lax.broadcasted_iota(jnp.int32, shape, len(shape) - 1)


def _half0(shape=(1, LANES)):
    return _lane_iota(shape) < HEAD_DIM


def _in_proj_kernel(x_ref, w_ref, o_ref):
    o_ref[...] = _dot(x_ref[...].astype(_bf16), w_ref[...]).astype(o_ref.dtype)


def _in_proj(x2, w):
    n = x2.shape[0]
    tm, tn = 512, 1280
    return pl.pallas_call(
        _in_proj_kernel,
        grid=(PROJ_W // tn, n // tm),
        in_specs=[pl.BlockSpec((tm, D_MODEL), lambda j, i: (i, 0)),
                  pl.BlockSpec((D_MODEL, tn), lambda j, i: (0, j))],
        out_specs=pl.BlockSpec((tm, tn), lambda j, i: (i, j)),
        out_shape=jax.ShapeDtypeStruct((n, PROJ_W), _bf16),
        compiler_params=_cparams(("arbitrary", "arbitrary")),
        name="in_proj",
    )(x2, w)


def _rope(x, cos, sin_signed):
    first = (_lane_iota((1, LANES)) % HEAD_DIM) < (HEAD_DIM // 2)
    swapped = jnp.where(first, pltpu.roll(x, LANES - HEAD_DIM // 2, 1), pltpu.roll(x, HEAD_DIM // 2, 1))
    return x * cos + swapped * sin_signed


def _dup(x, g):
    other = pltpu.roll(x, HEAD_DIM, 1)
    h0 = _half0()
    return jnp.where(h0, x, other) if g == 0 else jnp.where(h0, other, x)


def _rope_kernel(nq_ref, kc_ref, vc_ref, ks_ref, vs_ref, kw_ref, vw_ref, cos_ref, sin_ref,
                 nq_o, kc_o, vc_o, ka_o, vs_o, kw_o, vw_o, *, blocks_per_seq):
    ts = cos_ref.shape[0]
    cos = cos_ref[...]
    sin = sin_ref[...]
    for c in range(4):
        sl = slice(c * LANES, (c + 1) * LANES)
        nq_o[:, sl] = (_rope(nq_ref[:, sl].astype(_f32), cos, sin) * QK_SCALE).astype(_bf16)
    kc_o[...] = _rope(kc_ref[...].astype(_f32), cos, sin).astype(_bf16)
    vc_o[...] = vc_ref[...]
    ks = _rope(ks_ref[...].astype(_f32), cos, sin)
    kw = _rope(kw_ref[...].astype(_f32), cos, sin)
    vs = vs_ref[...].astype(_f32)
    vw = vw_ref[...].astype(_f32)
    pos = (pl.program_id(0) % blocks_per_seq) * ts + lax.broadcasted_iota(jnp.int32, (ts, LANES), 0)
    onehot = jnp.where(pos // SEL_BLOCK == _lane_iota((ts, LANES)), 1.0, 0.0).astype(_bf16)
    for g in range(2):
        ka_o[g, :, 0:LANES] = _dup(ks, g).astype(_bf16)
        ka_o[g, :, LANES:2 * LANES] = onehot
        vs_o[g] = _dup(vs, g).astype(_bf16)
        kw_o[g] = _dup(kw, g).astype(_bf16)
        vw_o[g] = _dup(vw, g).astype(_bf16)


def _rope_stage(proj, cos, sin_signed, seq):
    n = proj.shape[0]
    ts = 512
    bps = seq // ts
    col = lambda cb: pl.BlockSpec((ts, LANES), lambda i, cb=cb: (i, cb))
    tab = pl.BlockSpec((ts, LANES), lambda i: (i % bps, 0))
    grp = lambda w: pl.BlockSpec((2, ts, w), lambda i: (0, i, 0))
    return pl.pallas_call(
        functools.partial(_rope_kernel, blocks_per_seq=bps),
        grid=(n // ts,),
        in_specs=[pl.BlockSpec((ts, 4 * LANES), lambda i: (i, CB_NQ // 4)),
                  col(CB_KC), col(CB_VC), col(CB_KS), col(CB_VS), col(CB_KW), col(CB_VW), tab, tab],
        out_specs=[pl.BlockSpec((ts, 4 * LANES), lambda i: (i, 0)),
                   pl.BlockSpec((ts, LANES), lambda i: (i, 0)),
                   pl.BlockSpec((ts, LANES), lambda i: (i, 0)),
                   grp(2 * LANES), grp(LANES), grp(LANES), grp(LANES)],
        out_shape=[jax.ShapeDtypeStruct((n, 4 * LANES), _bf16),
                   jax.ShapeDtypeStruct((n, LANES), _bf16),
                   jax.ShapeDtypeStruct((n, LANES), _bf16),
                   jax.ShapeDtypeStruct((2, n, 2 * LANES), _bf16),
                   jax.ShapeDtypeStruct((2, n, LANES), _bf16),
                   jax.ShapeDtypeStruct((2, n, LANES), _bf16),
                   jax.ShapeDtypeStruct((2, n, LANES), _bf16)],
        compiler_params=_cparams(("arbitrary",)),
        name="rope_layout",
    )(proj, proj, proj, proj, proj, proj, proj, cos, sin_signed)


def _gelu_tanh(x):
    return 0.5 * x * (1.0 + jnp.tanh(0.7978845608028654 * (x + 0.044715 * (x * x * x))))


def _compress_one(x_ref, pe_t, pe_b, w_t, w_b, w2, out_ref):
    x = x_ref[0].astype(_f32)
    a = _dot((x + pe_t[...]).astype(_bf16), w_t[...])
    b = _dot((x + pe_b[...]).astype(_bf16), w_b[...])
    nc = a.shape[0]
    pre = a + pltpu.roll(b, nc - 1, 0)
    y = _dot(_gelu_tanh(pre).astype(_bf16), w2[...])
    for g in range(2):
        out_ref[0, g] = _dup(y, g).astype(_bf16)


def _compress_kernel(k_ref, v_ref, kpt, kpb, kwt, kwb, kw2, vpt, vpb, vwt, vwb, vw2, ko_ref, vo_ref):
    _compress_one(k_ref, kpt, kpb, kwt, kwb, kw2, ko_ref)
    _compress_one(v_ref, vpt, vpb, vwt, vwb, vw2, vo_ref)


def _compress_weights(pe, w1, w2):
    half = CMP_BLOCK // 2
    eye = jnp.eye(2, dtype=_f32)
    outs = []
    for part in range(2):
        w = w1[part * half * HEAD_DIM:(part + 1) * half * HEAD_DIM].reshape(half, HEAD_DIM, HEAD_DIM)
        wbd = (w[:, None, :, None, :] * eye[None, :, None, :, None]).reshape(half * 2 * HEAD_DIM, 2 * HEAD_DIM)
        p = jnp.broadcast_to(pe[part * half:(part + 1) * half, None, :], (half, 2, HEAD_DIM)).reshape(1, -1)
        outs.append((p.astype(_f32), wbd.astype(_bf16)))
    w2bd = (w2[None, :, None, :] * eye[:, None, :, None]).reshape(2 * HEAD_DIM, 2 * HEAD_DIM).astype(_bf16)
    (pt, wt), (pb, wb) = outs
    return pt, pb, wt, wb, w2bd


def _compress_stage(kc_r, vc_r, kparams, vparams, batch, seq):
    nc = seq // CMP_STRIDE
    width = CMP_STRIDE * LANES
    xs = pl.BlockSpec((1, nc, width), lambda b: (b, 0, 0))
    full = lambda a: pl.BlockSpec(a.shape, lambda b: (0,) * a.ndim)
    out = pl.BlockSpec((1, 2, nc, LANES), lambda b: (b, 0, 0, 0))
    weights = list(kparams) + list(vparams)
    return pl.pallas_call(
        _compress_kernel,
        grid=(batch,),
        in_specs=[xs, xs] + [full(a) for a in weights],
        out_specs=[out, out],
        out_shape=[jax.ShapeDtypeStruct((batch, 2, nc, LANES), _bf16)] * 2,
        compiler_params=_cparams(("arbitrary",)),
        name="compress",
    )(kc_r.reshape(batch, nc, width), vc_r.reshape(batch, nc, width), *weights)


def _head_q(q_ref, r):
    q2 = q_ref[:, (r // 2) * LANES:(r // 2 + 1) * LANES]
    keep = _half0() if r % 2 == 0 else jnp.logical_not(_half0())
    return jnp.where(keep, q2, jnp.zeros_like(q2))


def _masked_softmax(s, mask):
    s = jnp.where(mask, s, NEG_INF)
    e = jnp.where(mask, jnp.exp(s - jnp.max(s, axis=-1, keepdims=True)), 0.0)
    l = jnp.sum(e, axis=-1, keepdims=True)
    return e * (1.0 / jnp.where(l > 0.0, l, 1.0))


def _pair(even, odd):
    return jnp.where(_half0(), even, odd)


def _nsa_cw_kernel(q_ref, kc_ref, vc_ref, kw_ref, vw_ref, ng_ref, st_ref, yp_ref, mb_ref):
    t0 = pl.program_id(2) * Q_BLOCK
    rows = t0 + lax.broadcasted_iota(jnp.int32, (Q_BLOCK, 1), 0)
    gates = jax.nn.sigmoid(ng_ref[...].astype(_f32))
    kc = kc_ref[0, 0]
    vc = vc_ref[0, 0]
    nc = kc.shape[0]
    cmask = (_lane_iota((1, nc)) * CMP_STRIDE + (CMP_BLOCK - 1)) <= rows
    start = pl.multiple_of(jnp.maximum(t0 - WINDOW, 0), Q_BLOCK)
    wlen = WINDOW + Q_BLOCK
    kwin = kw_ref[0, 0, pl.ds(start, wlen), :]
    vwin = vw_ref[0, 0, pl.ds(start, wlen), :]
    kpos = start + _lane_iota((1, wlen))
    wmask = (kpos <= rows) & (rows - kpos < WINDOW)

    imp = jnp.zeros((Q_BLOCK, nc), _f32)
    ys = []
    for r in range(4):
        q = _head_q(q_ref, r)
        p = _masked_softmax(_dot_t(q, kc), cmask)
        imp = imp + p
        o_cmp = _dot(p.astype(_bf16), vc)
        pw = _masked_softmax(_dot_t(q, kwin), wmask)
        o_win = _dot(pw.astype(_bf16), vwin)
        ys.append(gates[:, 3 * r:3 * r + 1] * o_cmp + gates[:, 3 * r + 2:3 * r + 3] * o_win)
    yp_ref[:, 0:LANES] = _pair(ys[0], ys[1])
    yp_ref[:, LANES:2 * LANES] = _pair(ys[2], ys[3])

    p_slc = jnp.dot(imp, st_ref[...], preferred_element_type=_f32, precision=lax.Precision.HIGHEST)
    selj = _lane_iota((1, LANES))
    blk_t = rows // SEL_BLOCK
    forced = (selj == 0) | (selj == blk_t) | (selj == blk_t - 1)
    score = jnp.where(forced, FORCE_SCORE, jnp.where(selj <= blk_t, p_slc, NEG_INF))
    lanef = selj.astype(_f32)
    picked = jnp.zeros((Q_BLOCK, LANES), jnp.bool_)
    for _ in range(SEL_TOPK):
        m = jnp.max(score, axis=-1, keepdims=True)
        first = jnp.min(jnp.where(score == m, lanef, float(LANES)), axis=-1, keepdims=True)
        hit = lanef == first
        picked = picked | hit
        score = jnp.where(hit, TAKEN, score)
    mb_ref[0] = jnp.where(picked, 0.0, NEG_INF).astype(_bf16)


def _nsa_cw_stage(nq_r, kc_d, vc_d, kw_d, vw_d, proj, stencil, batch, seq):
    n = nq_r.shape[0]
    nblk = seq // Q_BLOCK
    nc = seq // CMP_STRIDE
    qrow = lambda b, g, i: b * nblk + i
    return pl.pallas_call(
        _nsa_cw_kernel,
        grid=(batch, 2, nblk),
        in_specs=[pl.BlockSpec((Q_BLOCK, 2 * LANES), lambda b, g, i: (qrow(b, g, i), g)),
                  pl.BlockSpec((1, 1, nc, LANES), lambda b, g, i: (b, g, 0, 0)),
                  pl.BlockSpec((1, 1, nc, LANES), lambda b, g, i: (b, g, 0, 0)),
                  pl.BlockSpec((1, 1, seq, LANES), lambda b, g, i: (g, b, 0, 0)),
                  pl.BlockSpec((1, 1, seq, LANES), lambda b, g, i: (g, b, 0, 0)),
                  pl.BlockSpec((Q_BLOCK, LANES), lambda b, g, i: (qrow(b, g, i), CB_NG + g)),
                  pl.BlockSpec((nc, LANES), lambda b, g, i: (0, 0))],
        out_specs=[pl.BlockSpec((Q_BLOCK, 2 * LANES), lambda b, g, i: (qrow(b, g, i), g)),
                   pl.BlockSpec((1, Q_BLOCK, LANES), lambda b, g, i: (g, qrow(b, g, i), 0))],
        out_shape=[jax.ShapeDtypeStruct((n, 4 * LANES), _f32),
                   jax.ShapeDtypeStruct((2, n, LANES), _bf16)],
        compiler_params=_cparams(("arbitrary", "arbitrary", "arbitrary")),
        name="nsa_cmp_win_select",
    )(nq_r, kc_d, vc_d, kw_d.reshape(2, batch, seq, LANES), vw_d.reshape(2, batch, seq, LANES), proj, stencil)


def _nsa_sel_kernel(q_ref, mb_ref, ka_ref, vs_ref, ng_ref, yp_ref, o_ref, qs_ref, m_ref, l_ref, acc_ref, *, tk):
    t0 = pl.program_id(2) * Q_BLOCK
    mb = mb_ref[0]
    for r in range(4):
        qs_ref[r * Q_BLOCK:(r + 1) * Q_BLOCK, 0:LANES] = _head_q(q_ref, r)
        qs_ref[r * Q_BLOCK:(r + 1) * Q_BLOCK, LANES:2 * LANES] = mb
    m_ref[...] = jnp.full(m_ref.shape, NEG_INF, _f32)
    l_ref[...] = jnp.zeros(l_ref.shape, _f32)
    acc_ref[...] = jnp.zeros(acc_ref.shape, _f32)
    rows = t0 + lax.broadcasted_iota(jnp.int32, (4 * Q_BLOCK, 1), 0) % Q_BLOCK

    def body(kt, carry):
        k0 = pl.multiple_of(kt * tk, tk)
        s = _dot_t(qs_ref[...], ka_ref[0, 0, pl.ds(k0, tk), :])
        s = jnp.where(k0 + _lane_iota((1, tk)) <= rows, s, NEG_INF)
        m_old = m_ref[...]
        m_new = jnp.maximum(m_old, jnp.max(s, axis=-1, keepdims=True))
        alpha = jnp.exp(m_old - m_new)
        p = jnp.exp(s - m_new)
        l_ref[...] = alpha * l_ref[...] + jnp.sum(p, axis=-1, keepdims=True)
        acc_ref[...] = alpha * acc_ref[...] + _dot(p.astype(_bf16), vs_ref[0, 0, pl.ds(k0, tk), :])
        m_ref[...] = m_new
        return carry

    lax.fori_loop(0, (t0 + Q_BLOCK - 1) // tk + 1, body, 0)
    o = acc_ref[...] * (1.0 / l_ref[...])
    gates = jax.nn.sigmoid(ng_ref[...].astype(_f32))
    ys = [gates[:, 3 * r + 1:3 * r + 2] * o[r * Q_BLOCK:(r + 1) * Q_BLOCK] for r in range(4)]
    o_ref[:, 0:LANES] = (yp_ref[:, 0:LANES] + _pair(ys[0], ys[1])).astype(o_ref.dtype)
    o_ref[:, LANES:2 * LANES] = (yp_ref[:, LANES:2 * LANES] + _pair(ys[2], ys[3])).astype(o_ref.dtype)


def _nsa_sel_stage(nq_r, mbias, k_aug, vs_d, proj, ypart, batch, seq):
    n = nq_r.shape[0]
    nblk = seq // Q_BLOCK
    tk = min(512, seq)
    qrow = lambda b, g, i: b * nblk + i
    return pl.pallas_call(
        functools.partial(_nsa_sel_kernel, tk=tk),
        grid=(batch, 2, nblk),
        in_specs=[pl.BlockSpec((Q_BLOCK, 2 * LANES), lambda b, g, i: (qrow(b, g, i), g)),
                  pl.BlockSpec((1, Q_BLOCK, LANES), lambda b, g, i: (g, qrow(b, g, i), 0)),
                  pl.BlockSpec((1, 1, seq, 2 * LANES), lambda b, g, i: (g, b, 0, 0)),
                  pl.BlockSpec((1, 1, seq, LANES), lambda b, g, i: (g, b, 0, 0)),
                  pl.BlockSpec((Q_BLOCK, LANES), lambda b, g, i: (qrow(b, g, i), CB_NG + g)),
                  pl.BlockSpec((Q_BLOCK, 2 * LANES), lambda b, g, i: (qrow(b, g, i), g))],
        out_specs=pl.BlockSpec((Q_BLOCK, 2 * LANES), lambda b, g, i: (qrow(b, g, i), g)),
        out_shape=jax.ShapeDtypeStruct((n, 4 * LANES), _bf16),
        scratch_shapes=[pltpu.VMEM((4 * Q_BLOCK, 2 * LANES), _bf16),
                        pltpu.VMEM((4 * Q_BLOCK, 1), _f32),
                        pltpu.VMEM((4 * Q_BLOCK, 1), _f32),
                        pltpu.VMEM((4 * Q_BLOCK, LANES), _f32)],
        compiler_params=_cparams(("arbitrary", "arbitrary", "arbitrary")),
        name="nsa_selected",
    )(nq_r, mbias, k_aug.reshape(2, batch, seq, 2 * LANES), vs_d.reshape(2, batch, seq, LANES), proj, ypart)


def _sb_kernel(q_ref, k_ref, v_ref, o_ref):
    i = pl.program_id(2)
    h0 = _half0()
    q = q_ref[...]
    zero = jnp.zeros_like(q)
    qh = [jnp.where(h0, q, zero) * QK_SCALE, jnp.where(h0, zero, q) * QK_SCALE]
    rloc = lax.broadcasted_iota(jnp.int32, (Q_BLOCK, Q_BLOCK), 0)
    cloc = lax.broadcasted_iota(jnp.int32, (Q_BLOCK, Q_BLOCK), 1)
    later = jnp.where(rloc > cloc, 1.0, 0.0).astype(_bf16)

    def cond(c):
        j, t0_, t1_, _, _ = c
        return (j >= 0) & (jnp.maximum(jnp.max(t0_), jnp.max(t1_)) > SB_TAIL_CUTOFF)

    def body(c):
        j, t0_, t1_, a0, a1 = c
        k0 = pl.multiple_of(j * Q_BLOCK, Q_BLOCK)
        kj = k_ref[0, pl.ds(k0, Q_BLOCK), :]
        vj = v_ref[0, pl.ds(k0, Q_BLOCK), :]
        past = (cloc + (j - i) * Q_BLOCK) < rloc
        tails, accs = [], []
        for qx, tail, acc in ((qh[0], t0_, a0), (qh[1], t1_, a1)):
            z = _dot_t(qx, kj)
            sp = jnp.maximum(z, 0.0) + jnp.log1p(jnp.exp(-jnp.abs(z)))
            log_keep = jnp.where(past, -sp, 0.0)
            hi = log_keep.astype(_bf16)
            lo = (log_keep - hi.astype(_f32)).astype(_bf16)
            inner = _dot(hi, later) + _dot(lo, later)
            a = jnp.where(past, jnp.exp((z - sp) + inner + tail), 0.0)
            accs.append(acc + _dot(a.astype(_bf16), vj))
            tails.append(tail + jnp.sum(log_keep, axis=-1, keepdims=True))
        return j - 1, tails[0], tails[1], accs[0], accs[1]

    zt = jnp.zeros((Q_BLOCK, 1), _f32)
    za = jnp.zeros((Q_BLOCK, LANES), _f32)
    _, _, _, a0, a1 = lax.while_loop(cond, body, (i, zt, zt, za, za))
    o_ref[...] = jnp.where(h0, a0, a1).astype(o_ref.dtype)


def _sb_stage(proj, batch, seq):
    n = proj.shape[0]
    nblk = seq // Q_BLOCK
    proj3 = proj.reshape(batch, seq, PROJ_W)
    return pl.pallas_call(
        _sb_kernel,
        grid=(batch, 4, nblk),
        in_specs=[pl.BlockSpec((Q_BLOCK, LANES), lambda b, p, i: (b * nblk + i, CB_SBQ + p)),
                  pl.BlockSpec((1, seq, LANES), lambda b, p, i: (b, 0, CB_SBK + p)),
                  pl.BlockSpec((1, seq, LANES), lambda b, p, i: (b, 0, CB_SBV + p))],
        out_specs=pl.BlockSpec((Q_BLOCK, LANES), lambda b, p, i: (b * nblk + i, p)),
        out_shape=jax.ShapeDtypeStruct((n, 4 * LANES), _bf16),
        compiler_params=_cparams(("arbitrary", "arbitrary", "arbitrary")),
        name="stick_breaking",
    )(proj, proj3, proj3)


def _layer_norm(x, g, b):
    mu = jnp.mean(x, axis=-1, keepdims=True)
    xc = x - mu
    var = jnp.mean(xc * xc, axis=-1, keepdims=True)
    return xc * lax.rsqrt(var + LN_EPS) * g + b


def _merge_kernel(x_ref, ysb_ref, yns_ref, mg0_ref, mg1_ref, wsb_ref, wns_ref, wo_ref, g_ref, b_ref,
                  wr_ref, br_ref, h_ref, idx_ref, gw_ref):
    m0 = jax.nn.sigmoid(mg0_ref[...].astype(_f32))
    m1 = jax.nn.sigmoid(mg1_ref[...].astype(_f32))
    merged = m0 * _dot(ysb_ref[...], wsb_ref[...]) + m1 * _dot(yns_ref[...], wns_ref[...])
    pre = DEEPNORM_ALPHA * x_ref[...] + _dot(merged.astype(_bf16), wo_ref[...])
    h = _layer_norm(pre, g_ref[...], b_ref[...])
    tm = h.shape[0]
    for s in range(SLAB):
        h_ref[pl.ds(s, tm, stride=SLAB), :] = h[:, s * LANES:(s + 1) * LANES]
    logits = jnp.dot(h, wr_ref[...], preferred_element_type=_f32, precision=lax.Precision.HIGHEST) + br_ref[...]
    lane = _lane_iota((1, LANES))
    lanef = lane.astype(_f32)
    lg = jnp.where(lane < N_EXPERTS, logits, TAKEN)
    vals, idxs = [], []
    for _ in range(TOP_K):
        m = jnp.max(lg, axis=-1, keepdims=True)
        first = jnp.min(jnp.where(lg == m, lanef, float(LANES)), axis=-1, keepdims=True)
        vals.append(m)
        idxs.append(first)
        lg = jnp.where(lanef == first, TAKEN, lg)
    es = [jnp.exp(v - vals[0]) for v in vals]
    inv = 1.0 / (es[0] + es[1] + es[2] + es[3])
    idx_t = jnp.zeros(lg.shape, _f32)
    gw_t = jnp.zeros(lg.shape, _f32)
    for k in range(TOP_K):
        idx_t = jnp.where(lane == k, idxs[k], idx_t)
        gw_t = jnp.where(lane == k, es[k] * inv, gw_t)
    idx_ref[...] = idx_t[:, :TOP_K].astype(jnp.int32)
    gw_ref[...] = gw_t[:, :TOP_K]


def _merge_stage(x2, y_sb, y_nsa, proj, wsb, wns, wo, g1, b1, wr, br):
    n = x2.shape[0]
    tm = 256
    row = lambda w: pl.BlockSpec((tm, w), lambda i: (i, 0))
    full = lambda a: pl.BlockSpec(a.shape, lambda i: (0,) * a.ndim)
    return pl.pallas_call(
        _merge_kernel,
        grid=(n // tm,),
        in_specs=[row(D_MODEL), row(4 * LANES), row(4 * LANES),
                  pl.BlockSpec((tm, D_MODEL), lambda i: (i, CB_MG // 8)),
                  pl.BlockSpec((tm, D_MODEL), lambda i: (i, CB_MG // 8 + 1)),
                  full(wsb), full(wns), full(wo), full(g1), full(b1), full(wr), full(br)],
        out_specs=[pl.BlockSpec((tm * SLAB, LANES), lambda i: (i, 0)), row(TOP_K), row(TOP_K)],
        out_shape=[jax.ShapeDtypeStruct((n * SLAB, LANES), _f32),
                   jax.ShapeDtypeStruct((n, TOP_K), jnp.int32),
                   jax.ShapeDtypeStruct((n, TOP_K), _f32)],
        compiler_params=_cparams(("arbitrary",)),
        name="merge_ln1_router",
    )(x2, y_sb, y_nsa, proj, proj, wsb, wns, wo, g1, b1, wr, br)


def _prep_w_in(w):
    main = w[:, :CB_NG * LANES]
    ng = w[:, CB_NG * LANES:CB_NG * LANES + 24]
    mg = w[:, CB_NG * LANES + 24:]
    pad = jnp.zeros((w.shape[0], LANES - 12), w.dtype)
    return jnp.concatenate([main, ng[:, :12], pad, ng[:, 12:], pad, mg], axis=1).astype(_bf16)


def _rope_tables(seq):
    half = HEAD_DIM // 2
    inv_freq = ROPE_THETA ** (-jnp.arange(half, dtype=_f32) / half)
    ang = jnp.arange(seq, dtype=_f32)[:, None] * inv_freq[None, :]
    cos = jnp.cos(ang)
    sin = jnp.sin(ang)
    cos128 = jnp.concatenate([cos, cos, cos, cos], axis=1)
    sin128 = jnp.concatenate([-sin, sin, -sin, sin], axis=1)
    return cos128, sin128


def _stencil(nc):
    n = np.arange(nc)[:, None]
    j = np.arange(LANES)[None, :]
    ratio = SEL_BLOCK // CMP_STRIDE
    ok = (n >= ratio * j - 1) & (n <= ratio * j + ratio - 1) & (n < nc - 1)
    return jnp.asarray(ok.astype(np.float32))


def _attention_half(x, w_in, cmp_pe_k, cmp_w1_k, cmp_w2_k, cmp_pe_v, cmp_w1_v, cmp_w2_v,
                    w_proj_sb, w_proj_nsa, w_out, ln1_g, ln1_b, w_router, b_router):
    batch, seq, _ = x.shape
    n = batch * seq
    x2 = x.reshape(n, D_MODEL)
    proj = _in_proj(x2, _prep_w_in(w_in))
    cos, sin_signed = _rope_tables(seq)
    nq_r, kc_r, vc_r, k_aug, vs_d, kw_d, vw_d = _rope_stage(proj, cos, sin_signed, seq)
    kc_d, vc_d = _compress_stage(kc_r, vc_r, _compress_weights(cmp_pe_k, cmp_w1_k, cmp_w2_k),
                                 _compress_weights(cmp_pe_v, cmp_w1_v, cmp_w2_v), batch, seq)
    ypart, mbias = _nsa_cw_stage(nq_r, kc_d, vc_d, kw_d, vw_d, proj, _stencil(seq // CMP_STRIDE), batch, seq)
    y_nsa = _nsa_sel_stage(nq_r, mbias, k_aug, vs_d, proj, ypart, batch, seq)
    y_sb = _sb_stage(proj, batch, seq)
    wr = jnp.pad(w_router.astype(_f32), ((0, 0), (0, LANES - N_EXPERTS)))
    br = jnp.pad(b_router.astype(_f32), (0, LANES - N_EXPERTS)).reshape(1, LANES)
    return _merge_stage(x2, y_sb, y_nsa, proj, w_proj_sb.astype(_bf16), w_proj_nsa.astype(_bf16),
                        w_out.astype(_bf16), ln1_g.reshape(1, -1), ln1_b.reshape(1, -1), wr, br)


def _row_copy(src, src_row, dst, dst_row, sem):
    return pltpu.make_async_copy(src.at[pl.ds(src_row * SLAB, SLAB)], dst.at[pl.ds(dst_row * SLAB, SLAB)], sem)


def _dispatch_kernel(dest_ref, h_ref, zeros_ref, buf_ref, sem, *, tokens):
    del zeros_ref

    def issue(t, c):
        for k in range(TOP_K):
            _row_copy(h_ref, t, buf_ref, dest_ref[t * TOP_K + k], sem).start()
        return c

    lax.fori_loop(0, tokens, issue, 0)
    for _ in range(TOP_K):
        pltpu.make_async_copy(h_ref, buf_ref.at[pl.ds(0, tokens * SLAB)], sem).wait()


def _dispatch_stage(dest, h_slab, buf_rows):
    n = h_slab.shape[0] // SLAB
    tokens = 256
    zeros = jnp.zeros((buf_rows * SLAB, LANES), _f32)
    return pl.pallas_call(
        functools.partial(_dispatch_kernel, tokens=tokens),
        grid=(n // tokens,),
        in_specs=[pl.BlockSpec((tokens * TOP_K,), lambda i: (i,), memory_space=pltpu.SMEM),
                  pl.BlockSpec((tokens * SLAB, LANES), lambda i: (i, 0)),
                  pl.BlockSpec(memory_space=pl.ANY)],
        out_specs=pl.BlockSpec(memory_space=pl.ANY),
        out_shape=jax.ShapeDtypeStruct(zeros.shape, _f32),
        scratch_shapes=[pltpu.SemaphoreType.DMA(())],
        input_output_aliases={2: 0},
        compiler_params=pltpu.CompilerParams(dimension_semantics=("arbitrary",), has_side_effects=True),
        name="moe_dispatch",
    )(dest, h_slab, zeros)


def _slab_load(ref, rows):
    return jnp.concatenate([ref[pl.ds(s, rows, stride=SLAB), :] for s in range(SLAB)], axis=1)


def _expert_kernel(ce_ref, used_ref, x_ref, wgu_ref, bgu_ref, wd_ref, bd_ref, o_ref):
    del ce_ref
    c = pl.program_id(0)

    @pl.when(c < used_ref[0])
    def _():
        x = _slab_load(x_ref, MOE_ROWS).astype(_bf16)
        gu = _dot(x, wgu_ref[0]) + bgu_ref[0]
        gate = jnp.minimum(gu[:, :D_MODEL], SWIGLU_LIMIT)
        up = jnp.clip(gu[:, D_MODEL:], -SWIGLU_LIMIT, SWIGLU_LIMIT)
        h = gate * jax.nn.sigmoid(SWIGLU_ALPHA * gate) * (up + 1.0)
        y = _dot(h.astype(_bf16), wd_ref[0]) + bd_ref[0]
        for s in range(SLAB):
            o_ref[pl.ds(s, MOE_ROWS, stride=SLAB), :] = y[:, s * LANES:(s + 1) * LANES]

    @pl.when(c >= used_ref[0])
    def _():
        o_ref[...] = jnp.zeros(o_ref.shape, o_ref.dtype)


def _expert_stage(chunk_expert, n_used, buf, wgu, bgu, wd, bd):
    n_chunks = chunk_expert.shape[0]
    rows = MOE_ROWS * SLAB
    grid_spec = pltpu.PrefetchScalarGridSpec(
        num_scalar_prefetch=2,
        grid=(n_chunks,),
        in_specs=[pl.BlockSpec((rows, LANES), lambda c, ce, nu: (c, 0)),
                  pl.BlockSpec((1, D_MODEL, 2 * D_MODEL), lambda c, ce, nu: (ce[c], 0, 0)),
                  pl.BlockSpec((1, 1, 2 * D_MODEL), lambda c, ce, nu: (ce[c], 0, 0)),
                  pl.BlockSpec((1, D_MODEL, D_MODEL), lambda c, ce, nu: (ce[c], 0, 0)),
                  pl.BlockSpec((1, 1, D_MODEL), lambda c, ce, nu: (ce[c], 0, 0))],
        out_specs=pl.BlockSpec((rows, LANES), lambda c, ce, nu: (c, 0)))
    return pl.pallas_call(
        _expert_kernel,
        grid_spec=grid_spec,
        out_shape=jax.ShapeDtypeStruct(buf.shape, _f32),
        compiler_params=_cparams(("arbitrary",)),
        name="moe_experts",
    )(chunk_expert, n_used, buf, wgu, bgu, wd, bd)


def _combine_kernel(dest_ref, gw_ref, h_ref, eo_ref, g_ref, b_ref, o_ref, gbuf, ysl, sem, *, tokens):
    count = tokens * TOP_K

    def issue(t, c):
        for k in range(TOP_K):
            j = t * TOP_K + k
            _row_copy(eo_ref, dest_ref[j], gbuf, j, sem).start()
        return c

    lax.fori_loop(0, tokens, issue, 0)
    pltpu.make_async_copy(eo_ref.at[pl.ds(0, count * SLAB)], gbuf, sem).wait()

    def token(t, c):
        acc = DEEPNORM_ALPHA * h_ref[pl.ds(pl.multiple_of(t * SLAB, SLAB), SLAB), :]
        for k in range(TOP_K):
            j = t * TOP_K + k
            acc = acc + gw_ref[j] * gbuf[pl.ds(pl.multiple_of(j * SLAB, SLAB), SLAB), :]
        ysl[pl.ds(pl.multiple_of(t * SLAB, SLAB), SLAB), :] = acc
        return c

    lax.fori_loop(0, tokens, token, 0)
    o_ref[...] = _layer_norm(_slab_load(ysl, tokens), g_ref[...], b_ref[...])


def _combine_stage(dest, gate_w, h_slab, expert_out, g2, b2):
    n = h_slab.shape[0] // SLAB
    tokens = 256
    smem = lambda: pl.BlockSpec((tokens * TOP_K,), lambda i: (i,), memory_space=pltpu.SMEM)
    return pl.pallas_call(
        functools.partial(_combine_kernel, tokens=tokens),
        grid=(n // tokens,),
        in_specs=[smem(), smem(),
                  pl.BlockSpec((tokens * SLAB, LANES), lambda i: (i, 0)),
                  pl.BlockSpec(memory_space=pl.ANY),
                  pl.BlockSpec((1, D_MODEL), lambda i: (0, 0)),
                  pl.BlockSpec((1, D_MODEL), lambda i: (0, 0))],
        out_specs=pl.BlockSpec((tokens, D_MODEL), lambda i: (i, 0)),
        out_shape=jax.ShapeDtypeStruct((n, D_MODEL), _f32),
        scratch_shapes=[pltpu.VMEM((tokens * TOP_K * SLAB, LANES), _f32),
                        pltpu.VMEM((tokens * SLAB, LANES), _f32),
                        pltpu.SemaphoreType.DMA(())],
        compiler_params=_cparams(("arbitrary",)),
        name="moe_combine_ln2",
    )(dest, gate_w, h_slab, expert_out, g2, b2)


def _dispatch_plan(top_idx):
    m = top_idx.size
    e_flat = top_idx.reshape(m)
    onehot = (e_flat[:, None] == jnp.arange(N_EXPERTS, dtype=jnp.int32)[None, :]).astype(jnp.int32)
    csum = jnp.cumsum(onehot, axis=0)
    counts = csum[-1]
    padded = (counts + MOE_ROWS - 1) // MOE_ROWS * MOE_ROWS
    pends = jnp.cumsum(padded)
    pstarts = pends - padded
    dest = jnp.sum(onehot * (csum - 1 + pstarts[None, :]), axis=1).astype(jnp.int32)
    n_chunks = m // MOE_ROWS + N_EXPERTS
    chunk_start = jnp.arange(n_chunks, dtype=jnp.int32) * MOE_ROWS
    chunk_expert = jnp.minimum(jnp.sum((chunk_start[:, None] >= pends[None, :]).astype(jnp.int32), axis=1), N_EXPERTS - 1)
    n_used = (pends[-1] // MOE_ROWS).astype(jnp.int32).reshape(1)
    return dest, chunk_expert, n_used, n_chunks * MOE_ROWS


def _moe_half(h_slab, top_idx, gate_w, w_gate_up, b_gate_up, w_down, b_down, ln2_g, ln2_b):
    dest, chunk_expert, n_used, buf_rows = _dispatch_plan(top_idx)
    buf = _dispatch_stage(dest, h_slab, buf_rows)
    expert_out = _expert_stage(chunk_expert, n_used, buf, w_gate_up.astype(_bf16),
                               b_gate_up.reshape(N_EXPERTS, 1, -1), w_down.astype(_bf16),
                               b_down.reshape(N_EXPERTS, 1, -1))
    return _combine_stage(dest, gate_w.reshape(-1), h_slab, expert_out, ln2_g.reshape(1, -1), ln2_b.reshape(1, -1))


def kernel(x, w_in, cmp_pe_k, cmp_w1_k, cmp_w2_k, cmp_pe_v, cmp_w1_v, cmp_w2_v, w_proj_sb, w_proj_nsa, w_out,
           ln1_g, ln1_b, w_router, b_router, w_gate_up, b_gate_up, w_down, b_down, ln2_g, ln2_b):
    assert w_in.shape[0] == 1, "single-layer block"
    batch, seq, _ = x.shape
    assert seq % 512 == 0 and seq // SEL_BLOCK <= LANES and seq >= WINDOW + Q_BLOCK
    h_slab, top_idx, gate_w = _attention_half(
        x, w_in[0], cmp_pe_k[0], cmp_w1_k[0], cmp_w2_k[0], cmp_pe_v[0], cmp_w1_v[0], cmp_w2_v[0],
        w_proj_sb[0], w_proj_nsa[0], w_out[0], ln1_g[0], ln1_b[0], w_router[0], b_router[0])
    out = _moe_half(h_slab, top_idx, gate_w, w_gate_up[0], b_gate_up[0], w_down[0], b_down[0], ln2_g[0], ln2_b[0])
    return out.reshape(batch, seq, D_MODEL)
```

```python
import functools

import numpy as np
import jax
import jax.numpy as jnp
from jax import lax
from jax.experimental import pallas as pl
from jax.experimental.pallas import tpu as pltpu

D_MODEL = 1024
HEAD_DIM = 64
LANES = 128
Q_BLOCK = 128
CMP_BLOCK = 32
CMP_STRIDE = 16
SEL_BLOCK = 64
SEL_TOPK = 16
WINDOW = 512
ROPE_THETA = 10000.0
N_EXPERTS = 32
TOP_K = 4
SWIGLU_LIMIT = 7.0
SWIGLU_ALPHA = 1.702
LN_EPS = 1e-5
NEG_INF = -1e30
FORCE_SCORE = 1e30
TAKEN = -3e38
DEEPNORM_ALPHA = 2.0 ** 0.25
QK_SCALE = HEAD_DIM ** -0.5
LOG2E = 1.4426950408889634

CB_SBQ, CB_SBK, CB_SBV, CB_NQ = 0, 4, 8, 12
CB_KC, CB_VC, CB_KS, CB_VS, CB_KW, CB_VW = 16, 17, 18, 19, 20, 21
CB_NG = 22
CB_MG = 24
PROJ_W = 40 * LANES

SB_TAIL_CUTOFF = -110.0

SEL_TK = 256
MOE_ROWS = 256
SLAB = D_MODEL // LANES
VMEM_LIMIT = 56 * 1024 * 1024

_bf16 = jnp.bfloat16
_f32 = jnp.float32


def _cparams(sem):
    return pltpu.CompilerParams(dimension_semantics=sem, vmem_limit_bytes=VMEM_LIMIT)


def _dot_t(a, b):
    return lax.dot_general(a, b, (((1,), (1,)), ((), ())), preferred_element_type=_f32)


def _dot(a, b):
    return jnp.dot(a, b, preferred_element_type=_f32)


def _lane_iota(shape):
    return lax.broadcasted_iota(jnp.int32, shape, len(shape) - 1)


def _half0(shape=(1, LANES)):
    return _lane_iota(shape) < HEAD_DIM


def _in_proj_kernel(x_ref, w_ref, o_ref):
    o_ref[...] = _dot(x_ref[...].astype(_bf16), w_ref[...]).astype(o_ref.dtype)


def _in_proj(x2, w):
    n = x2.shape[0]
    tm, tn = 512, 1280
    return pl.pallas_call(
        _in_proj_kernel,
        grid=(PROJ_W // tn, n // tm),
        in_specs=[pl.BlockSpec((tm, D_MODEL), lambda j, i: (i, 0)),
                  pl.BlockSpec((D_MODEL, tn), lambda j, i: (0, j))],
        out_specs=pl.BlockSpec((tm, tn), lambda j, i: (i, j)),
        out_shape=jax.ShapeDtypeStruct((n, PROJ_W), _bf16),
        compiler_params=_cparams(("arbitrary", "arbitrary")),
        name="in_proj",
    )(x2, w)


def _rope(x, cos, sin_signed):
    first = (_lane_iota((1, LANES)) % HEAD_DIM) < (HEAD_DIM // 2)
    swapped = jnp.where(first, pltpu.roll(x, LANES - HEAD_DIM // 2, 1), pltpu.roll(x, HEAD_DIM // 2, 1))
    return x * cos + swapped * sin_signed


def _dup(x, g):
    other = pltpu.roll(x, HEAD_DIM, 1)
    h0 = _half0()
    return jnp.where(h0, x, other) if g == 0 else jnp.where(h0, other, x)


def _rope_kernel(nq_ref, kc_ref, vc_ref, ks_ref, vs_ref, kw_ref, vw_ref, cos_ref, sin_ref,
                 nq_o, kc_o, vc_o, ka_o, vs_o, kw_o, vw_o, *, blocks_per_seq):
    ts = cos_ref.shape[0]
    cos = cos_ref[...]
    sin = sin_ref[...]
    for c in range(4):
        sl = slice(c * LANES, (c + 1) * LANES)
        nq_o[:, sl] = (_rope(nq_ref[:, sl].astype(_f32), cos, sin) * (QK_SCALE * LOG2E)).astype(_bf16)
    kc_o[...] = _rope(kc_ref[...].astype(_f32), cos, sin).astype(_bf16)
    vc_o[...] = vc_ref[...]
    ks = _rope(ks_ref[...].astype(_f32), cos, sin)
    kw = _rope(kw_ref[...].astype(_f32), cos, sin)
    vs = vs_ref[...].astype(_f32)
    vw = vw_ref[...].astype(_f32)
    pos = (pl.program_id(0) % blocks_per_seq) * ts + lax.broadcasted_iota(jnp.int32, (ts, LANES), 0)
    onehot = jnp.where(pos // SEL_BLOCK == _lane_iota((ts, LANES)), 1.0, 0.0).astype(_bf16)
    for g in range(2):
        ka_o[g, :, 0:LANES] = _dup(ks, g).astype(_bf16)
        ka_o[g, :, LANES:2 * LANES] = onehot
        vsa = jnp.where(_half0(), _dup(vs, g), 1.0)
        for c in range(ts // SEL_TK):
            vs_o[g, c] = vsa[c * SEL_TK:(c + 1) * SEL_TK, :].T.astype(_bf16)
        kw_o[g] = _dup(kw, g).astype(_bf16)
        vwd = _dup(vw, g)
        for c in range(ts // LANES):
            vw_o[g, c] = vwd[c * LANES:(c + 1) * LANES, :].T.astype(_bf16)


def _rope_stage(proj, cos, sin_signed, seq):
    n = proj.shape[0]
    ts = 512
    bps = seq // ts
    col = lambda cb: pl.BlockSpec((ts, LANES), lambda i, cb=cb: (i, cb))
    tab = pl.BlockSpec((ts, LANES), lambda i: (i % bps, 0))
    grp = lambda w: pl.BlockSpec((2, ts, w), lambda i: (0, i, 0))
    return pl.pallas_call(
        functools.partial(_rope_kernel, blocks_per_seq=bps),
        grid=(n // ts,),
        in_specs=[pl.BlockSpec((ts, 4 * LANES), lambda i: (i, CB_NQ // 4)),
                  col(CB_KC), col(CB_VC), col(CB_KS), col(CB_VS), col(CB_KW), col(CB_VW), tab, tab],
        out_specs=[pl.BlockSpec((ts, 4 * LANES), lambda i: (i, 0)),
                   pl.BlockSpec((ts, LANES), lambda i: (i, 0)),
                   pl.BlockSpec((ts, LANES), lambda i: (i, 0)),
                   grp(2 * LANES),
                   pl.BlockSpec((2, ts // SEL_TK, LANES, SEL_TK), lambda i: (0, i, 0, 0)),
                   grp(LANES),
                   pl.BlockSpec((2, ts // LANES, LANES, LANES), lambda i: (0, i, 0, 0))],
        out_shape=[jax.ShapeDtypeStruct((n, 4 * LANES), _bf16),
                   jax.ShapeDtypeStruct((n, LANES), _bf16),
                   jax.ShapeDtypeStruct((n, LANES), _bf16),
                   jax.ShapeDtypeStruct((2, n, 2 * LANES), _bf16),
                   jax.ShapeDtypeStruct((2, n // SEL_TK, LANES, SEL_TK), _bf16),
                   jax.ShapeDtypeStruct((2, n, LANES), _bf16),
                   jax.ShapeDtypeStruct((2, n // LANES, LANES, LANES), _bf16)],
        compiler_params=_cparams(("arbitrary",)),
        name="rope_layout",
    )(proj, proj, proj, proj, proj, proj, proj, cos, sin_signed)


def _gelu_tanh(x):
    return 0.5 * x * (1.0 + jnp.tanh(0.7978845608028654 * (x + 0.044715 * (x * x * x))))


def _compress_one(x_ref, pe_t, pe_b, w_t, w_b, w2, out_ref, transposed):
    x = x_ref[0].astype(_f32)
    a = _dot((x + pe_t[...]).astype(_bf16), w_t[...])
    b = _dot((x + pe_b[...]).astype(_bf16), w_b[...])
    nc = a.shape[0]
    pre = a + pltpu.roll(b, nc - 1, 0)
    y = _dot(_gelu_tanh(pre).astype(_bf16), w2[...])
    for g in range(2):
        d = _dup(y, g)
        out_ref[0, g] = (d.T if transposed else d).astype(_bf16)


def _compress_kernel(k_ref, v_ref, kpt, kpb, kwt, kwb, kw2, vpt, vpb, vwt, vwb, vw2, ko_ref, vo_ref):
    _compress_one(k_ref, kpt, kpb, kwt, kwb, kw2, ko_ref, False)
    _compress_one(v_ref, vpt, vpb, vwt, vwb, vw2, vo_ref, True)


def _compress_weights(pe, w1, w2):
    half = CMP_BLOCK // 2
    eye = jnp.eye(2, dtype=_f32)
    outs = []
    for part in range(2):
        w = w1[part * half * HEAD_DIM:(part + 1) * half * HEAD_DIM].reshape(half, HEAD_DIM, HEAD_DIM)
        wbd = (w[:, None, :, None, :] * eye[None, :, None, :, None]).reshape(half * 2 * HEAD_DIM, 2 * HEAD_DIM)
        p = jnp.broadcast_to(pe[part * half:(part + 1) * half, None, :], (half, 2, HEAD_DIM)).reshape(1, -1)
        outs.append((p.astype(_f32), wbd.astype(_bf16)))
    w2bd = (w2[None, :, None, :] * eye[:, None, :, None]).reshape(2 * HEAD_DIM, 2 * HEAD_DIM).astype(_bf16)
    (pt, wt), (pb, wb) = outs
    return pt, pb, wt, wb, w2bd


def _compress_stage(kc_r, vc_r, kparams, vparams, batch, seq):
    nc = seq // CMP_STRIDE
    width = CMP_STRIDE * LANES
    xs = pl.BlockSpec((1, nc, width), lambda b: (b, 0, 0))
    full = lambda a: pl.BlockSpec(a.shape, lambda b: (0,) * a.ndim)
    out = pl.BlockSpec((1, 2, nc, LANES), lambda b: (b, 0, 0, 0))
    weights = list(kparams) + list(vparams)
    return pl.pallas_call(
        _compress_kernel,
        grid=(batch,),
        in_specs=[xs, xs] + [full(a) for a in weights],
        out_specs=[out, pl.BlockSpec((1, 2, LANES, nc), lambda b: (b, 0, 0, 0))],
        out_shape=[jax.ShapeDtypeStruct((batch, 2, nc, LANES), _bf16),
                   jax.ShapeDtypeStruct((batch, 2, LANES, nc), _bf16)],
        compiler_params=_cparams(("arbitrary",)),
        name="compress",
    )(kc_r.reshape(batch, nc, width), vc_r.reshape(batch, nc, width), *weights)


def _head_q(q_ref, r):
    q2 = q_ref[:, (r // 2) * LANES:(r // 2 + 1) * LANES]
    keep = _half0() if r % 2 == 0 else jnp.logical_not(_half0())
    return jnp.where(keep, q2, jnp.zeros_like(q2))


def _softmax_over_rows(s):
    m = jnp.max(s, axis=0, keepdims=True)
    e = jnp.exp2(s - m)
    l = jnp.sum(e, axis=0, keepdims=True)
    return e * jnp.where(m > 0.5 * NEG_INF, 1.0 / l, 0.0)


def _pair(even, odd):
    return jnp.where(_half0(), even, odd)


def _nsa_cw_kernel(q_ref, kc_ref, vct_ref, kw_ref, vwt_ref, ng_ref, stt_ref, yp_ref, mb_ref):
    t0 = pl.program_id(2) * Q_BLOCK
    qpos = t0 + _lane_iota((1, Q_BLOCK))
    gates = jax.nn.sigmoid(ng_ref[...].astype(_f32).T[0:16, :])
    kc = kc_ref[0, 0]
    vct = vct_ref[0, 0]
    nc = kc.shape[0]
    cend = lax.broadcasted_iota(jnp.int32, (nc, 1), 0) * CMP_STRIDE + (CMP_BLOCK - 1)
    cmask = cend <= qpos
    start = pl.multiple_of(jnp.maximum(t0 - WINDOW, 0), Q_BLOCK)
    wlen = WINDOW + Q_BLOCK
    kwin = kw_ref[0, 0, pl.ds(start, wlen), :]
    kpos = start + lax.broadcasted_iota(jnp.int32, (wlen, 1), 0)
    wmask = (kpos <= qpos) & (qpos - kpos < WINDOW)
    blk0 = start // Q_BLOCK
    upper = lax.broadcasted_iota(jnp.int32, (LANES, 1), 0) < HEAD_DIM

    imp = jnp.zeros((nc, Q_BLOCK), _f32)
    yts = []
    for r in range(4):
        q = _head_q(q_ref, r)
        p = _softmax_over_rows(jnp.where(cmask, _dot_t(kc, q), NEG_INF))
        imp = imp + p
        o_cmp = _dot(vct, p.astype(_bf16))
        pw = _softmax_over_rows(jnp.where(wmask, _dot_t(kwin, q), NEG_INF)).astype(_bf16)
        o_win = _dot(vwt_ref[0, 0, blk0], pw[0:Q_BLOCK])
        for c in range(1, wlen // Q_BLOCK):
            o_win = o_win + _dot(vwt_ref[0, 0, blk0 + c], pw[c * Q_BLOCK:(c + 1) * Q_BLOCK])
        yts.append(gates[3 * r:3 * r + 1] * o_cmp + gates[3 * r + 2:3 * r + 3] * o_win)
    yp_ref[:, 0:LANES] = jnp.where(upper, yts[0], yts[1]).T
    yp_ref[:, LANES:2 * LANES] = jnp.where(upper, yts[2], yts[3]).T

    p_slc = jnp.dot(stt_ref[...], imp, preferred_element_type=_f32, precision=lax.Precision.HIGHEST)
    selj = lax.broadcasted_iota(jnp.int32, (LANES, 1), 0)
    blk_t = qpos // SEL_BLOCK
    forced = (selj == 0) | (selj == blk_t) | (selj == blk_t - 1)
    score = jnp.where(forced, FORCE_SCORE, jnp.where(selj <= blk_t, p_slc, NEG_INF))
    seljf = selj.astype(_f32)
    picked = jnp.zeros((LANES, Q_BLOCK), jnp.bool_)
    for _ in range(SEL_TOPK):
        m = jnp.max(score, axis=0, keepdims=True)
        first = jnp.min(jnp.where(score == m, seljf, float(LANES)), axis=0, keepdims=True)
        hit = seljf == first
        picked = picked | hit
        score = jnp.where(hit, TAKEN, score)
    mb_ref[0] = jnp.where(picked, 0.0, NEG_INF).T.astype(_bf16)


def _nsa_cw_stage(nq_r, kc_d, vc_t, kw_d, vw_t, proj, stencil_t, batch, seq):
    n = nq_r.shape[0]
    nblk = seq // Q_BLOCK
    nc = seq // CMP_STRIDE
    qrow = lambda b, g, i: b * nblk + i
    return pl.pallas_call(
        _nsa_cw_kernel,
        grid=(batch, 2, nblk),
        in_specs=[pl.BlockSpec((Q_BLOCK, 2 * LANES), lambda b, g, i: (qrow(b, g, i), g)),
                  pl.BlockSpec((1, 1, nc, LANES), lambda b, g, i: (b, g, 0, 0)),
                  pl.BlockSpec((1, 1, LANES, nc), lambda b, g, i: (b, g, 0, 0)),
                  pl.BlockSpec((1, 1, seq, LANES), lambda b, g, i: (g, b, 0, 0)),
                  pl.BlockSpec((1, 1, nblk, LANES, LANES), lambda b, g, i: (g, b, 0, 0, 0)),
                  pl.BlockSpec((Q_BLOCK, LANES), lambda b, g, i: (qrow(b, g, i), CB_NG + g)),
                  pl.BlockSpec((LANES, nc), lambda b, g, i: (0, 0))],
        out_specs=[pl.BlockSpec((Q_BLOCK, 2 * LANES), lambda b, g, i: (qrow(b, g, i), g)),
                   pl.BlockSpec((1, Q_BLOCK, LANES), lambda b, g, i: (g, qrow(b, g, i), 0))],
        out_shape=[jax.ShapeDtypeStruct((n, 4 * LANES), _f32),
                   jax.ShapeDtypeStruct((2, n, LANES), _bf16)],
        compiler_params=_cparams(("arbitrary", "arbitrary", "arbitrary")),
        name="nsa_cmp_win_select",
    )(nq_r, kc_d, vc_t, kw_d.reshape(2, batch, seq, LANES), vw_t.reshape(2, batch, nblk, LANES, LANES), proj, stencil_t)


def _nsa_sel_kernel(q_ref, mb_ref, ka_ref, vat_ref, ng_ref, yp_ref, o_ref, qs_ref, m_ref, acc_ref, s_ref, p_ref, alpha_ref):
    tk = SEL_TK
    t0 = pl.program_id(2) * Q_BLOCK
    mb = mb_ref[0]
    for r in range(4):
        qs_ref[r * Q_BLOCK:(r + 1) * Q_BLOCK, 0:LANES] = _head_q(q_ref, r)
        qs_ref[r * Q_BLOCK:(r + 1) * Q_BLOCK, LANES:2 * LANES] = mb
    m_ref[...] = jnp.full(m_ref.shape, NEG_INF, _f32)
    acc_ref[...] = jnp.zeros(acc_ref.shape, _f32)
    p_ref[...] = jnp.zeros(p_ref.shape, _bf16)
    alpha_ref[...] = jnp.ones(alpha_ref.shape, _f32)
    qpos = t0 + _lane_iota((1, 4 * Q_BLOCK)) % Q_BLOCK

    def scores(kt):
        return _dot_t(ka_ref[0, 0, pl.ds(pl.multiple_of(kt * tk, tk), tk), :], qs_ref[...])

    def softmax_step(s):
        m_old = m_ref[...]
        m_new = jnp.maximum(m_old, jnp.max(s, axis=0, keepdims=True))
        m_ref[...] = m_new
        return jnp.exp2(s - m_new).astype(_bf16), jnp.exp2(m_old - m_new)

    def accumulate(kt, alpha, p):
        acc_ref[...] = alpha * acc_ref[...] + _dot(vat_ref[0, 0, kt], p)

    def trip(kt, carry):
        accumulate(jnp.maximum(kt - 1, 0), alpha_ref[...], p_ref[...])
        s = s_ref[...]
        s_ref[...] = scores(kt + 1)
        p, alpha = softmax_step(s)
        p_ref[...] = p
        alpha_ref[...] = alpha
        return carry

    n_full = t0 // tk
    s_ref[...] = scores(0)
    lax.fori_loop(0, n_full, trip, 0)
    accumulate(jnp.maximum(n_full - 1, 0), alpha_ref[...], p_ref[...])
    kpos = n_full * tk + lax.broadcasted_iota(jnp.int32, (tk, 1), 0)
    p, alpha = softmax_step(jnp.where(kpos <= qpos, s_ref[...], NEG_INF))
    accumulate(n_full, alpha, p)
    acc = acc_ref[...]
    o = acc[0:HEAD_DIM] * (1.0 / acc[HEAD_DIM:2 * HEAD_DIM])
    gates = jax.nn.sigmoid(ng_ref[...].astype(_f32).T[0:16, :])
    ys = [gates[3 * r + 1:3 * r + 2] * o[:, r * Q_BLOCK:(r + 1) * Q_BLOCK] for r in range(4)]
    o_ref[:, 0:LANES] = (yp_ref[:, 0:LANES] + jnp.concatenate(ys[0:2], axis=0).T).astype(o_ref.dtype)
    o_ref[:, LANES:2 * LANES] = (yp_ref[:, LANES:2 * LANES] + jnp.concatenate(ys[2:4], axis=0).T).astype(o_ref.dtype)


def _nsa_sel_stage(nq_r, mbias, k_aug, vs_t, proj, ypart, batch, seq):
    n = nq_r.shape[0]
    nblk = seq // Q_BLOCK
    qrow = lambda b, g, i: b * nblk + i
    return pl.pallas_call(
        _nsa_sel_kernel,
        grid=(batch, 2, nblk),
        in_specs=[pl.BlockSpec((Q_BLOCK, 2 * LANES), lambda b, g, i: (qrow(b, g, i), g)),
                  pl.BlockSpec((1, Q_BLOCK, LANES), lambda b, g, i: (g, qrow(b, g, i), 0)),
                  pl.BlockSpec((1, 1, seq, 2 * LANES), lambda b, g, i: (g, b, 0, 0)),
                  pl.BlockSpec((1, 1, seq // SEL_TK, LANES, SEL_TK), lambda b, g, i: (g, b, 0, 0, 0)),
                  pl.BlockSpec((Q_BLOCK, LANES), lambda b, g, i: (qrow(b, g, i), CB_NG + g)),
                  pl.BlockSpec((Q_BLOCK, 2 * LANES), lambda b, g, i: (qrow(b, g, i), g))],
        out_specs=pl.BlockSpec((Q_BLOCK, 2 * LANES), lambda b, g, i: (qrow(b, g, i), g)),
        out_shape=jax.ShapeDtypeStruct((n, 4 * LANES), _bf16),
        scratch_shapes=[pltpu.VMEM((4 * Q_BLOCK, 2 * LANES), _bf16),
                        pltpu.VMEM((1, 4 * Q_BLOCK), _f32),
                        pltpu.VMEM((LANES, 4 * Q_BLOCK), _f32),
                        pltpu.VMEM((SEL_TK, 4 * Q_BLOCK), _f32),
                        pltpu.VMEM((SEL_TK, 4 * Q_BLOCK), _bf16),
                        pltpu.VMEM((1, 4 * Q_BLOCK), _f32)],
        compiler_params=_cparams(("arbitrary", "arbitrary", "arbitrary")),
        name="nsa_selected",
    )(nq_r, mbias, k_aug.reshape(2, batch, seq, 2 * LANES), vs_t.reshape(2, batch, seq // SEL_TK, LANES, SEL_TK), proj, ypart)


def _sb_kernel(q_ref, k_ref, v_ref, o_ref, qs_ref, tail_ref, acc_ref):
    i = pl.program_id(1)
    h0 = _half0()
    heads = 2 * (q_ref.shape[1] // LANES)
    for h in range(heads):
        q = q_ref[:, (h // 2) * LANES:(h // 2 + 1) * LANES]
        keep = h0 if h % 2 == 0 else jnp.logical_not(h0)
        qs_ref[h] = jnp.where(keep, q, jnp.zeros_like(q)) * QK_SCALE
    tail_ref[...] = jnp.zeros(tail_ref.shape, _f32)
    acc_ref[...] = jnp.zeros(acc_ref.shape, _f32)
    rloc = lax.broadcasted_iota(jnp.int32, (Q_BLOCK, Q_BLOCK), 0)
    cloc = lax.broadcasted_iota(jnp.int32, (Q_BLOCK, Q_BLOCK), 1)
    later = jnp.where(rloc > cloc, 1.0, 0.0).astype(_bf16)

    def cond(c):
        j, worst_tail = c
        return (j >= 0) & (worst_tail > SB_TAIL_CUTOFF)

    def body(c):
        j, _ = c
        k0 = pl.multiple_of(j * Q_BLOCK, Q_BLOCK)
        past = (cloc + (j - i) * Q_BLOCK) < rloc
        worst = jnp.full((Q_BLOCK, 1), -jnp.inf, _f32)
        for h in range(heads):
            cols = slice((h // 2) * LANES, (h // 2 + 1) * LANES)
            z = _dot_t(qs_ref[h], k_ref[0, pl.ds(k0, Q_BLOCK), cols])
            sp = jnp.maximum(z, 0.0) + jnp.log1p(jnp.exp(-jnp.abs(z)))
            log_keep = jnp.where(past, -sp, 0.0)
            hi = log_keep.astype(_bf16)
            lo = (log_keep - hi.astype(_f32)).astype(_bf16)
            inner = _dot(hi, later) + _dot(lo, later)
            tail = tail_ref[h]
            a = jnp.where(past, jnp.exp((z - sp) + inner + tail), 0.0)
            acc_ref[h] = acc_ref[h] + _dot(a.astype(_bf16), v_ref[0, pl.ds(k0, Q_BLOCK), cols])
            tail = tail + jnp.sum(log_keep, axis=-1, keepdims=True)
            tail_ref[h] = tail
            worst = jnp.maximum(worst, tail)
        return j - 1, jnp.max(worst)

    lax.while_loop(cond, body, (i, jnp.float32(0.0)))
    for p in range(heads // 2):
        o_ref[:, p * LANES:(p + 1) * LANES] = jnp.where(h0, acc_ref[2 * p], acc_ref[2 * p + 1]).astype(o_ref.dtype)


def _sb_stage(proj, batch, seq):
    n = proj.shape[0]
    nblk = seq // Q_BLOCK
    width = 4 * LANES
    proj3 = proj.reshape(batch, seq, PROJ_W)
    return pl.pallas_call(
        _sb_kernel,
        grid=(batch, nblk),
        in_specs=[pl.BlockSpec((Q_BLOCK, width), lambda b, i: (b * nblk + i, CB_SBQ // 4)),
                  pl.BlockSpec((1, seq, width), lambda b, i: (b, 0, CB_SBK // 4)),
                  pl.BlockSpec((1, seq, width), lambda b, i: (b, 0, CB_SBV // 4))],
        out_specs=pl.BlockSpec((Q_BLOCK, width), lambda b, i: (b * nblk + i, 0)),
        out_shape=jax.ShapeDtypeStruct((n, width), _bf16),
        scratch_shapes=[pltpu.VMEM((8, Q_BLOCK, LANES), _bf16),
                        pltpu.VMEM((8, Q_BLOCK, 1), _f32),
                        pltpu.VMEM((8, Q_BLOCK, LANES), _f32)],
        compiler_params=_cparams(("arbitrary", "arbitrary")),
        name="stick_breaking",
    )(proj, proj3, proj3)


def _layer_norm(x, g, b):
    mu = jnp.mean(x, axis=-1, keepdims=True)
    xc = x - mu
    var = jnp.mean(xc * xc, axis=-1, keepdims=True)
    return xc * lax.rsqrt(var + LN_EPS) * g + b


def _merge_kernel(x_ref, ysb_ref, yns_ref, mg0_ref, mg1_ref, wsb_ref, wns_ref, wo_ref, g_ref, b_ref,
                  wr_ref, br_ref, h_ref, idx_ref, gw_ref):
    m0 = jax.nn.sigmoid(mg0_ref[...].astype(_f32))
    m1 = jax.nn.sigmoid(mg1_ref[...].astype(_f32))
    merged = m0 * _dot(ysb_ref[...], wsb_ref[...]) + m1 * _dot(yns_ref[...], wns_ref[...])
    pre = DEEPNORM_ALPHA * x_ref[...] + _dot(merged.astype(_bf16), wo_ref[...])
    h = _layer_norm(pre, g_ref[...], b_ref[...])
    tm = h.shape[0]
    for s in range(SLAB):
        h_ref[pl.ds(s, tm, stride=SLAB), :] = h[:, s * LANES:(s + 1) * LANES]
    logits = jnp.dot(h, wr_ref[...], preferred_element_type=_f32, precision=lax.Precision.HIGHEST) + br_ref[...]
    lane = _lane_iota((1, LANES))
    lanef = lane.astype(_f32)
    lg = jnp.where(lane < N_EXPERTS, logits, TAKEN)
    vals, idxs = [], []
    for _ in range(TOP_K):
        m = jnp.max(lg, axis=-1, keepdims=True)
        first = jnp.min(jnp.where(lg == m, lanef, float(LANES)), axis=-1, keepdims=True)
        vals.append(m)
        idxs.append(first)
        lg = jnp.where(lanef == first, TAKEN, lg)
    es = [jnp.exp(v - vals[0]) for v in vals]
    inv = 1.0 / (es[0] + es[1] + es[2] + es[3])
    idx_t = jnp.zeros(lg.shape, _f32)
    gw_t = jnp.zeros(lg.shape, _f32)
    for k in range(TOP_K):
        idx_t = jnp.where(lane == k, idxs[k], idx_t)
        gw_t = jnp.where(lane == k, es[k] * inv, gw_t)
    idx_ref[...] = idx_t[:, :TOP_K].astype(jnp.int32)
    gw_ref[...] = gw_t[:, :TOP_K]


def _merge_stage(x2, y_sb, y_nsa, proj, wsb, wns, wo, g1, b1, wr, br):
    n = x2.shape[0]
    tm = 256
    row = lambda w: pl.BlockSpec((tm, w), lambda i: (i, 0))
    full = lambda a: pl.BlockSpec(a.shape, lambda i: (0,) * a.ndim)
    return pl.pallas_call(
        _merge_kernel,
        grid=(n // tm,),
        in_specs=[row(D_MODEL), row(4 * LANES), row(4 * LANES),
                  pl.BlockSpec((tm, D_MODEL), lambda i: (i, CB_MG // 8)),
                  pl.BlockSpec((tm, D_MODEL), lambda i: (i, CB_MG // 8 + 1)),
                  full(wsb), full(wns), full(wo), full(g1), full(b1), full(wr), full(br)],
        out_specs=[pl.BlockSpec((tm * SLAB, LANES), lambda i: (i, 0)), row(TOP_K), row(TOP_K)],
        out_shape=[jax.ShapeDtypeStruct((n * SLAB, LANES), _f32),
                   jax.ShapeDtypeStruct((n, TOP_K), jnp.int32),
                   jax.ShapeDtypeStruct((n, TOP_K), _f32)],
        compiler_params=_cparams(("arbitrary",)),
        name="merge_ln1_router",
    )(x2, y_sb, y_nsa, proj, proj, wsb, wns, wo, g1, b1, wr, br)


def _prep_w_in(w):
    main = w[:, :CB_NG * LANES]
    ng = w[:, CB_NG * LANES:CB_NG * LANES + 24]
    mg = w[:, CB_NG * LANES + 24:]
    pad = jnp.zeros((w.shape[0], LANES - 12), w.dtype)
    return jnp.concatenate([main, ng[:, :12], pad, ng[:, 12:], pad, mg], axis=1).astype(_bf16)


def _rope_tables(seq):
    half = HEAD_DIM // 2
    inv_freq = ROPE_THETA ** (-jnp.arange(half, dtype=_f32) / half)
    ang = jnp.arange(seq, dtype=_f32)[:, None] * inv_freq[None, :]
    cos = jnp.cos(ang)
    sin = jnp.sin(ang)
    cos128 = jnp.concatenate([cos, cos, cos, cos], axis=1)
    sin128 = jnp.concatenate([-sin, sin, -sin, sin], axis=1)
    return cos128, sin128


def _stencil(nc):
    n = np.arange(nc)[:, None]
    j = np.arange(LANES)[None, :]
    ratio = SEL_BLOCK // CMP_STRIDE
    ok = (n >= ratio * j - 1) & (n <= ratio * j + ratio - 1) & (n < nc - 1)
    return jnp.asarray(ok.astype(np.float32).T)


def _attention_half(x, w_in, cmp_pe_k, cmp_w1_k, cmp_w2_k, cmp_pe_v, cmp_w1_v, cmp_w2_v,
                    w_proj_sb, w_proj_nsa, w_out, ln1_g, ln1_b, w_router, b_router):
    batch, seq, _ = x.shape
    n = batch * seq
    x2 = x.reshape(n, D_MODEL)
    proj = _in_proj(x2, _prep_w_in(w_in))
    cos, sin_signed = _rope_tables(seq)
    nq_r, kc_r, vc_r, k_aug, vs_t, kw_d, vw_t = _rope_stage(proj, cos, sin_signed, seq)
    kc_d, vc_t = _compress_stage(kc_r, vc_r, _compress_weights(cmp_pe_k, cmp_w1_k, cmp_w2_k),
                                 _compress_weights(cmp_pe_v, cmp_w1_v, cmp_w2_v), batch, seq)
    ypart, mbias = _nsa_cw_stage(nq_r, kc_d, vc_t, kw_d, vw_t, proj, _stencil(seq // CMP_STRIDE), batch, seq)
    y_nsa = _nsa_sel_stage(nq_r, mbias, k_aug, vs_t, proj, ypart, batch, seq)
    y_sb = _sb_stage(proj, batch, seq)
    wr = jnp.pad(w_router.astype(_f32), ((0, 0), (0, LANES - N_EXPERTS)))
    br = jnp.pad(b_router.astype(_f32), (0, LANES - N_EXPERTS)).reshape(1, LANES)
    return _merge_stage(x2, y_sb, y_nsa, proj, w_proj_sb.astype(_bf16), w_proj_nsa.astype(_bf16),
                        w_out.astype(_bf16), ln1_g.reshape(1, -1), ln1_b.reshape(1, -1), wr, br)


def _row_copy(src, src_row, dst, dst_row, sem):
    return pltpu.make_async_copy(src.at[pl.ds(src_row * SLAB, SLAB)], dst.at[pl.ds(dst_row * SLAB, SLAB)], sem)


def _dispatch_kernel(dest_ref, h_ref, zeros_ref, buf_ref, sem, *, tokens):
    del zeros_ref

    def issue(t, c):
        for k in range(TOP_K):
            _row_copy(h_ref, t, buf_ref, dest_ref[t * TOP_K + k], sem).start()
        return c

    lax.fori_loop(0, tokens, issue, 0)
    for _ in range(TOP_K):
        pltpu.make_async_copy(h_ref, buf_ref.at[pl.ds(0, tokens * SLAB)], sem).wait()


def _dispatch_stage(dest, h_slab, buf_rows):
    n = h_slab.shape[0] // SLAB
    tokens = 256
    zeros = jnp.zeros((buf_rows * SLAB, LANES), _f32)
    return pl.pallas_call(
        functools.partial(_dispatch_kernel, tokens=tokens),
        grid=(n // tokens,),
        in_specs=[pl.BlockSpec((tokens * TOP_K,), lambda i: (i,), memory_space=pltpu.SMEM),
                  pl.BlockSpec((tokens * SLAB, LANES), lambda i: (i, 0)),
                  pl.BlockSpec(memory_space=pl.ANY)],
        out_specs=pl.BlockSpec(memory_space=pl.ANY),
        out_shape=jax.ShapeDtypeStruct(zeros.shape, _f32),
        scratch_shapes=[pltpu.SemaphoreType.DMA(())],
        input_output_aliases={2: 0},
        compiler_params=pltpu.CompilerParams(dimension_semantics=("arbitrary",), has_side_effects=True),
        name="moe_dispatch",
    )(dest, h_slab, zeros)


def _slab_load(ref, rows):
    return jnp.concatenate([ref[pl.ds(s, rows, stride=SLAB), :] for s in range(SLAB)], axis=1)


def _expert_kernel(ce_ref, used_ref, x_ref, wgu_ref, bgu_ref, wd_ref, bd_ref, o_ref):
    del ce_ref
    c = pl.program_id(0)

    @pl.when(c < used_ref[0])
    def _():
        x = _slab_load(x_ref, MOE_ROWS).astype(_bf16)
        gu = _dot(x, wgu_ref[0]) + bgu_ref[0]
        gate = jnp.minimum(gu[:, :D_MODEL], SWIGLU_LIMIT)
        up = jnp.clip(gu[:, D_MODEL:], -SWIGLU_LIMIT, SWIGLU_LIMIT)
        h = gate * jax.nn.sigmoid(SWIGLU_ALPHA * gate) * (up + 1.0)
        y = _dot(h.astype(_bf16), wd_ref[0]) + bd_ref[0]
        for s in range(SLAB):
            o_ref[pl.ds(s, MOE_ROWS, stride=SLAB), :] = y[:, s * LANES:(s + 1) * LANES]

    @pl.when(c >= used_ref[0])
    def _():
        o_ref[...] = jnp.zeros(o_ref.shape, o_ref.dtype)


def _expert_stage(chunk_expert, n_used, buf, wgu, bgu, wd, bd):
    n_chunks = chunk_expert.shape[0]
    rows = MOE_ROWS * SLAB
    grid_spec = pltpu.PrefetchScalarGridSpec(
        num_scalar_prefetch=2,
        grid=(n_chunks,),
        in_specs=[pl.BlockSpec((rows, LANES), lambda c, ce, nu: (c, 0)),
                  pl.BlockSpec((1, D_MODEL, 2 * D_MODEL), lambda c, ce, nu: (ce[c], 0, 0)),
                  pl.BlockSpec((1, 1, 2 * D_MODEL), lambda c, ce, nu: (ce[c], 0, 0)),
                  pl.BlockSpec((1, D_MODEL, D_MODEL), lambda c, ce, nu: (ce[c], 0, 0)),
                  pl.BlockSpec((1, 1, D_MODEL), lambda c, ce, nu: (ce[c], 0, 0))],
        out_specs=pl.BlockSpec((rows, LANES), lambda c, ce, nu: (c, 0)))
    return pl.pallas_call(
        _expert_kernel,
        grid_spec=grid_spec,
        out_shape=jax.ShapeDtypeStruct(buf.shape, _f32),
        compiler_params=_cparams(("arbitrary",)),
        name="moe_experts",
    )(chunk_expert, n_used, buf, wgu, bgu, wd, bd)


def _combine_kernel(dest_ref, gw_ref, h_ref, eo_ref, g_ref, b_ref, o_ref, gbuf, ysl, sem, *, tokens):
    count = tokens * TOP_K

    def issue(t, c):
        for k in range(TOP_K):
            j = t * TOP_K + k
            _row_copy(eo_ref, dest_ref[j], gbuf, j, sem).start()
        return c

    lax.fori_loop(0, tokens, issue, 0)
    pltpu.make_async_copy(eo_ref.at[pl.ds(0, count * SLAB)], gbuf, sem).wait()

    def token(t, c):
        acc = DEEPNORM_ALPHA * h_ref[pl.ds(pl.multiple_of(t * SLAB, SLAB), SLAB), :]
        for k in range(TOP_K):
            j = t * TOP_K + k
            acc = acc + gw_ref[j] * gbuf[pl.ds(pl.multiple_of(j * SLAB, SLAB), SLAB), :]
        ysl[pl.ds(pl.multiple_of(t * SLAB, SLAB), SLAB), :] = acc
        return c

    lax.fori_loop(0, tokens, token, 0)
    o_ref[...] = _layer_norm(_slab_load(ysl, tokens), g_ref[...], b_ref[...])


def _combine_stage(dest, gate_w, h_slab, expert_out, g2, b2):
    n = h_slab.shape[0] // SLAB
    tokens = 256
    smem = lambda: pl.BlockSpec((tokens * TOP_K,), lambda i: (i,), memory_space=pltpu.SMEM)
    return pl.pallas_call(
        functools.partial(_combine_kernel, tokens=tokens),
        grid=(n // tokens,),
        in_specs=[smem(), smem(),
                  pl.BlockSpec((tokens * SLAB, LANES), lambda i: (i, 0)),
                  pl.BlockSpec(memory_space=pl.ANY),
                  pl.BlockSpec((1, D_MODEL), lambda i: (0, 0)),
                  pl.BlockSpec((1, D_MODEL), lambda i: (0, 0))],
        out_specs=pl.BlockSpec((tokens, D_MODEL), lambda i: (i, 0)),
        out_shape=jax.ShapeDtypeStruct((n, D_MODEL), _f32),
        scratch_shapes=[pltpu.VMEM((tokens * TOP_K * SLAB, LANES), _f32),
                        pltpu.VMEM((tokens * SLAB, LANES), _f32),
                        pltpu.SemaphoreType.DMA(())],
        compiler_params=_cparams(("arbitrary",)),
        name="moe_combine_ln2",
    )(dest, gate_w, h_slab, expert_out, g2, b2)


def _dispatch_plan(top_idx):
    m = top_idx.size
    e_flat = top_idx.reshape(m)
    onehot = (e_flat[:, None] == jnp.arange(N_EXPERTS, dtype=jnp.int32)[None, :]).astype(jnp.int32)
    csum = jnp.cumsum(onehot, axis=0)
    counts = csum[-1]
    padded = (counts + MOE_ROWS - 1) // MOE_ROWS * MOE_ROWS
    pends = jnp.cumsum(padded)
    pstarts = pends - padded
    dest = jnp.sum(onehot * (csum - 1 + pstarts[None, :]), axis=1).astype(jnp.int32)
    n_chunks = m // MOE_ROWS + N_EXPERTS
    chunk_start = jnp.arange(n_chunks, dtype=jnp.int32) * MOE_ROWS
    chunk_expert = jnp.minimum(jnp.sum((chunk_start[:, None] >= pends[None, :]).astype(jnp.int32), axis=1), N_EXPERTS - 1)
    n_used = (pends[-1] // MOE_ROWS).astype(jnp.int32).reshape(1)
    return dest, chunk_expert, n_used, n_chunks * MOE_ROWS


def _moe_half(h_slab, top_idx, gate_w, w_gate_up, b_gate_up, w_down, b_down, ln2_g, ln2_b):
    dest, chunk_expert, n_used, buf_rows = _dispatch_plan(top_idx)
    buf = _dispatch_stage(dest, h_slab, buf_rows)
    expert_out = _expert_stage(chunk_expert, n_used, buf, w_gate_up.astype(_bf16),
                               b_gate_up.reshape(N_EXPERTS, 1, -1), w_down.astype(_bf16),
                               b_down.reshape(N_EXPERTS, 1, -1))
    return _combine_stage(dest, gate_w.reshape(-1), h_slab, expert_out, ln2_g.reshape(1, -1), ln2_b.reshape(1, -1))


def kernel(x, w_in, cmp_pe_k, cmp_w1_k, cmp_w2_k, cmp_pe_v, cmp_w1_v, cmp_w2_v, w_proj_sb, w_proj_nsa, w_out,
           ln1_g, ln1_b, w_router, b_router, w_gate_up, b_gate_up, w_down, b_down, ln2_g, ln2_b):
    assert w_in.shape[0] == 1, "single-layer block"
    batch, seq, _ = x.shape
    assert seq % 512 == 0 and seq // SEL_BLOCK <= LANES and seq >= WINDOW + Q_BLOCK
    h_slab, top_idx, gate_w = _attention_half(
        x, w_in[0], cmp_pe_k[0], cmp_w1_k[0], cmp_w2_k[0], cmp_pe_v[0], cmp_w1_v[0], cmp_w2_v[0],
        w_proj_sb[0], w_proj_nsa[0], w_out[0], ln1_g[0], ln1_b[0], w_router[0], b_router[0])
    out = _moe_half(h_slab, top_idx, gate_w, w_gate_up[0], b_gate_up[0], w_down[0], b_down[0], ln2_g[0], ln2_b[0])
    return out.reshape(batch, seq, D_MODEL)
```

```python
import functools

import numpy as np
import jax
import jax.numpy as jnp
from jax import lax
from jax.experimental import pallas as pl
from jax.experimental.pallas import tpu as pltpu

D_MODEL = 1024
HEAD_DIM = 64
LANES = 128
Q_BLOCK = 128
CMP_BLOCK = 32
CMP_STRIDE = 16
SEL_BLOCK = 64
SEL_TOPK = 16
WINDOW = 512
ROPE_THETA = 10000.0
N_EXPERTS = 32
TOP_K = 4
SWIGLU_LIMIT = 7.0
SWIGLU_ALPHA = 1.702
LN_EPS = 1e-5
NEG_INF = -1e30
FORCE_SCORE = 1e30
TAKEN = -3e38
DEEPNORM_ALPHA = 2.0 ** 0.25
QK_SCALE = HEAD_DIM ** -0.5
LOG2E = 1.4426950408889634

CB_SBQ, CB_SBK, CB_SBV, CB_NQ = 0, 4, 8, 12
CB_KC, CB_VC, CB_KS, CB_VS, CB_KW, CB_VW = 16, 17, 18, 19, 20, 21
CB_NG = 22
CB_MG = 24
PROJ_W = 40 * LANES

SB_TAIL_CUTOFF = -110.0

SEL_TK = 512
MOE_ROWS = 256
SLAB = D_MODEL // LANES
VMEM_LIMIT = 56 * 1024 * 1024

_bf16 = jnp.bfloat16
_f32 = jnp.float32


def _cparams(sem):
    return pltpu.CompilerParams(dimension_semantics=sem, vmem_limit_bytes=VMEM_LIMIT)


def _dot_t(a, b):
    return lax.dot_general(a, b, (((1,), (1,)), ((), ())), preferred_element_type=_f32)


def _dot(a, b):
    return jnp.dot(a, b, preferred_element_type=_f32)


def _lane_iota(shape):
    return lax.broadcasted_iota(jnp.int32, shape, len(shape) - 1)


def _half0(shape=(1, LANES)):
    return _lane_iota(shape) < HEAD_DIM


def _in_proj_kernel(x_ref, w_ref, o_ref):
    o_ref[...] = _dot(x_ref[...].astype(_bf16), w_ref[...]).astype(o_ref.dtype)


def _in_proj(x2, w):
    n = x2.shape[0]
    tm, tn = 512, 1280
    return pl.pallas_call(
        _in_proj_kernel,
        grid=(PROJ_W // tn, n // tm),
        in_specs=[pl.BlockSpec((tm, D_MODEL), lambda j, i: (i, 0)),
                  pl.BlockSpec((D_MODEL, tn), lambda j, i: (0, j))],
        out_specs=pl.BlockSpec((tm, tn), lambda j, i: (i, j)),
        out_shape=jax.ShapeDtypeStruct((n, PROJ_W), _bf16),
        compiler_params=_cparams(("arbitrary", "arbitrary")),
        name="in_proj",
    )(x2, w)


def _rope(x, cos, sin_signed):
    first = (_lane_iota((1, LANES)) % HEAD_DIM) < (HEAD_DIM // 2)
    swapped = jnp.where(first, pltpu.roll(x, LANES - HEAD_DIM // 2, 1), pltpu.roll(x, HEAD_DIM // 2, 1))
    return x * cos + swapped * sin_signed


def _dup(x, g):
    other = pltpu.roll(x, HEAD_DIM, 1)
    h0 = _half0()
    return jnp.where(h0, x, other) if g == 0 else jnp.where(h0, other, x)


def _rope_kernel(nq_ref, kc_ref, vc_ref, ks_ref, vs_ref, kw_ref, vw_ref, cos_ref, sin_ref,
                 nq_o, kc_o, vc_o, ka_o, vs_o, kw_o, vw_o, *, blocks_per_seq):
    ts = cos_ref.shape[0]
    cos = cos_ref[...]
    sin = sin_ref[...]
    for c in range(4):
        sl = slice(c * LANES, (c + 1) * LANES)
        nq_o[:, sl] = (_rope(nq_ref[:, sl].astype(_f32), cos, sin) * (QK_SCALE * LOG2E)).astype(_bf16)
    kc_o[...] = _rope(kc_ref[...].astype(_f32), cos, sin).astype(_bf16)
    vc_o[...] = vc_ref[...]
    ks = _rope(ks_ref[...].astype(_f32), cos, sin)
    kw = _rope(kw_ref[...].astype(_f32), cos, sin)
    vs = vs_ref[...].astype(_f32)
    vw = vw_ref[...].astype(_f32)
    pos = (pl.program_id(0) % blocks_per_seq) * ts + lax.broadcasted_iota(jnp.int32, (ts, LANES), 0)
    onehot = jnp.where(pos // SEL_BLOCK == _lane_iota((ts, LANES)), 1.0, 0.0).astype(_bf16)
    for g in range(2):
        ka_o[g, :, 0:LANES] = _dup(ks, g).astype(_bf16)
        ka_o[g, :, LANES:2 * LANES] = onehot
        vsa = jnp.where(_half0(), _dup(vs, g), 1.0)
        for c in range(ts // SEL_TK):
            vs_o[g, c] = vsa[c * SEL_TK:(c + 1) * SEL_TK, :].T.astype(_bf16)
        kw_o[g] = _dup(kw, g).astype(_bf16)
        vwd = _dup(vw, g)
        for c in range(ts // LANES):
            vw_o[g, c] = vwd[c * LANES:(c + 1) * LANES, :].T.astype(_bf16)


def _rope_stage(proj, cos, sin_signed, seq):
    n = proj.shape[0]
    ts = 512
    bps = seq // ts
    col = lambda cb: pl.BlockSpec((ts, LANES), lambda i, cb=cb: (i, cb))
    tab = pl.BlockSpec((ts, LANES), lambda i: (i % bps, 0))
    grp = lambda w: pl.BlockSpec((2, ts, w), lambda i: (0, i, 0))
    return pl.pallas_call(
        functools.partial(_rope_kernel, blocks_per_seq=bps),
        grid=(n // ts,),
        in_specs=[pl.BlockSpec((ts, 4 * LANES), lambda i: (i, CB_NQ // 4)),
                  col(CB_KC), col(CB_VC), col(CB_KS), col(CB_VS), col(CB_KW), col(CB_VW), tab, tab],
        out_specs=[pl.BlockSpec((ts, 4 * LANES), lambda i: (i, 0)),
                   pl.BlockSpec((ts, LANES), lambda i: (i, 0)),
                   pl.BlockSpec((ts, LANES), lambda i: (i, 0)),
                   grp(2 * LANES),
                   pl.BlockSpec((2, ts // SEL_TK, LANES, SEL_TK), lambda i: (0, i, 0, 0)),
                   grp(LANES),
                   pl.BlockSpec((2, ts // LANES, LANES, LANES), lambda i: (0, i, 0, 0))],
        out_shape=[jax.ShapeDtypeStruct((n, 4 * LANES), _bf16),
                   jax.ShapeDtypeStruct((n, LANES), _bf16),
                   jax.ShapeDtypeStruct((n, LANES), _bf16),
                   jax.ShapeDtypeStruct((2, n, 2 * LANES), _bf16),
                   jax.ShapeDtypeStruct((2, n // SEL_TK, LANES, SEL_TK), _bf16),
                   jax.ShapeDtypeStruct((2, n, LANES), _bf16),
                   jax.ShapeDtypeStruct((2, n // LANES, LANES, LANES), _bf16)],
        compiler_params=_cparams(("arbitrary",)),
        name="rope_layout",
    )(proj, proj, proj, proj, proj, proj, proj, cos, sin_signed)


def _gelu_tanh(x):
    return 0.5 * x * (1.0 + jnp.tanh(0.7978845608028654 * (x + 0.044715 * (x * x * x))))


def _compress_one(x_ref, pe_t, pe_b, w_t, w_b, w2, out_ref, transposed):
    x = x_ref[0].astype(_f32)
    a = _dot((x + pe_t[...]).astype(_bf16), w_t[...])
    b = _dot((x + pe_b[...]).astype(_bf16), w_b[...])
    nc = a.shape[0]
    pre = a + pltpu.roll(b, nc - 1, 0)
    y = _dot(_gelu_tanh(pre).astype(_bf16), w2[...])
    for g in range(2):
        d = _dup(y, g)
        out_ref[0, g] = (d.T if transposed else d).astype(_bf16)


def _compress_kernel(k_ref, v_ref, kpt, kpb, kwt, kwb, kw2, vpt, vpb, vwt, vwb, vw2, ko_ref, vo_ref):
    _compress_one(k_ref, kpt, kpb, kwt, kwb, kw2, ko_ref, False)
    _compress_one(v_ref, vpt, vpb, vwt, vwb, vw2, vo_ref, True)


def _compress_weights(pe, w1, w2):
    half = CMP_BLOCK // 2
    eye = jnp.eye(2, dtype=_f32)
    outs = []
    for part in range(2):
        w = w1[part * half * HEAD_DIM:(part + 1) * half * HEAD_DIM].reshape(half, HEAD_DIM, HEAD_DIM)
        wbd = (w[:, None, :, None, :] * eye[None, :, None, :, None]).reshape(half * 2 * HEAD_DIM, 2 * HEAD_DIM)
        p = jnp.broadcast_to(pe[part * half:(part + 1) * half, None, :], (half, 2, HEAD_DIM)).reshape(1, -1)
        outs.append((p.astype(_f32), wbd.astype(_bf16)))
    w2bd = (w2[None, :, None, :] * eye[:, None, :, None]).reshape(2 * HEAD_DIM, 2 * HEAD_DIM).astype(_bf16)
    (pt, wt), (pb, wb) = outs
    return pt, pb, wt, wb, w2bd


def _compress_stage(kc_r, vc_r, kparams, vparams, batch, seq):
    nc = seq // CMP_STRIDE
    width = CMP_STRIDE * LANES
    xs = pl.BlockSpec((1, nc, width), lambda b: (b, 0, 0))
    full = lambda a: pl.BlockSpec(a.shape, lambda b: (0,) * a.ndim)
    out = pl.BlockSpec((1, 2, nc, LANES), lambda b: (b, 0, 0, 0))
    weights = list(kparams) + list(vparams)
    return pl.pallas_call(
        _compress_kernel,
        grid=(batch,),
        in_specs=[xs, xs] + [full(a) for a in weights],
        out_specs=[out, pl.BlockSpec((1, 2, LANES, nc), lambda b: (b, 0, 0, 0))],
        out_shape=[jax.ShapeDtypeStruct((batch, 2, nc, LANES), _bf16),
                   jax.ShapeDtypeStruct((batch, 2, LANES, nc), _bf16)],
        compiler_params=_cparams(("arbitrary",)),
        name="compress",
    )(kc_r.reshape(batch, nc, width), vc_r.reshape(batch, nc, width), *weights)


def _head_q(q_ref, r):
    q2 = q_ref[:, (r // 2) * LANES:(r // 2 + 1) * LANES]
    keep = _half0() if r % 2 == 0 else jnp.logical_not(_half0())
    return jnp.where(keep, q2, jnp.zeros_like(q2))


def _softmax_over_rows(s):
    m = jnp.max(s, axis=0, keepdims=True)
    e = jnp.exp2(s - m)
    l = jnp.sum(e, axis=0, keepdims=True)
    return e * jnp.where(m > 0.5 * NEG_INF, 1.0 / l, 0.0)


def _pair(even, odd):
    return jnp.where(_half0(), even, odd)


def _nsa_cw_kernel(q_ref, kc_ref, vct_ref, kw_ref, vwt_ref, ng_ref, stt_ref, yp_ref, mb_ref):
    t0 = pl.program_id(2) * Q_BLOCK
    qpos = t0 + _lane_iota((1, Q_BLOCK))
    gates = jax.nn.sigmoid(ng_ref[...].astype(_f32).T[0:16, :])
    kc = kc_ref[0, 0]
    vct = vct_ref[0, 0]
    nc = kc.shape[0]
    cend = lax.broadcasted_iota(jnp.int32, (nc, 1), 0) * CMP_STRIDE + (CMP_BLOCK - 1)
    cmask = cend <= qpos
    start = pl.multiple_of(jnp.maximum(t0 - WINDOW, 0), Q_BLOCK)
    wlen = WINDOW + Q_BLOCK
    kwin = kw_ref[0, 0, pl.ds(start, wlen), :]
    kpos = start + lax.broadcasted_iota(jnp.int32, (wlen, 1), 0)
    wmask = (kpos <= qpos) & (qpos - kpos < WINDOW)
    blk0 = start // Q_BLOCK
    upper = lax.broadcasted_iota(jnp.int32, (LANES, 1), 0) < HEAD_DIM

    qs = [_head_q(q_ref, r) for r in range(4)]
    s_cmp = [_dot_t(kc, q) for q in qs]
    s_win = [_dot_t(kwin, q) for q in qs]
    p_cmp = [_softmax_over_rows(jnp.where(cmask, s, NEG_INF)) for s in s_cmp]
    imp = (p_cmp[0] + p_cmp[1]) + (p_cmp[2] + p_cmp[3])
    p_win = [_softmax_over_rows(jnp.where(wmask, s, NEG_INF)).astype(_bf16) for s in s_win]
    vwt = jnp.concatenate([vwt_ref[0, 0, blk0 + c] for c in range(wlen // Q_BLOCK)], axis=1)
    yts = []
    for r in range(4):
        o_cmp = _dot(vct, p_cmp[r].astype(_bf16))
        o_win = _dot(vwt, p_win[r])
        yts.append(gates[3 * r:3 * r + 1] * o_cmp + gates[3 * r + 2:3 * r + 3] * o_win)
    yp_ref[:, 0:LANES] = jnp.where(upper, yts[0], yts[1]).T
    yp_ref[:, LANES:2 * LANES] = jnp.where(upper, yts[2], yts[3]).T

    p_slc = jnp.dot(stt_ref[...], imp, preferred_element_type=_f32, precision=lax.Precision.HIGHEST)
    selj = lax.broadcasted_iota(jnp.int32, (LANES, 1), 0)
    blk_t = qpos // SEL_BLOCK
    forced = (selj == 0) | (selj == blk_t) | (selj == blk_t - 1)
    score = jnp.where(forced, FORCE_SCORE, jnp.where(selj <= blk_t, p_slc, NEG_INF))
    seljf = selj.astype(_f32)
    picked = jnp.zeros((LANES, Q_BLOCK), jnp.bool_)
    for _ in range(SEL_TOPK):
        m = jnp.max(score, axis=0, keepdims=True)
        first = jnp.min(jnp.where(score == m, seljf, float(LANES)), axis=0, keepdims=True)
        hit = seljf == first
        picked = picked | hit
        score = jnp.where(hit, TAKEN, score)
    mb_ref[0] = jnp.where(picked, 0.0, NEG_INF).T.astype(_bf16)


def _nsa_cw_stage(nq_r, kc_d, vc_t, kw_d, vw_t, proj, stencil_t, batch, seq):
    n = nq_r.shape[0]
    nblk = seq // Q_BLOCK
    nc = seq // CMP_STRIDE
    qrow = lambda b, g, i: b * nblk + i
    return pl.pallas_call(
        _nsa_cw_kernel,
        grid=(batch, 2, nblk),
        in_specs=[pl.BlockSpec((Q_BLOCK, 2 * LANES), lambda b, g, i: (qrow(b, g, i), g)),
                  pl.BlockSpec((1, 1, nc, LANES), lambda b, g, i: (b, g, 0, 0)),
                  pl.BlockSpec((1, 1, LANES, nc), lambda b, g, i: (b, g, 0, 0)),
                  pl.BlockSpec((1, 1, seq, LANES), lambda b, g, i: (g, b, 0, 0)),
                  pl.BlockSpec((1, 1, nblk, LANES, LANES), lambda b, g, i: (g, b, 0, 0, 0)),
                  pl.BlockSpec((Q_BLOCK, LANES), lambda b, g, i: (qrow(b, g, i), CB_NG + g)),
                  pl.BlockSpec((LANES, nc), lambda b, g, i: (0, 0))],
        out_specs=[pl.BlockSpec((Q_BLOCK, 2 * LANES), lambda b, g, i: (qrow(b, g, i), g)),
                   pl.BlockSpec((1, Q_BLOCK, LANES), lambda b, g, i: (g, qrow(b, g, i), 0))],
        out_shape=[jax.ShapeDtypeStruct((n, 4 * LANES), _f32),
                   jax.ShapeDtypeStruct((2, n, LANES), _bf16)],
        compiler_params=_cparams(("arbitrary", "arbitrary", "arbitrary")),
        name="nsa_cmp_win_select",
    )(nq_r, kc_d, vc_t, kw_d.reshape(2, batch, seq, LANES), vw_t.reshape(2, batch, nblk, LANES, LANES), proj, stencil_t)


def _nsa_sel_kernel(q_ref, mb_ref, ka_ref, vat_ref, ng_ref, yp_ref, o_ref, qs_ref, m_ref, acc_ref, s_ref, p_ref, alpha_ref):
    tk = SEL_TK
    t0 = pl.program_id(2) * Q_BLOCK
    mb = mb_ref[0]
    for r in range(4):
        qs_ref[r * Q_BLOCK:(r + 1) * Q_BLOCK, 0:LANES] = _head_q(q_ref, r)
        qs_ref[r * Q_BLOCK:(r + 1) * Q_BLOCK, LANES:2 * LANES] = mb
    m_ref[...] = jnp.full(m_ref.shape, NEG_INF, _f32)
    acc_ref[...] = jnp.zeros(acc_ref.shape, _f32)
    p_ref[...] = jnp.zeros(p_ref.shape, _bf16)
    alpha_ref[...] = jnp.ones(alpha_ref.shape, _f32)
    qpos = t0 + _lane_iota((1, 4 * Q_BLOCK)) % Q_BLOCK

    def scores(kt):
        return _dot_t(ka_ref[0, 0, pl.ds(pl.multiple_of(kt * tk, tk), tk), :], qs_ref[...])

    def softmax_step(s):
        m_old = m_ref[...]
        m_new = jnp.maximum(m_old, jnp.max(s, axis=0, keepdims=True))
        m_ref[...] = m_new
        return jnp.exp2(s - m_new).astype(_bf16), jnp.exp2(m_old - m_new)

    def accumulate(kt, alpha, p):
        acc_ref[...] = alpha * acc_ref[...] + _dot(vat_ref[0, 0, kt], p)

    def trip(kt, carry):
        accumulate(jnp.maximum(kt - 1, 0), alpha_ref[...], p_ref[...])
        s = s_ref[...]
        s_ref[...] = scores(kt + 1)
        p, alpha = softmax_step(s)
        p_ref[...] = p
        alpha_ref[...] = alpha
        return carry

    n_full = t0 // tk
    s_ref[...] = scores(0)
    lax.fori_loop(0, n_full, trip, 0)
    accumulate(jnp.maximum(n_full - 1, 0), alpha_ref[...], p_ref[...])
    kpos = n_full * tk + lax.broadcasted_iota(jnp.int32, (tk, 1), 0)
    p, alpha = softmax_step(jnp.where(kpos <= qpos, s_ref[...], NEG_INF))
    accumulate(n_full, alpha, p)
    acc = acc_ref[...]
    o = acc[0:HEAD_DIM] * (1.0 / acc[HEAD_DIM:2 * HEAD_DIM])
    gates = jax.nn.sigmoid(ng_ref[...].astype(_f32).T[0:16, :])
    ys = [gates[3 * r + 1:3 * r + 2] * o[:, r * Q_BLOCK:(r + 1) * Q_BLOCK] for r in range(4)]
    o_ref[:, 0:LANES] = (yp_ref[:, 0:LANES] + jnp.concatenate(ys[0:2], axis=0).T).astype(o_ref.dtype)
    o_ref[:, LANES:2 * LANES] = (yp_ref[:, LANES:2 * LANES] + jnp.concatenate(ys[2:4], axis=0).T).astype(o_ref.dtype)


def _nsa_sel_stage(nq_r, mbias, k_aug, vs_t, proj, ypart, batch, seq):
    n = nq_r.shape[0]
    nblk = seq // Q_BLOCK
    qrow = lambda b, g, i: b * nblk + i
    return pl.pallas_call(
        _nsa_sel_kernel,
        grid=(batch, 2, nblk),
        in_specs=[pl.BlockSpec((Q_BLOCK, 2 * LANES), lambda b, g, i: (qrow(b, g, i), g)),
                  pl.BlockSpec((1, Q_BLOCK, LANES), lambda b, g, i: (g, qrow(b, g, i), 0)),
                  pl.BlockSpec((1, 1, seq, 2 * LANES), lambda b, g, i: (g, b, 0, 0)),
                  pl.BlockSpec((1, 1, seq // SEL_TK, LANES, SEL_TK), lambda b, g, i: (g, b, 0, 0, 0)),
                  pl.BlockSpec((Q_BLOCK, LANES), lambda b, g, i: (qrow(b, g, i), CB_NG + g)),
                  pl.BlockSpec((Q_BLOCK, 2 * LANES), lambda b, g, i: (qrow(b, g, i), g))],
        out_specs=pl.BlockSpec((Q_BLOCK, 2 * LANES), lambda b, g, i: (qrow(b, g, i), g)),
        out_shape=jax.ShapeDtypeStruct((n, 4 * LANES), _bf16),
        scratch_shapes=[pltpu.VMEM((4 * Q_BLOCK, 2 * LANES), _bf16),
                        pltpu.VMEM((1, 4 * Q_BLOCK), _f32),
                        pltpu.VMEM((LANES, 4 * Q_BLOCK), _f32),
                        pltpu.VMEM((SEL_TK, 4 * Q_BLOCK), _f32),
                        pltpu.VMEM((SEL_TK, 4 * Q_BLOCK), _bf16),
                        pltpu.VMEM((1, 4 * Q_BLOCK), _f32)],
        compiler_params=_cparams(("arbitrary", "arbitrary", "arbitrary")),
        name="nsa_selected",
    )(nq_r, mbias, k_aug.reshape(2, batch, seq, 2 * LANES), vs_t.reshape(2, batch, seq // SEL_TK, LANES, SEL_TK), proj, ypart)


def _sb_kernel(q_ref, k_ref, v_ref, o_ref, qs_ref, tail_ref, acc_ref):
    i = pl.program_id(1)
    h0 = _half0()
    heads = 2 * (q_ref.shape[1] // LANES)
    for h in range(heads):
        q = q_ref[:, (h // 2) * LANES:(h // 2 + 1) * LANES]
        keep = h0 if h % 2 == 0 else jnp.logical_not(h0)
        qs_ref[h] = jnp.where(keep, q, jnp.zeros_like(q)) * QK_SCALE
    tail_ref[...] = jnp.zeros(tail_ref.shape, _f32)
    acc_ref[...] = jnp.zeros(acc_ref.shape, _f32)
    rloc = lax.broadcasted_iota(jnp.int32, (Q_BLOCK, Q_BLOCK), 0)
    cloc = lax.broadcasted_iota(jnp.int32, (Q_BLOCK, Q_BLOCK), 1)
    later = jnp.where(rloc > cloc, 1.0, 0.0).astype(_bf16)

    def cond(c):
        j, worst_tail = c
        return (j >= 0) & (worst_tail > SB_TAIL_CUTOFF)

    def body(c):
        j, _ = c
        k0 = pl.multiple_of(j * Q_BLOCK, Q_BLOCK)
        past = (cloc + (j - i) * Q_BLOCK) < rloc
        cols = [slice((h // 2) * LANES, (h // 2 + 1) * LANES) for h in range(heads)]
        zs = [_dot_t(qs_ref[h], k_ref[0, pl.ds(k0, Q_BLOCK), cols[h]]) for h in range(heads)]
        log_beta, log_keep = [], []
        for z in zs:
            sp = jnp.maximum(z, 0.0) + jnp.log1p(jnp.exp(-jnp.abs(z)))
            log_beta.append(z - sp)
            log_keep.append(jnp.where(past, -sp, 0.0))
        inner = []
        for lk in log_keep:
            hi = lk.astype(_bf16)
            lo = (lk - hi.astype(_f32)).astype(_bf16)
            inner.append(_dot(hi, later) + _dot(lo, later))
        probs = [jnp.where(past, jnp.exp(log_beta[h] + inner[h] + tail_ref[h]), 0.0).astype(_bf16)
                 for h in range(heads)]
        worst = jnp.full((Q_BLOCK, 1), -jnp.inf, _f32)
        for h in range(heads):
            acc_ref[h] = acc_ref[h] + _dot(probs[h], v_ref[0, pl.ds(k0, Q_BLOCK), cols[h]])
            tail = tail_ref[h] + jnp.sum(log_keep[h], axis=-1, keepdims=True)
            tail_ref[h] = tail
            worst = jnp.maximum(worst, tail)
        return j - 1, jnp.max(worst)

    lax.while_loop(cond, body, (i, jnp.float32(0.0)))
    for p in range(heads // 2):
        o_ref[:, p * LANES:(p + 1) * LANES] = jnp.where(h0, acc_ref[2 * p], acc_ref[2 * p + 1]).astype(o_ref.dtype)


def _sb_stage(proj, batch, seq):
    n = proj.shape[0]
    nblk = seq // Q_BLOCK
    width = 4 * LANES
    proj3 = proj.reshape(batch, seq, PROJ_W)
    return pl.pallas_call(
        _sb_kernel,
        grid=(batch, nblk),
        in_specs=[pl.BlockSpec((Q_BLOCK, width), lambda b, i: (b * nblk + i, CB_SBQ // 4)),
                  pl.BlockSpec((1, seq, width), lambda b, i: (b, 0, CB_SBK // 4)),
                  pl.BlockSpec((1, seq, width), lambda b, i: (b, 0, CB_SBV // 4))],
        out_specs=pl.BlockSpec((Q_BLOCK, width), lambda b, i: (b * nblk + i, 0)),
        out_shape=jax.ShapeDtypeStruct((n, width), _bf16),
        scratch_shapes=[pltpu.VMEM((8, Q_BLOCK, LANES), _bf16),
                        pltpu.VMEM((8, Q_BLOCK, 1), _f32),
                        pltpu.VMEM((8, Q_BLOCK, LANES), _f32)],
        compiler_params=_cparams(("arbitrary", "arbitrary")),
        name="stick_breaking",
    )(proj, proj3, proj3)


def _layer_norm(x, g, b):
    mu = jnp.mean(x, axis=-1, keepdims=True)
    xc = x - mu
    var = jnp.mean(xc * xc, axis=-1, keepdims=True)
    return xc * lax.rsqrt(var + LN_EPS) * g + b


def _merge_kernel(x_ref, ysb_ref, yns_ref, mg0_ref, mg1_ref, wsb_ref, wns_ref, wo_ref, g_ref, b_ref,
                  wr_ref, br_ref, h_ref, idx_ref, gw_ref):
    m0 = jax.nn.sigmoid(mg0_ref[...].astype(_f32))
    m1 = jax.nn.sigmoid(mg1_ref[...].astype(_f32))
    merged = m0 * _dot(ysb_ref[...], wsb_ref[...]) + m1 * _dot(yns_ref[...], wns_ref[...])
    pre = DEEPNORM_ALPHA * x_ref[...] + _dot(merged.astype(_bf16), wo_ref[...])
    h = _layer_norm(pre, g_ref[...], b_ref[...])
    tm = h.shape[0]
    for s in range(SLAB):
        h_ref[pl.ds(s, tm, stride=SLAB), :] = h[:, s * LANES:(s + 1) * LANES]
    logits = jnp.dot(h, wr_ref[...], preferred_element_type=_f32, precision=lax.Precision.HIGHEST) + br_ref[...]
    lane = _lane_iota((1, LANES))
    lanef = lane.astype(_f32)
    lg = jnp.where(lane < N_EXPERTS, logits, TAKEN)
    vals, idxs = [], []
    for _ in range(TOP_K):
        m = jnp.max(lg, axis=-1, keepdims=True)
        first = jnp.min(jnp.where(lg == m, lanef, float(LANES)), axis=-1, keepdims=True)
        vals.append(m)
        idxs.append(first)
        lg = jnp.where(lanef == first, TAKEN, lg)
    es = [jnp.exp(v - vals[0]) for v in vals]
    inv = 1.0 / (es[0] + es[1] + es[2] + es[3])
    idx_t = jnp.zeros(lg.shape, _f32)
    gw_t = jnp.zeros(lg.shape, _f32)
    for k in range(TOP_K):
        idx_t = jnp.where(lane == k, idxs[k], idx_t)
        gw_t = jnp.where(lane == k, es[k] * inv, gw_t)
    idx_ref[...] = idx_t[:, :TOP_K].astype(jnp.int32)
    gw_ref[...] = gw_t[:, :TOP_K]


def _merge_stage(x2, y_sb, y_nsa, proj, wsb, wns, wo, g1, b1, wr, br):
    n = x2.shape[0]
    tm = 256
    row = lambda w: pl.BlockSpec((tm, w), lambda i: (i, 0))
    full = lambda a: pl.BlockSpec(a.shape, lambda i: (0,) * a.ndim)
    return pl.pallas_call(
        _merge_kernel,
        grid=(n // tm,),
        in_specs=[row(D_MODEL), row(4 * LANES), row(4 * LANES),
                  pl.BlockSpec((tm, D_MODEL), lambda i: (i, CB_MG // 8)),
                  pl.BlockSpec((tm, D_MODEL), lambda i: (i, CB_MG // 8 + 1)),
                  full(wsb), full(wns), full(wo), full(g1), full(b1), full(wr), full(br)],
        out_specs=[pl.BlockSpec((tm * SLAB, LANES), lambda i: (i, 0)), row(TOP_K), row(TOP_K)],
        out_shape=[jax.ShapeDtypeStruct((n * SLAB, LANES), _f32),
                   jax.ShapeDtypeStruct((n, TOP_K), jnp.int32),
                   jax.ShapeDtypeStruct((n, TOP_K), _f32)],
        compiler_params=_cparams(("arbitrary",)),
        name="merge_ln1_router",
    )(x2, y_sb, y_nsa, proj, proj, wsb, wns, wo, g1, b1, wr, br)


def _prep_w_in(w):
    main = w[:, :CB_NG * LANES]
    ng = w[:, CB_NG * LANES:CB_NG * LANES + 24]
    mg = w[:, CB_NG * LANES + 24:]
    pad = jnp.zeros((w.shape[0], LANES - 12), w.dtype)
    return jnp.concatenate([main, ng[:, :12], pad, ng[:, 12:], pad, mg], axis=1).astype(_bf16)


def _rope_tables(seq):
    half = HEAD_DIM // 2
    inv_freq = ROPE_THETA ** (-jnp.arange(half, dtype=_f32) / half)
    ang = jnp.arange(seq, dtype=_f32)[:, None] * inv_freq[None, :]
    cos = jnp.cos(ang)
    sin = jnp.sin(ang)
    cos128 = jnp.concatenate([cos, cos, cos, cos], axis=1)
    sin128 = jnp.concatenate([-sin, sin, -sin, sin], axis=1)
    return cos128, sin128


def _stencil(nc):
    n = np.arange(nc)[:, None]
    j = np.arange(LANES)[None, :]
    ratio = SEL_BLOCK // CMP_STRIDE
    ok = (n >= ratio * j - 1) & (n <= ratio * j + ratio - 1) & (n < nc - 1)
    return jnp.asarray(ok.astype(np.float32).T)


def _attention_half(x, w_in, cmp_pe_k, cmp_w1_k, cmp_w2_k, cmp_pe_v, cmp_w1_v, cmp_w2_v,
                    w_proj_sb, w_proj_nsa, w_out, ln1_g, ln1_b, w_router, b_router):
    batch, seq, _ = x.shape
    n = batch * seq
    x2 = x.reshape(n, D_MODEL)
    proj = _in_proj(x2, _prep_w_in(w_in))
    cos, sin_signed = _rope_tables(seq)
    nq_r, kc_r, vc_r, k_aug, vs_t, kw_d, vw_t = _rope_stage(proj, cos, sin_signed, seq)
    kc_d, vc_t = _compress_stage(kc_r, vc_r, _compress_weights(cmp_pe_k, cmp_w1_k, cmp_w2_k),
                                 _compress_weights(cmp_pe_v, cmp_w1_v, cmp_w2_v), batch, seq)
    ypart, mbias = _nsa_cw_stage(nq_r, kc_d, vc_t, kw_d, vw_t, proj, _stencil(seq // CMP_STRIDE), batch, seq)
    y_nsa = _nsa_sel_stage(nq_r, mbias, k_aug, vs_t, proj, ypart, batch, seq)
    y_sb = _sb_stage(proj, batch, seq)
    wr = jnp.pad(w_router.astype(_f32), ((0, 0), (0, LANES - N_EXPERTS)))
    br = jnp.pad(b_router.astype(_f32), (0, LANES - N_EXPERTS)).reshape(1, LANES)
    return _merge_stage(x2, y_sb, y_nsa, proj, w_proj_sb.astype(_bf16), w_proj_nsa.astype(_bf16),
                        w_out.astype(_bf16), ln1_g.reshape(1, -1), ln1_b.reshape(1, -1), wr, br)


def _row_copy(src, src_row, dst, dst_row, sem):
    return pltpu.make_async_copy(src.at[pl.ds(src_row * SLAB, SLAB)], dst.at[pl.ds(dst_row * SLAB, SLAB)], sem)


def _dispatch_kernel(dest_ref, h_ref, zeros_ref, buf_ref, sem, *, tokens):
    del zeros_ref

    def issue(t, c):
        for k in range(TOP_K):
            _row_copy(h_ref, t, buf_ref, dest_ref[t * TOP_K + k], sem).start()
        return c

    lax.fori_loop(0, tokens, issue, 0)
    for _ in range(TOP_K):
        pltpu.make_async_copy(h_ref, buf_ref.at[pl.ds(0, tokens * SLAB)], sem).wait()


def _dispatch_stage(dest, h_slab, buf_rows):
    n = h_slab.shape[0] // SLAB
    tokens = 256
    zeros = jnp.zeros((buf_rows * SLAB, LANES), _f32)
    return pl.pallas_call(
        functools.partial(_dispatch_kernel, tokens=tokens),
        grid=(n // tokens,),
        in_specs=[pl.BlockSpec((tokens * TOP_K,), lambda i: (i,), memory_space=pltpu.SMEM),
                  pl.BlockSpec((tokens * SLAB, LANES), lambda i: (i, 0)),
                  pl.BlockSpec(memory_space=pl.ANY)],
        out_specs=pl.BlockSpec(memory_space=pl.ANY),
        out_shape=jax.ShapeDtypeStruct(zeros.shape, _f32),
        scratch_shapes=[pltpu.SemaphoreType.DMA(())],
        input_output_aliases={2: 0},
        compiler_params=pltpu.CompilerParams(dimension_semantics=("arbitrary",), has_side_effects=True),
        name="moe_dispatch",
    )(dest, h_slab, zeros)


def _slab_load(ref, rows):
    return jnp.concatenate([ref[pl.ds(s, rows, stride=SLAB), :] for s in range(SLAB)], axis=1)


def _expert_kernel(ce_ref, used_ref, x_ref, wgu_ref, bgu_ref, wd_ref, bd_ref, o_ref):
    del ce_ref
    c = pl.program_id(0)

    @pl.when(c < used_ref[0])
    def _():
        x = _slab_load(x_ref, MOE_ROWS).astype(_bf16)
        gu = _dot(x, wgu_ref[0]) + bgu_ref[0]
        gate = jnp.minimum(gu[:, :D_MODEL], SWIGLU_LIMIT)
        up = jnp.clip(gu[:, D_MODEL:], -SWIGLU_LIMIT, SWIGLU_LIMIT)
        h = gate * jax.nn.sigmoid(SWIGLU_ALPHA * gate) * (up + 1.0)
        y = _dot(h.astype(_bf16), wd_ref[0]) + bd_ref[0]
        for s in range(SLAB):
            o_ref[pl.ds(s, MOE_ROWS, stride=SLAB), :] = y[:, s * LANES:(s + 1) * LANES]

    @pl.when(c >= used_ref[0])
    def _():
        o_ref[...] = jnp.zeros(o_ref.shape, o_ref.dtype)


def _expert_stage(chunk_expert, n_used, buf, wgu, bgu, wd, bd):
    n_chunks = chunk_expert.shape[0]
    rows = MOE_ROWS * SLAB
    grid_spec = pltpu.PrefetchScalarGridSpec(
        num_scalar_prefetch=2,
        grid=(n_chunks,),
        in_specs=[pl.BlockSpec((rows, LANES), lambda c, ce, nu: (c, 0)),
                  pl.BlockSpec((1, D_MODEL, 2 * D_MODEL), lambda c, ce, nu: (ce[c], 0, 0)),
                  pl.BlockSpec((1, 1, 2 * D_MODEL), lambda c, ce, nu: (ce[c], 0, 0)),
                  pl.BlockSpec((1, D_MODEL, D_MODEL), lambda c, ce, nu: (ce[c], 0, 0)),
                  pl.BlockSpec((1, 1, D_MODEL), lambda c, ce, nu: (ce[c], 0, 0))],
        out_specs=pl.BlockSpec((rows, LANES), lambda c, ce, nu: (c, 0)))
    return pl.pallas_call(
        _expert_kernel,
        grid_spec=grid_spec,
        out_shape=jax.ShapeDtypeStruct(buf.shape, _f32),
        compiler_params=_cparams(("arbitrary",)),
        name="moe_experts",
    )(chunk_expert, n_used, buf, wgu, bgu, wd, bd)


def _combine_kernel(dest_ref, dest_next_ref, gw_ref, h_ref, eo_ref, g_ref, b_ref, o_ref, gbuf, ysl, sems, *, tokens):
    i = pl.program_id(0)
    slot = i % 2
    count = tokens * TOP_K

    def gather(idx_ref, into):
        def issue(t, c):
            for k in range(TOP_K):
                j = t * TOP_K + k
                _row_copy(eo_ref, idx_ref[j], gbuf.at[into], j, sems.at[into]).start()
            return c
        lax.fori_loop(0, tokens, issue, 0)

    @pl.when(i == 0)
    def _():
        gather(dest_ref, 0)

    @pl.when(i + 1 < pl.num_programs(0))
    def _():
        gather(dest_next_ref, 1 - slot)

    pltpu.make_async_copy(eo_ref.at[pl.ds(0, count * SLAB)], gbuf.at[slot], sems.at[slot]).wait()
    rows = gbuf.at[slot]

    def token(t, c):
        acc = DEEPNORM_ALPHA * h_ref[pl.ds(pl.multiple_of(t * SLAB, SLAB), SLAB), :]
        for k in range(TOP_K):
            j = t * TOP_K + k
            acc = acc + gw_ref[j] * rows[pl.ds(pl.multiple_of(j * SLAB, SLAB), SLAB), :]
        ysl[pl.ds(pl.multiple_of(t * SLAB, SLAB), SLAB), :] = acc
        return c

    lax.fori_loop(0, tokens, token, 0)
    o_ref[...] = _layer_norm(_slab_load(ysl, tokens), g_ref[...], b_ref[...])


def _combine_stage(dest, gate_w, h_slab, expert_out, g2, b2):
    n = h_slab.shape[0] // SLAB
    tokens = 256
    steps = n // tokens
    smem = lambda: pl.BlockSpec((tokens * TOP_K,), lambda i: (i,), memory_space=pltpu.SMEM)
    return pl.pallas_call(
        functools.partial(_combine_kernel, tokens=tokens),
        grid=(steps,),
        in_specs=[smem(),
                  pl.BlockSpec((tokens * TOP_K,), lambda i: (jnp.minimum(i + 1, steps - 1),), memory_space=pltpu.SMEM),
                  smem(),
                  pl.BlockSpec((tokens * SLAB, LANES), lambda i: (i, 0)),
                  pl.BlockSpec(memory_space=pl.ANY),
                  pl.BlockSpec((1, D_MODEL), lambda i: (0, 0)),
                  pl.BlockSpec((1, D_MODEL), lambda i: (0, 0))],
        out_specs=pl.BlockSpec((tokens, D_MODEL), lambda i: (i, 0)),
        out_shape=jax.ShapeDtypeStruct((n, D_MODEL), _f32),
        scratch_shapes=[pltpu.VMEM((2, tokens * TOP_K * SLAB, LANES), _f32),
                        pltpu.VMEM((tokens * SLAB, LANES), _f32),
                        pltpu.SemaphoreType.DMA((2,))],
        compiler_params=_cparams(("arbitrary",)),
        name="moe_combine_ln2",
    )(dest, dest, gate_w, h_slab, expert_out, g2, b2)


def _dispatch_plan(top_idx):
    m = top_idx.size
    e_flat = top_idx.reshape(m)
    onehot = (e_flat[:, None] == jnp.arange(N_EXPERTS, dtype=jnp.int32)[None, :]).astype(jnp.int32)
    csum = jnp.cumsum(onehot, axis=0)
    counts = csum[-1]
    padded = (counts + MOE_ROWS - 1) // MOE_ROWS * MOE_ROWS
    pends = jnp.cumsum(padded)
    pstarts = pends - padded
    dest = jnp.sum(onehot * (csum - 1 + pstarts[None, :]), axis=1).astype(jnp.int32)
    n_chunks = m // MOE_ROWS + N_EXPERTS
    chunk_start = jnp.arange(n_chunks, dtype=jnp.int32) * MOE_ROWS
    chunk_expert = jnp.minimum(jnp.sum((chunk_start[:, None] >= pends[None, :]).astype(jnp.int32), axis=1), N_EXPERTS - 1)
    n_used = (pends[-1] // MOE_ROWS).astype(jnp.int32).reshape(1)
    return dest, chunk_expert, n_used, n_chunks * MOE_ROWS


def _moe_half(h_slab, top_idx, gate_w, w_gate_up, b_gate_up, w_down, b_down, ln2_g, ln2_b):
    dest, chunk_expert, n_used, buf_rows = _dispatch_plan(top_idx)
    buf = _dispatch_stage(dest, h_slab, buf_rows)
    expert_out = _expert_stage(chunk_expert, n_used, buf, w_gate_up.astype(_bf16),
                               b_gate_up.reshape(N_EXPERTS, 1, -1), w_down.astype(_bf16),
                               b_down.reshape(N_EXPERTS, 1, -1))
    return _combine_stage(dest, gate_w.reshape(-1), h_slab, expert_out, ln2_g.reshape(1, -1), ln2_b.reshape(1, -1))


def kernel(x, w_in, cmp_pe_k, cmp_w1_k, cmp_w2_k, cmp_pe_v, cmp_w1_v, cmp_w2_v, w_proj_sb, w_proj_nsa, w_out,
           ln1_g, ln1_b, w_router, b_router, w_gate_up, b_gate_up, w_down, b_down, ln2_g, ln2_b):
    assert w_in.shape[0] == 1, "single-layer block"
    batch, seq, _ = x.shape
    assert seq % 512 == 0 and seq // SEL_BLOCK <= LANES and seq >= WINDOW + Q_BLOCK
    h_slab, top_idx, gate_w = _attention_half(
        x, w_in[0], cmp_pe_k[0], cmp_w1_k[0], cmp_w2_k[0], cmp_pe_v[0], cmp_w1_v[0], cmp_w2_v[0],
        w_proj_sb[0], w_proj_nsa[0], w_out[0], ln1_g[0], ln1_b[0], w_router[0], b_router[0])
    out = _moe_half(h_slab, top_idx, gate_w, w_gate_up[0], b_gate_up[0], w_down[0], b_down[0], ln2_g[0], ln2_b[0])
    return out.reshape(batch, seq, D_MODEL)
```

```python
import functools

import numpy as np
import jax
import jax.numpy as jnp
from jax import lax
from jax.experimental import pallas as pl
from jax.experimental.pallas import tpu as pltpu

D_MODEL = 1024
HEAD_DIM = 64
LANES = 128
Q_BLOCK = 128
CMP_BLOCK = 32
CMP_STRIDE = 16
SEL_BLOCK = 64
SEL_TOPK = 16
WINDOW = 512
ROPE_THETA = 10000.0
N_EXPERTS = 32
TOP_K = 4
SWIGLU_LIMIT = 7.0
SWIGLU_ALPHA = 1.702
LN_EPS = 1e-5
NEG_INF = -1e30
TAKEN = -3e38
DEEPNORM_ALPHA = 2.0 ** 0.25
QK_SCALE = HEAD_DIM ** -0.5
LOG2E = 1.4426950408889634

CB_SBQ, CB_SBK, CB_SBV, CB_NQ = 0, 4, 8, 12
CB_KC, CB_VC, CB_KS, CB_VS, CB_KW, CB_VW = 16, 17, 18, 19, 20, 21
CB_NG = 22
CB_MG = 24
PROJ_W = 40 * LANES

SB_TAIL_CUTOFF = -110.0

SEL_TK = 512
MOE_ROWS = 512
SLAB = D_MODEL // LANES
VMEM_LIMIT = 56 * 1024 * 1024

_bf16 = jnp.bfloat16
_f32 = jnp.float32


def _cparams(sem):
    return pltpu.CompilerParams(dimension_semantics=sem, vmem_limit_bytes=VMEM_LIMIT)


def _dot_t(a, b):
    return lax.dot_general(a, b, (((1,), (1,)), ((), ())), preferred_element_type=_f32)


def _dot(a, b):
    return jnp.dot(a, b, preferred_element_type=_f32)


def _lane_iota(shape):
    return lax.broadcasted_iota(jnp.int32, shape, len(shape) - 1)


def _half0(shape=(1, LANES)):
    return _lane_iota(shape) < HEAD_DIM


def _in_proj_kernel(x_ref, w_ref, o_ref):
    o_ref[...] = _dot(x_ref[...].astype(_bf16), w_ref[...]).astype(o_ref.dtype)


def _in_proj(x2, w):
    n = x2.shape[0]
    tm, tn = 512, 1280
    return pl.pallas_call(
        _in_proj_kernel,
        grid=(PROJ_W // tn, n // tm),
        in_specs=[pl.BlockSpec((tm, D_MODEL), lambda j, i: (i, 0)),
                  pl.BlockSpec((D_MODEL, tn), lambda j, i: (0, j))],
        out_specs=pl.BlockSpec((tm, tn), lambda j, i: (i, j)),
        out_shape=jax.ShapeDtypeStruct((n, PROJ_W), _bf16),
        compiler_params=_cparams(("arbitrary", "arbitrary")),
        name="in_proj",
    )(x2, w)


def _rope(x, cos, sin_signed):
    first = (_lane_iota((1, LANES)) % HEAD_DIM) < (HEAD_DIM // 2)
    swapped = jnp.where(first, pltpu.roll(x, LANES - HEAD_DIM // 2, 1), pltpu.roll(x, HEAD_DIM // 2, 1))
    return x * cos + swapped * sin_signed


def _dup(x, g):
    other = pltpu.roll(x, HEAD_DIM, 1)
    h0 = _half0()
    return jnp.where(h0, x, other) if g == 0 else jnp.where(h0, other, x)


def _rope_kernel(nq_ref, kc_ref, vc_ref, ks_ref, vs_ref, kw_ref, vw_ref, cos_ref, sin_ref,
                 nq_o, kc_o, vc_o, ka_o, vs_o, kw_o, vw_o, *, blocks_per_seq):
    ts = cos_ref.shape[0]
    cos = cos_ref[...]
    sin = sin_ref[...]
    for c in range(4):
        sl = slice(c * LANES, (c + 1) * LANES)
        nq_o[:, sl] = (_rope(nq_ref[:, sl].astype(_f32), cos, sin) * (QK_SCALE * LOG2E)).astype(_bf16)
    kc_o[...] = _rope(kc_ref[...].astype(_f32), cos, sin).astype(_bf16)
    vc_o[...] = vc_ref[...]
    ks = _rope(ks_ref[...].astype(_f32), cos, sin)
    kw = _rope(kw_ref[...].astype(_f32), cos, sin)
    vs = vs_ref[...].astype(_f32)
    vw = vw_ref[...].astype(_f32)
    pos = (pl.program_id(0) % blocks_per_seq) * ts + lax.broadcasted_iota(jnp.int32, (ts, LANES), 0)
    onehot = jnp.where(pos // SEL_BLOCK == _lane_iota((ts, LANES)), 1.0, 0.0).astype(_bf16)
    for g in range(2):
        ka_o[g, :, 0:LANES] = _dup(ks, g).astype(_bf16)
        ka_o[g, :, LANES:2 * LANES] = onehot
        vsa = jnp.where(_half0(), _dup(vs, g), 1.0)
        for c in range(ts // SEL_TK):
            vs_o[g, c] = vsa[c * SEL_TK:(c + 1) * SEL_TK, :].T.astype(_bf16)
        kw_o[g] = _dup(kw, g).astype(_bf16)
        vwd = jnp.where(_half0(), _dup(vw, g), 1.0)
        for c in range(ts // LANES):
            vw_o[g, c] = vwd[c * LANES:(c + 1) * LANES, :].T.astype(_bf16)


def _rope_stage(proj, cos, sin_signed, seq):
    n = proj.shape[0]
    ts = 512
    bps = seq // ts
    col = lambda cb: pl.BlockSpec((ts, LANES), lambda i, cb=cb: (i, cb))
    tab = pl.BlockSpec((ts, LANES), lambda i: (i % bps, 0))
    grp = lambda w: pl.BlockSpec((2, ts, w), lambda i: (0, i, 0))
    return pl.pallas_call(
        functools.partial(_rope_kernel, blocks_per_seq=bps),
        grid=(n // ts,),
        in_specs=[pl.BlockSpec((ts, 4 * LANES), lambda i: (i, CB_NQ // 4)),
                  col(CB_KC), col(CB_VC), col(CB_KS), col(CB_VS), col(CB_KW), col(CB_VW), tab, tab],
        out_specs=[pl.BlockSpec((ts, 4 * LANES), lambda i: (i, 0)),
                   pl.BlockSpec((ts, LANES), lambda i: (i, 0)),
                   pl.BlockSpec((ts, LANES), lambda i: (i, 0)),
                   grp(2 * LANES),
                   pl.BlockSpec((2, ts // SEL_TK, LANES, SEL_TK), lambda i: (0, i, 0, 0)),
                   grp(LANES),
                   pl.BlockSpec((2, ts // LANES, LANES, LANES), lambda i: (0, i, 0, 0))],
        out_shape=[jax.ShapeDtypeStruct((n, 4 * LANES), _bf16),
                   jax.ShapeDtypeStruct((n, LANES), _bf16),
                   jax.ShapeDtypeStruct((n, LANES), _bf16),
                   jax.ShapeDtypeStruct((2, n, 2 * LANES), _bf16),
                   jax.ShapeDtypeStruct((2, n // SEL_TK, LANES, SEL_TK), _bf16),
                   jax.ShapeDtypeStruct((2, n, LANES), _bf16),
                   jax.ShapeDtypeStruct((2, n // LANES, LANES, LANES), _bf16)],
        compiler_params=_cparams(("arbitrary",)),
        name="rope_layout",
    )(proj, proj, proj, proj, proj, proj, proj, cos, sin_signed)


def _gelu_tanh(x):
    return 0.5 * x * (1.0 + jnp.tanh(0.7978845608028654 * (x + 0.044715 * (x * x * x))))


def _compress_one(x_ref, pe_t, pe_b, w_t, w_b, w2, out_ref, transposed):
    x = x_ref[0].astype(_f32)
    a = _dot((x + pe_t[...]).astype(_bf16), w_t[...])
    b = _dot((x + pe_b[...]).astype(_bf16), w_b[...])
    nc = a.shape[0]
    pre = a + pltpu.roll(b, nc - 1, 0)
    y = _dot(_gelu_tanh(pre).astype(_bf16), w2[...])
    for g in range(2):
        d = _dup(y, g)
        out_ref[0, g] = (d.T if transposed else d).astype(_bf16)


def _compress_kernel(k_ref, v_ref, kpt, kpb, kwt, kwb, kw2, vpt, vpb, vwt, vwb, vw2, ko_ref, vo_ref):
    _compress_one(k_ref, kpt, kpb, kwt, kwb, kw2, ko_ref, False)
    _compress_one(v_ref, vpt, vpb, vwt, vwb, vw2, vo_ref, True)


def _compress_weights(pe, w1, w2):
    half = CMP_BLOCK // 2
    eye = jnp.eye(2, dtype=_f32)
    outs = []
    for part in range(2):
        w = w1[part * half * HEAD_DIM:(part + 1) * half * HEAD_DIM].reshape(half, HEAD_DIM, HEAD_DIM)
        wbd = (w[:, None, :, None, :] * eye[None, :, None, :, None]).reshape(half * 2 * HEAD_DIM, 2 * HEAD_DIM)
        p = jnp.broadcast_to(pe[part * half:(part + 1) * half, None, :], (half, 2, HEAD_DIM)).reshape(1, -1)
        outs.append((p.astype(_f32), wbd.astype(_bf16)))
    w2bd = (w2[None, :, None, :] * eye[:, None, :, None]).reshape(2 * HEAD_DIM, 2 * HEAD_DIM).astype(_bf16)
    (pt, wt), (pb, wb) = outs
    return pt, pb, wt, wb, w2bd


def _compress_stage(kc_r, vc_r, kparams, vparams, batch, seq):
    nc = seq // CMP_STRIDE
    width = CMP_STRIDE * LANES
    xs = pl.BlockSpec((1, nc, width), lambda b: (b, 0, 0))
    full = lambda a: pl.BlockSpec(a.shape, lambda b: (0,) * a.ndim)
    out = pl.BlockSpec((1, 2, nc, LANES), lambda b: (b, 0, 0, 0))
    weights = list(kparams) + list(vparams)
    return pl.pallas_call(
        _compress_kernel,
        grid=(batch,),
        in_specs=[xs, xs] + [full(a) for a in weights],
        out_specs=[out, pl.BlockSpec((1, 2, LANES, nc), lambda b: (b, 0, 0, 0))],
        out_shape=[jax.ShapeDtypeStruct((batch, 2, nc, LANES), _bf16),
                   jax.ShapeDtypeStruct((batch, 2, LANES, nc), _bf16)],
        compiler_params=_cparams(("arbitrary",)),
        name="compress",
    )(kc_r.reshape(batch, nc, width), vc_r.reshape(batch, nc, width), *weights)


def _head_q(q_ref, r):
    q2 = q_ref[:, (r // 2) * LANES:(r // 2 + 1) * LANES]
    keep = _half0() if r % 2 == 0 else jnp.logical_not(_half0())
    return jnp.where(keep, q2, jnp.zeros_like(q2))


def _softmax_over_rows(s):
    m = jnp.max(s, axis=0, keepdims=True)
    e = jnp.exp2(s - m)
    l = jnp.sum(e, axis=0, keepdims=True)
    return e * jnp.where(m > 0.5 * NEG_INF, 1.0 / l, 0.0)


def _pair(even, odd):
    return jnp.where(_half0(), even, odd)


def _nsa_cw_kernel(q_ref, kc_ref, vct_ref, kw_ref, vwt_ref, ng_ref, stt_ref, yp_ref, mb_ref):
    t0 = pl.program_id(2) * Q_BLOCK
    qpos = t0 + _lane_iota((1, Q_BLOCK))
    gates = jax.nn.sigmoid(ng_ref[...].astype(_f32).T[0:16, :])
    kc = kc_ref[0, 0]
    vct = vct_ref[0, 0]
    nc = kc.shape[0]
    cend = lax.broadcasted_iota(jnp.int32, (nc, 1), 0) * CMP_STRIDE + (CMP_BLOCK - 1)
    cmask = cend <= qpos
    start = pl.multiple_of(jnp.maximum(t0 - WINDOW, 0), Q_BLOCK)
    wlen = WINDOW + Q_BLOCK
    kwin = kw_ref[0, 0, pl.ds(start, wlen), :]
    kpos = start + lax.broadcasted_iota(jnp.int32, (wlen, 1), 0)
    wmask = (kpos <= qpos) & (qpos - kpos < WINDOW)
    blk0 = start // Q_BLOCK

    qs = [_head_q(q_ref, r) for r in range(4)]
    s_cmp = [_dot_t(kc, q) for q in qs]
    s_win = [_dot_t(kwin, q) for q in qs]
    p_cmp = [_softmax_over_rows(jnp.where(cmask, s, NEG_INF)) for s in s_cmp]
    imp = (p_cmp[0] + p_cmp[1]) + (p_cmp[2] + p_cmp[3])
    e_win = []
    for s in s_win:
        s = jnp.where(wmask, s, NEG_INF)
        e_win.append(jnp.exp2(s - jnp.max(s, axis=0, keepdims=True)).astype(_bf16))
    vwt = jnp.concatenate([vwt_ref[0, 0, blk0 + c] for c in range(wlen // Q_BLOCK)], axis=1)
    yts = []
    for r in range(4):
        o_cmp = _dot(vct, p_cmp[r].astype(_bf16))[0:HEAD_DIM]
        win = _dot(vwt, e_win[r])
        o_win = win[0:HEAD_DIM] * (1.0 / win[HEAD_DIM:2 * HEAD_DIM])
        yts.append(gates[3 * r:3 * r + 1] * o_cmp + gates[3 * r + 2:3 * r + 3] * o_win)
    yp_ref[:, 0:LANES] = jnp.concatenate(yts[0:2], axis=0).T
    yp_ref[:, LANES:2 * LANES] = jnp.concatenate(yts[2:4], axis=0).T

    p_slc = jnp.dot(stt_ref[...], imp, preferred_element_type=_f32, precision=lax.Precision.HIGHEST)
    selj = lax.broadcasted_iota(jnp.int32, (LANES, 1), 0)
    blk_t = qpos // SEL_BLOCK
    forced = (selj == 0) | (selj == blk_t) | (selj == blk_t - 1)
    score = jnp.where(forced, TAKEN, jnp.where(selj <= blk_t, p_slc, NEG_INF))
    seljf = selj.astype(_f32)
    picked = forced
    for _ in range(SEL_TOPK - 3):
        m = jnp.max(score, axis=0, keepdims=True)
        first = jnp.min(jnp.where(score == m, seljf, float(LANES)), axis=0, keepdims=True)
        hit = seljf == first
        picked = picked | hit
        score = jnp.where(hit, TAKEN, score)
    mb_ref[0] = jnp.where(picked, 0.0, NEG_INF).T.astype(_bf16)


def _nsa_cw_stage(nq_r, kc_d, vc_t, kw_d, vw_t, proj, stencil_t, batch, seq):
    n = nq_r.shape[0]
    nblk = seq // Q_BLOCK
    nc = seq // CMP_STRIDE
    qrow = lambda b, g, i: b * nblk + i
    return pl.pallas_call(
        _nsa_cw_kernel,
        grid=(batch, 2, nblk),
        in_specs=[pl.BlockSpec((Q_BLOCK, 2 * LANES), lambda b, g, i: (qrow(b, g, i), g)),
                  pl.BlockSpec((1, 1, nc, LANES), lambda b, g, i: (b, g, 0, 0)),
                  pl.BlockSpec((1, 1, LANES, nc), lambda b, g, i: (b, g, 0, 0)),
                  pl.BlockSpec((1, 1, seq, LANES), lambda b, g, i: (g, b, 0, 0)),
                  pl.BlockSpec((1, 1, nblk, LANES, LANES), lambda b, g, i: (g, b, 0, 0, 0)),
                  pl.BlockSpec((Q_BLOCK, LANES), lambda b, g, i: (qrow(b, g, i), CB_NG + g)),
                  pl.BlockSpec((LANES, nc), lambda b, g, i: (0, 0))],
        out_specs=[pl.BlockSpec((Q_BLOCK, 2 * LANES), lambda b, g, i: (qrow(b, g, i), g)),
                   pl.BlockSpec((1, Q_BLOCK, LANES), lambda b, g, i: (g, qrow(b, g, i), 0))],
        out_shape=[jax.ShapeDtypeStruct((n, 4 * LANES), _f32),
                   jax.ShapeDtypeStruct((2, n, LANES), _bf16)],
        compiler_params=_cparams(("arbitrary", "arbitrary", "arbitrary")),
        name="nsa_cmp_win_select",
    )(nq_r, kc_d, vc_t, kw_d.reshape(2, batch, seq, LANES), vw_t.reshape(2, batch, nblk, LANES, LANES), proj, stencil_t)


def _nsa_sel_kernel(q_ref, mb_ref, ka_ref, vat_ref, ng_ref, yp_ref, o_ref, qs_ref, m_ref, acc_ref, s_ref, p_ref, alpha_ref):
    tk = SEL_TK
    t0 = pl.program_id(2) * Q_BLOCK
    mb = mb_ref[0]
    for r in range(4):
        qs_ref[r * Q_BLOCK:(r + 1) * Q_BLOCK, 0:LANES] = _head_q(q_ref, r)
        qs_ref[r * Q_BLOCK:(r + 1) * Q_BLOCK, LANES:2 * LANES] = mb
    m_ref[...] = jnp.full(m_ref.shape, NEG_INF, _f32)
    acc_ref[...] = jnp.zeros(acc_ref.shape, _f32)
    p_ref[...] = jnp.zeros(p_ref.shape, _bf16)
    alpha_ref[...] = jnp.ones(alpha_ref.shape, _f32)
    qpos = t0 + _lane_iota((1, 4 * Q_BLOCK)) % Q_BLOCK

    def scores(kt):
        return _dot_t(ka_ref[0, 0, pl.ds(pl.multiple_of(kt * tk, tk), tk), :], qs_ref[...])

    def softmax_step(s):
        m_old = m_ref[...]
        m_new = jnp.maximum(m_old, jnp.max(s, axis=0, keepdims=True))
        m_ref[...] = m_new
        return jnp.exp2(s - m_new).astype(_bf16), jnp.exp2(m_old - m_new)

    def accumulate(kt, alpha, p):
        acc_ref[...] = alpha * acc_ref[...] + _dot(vat_ref[0, 0, kt], p)

    def trip(kt, carry):
        accumulate(jnp.maximum(kt - 1, 0), alpha_ref[...], p_ref[...])
        s = s_ref[...]
        s_ref[...] = scores(kt + 1)
        p, alpha = softmax_step(s)
        p_ref[...] = p
        alpha_ref[...] = alpha
        return carry

    n_full = t0 // tk
    s_ref[...] = scores(0)

    def two_trips(kp, carry):
        trip(2 * kp, carry)
        return trip(2 * kp + 1, carry)

    lax.fori_loop(0, n_full // 2, two_trips, 0)

    @pl.when(n_full % 2 == 1)
    def _():
        trip(n_full - 1, 0)

    accumulate(jnp.maximum(n_full - 1, 0), alpha_ref[...], p_ref[...])
    kpos = n_full * tk + lax.broadcasted_iota(jnp.int32, (tk, 1), 0)
    p, alpha = softmax_step(jnp.where(kpos <= qpos, s_ref[...], NEG_INF))
    accumulate(n_full, alpha, p)
    acc = acc_ref[...]
    o = acc[0:HEAD_DIM] * (1.0 / acc[HEAD_DIM:2 * HEAD_DIM])
    gates = jax.nn.sigmoid(ng_ref[...].astype(_f32).T[0:16, :])
    ys = [gates[3 * r + 1:3 * r + 2] * o[:, r * Q_BLOCK:(r + 1) * Q_BLOCK] for r in range(4)]
    o_ref[:, 0:LANES] = (yp_ref[:, 0:LANES] + jnp.concatenate(ys[0:2], axis=0).T).astype(o_ref.dtype)
    o_ref[:, LANES:2 * LANES] = (yp_ref[:, LANES:2 * LANES] + jnp.concatenate(ys[2:4], axis=0).T).astype(o_ref.dtype)


def _nsa_sel_stage(nq_r, mbias, k_aug, vs_t, proj, ypart, batch, seq):
    n = nq_r.shape[0]
    nblk = seq // Q_BLOCK
    qrow = lambda b, g, i: b * nblk + i
    return pl.pallas_call(
        _nsa_sel_kernel,
        grid=(batch, 2, nblk),
        in_specs=[pl.BlockSpec((Q_BLOCK, 2 * LANES), lambda b, g, i: (qrow(b, g, i), g)),
                  pl.BlockSpec((1, Q_BLOCK, LANES), lambda b, g, i: (g, qrow(b, g, i), 0)),
                  pl.BlockSpec((1, 1, seq, 2 * LANES), lambda b, g, i: (g, b, 0, 0)),
                  pl.BlockSpec((1, 1, seq // SEL_TK, LANES, SEL_TK), lambda b, g, i: (g, b, 0, 0, 0)),
                  pl.BlockSpec((Q_BLOCK, LANES), lambda b, g, i: (qrow(b, g, i), CB_NG + g)),
                  pl.BlockSpec((Q_BLOCK, 2 * LANES), lambda b, g, i: (qrow(b, g, i), g))],
        out_specs=pl.BlockSpec((Q_BLOCK, 2 * LANES), lambda b, g, i: (qrow(b, g, i), g)),
        out_shape=jax.ShapeDtypeStruct((n, 4 * LANES), _bf16),
        scratch_shapes=[pltpu.VMEM((4 * Q_BLOCK, 2 * LANES), _bf16),
                        pltpu.VMEM((1, 4 * Q_BLOCK), _f32),
                        pltpu.VMEM((LANES, 4 * Q_BLOCK), _f32),
                        pltpu.VMEM((SEL_TK, 4 * Q_BLOCK), _f32),
                        pltpu.VMEM((SEL_TK, 4 * Q_BLOCK), _bf16),
                        pltpu.VMEM((1, 4 * Q_BLOCK), _f32)],
        compiler_params=_cparams(("arbitrary", "arbitrary", "arbitrary")),
        name="nsa_selected",
    )(nq_r, mbias, k_aug.reshape(2, batch, seq, 2 * LANES), vs_t.reshape(2, batch, seq // SEL_TK, LANES, SEL_TK), proj, ypart)


def _sb_kernel(q_ref, k_ref, v_ref, o_ref, qs_ref, tail_ref, acc_ref):
    i = pl.program_id(1)
    h0 = _half0()
    heads = 2 * (q_ref.shape[1] // LANES)
    for h in range(heads):
        q = q_ref[:, (h // 2) * LANES:(h // 2 + 1) * LANES]
        keep = h0 if h % 2 == 0 else jnp.logical_not(h0)
        qs_ref[h] = jnp.where(keep, q, jnp.zeros_like(q)) * QK_SCALE
    tail_ref[...] = jnp.zeros(tail_ref.shape, _f32)
    acc_ref[...] = jnp.zeros(acc_ref.shape, _f32)
    rloc = lax.broadcasted_iota(jnp.int32, (Q_BLOCK, Q_BLOCK), 0)
    cloc = lax.broadcasted_iota(jnp.int32, (Q_BLOCK, Q_BLOCK), 1)
    later = jnp.where(rloc > cloc, 1.0, 0.0).astype(_bf16)

    def cond(c):
        j, worst_tail = c
        return (j >= 0) & (worst_tail > SB_TAIL_CUTOFF)

    def body(c):
        j, _ = c
        k0 = pl.multiple_of(j * Q_BLOCK, Q_BLOCK)
        past = (cloc + (j - i) * Q_BLOCK) < rloc
        cols = [slice((h // 2) * LANES, (h // 2 + 1) * LANES) for h in range(heads)]
        zs = [_dot_t(qs_ref[h], k_ref[0, pl.ds(k0, Q_BLOCK), cols[h]]) for h in range(heads)]
        log_beta, log_keep = [], []
        for z in zs:
            sp = jnp.maximum(z, 0.0) + jnp.log1p(jnp.exp(-jnp.abs(z)))
            log_beta.append(z - sp)
            log_keep.append(jnp.where(past, -sp, 0.0))
        inner = []
        for lk in log_keep:
            hi = lk.astype(_bf16)
            lo = (lk - hi.astype(_f32)).astype(_bf16)
            inner.append(_dot(hi, later) + _dot(lo, later))
        probs = [jnp.where(past, jnp.exp(log_beta[h] + inner[h] + tail_ref[h]), 0.0).astype(_bf16)
                 for h in range(heads)]
        worst = jnp.full((Q_BLOCK, 1), -jnp.inf, _f32)
        for h in range(heads):
            acc_ref[h] = acc_ref[h] + _dot(probs[h], v_ref[0, pl.ds(k0, Q_BLOCK), cols[h]])
            tail = tail_ref[h] + jnp.sum(log_keep[h], axis=-1, keepdims=True)
            tail_ref[h] = tail
            worst = jnp.maximum(worst, tail)
        return j - 1, jnp.max(worst)

    lax.while_loop(cond, body, (i, jnp.float32(0.0)))
    for p in range(heads // 2):
        o_ref[:, p * LANES:(p + 1) * LANES] = jnp.where(h0, acc_ref[2 * p], acc_ref[2 * p + 1]).astype(o_ref.dtype)


def _sb_stage(proj, batch, seq):
    n = proj.shape[0]
    nblk = seq // Q_BLOCK
    width = 4 * LANES
    proj3 = proj.reshape(batch, seq, PROJ_W)
    return pl.pallas_call(
        _sb_kernel,
        grid=(batch, nblk),
        in_specs=[pl.BlockSpec((Q_BLOCK, width), lambda b, i: (b * nblk + i, CB_SBQ // 4)),
                  pl.BlockSpec((1, seq, width), lambda b, i: (b, 0, CB_SBK // 4)),
                  pl.BlockSpec((1, seq, width), lambda b, i: (b, 0, CB_SBV // 4))],
        out_specs=pl.BlockSpec((Q_BLOCK, width), lambda b, i: (b * nblk + i, 0)),
        out_shape=jax.ShapeDtypeStruct((n, width), _bf16),
        scratch_shapes=[pltpu.VMEM((8, Q_BLOCK, LANES), _bf16),
                        pltpu.VMEM((8, Q_BLOCK, 1), _f32),
                        pltpu.VMEM((8, Q_BLOCK, LANES), _f32)],
        compiler_params=_cparams(("arbitrary", "arbitrary")),
        name="stick_breaking",
    )(proj, proj3, proj3)


def _layer_norm(x, g, b):
    mu = jnp.mean(x, axis=-1, keepdims=True)
    xc = x - mu
    var = jnp.mean(xc * xc, axis=-1, keepdims=True)
    return xc * lax.rsqrt(var + LN_EPS) * g + b


def _merge_kernel(x_ref, ysb_ref, yns_ref, mg0_ref, mg1_ref, wsb_ref, wns_ref, wo_ref, g_ref, b_ref,
                  wr_ref, br_ref, h_ref, idx_ref, gw_ref):
    m0 = jax.nn.sigmoid(mg0_ref[...].astype(_f32))
    m1 = jax.nn.sigmoid(mg1_ref[...].astype(_f32))
    merged = m0 * _dot(ysb_ref[...], wsb_ref[...]) + m1 * _dot(yns_ref[...], wns_ref[...])
    pre = DEEPNORM_ALPHA * x_ref[...] + _dot(merged.astype(_bf16), wo_ref[...])
    h = _layer_norm(pre, g_ref[...], b_ref[...])
    tm = h.shape[0]
    for s in range(SLAB):
        h_ref[pl.ds(s, tm, stride=SLAB), :] = h[:, s * LANES:(s + 1) * LANES]
    h_hi = h.astype(_bf16)
    h_lo = (h - h_hi.astype(_f32)).astype(_bf16)
    logits = (_dot(h_hi, wr_ref[0]) + (_dot(h_hi, wr_ref[1]) + _dot(h_lo, wr_ref[0]))) + br_ref[...]
    lane = _lane_iota((1, LANES))
    lanef = lane.astype(_f32)
    lg = jnp.where(lane < N_EXPERTS, logits, TAKEN)
    vals, idxs = [], []
    for _ in range(TOP_K):
        m = jnp.max(lg, axis=-1, keepdims=True)
        first = jnp.min(jnp.where(lg == m, lanef, float(LANES)), axis=-1, keepdims=True)
        vals.append(m)
        idxs.append(first)
        lg = jnp.where(lanef == first, TAKEN, lg)
    es = [jnp.exp(v - vals[0]) for v in vals]
    inv = 1.0 / (es[0] + es[1] + es[2] + es[3])
    idx_t = jnp.zeros(lg.shape, _f32)
    gw_t = jnp.zeros(lg.shape, _f32)
    for k in range(TOP_K):
        idx_t = jnp.where(lane == k, idxs[k], idx_t)
        gw_t = jnp.where(lane == k, es[k] * inv, gw_t)
    idx_ref[...] = idx_t[:, :TOP_K].astype(jnp.int32)
    gw_ref[...] = gw_t[:, :TOP_K]


def _merge_stage(x2, y_sb, y_nsa, proj, wsb, wns, wo, g1, b1, wr, br):
    n = x2.shape[0]
    tm = 512
    row = lambda w: pl.BlockSpec((tm, w), lambda i: (i, 0))
    full = lambda a: pl.BlockSpec(a.shape, lambda i: (0,) * a.ndim)
    return pl.pallas_call(
        _merge_kernel,
        grid=(n // tm,),
        in_specs=[row(D_MODEL), row(4 * LANES), row(4 * LANES),
                  pl.BlockSpec((tm, D_MODEL), lambda i: (i, CB_MG // 8)),
                  pl.BlockSpec((tm, D_MODEL), lambda i: (i, CB_MG // 8 + 1)),
                  full(wsb), full(wns), full(wo), full(g1), full(b1), full(wr), full(br)],
        out_specs=[pl.BlockSpec((tm * SLAB, LANES), lambda i: (i, 0)), row(TOP_K), row(TOP_K)],
        out_shape=[jax.ShapeDtypeStruct((n * SLAB, LANES), _f32),
                   jax.ShapeDtypeStruct((n, TOP_K), jnp.int32),
                   jax.ShapeDtypeStruct((n, TOP_K), _f32)],
        compiler_params=_cparams(("arbitrary",)),
        name="merge_ln1_router",
    )(x2, y_sb, y_nsa, proj, proj, wsb, wns, wo, g1, b1, wr, br)


def _prep_w_in(w):
    main = w[:, :CB_NG * LANES]
    ng = w[:, CB_NG * LANES:CB_NG * LANES + 24]
    mg = w[:, CB_NG * LANES + 24:]
    pad = jnp.zeros((w.shape[0], LANES - 12), w.dtype)
    return jnp.concatenate([main, ng[:, :12], pad, ng[:, 12:], pad, mg], axis=1).astype(_bf16)


def _rope_tables(seq):
    half = HEAD_DIM // 2
    inv_freq = ROPE_THETA ** (-jnp.arange(half, dtype=_f32) / half)
    ang = jnp.arange(seq, dtype=_f32)[:, None] * inv_freq[None, :]
    cos = jnp.cos(ang)
    sin = jnp.sin(ang)
    cos128 = jnp.concatenate([cos, cos, cos, cos], axis=1)
    sin128 = jnp.concatenate([-sin, sin, -sin, sin], axis=1)
    return cos128, sin128


def _stencil(nc):
    n = np.arange(nc)[:, None]
    j = np.arange(LANES)[None, :]
    ratio = SEL_BLOCK // CMP_STRIDE
    ok = (n >= ratio * j - 1) & (n <= ratio * j + ratio - 1) & (n < nc - 1)
    return jnp.asarray(ok.astype(np.float32).T)


def _attention_half(x, w_in, cmp_pe_k, cmp_w1_k, cmp_w2_k, cmp_pe_v, cmp_w1_v, cmp_w2_v,
                    w_proj_sb, w_proj_nsa, w_out, ln1_g, ln1_b, w_router, b_router):
    batch, seq, _ = x.shape
    n = batch * seq
    x2 = x.reshape(n, D_MODEL)
    proj = _in_proj(x2, _prep_w_in(w_in))
    cos, sin_signed = _rope_tables(seq)
    nq_r, kc_r, vc_r, k_aug, vs_t, kw_d, vw_t = _rope_stage(proj, cos, sin_signed, seq)
    kc_d, vc_t = _compress_stage(kc_r, vc_r, _compress_weights(cmp_pe_k, cmp_w1_k, cmp_w2_k),
                                 _compress_weights(cmp_pe_v, cmp_w1_v, cmp_w2_v), batch, seq)
    ypart, mbias = _nsa_cw_stage(nq_r, kc_d, vc_t, kw_d, vw_t, proj, _stencil(seq // CMP_STRIDE), batch, seq)
    y_nsa = _nsa_sel_stage(nq_r, mbias, k_aug, vs_t, proj, ypart, batch, seq)
    y_sb = _sb_stage(proj, batch, seq)
    wr = jnp.pad(w_router.astype(_f32), ((0, 0), (0, LANES - N_EXPERTS)))
    wr_hi = wr.astype(_bf16)
    wr = jnp.stack([wr_hi, (wr - wr_hi.astype(_f32)).astype(_bf16)])
    br = jnp.pad(b_router.astype(_f32), (0, LANES - N_EXPERTS)).reshape(1, LANES)
    return _merge_stage(x2, y_sb, y_nsa, proj, w_proj_sb.astype(_bf16), w_proj_nsa.astype(_bf16),
                        w_out.astype(_bf16), ln1_g.reshape(1, -1), ln1_b.reshape(1, -1), wr, br)


def _row_copy(src, src_row, dst, dst_row, sem):
    return pltpu.make_async_copy(src.at[pl.ds(src_row * SLAB, SLAB)], dst.at[pl.ds(dst_row * SLAB, SLAB)], sem)


def _dispatch_kernel(dest_ref, h_ref, zeros_ref, buf_ref, sem, *, tokens):
    del zeros_ref

    def issue(t, c):
        for k in range(TOP_K):
            _row_copy(h_ref, t, buf_ref, dest_ref[t * TOP_K + k], sem).start()
        return c

    lax.fori_loop(0, tokens, issue, 0)
    for _ in range(TOP_K):
        pltpu.make_async_copy(h_ref, buf_ref.at[pl.ds(0, tokens * SLAB)], sem).wait()


def _dispatch_stage(dest, h_slab, buf_rows):
    n = h_slab.shape[0] // SLAB
    tokens = 256
    zeros = jnp.zeros((buf_rows * SLAB, LANES), _f32)
    return pl.pallas_call(
        functools.partial(_dispatch_kernel, tokens=tokens),
        grid=(n // tokens,),
        in_specs=[pl.BlockSpec((tokens * TOP_K,), lambda i: (i,), memory_space=pltpu.SMEM),
                  pl.BlockSpec((tokens * SLAB, LANES), lambda i: (i, 0)),
                  pl.BlockSpec(memory_space=pl.ANY)],
        out_specs=pl.BlockSpec(memory_space=pl.ANY),
        out_shape=jax.ShapeDtypeStruct(zeros.shape, _f32),
        scratch_shapes=[pltpu.SemaphoreType.DMA(())],
        input_output_aliases={2: 0},
        compiler_params=pltpu.CompilerParams(dimension_semantics=("arbitrary",), has_side_effects=True),
        name="moe_dispatch",
    )(dest, h_slab, zeros)


def _slab_load(ref, rows):
    return jnp.concatenate([ref[pl.ds(s, rows, stride=SLAB), :] for s in range(SLAB)], axis=1)


def _expert_kernel(ce_ref, used_ref, x_ref, wgu_ref, bgu_ref, wd_ref, bd_ref, o_ref):
    del ce_ref
    c = pl.program_id(0)

    @pl.when(c < used_ref[0])
    def _():
        x = _slab_load(x_ref, MOE_ROWS).astype(_bf16)
        gu = _dot(x, wgu_ref[0]) + bgu_ref[0]
        gate = jnp.minimum(gu[:, :D_MODEL], SWIGLU_LIMIT)
        up = jnp.clip(gu[:, D_MODEL:], -SWIGLU_LIMIT, SWIGLU_LIMIT)
        h = gate * jax.nn.sigmoid(SWIGLU_ALPHA * gate) * (up + 1.0)
        y = _dot(h.astype(_bf16), wd_ref[0]) + bd_ref[0]
        for s in range(SLAB):
            o_ref[pl.ds(s, MOE_ROWS, stride=SLAB), :] = y[:, s * LANES:(s + 1) * LANES]

    @pl.when(c >= used_ref[0])
    def _():
        o_ref[...] = jnp.zeros(o_ref.shape, o_ref.dtype)


def _expert_stage(chunk_expert, n_used, buf, wgu, bgu, wd, bd):
    n_chunks = chunk_expert.shape[0]
    rows = MOE_ROWS * SLAB
    grid_spec = pltpu.PrefetchScalarGridSpec(
        num_scalar_prefetch=2,
        grid=(n_chunks,),
        in_specs=[pl.BlockSpec((rows, LANES), lambda c, ce, nu: (c, 0)),
                  pl.BlockSpec((1, D_MODEL, 2 * D_MODEL), lambda c, ce, nu: (ce[c], 0, 0)),
                  pl.BlockSpec((1, 1, 2 * D_MODEL), lambda c, ce, nu: (ce[c], 0, 0)),
                  pl.BlockSpec((1, D_MODEL, D_MODEL), lambda c, ce, nu: (ce[c], 0, 0)),
                  pl.BlockSpec((1, 1, D_MODEL), lambda c, ce, nu: (ce[c], 0, 0))],
        out_specs=pl.BlockSpec((rows, LANES), lambda c, ce, nu: (c, 0)))
    return pl.pallas_call(
        _expert_kernel,
        grid_spec=grid_spec,
        out_shape=jax.ShapeDtypeStruct(buf.shape, _f32),
        compiler_params=_cparams(("arbitrary",)),
        name="moe_experts",
    )(chunk_expert, n_used, buf, wgu, bgu, wd, bd)


def _combine_kernel(dest_ref, dest_next_ref, gw_ref, h_ref, eo_ref, g_ref, b_ref, o_ref, gbuf, ysl, sems, *, tokens):
    i = pl.program_id(0)
    slot = i % 2
    count = tokens * TOP_K

    def gather(idx_ref, into):
        def issue(t, c):
            for k in range(TOP_K):
                j = t * TOP_K + k
                _row_copy(eo_ref, idx_ref[j], gbuf.at[into], j, sems.at[into]).start()
            return c
        lax.fori_loop(0, tokens, issue, 0)

    @pl.when(i == 0)
    def _():
        gather(dest_ref, 0)

    @pl.when(i + 1 < pl.num_programs(0))
    def _():
        gather(dest_next_ref, 1 - slot)

    pltpu.make_async_copy(eo_ref.at[pl.ds(0, count * SLAB)], gbuf.at[slot], sems.at[slot]).wait()
    rows = gbuf.at[slot]

    def token(t, c):
        acc = DEEPNORM_ALPHA * h_ref[pl.ds(pl.multiple_of(t * SLAB, SLAB), SLAB), :]
        for k in range(TOP_K):
            j = t * TOP_K + k
            acc = acc + gw_ref[j] * rows[pl.ds(pl.multiple_of(j * SLAB, SLAB), SLAB), :]
        ysl[pl.ds(pl.multiple_of(t * SLAB, SLAB), SLAB), :] = acc
        return c

    lax.fori_loop(0, tokens, token, 0)
    o_ref[...] = _layer_norm(_slab_load(ysl, tokens), g_ref[...], b_ref[...])


def _combine_stage(dest, gate_w, h_slab, expert_out, g2, b2):
    n = h_slab.shape[0] // SLAB
    tokens = 256
    steps = n // tokens
    smem = lambda: pl.BlockSpec((tokens * TOP_K,), lambda i: (i,), memory_space=pltpu.SMEM)
    return pl.pallas_call(
        functools.partial(_combine_kernel, tokens=tokens),
        grid=(steps,),
        in_specs=[smem(),
                  pl.BlockSpec((tokens * TOP_K,), lambda i: (jnp.minimum(i + 1, steps - 1),), memory_space=pltpu.SMEM),
                  smem(),
                  pl.BlockSpec((tokens * SLAB, LANES), lambda i: (i, 0)),
                  pl.BlockSpec(memory_space=pl.ANY),
                  pl.BlockSpec((1, D_MODEL), lambda i: (0, 0)),
                  pl.BlockSpec((1, D_MODEL), lambda i: (0, 0))],
        out_specs=pl.BlockSpec((tokens, D_MODEL), lambda i: (i, 0)),
        out_shape=jax.ShapeDtypeStruct((n, D_MODEL), _f32),
        scratch_shapes=[pltpu.VMEM((2, tokens * TOP_K * SLAB, LANES), _f32),
                        pltpu.VMEM((tokens * SLAB, LANES), _f32),
                        pltpu.SemaphoreType.DMA((2,))],
        compiler_params=_cparams(("arbitrary",)),
        name="moe_combine_ln2",
    )(dest, dest, gate_w, h_slab, expert_out, g2, b2)


def _dispatch_plan(top_idx):
    m = top_idx.size
    e_flat = top_idx.reshape(m)
    onehot = (e_flat[:, None] == jnp.arange(N_EXPERTS, dtype=jnp.int32)[None, :]).astype(jnp.int32)
    csum = jnp.cumsum(onehot, axis=0)
    counts = csum[-1]
    padded = (counts + MOE_ROWS - 1) // MOE_ROWS * MOE_ROWS
    pends = jnp.cumsum(padded)
    pstarts = pends - padded
    dest = jnp.sum(onehot * (csum - 1 + pstarts[None, :]), axis=1).astype(jnp.int32)
    n_chunks = m // MOE_ROWS + N_EXPERTS
    chunk_start = jnp.arange(n_chunks, dtype=jnp.int32) * MOE_ROWS
    chunk_expert = jnp.minimum(jnp.sum((chunk_start[:, None] >= pends[None, :]).astype(jnp.int32), axis=1), N_EXPERTS - 1)
    n_used = (pends[-1] // MOE_ROWS).astype(jnp.int32).reshape(1)
    return dest, chunk_expert, n_used, n_chunks * MOE_ROWS


def _moe_half(h_slab, top_idx, gate_w, w_gate_up, b_gate_up, w_down, b_down, ln2_g, ln2_b):
    dest, chunk_expert, n_used, buf_rows = _dispatch_plan(top_idx)
    buf = _dispatch_stage(dest, h_slab, buf_rows)
    expert_out = _expert_stage(chunk_expert, n_used, buf, w_gate_up.astype(_bf16),
                               b_gate_up.reshape(N_EXPERTS, 1, -1), w_down.astype(_bf16),
                               b_down.reshape(N_EXPERTS, 1, -1))
    return _combine_stage(dest, gate_w.reshape(-1), h_slab, expert_out, ln2_g.reshape(1, -1), ln2_b.reshape(1, -1))


def kernel(x, w_in, cmp_pe_k, cmp_w1_k, cmp_w2_k, cmp_pe_v, cmp_w1_v, cmp_w2_v, w_proj_sb, w_proj_nsa, w_out,
           ln1_g, ln1_b, w_router, b_router, w_gate_up, b_gate_up, w_down, b_down, ln2_g, ln2_b):
    assert w_in.shape[0] == 1, "single-layer block"
    batch, seq, _ = x.shape
    assert seq % 512 == 0 and seq // SEL_BLOCK <= LANES and seq >= WINDOW + Q_BLOCK
    h_slab, top_idx, gate_w = _attention_half(
        x, w_in[0], cmp_pe_k[0], cmp_w1_k[0], cmp_w2_k[0], cmp_pe_v[0], cmp_w1_v[0], cmp_w2_v[0],
        w_proj_sb[0], w_proj_nsa[0], w_out[0], ln1_g[0], ln1_b[0], w_router[0], b_router[0])
    out = _moe_half(h_slab, top_idx, gate_w, w_gate_up[0], b_gate_up[0], w_down[0], b_down[0], ln2_g[0], ln2_b[0])
    return out.reshape(batch, seq, D_MODEL)
```

```python
import functools

import numpy as np
import jax
import jax.numpy as jnp
from jax import lax
from jax.experimental import pallas as pl
from jax.experimental.pallas import tpu as pltpu

D_MODEL = 1024
HEAD_DIM = 64
LANES = 128
Q_BLOCK = 128
CMP_BLOCK = 32
CMP_STRIDE = 16
SEL_BLOCK = 64
SEL_TOPK = 16
WINDOW = 512
ROPE_THETA = 10000.0
N_EXPERTS = 32
TOP_K = 4
SWIGLU_LIMIT = 7.0
SWIGLU_ALPHA = 1.702
LN_EPS = 1e-5
NEG_INF = -1e30
TAKEN = -3e38
DEEPNORM_ALPHA = 2.0 ** 0.25
QK_SCALE = HEAD_DIM ** -0.5
LOG2E = 1.4426950408889634

CB_SBQ, CB_SBK, CB_SBV, CB_NQ = 0, 4, 8, 12
CB_KC, CB_VC, CB_KS, CB_VS, CB_KW, CB_VW = 16, 17, 18, 19, 20, 21
CB_NG = 22
CB_MG = 24
PROJ_W = 40 * LANES

SB_TAIL_CUTOFF = -110.0

SEL_Q = 256
SEL_TK = 512
MOE_ROWS = 512
SLAB = D_MODEL // LANES
VMEM_LIMIT = 56 * 1024 * 1024

_bf16 = jnp.bfloat16
_f32 = jnp.float32


def _cparams(sem):
    return pltpu.CompilerParams(dimension_semantics=sem, vmem_limit_bytes=VMEM_LIMIT)


def _dot_t(a, b):
    return lax.dot_general(a, b, (((1,), (1,)), ((), ())), preferred_element_type=_f32)


def _dot(a, b):
    return jnp.dot(a, b, preferred_element_type=_f32)


def _lane_iota(shape):
    return lax.broadcasted_iota(jnp.int32, shape, len(shape) - 1)


def _half0(shape=(1, LANES)):
    return _lane_iota(shape) < HEAD_DIM


def _in_proj_kernel(x_ref, w_ref, o_ref):
    o_ref[...] = _dot(x_ref[...].astype(_bf16), w_ref[...]).astype(o_ref.dtype)


def _in_proj(x2, w):
    n = x2.shape[0]
    tm, tn = 512, 1280
    return pl.pallas_call(
        _in_proj_kernel,
        grid=(PROJ_W // tn, n // tm),
        in_specs=[pl.BlockSpec((tm, D_MODEL), lambda j, i: (i, 0)),
                  pl.BlockSpec((D_MODEL, tn), lambda j, i: (0, j))],
        out_specs=pl.BlockSpec((tm, tn), lambda j, i: (i, j)),
        out_shape=jax.ShapeDtypeStruct((n, PROJ_W), _bf16),
        compiler_params=_cparams(("arbitrary", "arbitrary")),
        name="in_proj",
    )(x2, w)


def _rope(x, cos, sin_signed):
    first = (_lane_iota((1, LANES)) % HEAD_DIM) < (HEAD_DIM // 2)
    swapped = jnp.where(first, pltpu.roll(x, LANES - HEAD_DIM // 2, 1), pltpu.roll(x, HEAD_DIM // 2, 1))
    return x * cos + swapped * sin_signed


def _dup(x, g):
    other = pltpu.roll(x, HEAD_DIM, 1)
    h0 = _half0()
    return jnp.where(h0, x, other) if g == 0 else jnp.where(h0, other, x)


def _rope_kernel(nq_ref, kc_ref, vc_ref, ks_ref, vs_ref, kw_ref, vw_ref, cos_ref, sin_ref,
                 nq_o, kc_o, vc_o, ka_o, vs_o, kw_o, vw_o, *, blocks_per_seq):
    ts = cos_ref.shape[0]
    cos = cos_ref[...]
    sin = sin_ref[...]
    for c in range(4):
        sl = slice(c * LANES, (c + 1) * LANES)
        nq_o[:, sl] = (_rope(nq_ref[:, sl].astype(_f32), cos, sin) * (QK_SCALE * LOG2E)).astype(_bf16)
    kc_o[...] = _rope(kc_ref[...].astype(_f32), cos, sin).astype(_bf16)
    vc_o[...] = vc_ref[...]
    ks = _rope(ks_ref[...].astype(_f32), cos, sin)
    kw = _rope(kw_ref[...].astype(_f32), cos, sin)
    vs = vs_ref[...].astype(_f32)
    vw = vw_ref[...].astype(_f32)
    pos = (pl.program_id(0) % blocks_per_seq) * ts + lax.broadcasted_iota(jnp.int32, (ts, LANES), 0)
    onehot = jnp.where(pos // SEL_BLOCK == _lane_iota((ts, LANES)), 1.0, 0.0).astype(_bf16)
    for g in range(2):
        ka_o[g, :, 0:LANES] = _dup(ks, g).astype(_bf16)
        ka_o[g, :, LANES:2 * LANES] = onehot
        vsa = jnp.where(_half0(), _dup(vs, g), 1.0)
        for c in range(ts // SEL_TK):
            vs_o[g, c] = vsa[c * SEL_TK:(c + 1) * SEL_TK, :].T.astype(_bf16)
        kw_o[g] = _dup(kw, g).astype(_bf16)
        vwd = jnp.where(_half0(), _dup(vw, g), 1.0)
        for c in range(ts // LANES):
            vw_o[g, c] = vwd[c * LANES:(c + 1) * LANES, :].T.astype(_bf16)


def _rope_stage(proj, cos, sin_signed, seq):
    n = proj.shape[0]
    ts = 512
    bps = seq // ts
    col = lambda cb: pl.BlockSpec((ts, LANES), lambda i, cb=cb: (i, cb))
    tab = pl.BlockSpec((ts, LANES), lambda i: (i % bps, 0))
    grp = lambda w: pl.BlockSpec((2, ts, w), lambda i: (0, i, 0))
    return pl.pallas_call(
        functools.partial(_rope_kernel, blocks_per_seq=bps),
        grid=(n // ts,),
        in_specs=[pl.BlockSpec((ts, 4 * LANES), lambda i: (i, CB_NQ // 4)),
                  col(CB_KC), col(CB_VC), col(CB_KS), col(CB_VS), col(CB_KW), col(CB_VW), tab, tab],
        out_specs=[pl.BlockSpec((ts, 4 * LANES), lambda i: (i, 0)),
                   pl.BlockSpec((ts, LANES), lambda i: (i, 0)),
                   pl.BlockSpec((ts, LANES), lambda i: (i, 0)),
                   grp(2 * LANES),
                   pl.BlockSpec((2, ts // SEL_TK, LANES, SEL_TK), lambda i: (0, i, 0, 0)),
                   grp(LANES),
                   pl.BlockSpec((2, ts // LANES, LANES, LANES), lambda i: (0, i, 0, 0))],
        out_shape=[jax.ShapeDtypeStruct((n, 4 * LANES), _bf16),
                   jax.ShapeDtypeStruct((n, LANES), _bf16),
                   jax.ShapeDtypeStruct((n, LANES), _bf16),
                   jax.ShapeDtypeStruct((2, n, 2 * LANES), _bf16),
                   jax.ShapeDtypeStruct((2, n // SEL_TK, LANES, SEL_TK), _bf16),
                   jax.ShapeDtypeStruct((2, n, LANES), _bf16),
                   jax.ShapeDtypeStruct((2, n // LANES, LANES, LANES), _bf16)],
        compiler_params=_cparams(("arbitrary",)),
        name="rope_layout",
    )(proj, proj, proj, proj, proj, proj, proj, cos, sin_signed)


def _gelu_tanh(x):
    return 0.5 * x * (1.0 + jnp.tanh(0.7978845608028654 * (x + 0.044715 * (x * x * x))))


def _compress_one(x_ref, pe_t, pe_b, w_t, w_b, w2, out_ref, transposed):
    x = x_ref[0].astype(_f32)
    a = _dot((x + pe_t[...]).astype(_bf16), w_t[...])
    b = _dot((x + pe_b[...]).astype(_bf16), w_b[...])
    nc = a.shape[0]
    pre = a + pltpu.roll(b, nc - 1, 0)
    y = _dot(_gelu_tanh(pre).astype(_bf16), w2[...])
    for g in range(2):
        d = _dup(y, g)
        out_ref[0, g] = (d.T if transposed else d).astype(_bf16)


def _compress_kernel(k_ref, v_ref, kpt, kpb, kwt, kwb, kw2, vpt, vpb, vwt, vwb, vw2, ko_ref, vo_ref):
    _compress_one(k_ref, kpt, kpb, kwt, kwb, kw2, ko_ref, False)
    _compress_one(v_ref, vpt, vpb, vwt, vwb, vw2, vo_ref, True)


def _compress_weights(pe, w1, w2):
    half = CMP_BLOCK // 2
    eye = jnp.eye(2, dtype=_f32)
    outs = []
    for part in range(2):
        w = w1[part * half * HEAD_DIM:(part + 1) * half * HEAD_DIM].reshape(half, HEAD_DIM, HEAD_DIM)
        wbd = (w[:, None, :, None, :] * eye[None, :, None, :, None]).reshape(half * 2 * HEAD_DIM, 2 * HEAD_DIM)
        p = jnp.broadcast_to(pe[part * half:(part + 1) * half, None, :], (half, 2, HEAD_DIM)).reshape(1, -1)
        outs.append((p.astype(_f32), wbd.astype(_bf16)))
    w2bd = (w2[None, :, None, :] * eye[:, None, :, None]).reshape(2 * HEAD_DIM, 2 * HEAD_DIM).astype(_bf16)
    (pt, wt), (pb, wb) = outs
    return pt, pb, wt, wb, w2bd


def _compress_stage(kc_r, vc_r, kparams, vparams, batch, seq):
    nc = seq // CMP_STRIDE
    width = CMP_STRIDE * LANES
    xs = pl.BlockSpec((1, nc, width), lambda b: (b, 0, 0))
    full = lambda a: pl.BlockSpec(a.shape, lambda b: (0,) * a.ndim)
    out = pl.BlockSpec((1, 2, nc, LANES), lambda b: (b, 0, 0, 0))
    weights = list(kparams) + list(vparams)
    return pl.pallas_call(
        _compress_kernel,
        grid=(batch,),
        in_specs=[xs, xs] + [full(a) for a in weights],
        out_specs=[out, pl.BlockSpec((1, 2, LANES, nc), lambda b: (b, 0, 0, 0))],
        out_shape=[jax.ShapeDtypeStruct((batch, 2, nc, LANES), _bf16),
                   jax.ShapeDtypeStruct((batch, 2, LANES, nc), _bf16)],
        compiler_params=_cparams(("arbitrary",)),
        name="compress",
    )(kc_r.reshape(batch, nc, width), vc_r.reshape(batch, nc, width), *weights)


def _head_q(q_ref, r):
    q2 = q_ref[:, (r // 2) * LANES:(r // 2 + 1) * LANES]
    keep = _half0() if r % 2 == 0 else jnp.logical_not(_half0())
    return jnp.where(keep, q2, jnp.zeros_like(q2))


def _softmax_over_rows(s):
    m = jnp.max(s, axis=0, keepdims=True)
    e = jnp.exp2(s - m)
    l = jnp.sum(e, axis=0, keepdims=True)
    return e * jnp.where(m > 0.5 * NEG_INF, 1.0 / l, 0.0)


def _pair(even, odd):
    return jnp.where(_half0(), even, odd)


def _nsa_cw_kernel(q_ref, kc_ref, vct_ref, kw_ref, vwt_ref, ng_ref, stt_ref, yp_ref, mb_ref):
    t0 = pl.program_id(2) * Q_BLOCK
    qpos = t0 + _lane_iota((1, Q_BLOCK))
    gates = jax.nn.sigmoid(ng_ref[...].astype(_f32).T[0:16, :])
    kc = kc_ref[0, 0]
    vct = vct_ref[0, 0]
    nc = kc.shape[0]
    cend = lax.broadcasted_iota(jnp.int32, (nc, 1), 0) * CMP_STRIDE + (CMP_BLOCK - 1)
    cmask = cend <= qpos
    start = pl.multiple_of(jnp.maximum(t0 - WINDOW, 0), Q_BLOCK)
    wlen = WINDOW + Q_BLOCK
    kwin = kw_ref[0, 0, pl.ds(start, wlen), :]
    kpos = start + lax.broadcasted_iota(jnp.int32, (wlen, 1), 0)
    wmask = (kpos <= qpos) & (qpos - kpos < WINDOW)
    blk0 = start // Q_BLOCK

    qs = [_head_q(q_ref, r) for r in range(4)]
    s_cmp = [_dot_t(kc, q) for q in qs]
    s_win = [_dot_t(kwin, q) for q in qs]
    p_cmp = [_softmax_over_rows(jnp.where(cmask, s, NEG_INF)) for s in s_cmp]
    imp = (p_cmp[0] + p_cmp[1]) + (p_cmp[2] + p_cmp[3])
    e_win = []
    for s in s_win:
        s = jnp.where(wmask, s, NEG_INF)
        e_win.append(jnp.exp2(s - jnp.max(s, axis=0, keepdims=True)).astype(_bf16))
    vwt = jnp.concatenate([vwt_ref[0, 0, blk0 + c] for c in range(wlen // Q_BLOCK)], axis=1)
    yts = []
    for r in range(4):
        o_cmp = _dot(vct, p_cmp[r].astype(_bf16))[0:HEAD_DIM]
        win = _dot(vwt, e_win[r])
        o_win = win[0:HEAD_DIM] * (1.0 / win[HEAD_DIM:2 * HEAD_DIM])
        yts.append(gates[3 * r:3 * r + 1] * o_cmp + gates[3 * r + 2:3 * r + 3] * o_win)
    yp_ref[:, 0:LANES] = jnp.concatenate(yts[0:2], axis=0).T
    yp_ref[:, LANES:2 * LANES] = jnp.concatenate(yts[2:4], axis=0).T

    p_slc = jnp.dot(stt_ref[...], imp, preferred_element_type=_f32, precision=lax.Precision.HIGHEST)
    selj = lax.broadcasted_iota(jnp.int32, (LANES, 1), 0)
    blk_t = qpos // SEL_BLOCK
    forced = (selj == 0) | (selj == blk_t) | (selj == blk_t - 1)
    score = jnp.where(forced, TAKEN, jnp.where(selj <= blk_t, p_slc, NEG_INF))
    seljf = selj.astype(_f32)
    picked = forced
    for _ in range(SEL_TOPK - 3):
        m = jnp.max(score, axis=0, keepdims=True)
        first = jnp.min(jnp.where(score == m, seljf, float(LANES)), axis=0, keepdims=True)
        hit = seljf == first
        picked = picked | hit
        score = jnp.where(hit, TAKEN, score)
    mb_ref[0] = jnp.where(picked, 0.0, NEG_INF).T.astype(_bf16)


def _nsa_cw_stage(nq_r, kc_d, vc_t, kw_d, vw_t, proj, stencil_t, batch, seq):
    n = nq_r.shape[0]
    nblk = seq // Q_BLOCK
    nc = seq // CMP_STRIDE
    qrow = lambda b, g, i: b * nblk + i
    return pl.pallas_call(
        _nsa_cw_kernel,
        grid=(batch, 2, nblk),
        in_specs=[pl.BlockSpec((Q_BLOCK, 2 * LANES), lambda b, g, i: (qrow(b, g, i), g)),
                  pl.BlockSpec((1, 1, nc, LANES), lambda b, g, i: (b, g, 0, 0)),
                  pl.BlockSpec((1, 1, LANES, nc), lambda b, g, i: (b, g, 0, 0)),
                  pl.BlockSpec((1, 1, seq, LANES), lambda b, g, i: (g, b, 0, 0)),
                  pl.BlockSpec((1, 1, nblk, LANES, LANES), lambda b, g, i: (g, b, 0, 0, 0)),
                  pl.BlockSpec((Q_BLOCK, LANES), lambda b, g, i: (qrow(b, g, i), CB_NG + g)),
                  pl.BlockSpec((LANES, nc), lambda b, g, i: (0, 0))],
        out_specs=[pl.BlockSpec((Q_BLOCK, 2 * LANES), lambda b, g, i: (qrow(b, g, i), g)),
                   pl.BlockSpec((1, Q_BLOCK, LANES), lambda b, g, i: (g, qrow(b, g, i), 0))],
        out_shape=[jax.ShapeDtypeStruct((n, 4 * LANES), _f32),
                   jax.ShapeDtypeStruct((2, n, LANES), _bf16)],
        compiler_params=_cparams(("arbitrary", "arbitrary", "arbitrary")),
        name="nsa_cmp_win_select",
    )(nq_r, kc_d, vc_t, kw_d.reshape(2, batch, seq, LANES), vw_t.reshape(2, batch, nblk, LANES, LANES), proj, stencil_t)


def _nsa_sel_kernel(q_ref, mb_ref, ka_ref, vat_ref, ng_ref, yp_ref, o_ref, qs_ref, m_ref, acc_ref, s_ref, p_ref, alpha_ref):
    tk = SEL_TK
    t0 = pl.program_id(2) * SEL_Q
    mb = mb_ref[0]
    for r in range(4):
        qs_ref[r * SEL_Q:(r + 1) * SEL_Q, 0:LANES] = _head_q(q_ref, r)
        qs_ref[r * SEL_Q:(r + 1) * SEL_Q, LANES:2 * LANES] = mb
    m_ref[...] = jnp.full(m_ref.shape, NEG_INF, _f32)
    acc_ref[...] = jnp.zeros(acc_ref.shape, _f32)
    p_ref[...] = jnp.zeros(p_ref.shape, _bf16)
    alpha_ref[...] = jnp.ones(alpha_ref.shape, _f32)
    qpos = t0 + _lane_iota((1, 4 * SEL_Q)) % SEL_Q

    def scores(kt):
        return _dot_t(ka_ref[0, 0, pl.ds(pl.multiple_of(kt * tk, tk), tk), :], qs_ref[...])

    def softmax_step(s):
        m_old = m_ref[...]
        m_new = jnp.maximum(m_old, jnp.max(s, axis=0, keepdims=True))
        m_ref[...] = m_new
        return jnp.exp2(s - m_new).astype(_bf16), jnp.exp2(m_old - m_new)

    def accumulate(kt, alpha, p):
        acc_ref[...] = alpha * acc_ref[...] + _dot(vat_ref[0, 0, kt], p)

    def trip(kt, carry):
        accumulate(jnp.maximum(kt - 1, 0), alpha_ref[...], p_ref[...])
        s = s_ref[...]
        s_ref[...] = scores(kt + 1)
        p, alpha = softmax_step(s)
        p_ref[...] = p
        alpha_ref[...] = alpha
        return carry

    n_full = t0 // tk
    s_ref[...] = scores(0)

    def two_trips(kp, carry):
        trip(2 * kp, carry)
        return trip(2 * kp + 1, carry)

    lax.fori_loop(0, n_full // 2, two_trips, 0)

    @pl.when(n_full % 2 == 1)
    def _():
        trip(n_full - 1, 0)

    accumulate(jnp.maximum(n_full - 1, 0), alpha_ref[...], p_ref[...])
    kpos = n_full * tk + lax.broadcasted_iota(jnp.int32, (tk, 1), 0)
    p, alpha = softmax_step(jnp.where(kpos <= qpos, s_ref[...], NEG_INF))
    accumulate(n_full, alpha, p)
    acc = acc_ref[...]
    o = acc[0:HEAD_DIM] * (1.0 / acc[HEAD_DIM:2 * HEAD_DIM])
    gates = jax.nn.sigmoid(ng_ref[...].astype(_f32).T[0:16, :])
    ys = [gates[3 * r + 1:3 * r + 2] * o[:, r * SEL_Q:(r + 1) * SEL_Q] for r in range(4)]
    o_ref[:, 0:LANES] = (yp_ref[:, 0:LANES] + jnp.concatenate(ys[0:2], axis=0).T).astype(o_ref.dtype)
    o_ref[:, LANES:2 * LANES] = (yp_ref[:, LANES:2 * LANES] + jnp.concatenate(ys[2:4], axis=0).T).astype(o_ref.dtype)


def _nsa_sel_stage(nq_r, mbias, k_aug, vs_t, proj, ypart, batch, seq):
    n = nq_r.shape[0]
    nblk = seq // SEL_Q
    qrow = lambda b, g, i: b * nblk + i
    return pl.pallas_call(
        _nsa_sel_kernel,
        grid=(batch, 2, nblk),
        in_specs=[pl.BlockSpec((SEL_Q, 2 * LANES), lambda b, g, i: (qrow(b, g, i), g)),
                  pl.BlockSpec((1, SEL_Q, LANES), lambda b, g, i: (g, qrow(b, g, i), 0)),
                  pl.BlockSpec((1, 1, seq, 2 * LANES), lambda b, g, i: (g, b, 0, 0)),
                  pl.BlockSpec((1, 1, seq // SEL_TK, LANES, SEL_TK), lambda b, g, i: (g, b, 0, 0, 0)),
                  pl.BlockSpec((SEL_Q, LANES), lambda b, g, i: (qrow(b, g, i), CB_NG + g)),
                  pl.BlockSpec((SEL_Q, 2 * LANES), lambda b, g, i: (qrow(b, g, i), g))],
        out_specs=pl.BlockSpec((SEL_Q, 2 * LANES), lambda b, g, i: (qrow(b, g, i), g)),
        out_shape=jax.ShapeDtypeStruct((n, 4 * LANES), _bf16),
        scratch_shapes=[pltpu.VMEM((4 * SEL_Q, 2 * LANES), _bf16),
                        pltpu.VMEM((1, 4 * SEL_Q), _f32),
                        pltpu.VMEM((LANES, 4 * SEL_Q), _f32),
                        pltpu.VMEM((SEL_TK, 4 * SEL_Q), _f32),
                        pltpu.VMEM((SEL_TK, 4 * SEL_Q), _bf16),
                        pltpu.VMEM((1, 4 * SEL_Q), _f32)],
        compiler_params=_cparams(("arbitrary", "arbitrary", "arbitrary")),
        name="nsa_selected",
    )(nq_r, mbias, k_aug.reshape(2, batch, seq, 2 * LANES), vs_t.reshape(2, batch, seq // SEL_TK, LANES, SEL_TK), proj, ypart)


def _sb_kernel(q_ref, k_ref, v_ref, o_ref, qs_ref, tail_ref, acc_ref):
    i = pl.program_id(1)
    h0 = _half0()
    heads = 2 * (q_ref.shape[1] // LANES)
    for h in range(heads):
        q = q_ref[:, (h // 2) * LANES:(h // 2 + 1) * LANES]
        keep = h0 if h % 2 == 0 else jnp.logical_not(h0)
        qs_ref[h] = jnp.where(keep, q, jnp.zeros_like(q)) * QK_SCALE
    tail_ref[...] = jnp.zeros(tail_ref.shape, _f32)
    acc_ref[...] = jnp.zeros(acc_ref.shape, _f32)
    rloc = lax.broadcasted_iota(jnp.int32, (Q_BLOCK, Q_BLOCK), 0)
    cloc = lax.broadcasted_iota(jnp.int32, (Q_BLOCK, Q_BLOCK), 1)
    later = jnp.where(rloc > cloc, 1.0, 0.0).astype(_bf16)

    def cond(c):
        j, worst_tail = c
        return (j >= 0) & (worst_tail > SB_TAIL_CUTOFF)

    def body(c):
        j, _ = c
        k0 = pl.multiple_of(j * Q_BLOCK, Q_BLOCK)
        past = (cloc + (j - i) * Q_BLOCK) < rloc
        cols = [slice((h // 2) * LANES, (h // 2 + 1) * LANES) for h in range(heads)]
        zs = [_dot_t(qs_ref[h], k_ref[0, pl.ds(k0, Q_BLOCK), cols[h]]) for h in range(heads)]
        log_beta, log_keep = [], []
        for z in zs:
            sp = jnp.maximum(z, 0.0) + jnp.log1p(jnp.exp(-jnp.abs(z)))
            log_beta.append(z - sp)
            log_keep.append(jnp.where(past, -sp, 0.0))
        inner = []
        for lk in log_keep:
            hi = lk.astype(_bf16)
            lo = (lk - hi.astype(_f32)).astype(_bf16)
            inner.append(_dot(hi, later) + _dot(lo, later))
        probs = [jnp.where(past, jnp.exp(log_beta[h] + inner[h] + tail_ref[h]), 0.0).astype(_bf16)
                 for h in range(heads)]
        worst = jnp.full((Q_BLOCK, 1), -jnp.inf, _f32)
        for h in range(heads):
            acc_ref[h] = acc_ref[h] + _dot(probs[h], v_ref[0, pl.ds(k0, Q_BLOCK), cols[h]])
            tail = tail_ref[h] + jnp.sum(log_keep[h], axis=-1, keepdims=True)
            tail_ref[h] = tail
            worst = jnp.maximum(worst, tail)
        return j - 1, jnp.max(worst)

    lax.while_loop(cond, body, (i, jnp.float32(0.0)))
    for p in range(heads // 2):
        o_ref[:, p * LANES:(p + 1) * LANES] = jnp.where(h0, acc_ref[2 * p], acc_ref[2 * p + 1]).astype(o_ref.dtype)


def _sb_stage(proj, batch, seq):
    n = proj.shape[0]
    nblk = seq // Q_BLOCK
    width = 4 * LANES
    proj3 = proj.reshape(batch, seq, PROJ_W)
    return pl.pallas_call(
        _sb_kernel,
        grid=(batch, nblk),
        in_specs=[pl.BlockSpec((Q_BLOCK, width), lambda b, i: (b * nblk + i, CB_SBQ // 4)),
                  pl.BlockSpec((1, seq, width), lambda b, i: (b, 0, CB_SBK // 4)),
                  pl.BlockSpec((1, seq, width), lambda b, i: (b, 0, CB_SBV // 4))],
        out_specs=pl.BlockSpec((Q_BLOCK, width), lambda b, i: (b * nblk + i, 0)),
        out_shape=jax.ShapeDtypeStruct((n, width), _bf16),
        scratch_shapes=[pltpu.VMEM((8, Q_BLOCK, LANES), _bf16),
                        pltpu.VMEM((8, Q_BLOCK, 1), _f32),
                        pltpu.VMEM((8, Q_BLOCK, LANES), _f32)],
        compiler_params=_cparams(("arbitrary", "arbitrary")),
        name="stick_breaking",
    )(proj, proj3, proj3)


def _layer_norm(x, g, b):
    mu = jnp.mean(x, axis=-1, keepdims=True)
    xc = x - mu
    var = jnp.mean(xc * xc, axis=-1, keepdims=True)
    return xc * lax.rsqrt(var + LN_EPS) * g + b


def _merge_kernel(x_ref, ysb_ref, yns_ref, mg0_ref, mg1_ref, wsb_ref, wns_ref, wo_ref, g_ref, b_ref,
                  wr_ref, br_ref, h_ref, idx_ref, gw_ref):
    m0 = jax.nn.sigmoid(mg0_ref[...].astype(_f32))
    m1 = jax.nn.sigmoid(mg1_ref[...].astype(_f32))
    merged = m0 * _dot(ysb_ref[...], wsb_ref[...]) + m1 * _dot(yns_ref[...], wns_ref[...])
    pre = DEEPNORM_ALPHA * x_ref[...] + _dot(merged.astype(_bf16), wo_ref[...])
    h = _layer_norm(pre, g_ref[...], b_ref[...])
    tm = h.shape[0]
    for s in range(SLAB):
        h_ref[pl.ds(s, tm, stride=SLAB), :] = h[:, s * LANES:(s + 1) * LANES]
    h_hi = h.astype(_bf16)
    h_lo = (h - h_hi.astype(_f32)).astype(_bf16)
    w = wr_ref[...]
    w_hi = w.astype(_bf16)
    w_lo = (w - w_hi.astype(_f32)).astype(_bf16)
    logits = (_dot(h_hi, w_hi) + (_dot(h_hi, w_lo) + _dot(h_lo, w_hi))) + br_ref[...]
    lane = _lane_iota((1, LANES))
    lanef = lane.astype(_f32)
    lg = jnp.where(lane < N_EXPERTS, logits, TAKEN)
    vals, idxs = [], []
    for _ in range(TOP_K):
        m = jnp.max(lg, axis=-1, keepdims=True)
        first = jnp.min(jnp.where(lg == m, lanef, float(LANES)), axis=-1, keepdims=True)
        vals.append(m)
        idxs.append(first)
        lg = jnp.where(lanef == first, TAKEN, lg)
    es = [jnp.exp(v - vals[0]) for v in vals]
    inv = 1.0 / (es[0] + es[1] + es[2] + es[3])
    idx_t = jnp.zeros(lg.shape, _f32)
    gw_t = jnp.zeros(lg.shape, _f32)
    for k in range(TOP_K):
        idx_t = jnp.where(lane == k, idxs[k], idx_t)
        gw_t = jnp.where(lane == k, es[k] * inv, gw_t)
    idx_ref[...] = idx_t[:, :TOP_K].astype(jnp.int32)
    gw_ref[...] = gw_t[:, :TOP_K]


def _merge_stage(x2, y_sb, y_nsa, proj, wsb, wns, wo, g1, b1, wr, br):
    n = x2.shape[0]
    tm = 512
    row = lambda w: pl.BlockSpec((tm, w), lambda i: (i, 0))
    full = lambda a: pl.BlockSpec(a.shape, lambda i: (0,) * a.ndim)
    return pl.pallas_call(
        _merge_kernel,
        grid=(n // tm,),
        in_specs=[row(D_MODEL), row(4 * LANES), row(4 * LANES),
                  pl.BlockSpec((tm, D_MODEL), lambda i: (i, CB_MG // 8)),
                  pl.BlockSpec((tm, D_MODEL), lambda i: (i, CB_MG // 8 + 1)),
                  full(wsb), full(wns), full(wo), full(g1), full(b1), full(wr), full(br)],
        out_specs=[pl.BlockSpec((tm * SLAB, LANES), lambda i: (i, 0)), row(TOP_K), row(TOP_K)],
        out_shape=[jax.ShapeDtypeStruct((n * SLAB, LANES), _f32),
                   jax.ShapeDtypeStruct((n, TOP_K), jnp.int32),
                   jax.ShapeDtypeStruct((n, TOP_K), _f32)],
        compiler_params=_cparams(("arbitrary",)),
        name="merge_ln1_router",
    )(x2, y_sb, y_nsa, proj, proj, wsb, wns, wo, g1, b1, wr, br)


def _prep_w_in(w):
    main = w[:, :CB_NG * LANES]
    ng = w[:, CB_NG * LANES:CB_NG * LANES + 24]
    mg = w[:, CB_NG * LANES + 24:]
    pad = jnp.zeros((w.shape[0], LANES - 12), w.dtype)
    return jnp.concatenate([main, ng[:, :12], pad, ng[:, 12:], pad, mg], axis=1).astype(_bf16)


def _rope_tables(seq):
    half = HEAD_DIM // 2
    inv_freq = ROPE_THETA ** (-jnp.arange(half, dtype=_f32) / half)
    ang = jnp.arange(seq, dtype=_f32)[:, None] * inv_freq[None, :]
    cos = jnp.cos(ang)
    sin = jnp.sin(ang)
    cos128 = jnp.concatenate([cos, cos, cos, cos], axis=1)
    sin128 = jnp.concatenate([-sin, sin, -sin, sin], axis=1)
    return cos128, sin128


def _stencil(nc):
    n = np.arange(nc)[:, None]
    j = np.arange(LANES)[None, :]
    ratio = SEL_BLOCK // CMP_STRIDE
    ok = (n >= ratio * j - 1) & (n <= ratio * j + ratio - 1) & (n < nc - 1)
    return jnp.asarray(ok.astype(np.float32).T)


def _attention_half(x, w_in, cmp_pe_k, cmp_w1_k, cmp_w2_k, cmp_pe_v, cmp_w1_v, cmp_w2_v,
                    w_proj_sb, w_proj_nsa, w_out, ln1_g, ln1_b, w_router, b_router):
    batch, seq, _ = x.shape
    n = batch * seq
    x2 = x.reshape(n, D_MODEL)
    proj = _in_proj(x2, _prep_w_in(w_in))
    cos, sin_signed = _rope_tables(seq)
    nq_r, kc_r, vc_r, k_aug, vs_t, kw_d, vw_t = _rope_stage(proj, cos, sin_signed, seq)
    kc_d, vc_t = _compress_stage(kc_r, vc_r, _compress_weights(cmp_pe_k, cmp_w1_k, cmp_w2_k),
                                 _compress_weights(cmp_pe_v, cmp_w1_v, cmp_w2_v), batch, seq)
    ypart, mbias = _nsa_cw_stage(nq_r, kc_d, vc_t, kw_d, vw_t, proj, _stencil(seq // CMP_STRIDE), batch, seq)
    y_nsa = _nsa_sel_stage(nq_r, mbias, k_aug, vs_t, proj, ypart, batch, seq)
    y_sb = _sb_stage(proj, batch, seq)
    wr = jnp.pad(w_router.astype(_f32), ((0, 0), (0, LANES - N_EXPERTS)))
    br = jnp.pad(b_router.astype(_f32), (0, LANES - N_EXPERTS)).reshape(1, LANES)
    return _merge_stage(x2, y_sb, y_nsa, proj, w_proj_sb.astype(_bf16), w_proj_nsa.astype(_bf16),
                        w_out.astype(_bf16), ln1_g.reshape(1, -1), ln1_b.reshape(1, -1), wr, br)


def _row_copy(src, src_row, dst, dst_row, sem):
    return pltpu.make_async_copy(src.at[pl.ds(src_row * SLAB, SLAB)], dst.at[pl.ds(dst_row * SLAB, SLAB)], sem)


def _dispatch_kernel(dest_ref, h_ref, zeros_ref, buf_ref, sem, *, tokens):
    del zeros_ref

    def issue(t, c):
        for k in range(TOP_K):
            _row_copy(h_ref, t, buf_ref, dest_ref[t * TOP_K + k], sem).start()
        return c

    lax.fori_loop(0, tokens, issue, 0)
    for _ in range(TOP_K):
        pltpu.make_async_copy(h_ref, buf_ref.at[pl.ds(0, tokens * SLAB)], sem).wait()


def _dispatch_stage(dest, h_slab, buf_rows):
    n = h_slab.shape[0] // SLAB
    tokens = 512
    zeros = jnp.zeros((buf_rows * SLAB, LANES), _f32)
    return pl.pallas_call(
        functools.partial(_dispatch_kernel, tokens=tokens),
        grid=(n // tokens,),
        in_specs=[pl.BlockSpec((tokens * TOP_K,), lambda i: (i,), memory_space=pltpu.SMEM),
                  pl.BlockSpec((tokens * SLAB, LANES), lambda i: (i, 0)),
                  pl.BlockSpec(memory_space=pl.ANY)],
        out_specs=pl.BlockSpec(memory_space=pl.ANY),
        out_shape=jax.ShapeDtypeStruct(zeros.shape, _f32),
        scratch_shapes=[pltpu.SemaphoreType.DMA(())],
        input_output_aliases={2: 0},
        compiler_params=pltpu.CompilerParams(dimension_semantics=("arbitrary",), has_side_effects=True),
        name="moe_dispatch",
    )(dest, h_slab, zeros)


def _slab_load(ref, rows):
    return jnp.concatenate([ref[pl.ds(s, rows, stride=SLAB), :] for s in range(SLAB)], axis=1)


def _expert_kernel(ce_ref, used_ref, x_ref, wgu_ref, bgu_ref, wd_ref, bd_ref, o_ref):
    del ce_ref
    c = pl.program_id(0)

    @pl.when(c < used_ref[0])
    def _():
        x = _slab_load(x_ref, MOE_ROWS).astype(_bf16)
        gu = _dot(x, wgu_ref[0]) + bgu_ref[0]
        gate = jnp.minimum(gu[:, :D_MODEL], SWIGLU_LIMIT)
        up = jnp.clip(gu[:, D_MODEL:], -SWIGLU_LIMIT, SWIGLU_LIMIT)
        h = gate * jax.nn.sigmoid(SWIGLU_ALPHA * gate) * (up + 1.0)
        y = _dot(h.astype(_bf16), wd_ref[0]) + bd_ref[0]
        for s in range(SLAB):
            o_ref[pl.ds(s, MOE_ROWS, stride=SLAB), :] = y[:, s * LANES:(s + 1) * LANES]

    @pl.when(c >= used_ref[0])
    def _():
        o_ref[...] = jnp.zeros(o_ref.shape, o_ref.dtype)


def _expert_stage(chunk_expert, n_used, buf, wgu, bgu, wd, bd):
    n_chunks = chunk_expert.shape[0]
    rows = MOE_ROWS * SLAB
    grid_spec = pltpu.PrefetchScalarGridSpec(
        num_scalar_prefetch=2,
        grid=(n_chunks,),
        in_specs=[pl.BlockSpec((rows, LANES), lambda c, ce, nu: (c, 0)),
                  pl.BlockSpec((1, D_MODEL, 2 * D_MODEL), lambda c, ce, nu: (ce[c], 0, 0)),
                  pl.BlockSpec((1, 1, 2 * D_MODEL), lambda c, ce, nu: (ce[c], 0, 0)),
                  pl.BlockSpec((1, D_MODEL, D_MODEL), lambda c, ce, nu: (ce[c], 0, 0)),
                  pl.BlockSpec((1, 1, D_MODEL), lambda c, ce, nu: (ce[c], 0, 0))],
        out_specs=pl.BlockSpec((rows, LANES), lambda c, ce, nu: (c, 0)))
    return pl.pallas_call(
        _expert_kernel,
        grid_spec=grid_spec,
        out_shape=jax.ShapeDtypeStruct(buf.shape, _f32),
        compiler_params=_cparams(("arbitrary",)),
        name="moe_experts",
    )(chunk_expert, n_used, buf, wgu, bgu, wd, bd)


def _combine_kernel(dest_ref, dest_next_ref, gw_ref, h_ref, eo_ref, g_ref, b_ref, o_ref, gbuf, ysl, sems, *, tokens):
    i = pl.program_id(0)
    slot = i % 2
    count = tokens * TOP_K

    def gather(idx_ref, into):
        def issue(t, c):
            for k in range(TOP_K):
                j = t * TOP_K + k
                _row_copy(eo_ref, idx_ref[j], gbuf.at[into], j, sems.at[into]).start()
            return c
        lax.fori_loop(0, tokens, issue, 0)

    @pl.when(i == 0)
    def _():
        gather(dest_ref, 0)

    @pl.when(i + 1 < pl.num_programs(0))
    def _():
        gather(dest_next_ref, 1 - slot)

    pltpu.make_async_copy(eo_ref.at[pl.ds(0, count * SLAB)], gbuf.at[slot], sems.at[slot]).wait()
    rows = gbuf.at[slot]

    def token(t, c):
        acc = DEEPNORM_ALPHA * h_ref[pl.ds(pl.multiple_of(t * SLAB, SLAB), SLAB), :]
        for k in range(TOP_K):
            j = t * TOP_K + k
            acc = acc + gw_ref[j] * rows[pl.ds(pl.multiple_of(j * SLAB, SLAB), SLAB), :]
        ysl[pl.ds(pl.multiple_of(t * SLAB, SLAB), SLAB), :] = acc
        return c

    lax.fori_loop(0, tokens, token, 0)
    o_ref[...] = _layer_norm(_slab_load(ysl, tokens), g_ref[...], b_ref[...])


def _combine_stage(dest, gate_w, h_slab, expert_out, g2, b2):
    n = h_slab.shape[0] // SLAB
    tokens = 256
    steps = n // tokens
    smem = lambda: pl.BlockSpec((tokens * TOP_K,), lambda i: (i,), memory_space=pltpu.SMEM)
    return pl.pallas_call(
        functools.partial(_combine_kernel, tokens=tokens),
        grid=(steps,),
        in_specs=[smem(),
                  pl.BlockSpec((tokens * TOP_K,), lambda i: (jnp.minimum(i + 1, steps - 1),), memory_space=pltpu.SMEM),
                  smem(),
                  pl.BlockSpec((tokens * SLAB, LANES), lambda i: (i, 0)),
                  pl.BlockSpec(memory_space=pl.ANY),
                  pl.BlockSpec((1, D_MODEL), lambda i: (0, 0)),
                  pl.BlockSpec((1, D_MODEL), lambda i: (0, 0))],
        out_specs=pl.BlockSpec((tokens, D_MODEL), lambda i: (i, 0)),
        out_shape=jax.ShapeDtypeStruct((n, D_MODEL), _f32),
        scratch_shapes=[pltpu.VMEM((2, tokens * TOP_K * SLAB, LANES), _f32),
                        pltpu.VMEM((tokens * SLAB, LANES), _f32),
                        pltpu.SemaphoreType.DMA((2,))],
        compiler_params=_cparams(("arbitrary",)),
        name="moe_combine_ln2",
    )(dest, dest, gate_w, h_slab, expert_out, g2, b2)


def _dispatch_plan(top_idx):
    m = top_idx.size
    e_flat = top_idx.reshape(m)
    onehot = (e_flat[:, None] == jnp.arange(N_EXPERTS, dtype=jnp.int32)[None, :]).astype(jnp.int32)
    csum = jnp.cumsum(onehot, axis=0)
    counts = csum[-1]
    padded = (counts + MOE_ROWS - 1) // MOE_ROWS * MOE_ROWS
    pends = jnp.cumsum(padded)
    pstarts = pends - padded
    dest = jnp.sum(onehot * (csum - 1 + pstarts[None, :]), axis=1).astype(jnp.int32)
    n_chunks = m // MOE_ROWS + N_EXPERTS
    chunk_start = jnp.arange(n_chunks, dtype=jnp.int32) * MOE_ROWS
    chunk_expert = jnp.minimum(jnp.sum((chunk_start[:, None] >= pends[None, :]).astype(jnp.int32), axis=1), N_EXPERTS - 1)
    n_used = (pends[-1] // MOE_ROWS).astype(jnp.int32).reshape(1)
    return dest, chunk_expert, n_used, n_chunks * MOE_ROWS


def _moe_half(h_slab, top_idx, gate_w, w_gate_up, b_gate_up, w_down, b_down, ln2_g, ln2_b):
    dest, chunk_expert, n_used, buf_rows = _dispatch_plan(top_idx)
    buf = _dispatch_stage(dest, h_slab, buf_rows)
    expert_out = _expert_stage(chunk_expert, n_used, buf, w_gate_up.astype(_bf16),
                               b_gate_up.reshape(N_EXPERTS, 1, -1), w_down.astype(_bf16),
                               b_down.reshape(N_EXPERTS, 1, -1))
    return _combine_stage(dest, gate_w.reshape(-1), h_slab, expert_out, ln2_g.reshape(1, -1), ln2_b.reshape(1, -1))


def kernel(x, w_in, cmp_pe_k, cmp_w1_k, cmp_w2_k, cmp_pe_v, cmp_w1_v, cmp_w2_v, w_proj_sb, w_proj_nsa, w_out,
           ln1_g, ln1_b, w_router, b_router, w_gate_up, b_gate_up, w_down, b_down, ln2_g, ln2_b):
    assert w_in.shape[0] == 1, "single-layer block"
    batch, seq, _ = x.shape
    assert seq % 512 == 0 and seq // SEL_BLOCK <= LANES and seq >= WINDOW + Q_BLOCK
    h_slab, top_idx, gate_w = _attention_half(
        x, w_in[0], cmp_pe_k[0], cmp_w1_k[0], cmp_w2_k[0], cmp_pe_v[0], cmp_w1_v[0], cmp_w2_v[0],
        w_proj_sb[0], w_proj_nsa[0], w_out[0], ln1_g[0], ln1_b[0], w_router[0], b_router[0])
    out = _moe_half(h_slab, top_idx, gate_w, w_gate_up[0], b_gate_up[0], w_down[0], b_down[0], ln2_g[0], ln2_b[0])
    return out.reshape(batch, seq, D_MODEL)
```

```python
import functools

import numpy as np
import jax
import jax.numpy as jnp
from jax import lax
from jax.experimental import pallas as pl
from jax.experimental.pallas import tpu as pltpu

D_MODEL = 1024
HEAD_DIM = 64
LANES = 128
Q_BLOCK = 128
CMP_BLOCK = 32
CMP_STRIDE = 16
SEL_BLOCK = 64
SEL_TOPK = 16
WINDOW = 512
ROPE_THETA = 10000.0
N_EXPERTS = 32
TOP_K = 4
SWIGLU_LIMIT = 7.0
SWIGLU_ALPHA = 1.702
LN_EPS = 1e-5
NEG_INF = -1e30
TAKEN = -3e38
DEEPNORM_ALPHA = 2.0 ** 0.25
QK_SCALE = HEAD_DIM ** -0.5
LOG2E = 1.4426950408889634

CB_SBQ, CB_SBK, CB_SBV, CB_NQ = 0, 4, 8, 12
CB_KC, CB_VC, CB_KS, CB_VS, CB_KW, CB_VW = 16, 17, 18, 19, 20, 21
CB_NG = 22
CB_MG = 24
PROJ_W = 40 * LANES

SB_TAIL_CUTOFF = -110.0

SEL_Q = 512
SEL_TK = 512
MOE_ROWS = 512
SLAB = D_MODEL // LANES
VMEM_LIMIT = 56 * 1024 * 1024

_bf16 = jnp.bfloat16
_f32 = jnp.float32


def _cparams(sem):
    return pltpu.CompilerParams(dimension_semantics=sem, vmem_limit_bytes=VMEM_LIMIT)


def _dot_t(a, b):
    return lax.dot_general(a, b, (((1,), (1,)), ((), ())), preferred_element_type=_f32)


def _dot(a, b):
    return jnp.dot(a, b, preferred_element_type=_f32)


def _lane_iota(shape):
    return lax.broadcasted_iota(jnp.int32, shape, len(shape) - 1)


def _half0(shape=(1, LANES)):
    return _lane_iota(shape) < HEAD_DIM


def _in_proj_kernel(x_ref, w_ref, o_ref):
    o_ref[...] = _dot(x_ref[...].astype(_bf16), w_ref[...]).astype(o_ref.dtype)


def _in_proj(x2, w):
    n = x2.shape[0]
    tm, tn = 512, 1280
    return pl.pallas_call(
        _in_proj_kernel,
        grid=(PROJ_W // tn, n // tm),
        in_specs=[pl.BlockSpec((tm, D_MODEL), lambda j, i: (i, 0)),
                  pl.BlockSpec((D_MODEL, tn), lambda j, i: (0, j))],
        out_specs=pl.BlockSpec((tm, tn), lambda j, i: (i, j)),
        out_shape=jax.ShapeDtypeStruct((n, PROJ_W), _bf16),
        compiler_params=_cparams(("arbitrary", "arbitrary")),
        name="in_proj",
    )(x2, w)


def _rope(x, cos, sin_signed):
    first = (_lane_iota((1, LANES)) % HEAD_DIM) < (HEAD_DIM // 2)
    swapped = jnp.where(first, pltpu.roll(x, LANES - HEAD_DIM // 2, 1), pltpu.roll(x, HEAD_DIM // 2, 1))
    return x * cos + swapped * sin_signed


def _dup(x, g):
    other = pltpu.roll(x, HEAD_DIM, 1)
    h0 = _half0()
    return jnp.where(h0, x, other) if g == 0 else jnp.where(h0, other, x)


def _rope_kernel(nq_ref, kc_ref, vc_ref, ks_ref, vs_ref, kw_ref, vw_ref, cos_ref, sin_ref,
                 nq_o, kc_o, vc_o, ka_o, vs_o, kw_o, vw_o, *, blocks_per_seq):
    ts = cos_ref.shape[0]
    cos = cos_ref[...]
    sin = sin_ref[...]
    for c in range(4):
        sl = slice(c * LANES, (c + 1) * LANES)
        nq_o[:, sl] = (_rope(nq_ref[:, sl].astype(_f32), cos, sin) * (QK_SCALE * LOG2E)).astype(_bf16)
    kc_o[...] = _rope(kc_ref[...].astype(_f32), cos, sin).astype(_bf16)
    vc_o[...] = vc_ref[...]
    ks = _rope(ks_ref[...].astype(_f32), cos, sin)
    kw = _rope(kw_ref[...].astype(_f32), cos, sin)
    vs = vs_ref[...].astype(_f32)
    vw = vw_ref[...].astype(_f32)
    pos = (pl.program_id(0) % blocks_per_seq) * ts + lax.broadcasted_iota(jnp.int32, (ts, LANES), 0)
    onehot = jnp.where(pos // SEL_BLOCK == _lane_iota((ts, LANES)), 1.0, 0.0).astype(_bf16)
    for g in range(2):
        ka_o[g, :, 0:LANES] = _dup(ks, g).astype(_bf16)
        ka_o[g, :, LANES:2 * LANES] = onehot
        vsa = jnp.where(_half0(), _dup(vs, g), 1.0)
        for c in range(ts // SEL_TK):
            vs_o[g, c] = vsa[c * SEL_TK:(c + 1) * SEL_TK, :].T.astype(_bf16)
        kw_o[g] = _dup(kw, g).astype(_bf16)
        vwd = jnp.where(_half0(), _dup(vw, g), 1.0)
        for c in range(ts // LANES):
            vw_o[g, c] = vwd[c * LANES:(c + 1) * LANES, :].T.astype(_bf16)


def _rope_stage(proj, cos, sin_signed, seq):
    n = proj.shape[0]
    ts = 512
    bps = seq // ts
    col = lambda cb: pl.BlockSpec((ts, LANES), lambda i, cb=cb: (i, cb))
    tab = pl.BlockSpec((ts, LANES), lambda i: (i % bps, 0))
    grp = lambda w: pl.BlockSpec((2, ts, w), lambda i: (0, i, 0))
    return pl.pallas_call(
        functools.partial(_rope_kernel, blocks_per_seq=bps),
        grid=(n // ts,),
        in_specs=[pl.BlockSpec((ts, 4 * LANES), lambda i: (i, CB_NQ // 4)),
                  col(CB_KC), col(CB_VC), col(CB_KS), col(CB_VS), col(CB_KW), col(CB_VW), tab, tab],
        out_specs=[pl.BlockSpec((ts, 4 * LANES), lambda i: (i, 0)),
                   pl.BlockSpec((ts, LANES), lambda i: (i, 0)),
                   pl.BlockSpec((ts, LANES), lambda i: (i, 0)),
                   grp(2 * LANES),
                   pl.BlockSpec((2, ts // SEL_TK, LANES, SEL_TK), lambda i: (0, i, 0, 0)),
                   grp(LANES),
                   pl.BlockSpec((2, ts // LANES, LANES, LANES), lambda i: (0, i, 0, 0))],
        out_shape=[jax.ShapeDtypeStruct((n, 4 * LANES), _bf16),
                   jax.ShapeDtypeStruct((n, LANES), _bf16),
                   jax.ShapeDtypeStruct((n, LANES), _bf16),
                   jax.ShapeDtypeStruct((2, n, 2 * LANES), _bf16),
                   jax.ShapeDtypeStruct((2, n // SEL_TK, LANES, SEL_TK), _bf16),
                   jax.ShapeDtypeStruct((2, n, LANES), _bf16),
                   jax.ShapeDtypeStruct((2, n // LANES, LANES, LANES), _bf16)],
        compiler_params=_cparams(("arbitrary",)),
        name="rope_layout",
    )(proj, proj, proj, proj, proj, proj, proj, cos, sin_signed)


def _gelu_tanh(x):
    return 0.5 * x * (1.0 + jnp.tanh(0.7978845608028654 * (x + 0.044715 * (x * x * x))))


def _compress_one(x_ref, pe_t, pe_b, w_t, w_b, w2, out_ref, transposed):
    x = x_ref[0].astype(_f32)
    a = _dot((x + pe_t[...]).astype(_bf16), w_t[...])
    b = _dot((x + pe_b[...]).astype(_bf16), w_b[...])
    nc = a.shape[0]
    pre = a + pltpu.roll(b, nc - 1, 0)
    y = _dot(_gelu_tanh(pre).astype(_bf16), w2[...])
    for g in range(2):
        d = _dup(y, g)
        out_ref[0, g] = (d.T if transposed else d).astype(_bf16)


def _compress_kernel(k_ref, v_ref, kpt, kpb, kwt, kwb, kw2, vpt, vpb, vwt, vwb, vw2, ko_ref, vo_ref):
    _compress_one(k_ref, kpt, kpb, kwt, kwb, kw2, ko_ref, False)
    _compress_one(v_ref, vpt, vpb, vwt, vwb, vw2, vo_ref, True)


def _compress_weights(pe, w1, w2):
    half = CMP_BLOCK // 2
    eye = jnp.eye(2, dtype=_f32)
    outs = []
    for part in range(2):
        w = w1[part * half * HEAD_DIM:(part + 1) * half * HEAD_DIM].reshape(half, HEAD_DIM, HEAD_DIM)
        wbd = (w[:, None, :, None, :] * eye[None, :, None, :, None]).reshape(half * 2 * HEAD_DIM, 2 * HEAD_DIM)
        p = jnp.broadcast_to(pe[part * half:(part + 1) * half, None, :], (half, 2, HEAD_DIM)).reshape(1, -1)
        outs.append((p.astype(_f32), wbd.astype(_bf16)))
    w2bd = (w2[None, :, None, :] * eye[:, None, :, None]).reshape(2 * HEAD_DIM, 2 * HEAD_DIM).astype(_bf16)
    (pt, wt), (pb, wb) = outs
    return pt, pb, wt, wb, w2bd


def _compress_stage(kc_r, vc_r, kparams, vparams, batch, seq):
    nc = seq // CMP_STRIDE
    width = CMP_STRIDE * LANES
    xs = pl.BlockSpec((1, nc, width), lambda b: (b, 0, 0))
    full = lambda a: pl.BlockSpec(a.shape, lambda b: (0,) * a.ndim)
    out = pl.BlockSpec((1, 2, nc, LANES), lambda b: (b, 0, 0, 0))
    weights = list(kparams) + list(vparams)
    return pl.pallas_call(
        _compress_kernel,
        grid=(batch,),
        in_specs=[xs, xs] + [full(a) for a in weights],
        out_specs=[out, pl.BlockSpec((1, 2, LANES, nc), lambda b: (b, 0, 0, 0))],
        out_shape=[jax.ShapeDtypeStruct((batch, 2, nc, LANES), _bf16),
                   jax.ShapeDtypeStruct((batch, 2, LANES, nc), _bf16)],
        compiler_params=_cparams(("arbitrary",)),
        name="compress",
    )(kc_r.reshape(batch, nc, width), vc_r.reshape(batch, nc, width), *weights)


def _head_q(q_ref, r):
    q2 = q_ref[:, (r // 2) * LANES:(r // 2 + 1) * LANES]
    keep = _half0() if r % 2 == 0 else jnp.logical_not(_half0())
    return jnp.where(keep, q2, jnp.zeros_like(q2))


def _softmax_over_rows(s):
    m = jnp.max(s, axis=0, keepdims=True)
    e = jnp.exp2(s - m)
    l = jnp.sum(e, axis=0, keepdims=True)
    return e * jnp.where(m > 0.5 * NEG_INF, 1.0 / l, 0.0)


def _pair(even, odd):
    return jnp.where(_half0(), even, odd)


def _nsa_cw_kernel(q_ref, kc_ref, vct_ref, kw_ref, vwt_ref, ng_ref, stt_ref, yp_ref, mb_ref):
    t0 = pl.program_id(2) * Q_BLOCK
    qpos = t0 + _lane_iota((1, Q_BLOCK))
    gates = jax.nn.sigmoid(ng_ref[...].astype(_f32).T[0:16, :])
    kc = kc_ref[0, 0]
    vct = vct_ref[0, 0]
    nc = kc.shape[0]
    cend = lax.broadcasted_iota(jnp.int32, (nc, 1), 0) * CMP_STRIDE + (CMP_BLOCK - 1)
    cmask = cend <= qpos
    start = pl.multiple_of(jnp.maximum(t0 - WINDOW, 0), Q_BLOCK)
    wlen = WINDOW + Q_BLOCK
    kwin = kw_ref[0, 0, pl.ds(start, wlen), :]
    kpos = start + lax.broadcasted_iota(jnp.int32, (wlen, 1), 0)
    wmask = (kpos <= qpos) & (qpos - kpos < WINDOW)
    blk0 = start // Q_BLOCK

    qs = [_head_q(q_ref, r) for r in range(4)]
    s_cmp = [_dot_t(kc, q) for q in qs]
    s_win = [_dot_t(kwin, q) for q in qs]
    p_cmp = [_softmax_over_rows(jnp.where(cmask, s, NEG_INF)) for s in s_cmp]
    imp = (p_cmp[0] + p_cmp[1]) + (p_cmp[2] + p_cmp[3])
    e_win = []
    for s in s_win:
        s = jnp.where(wmask, s, NEG_INF)
        e_win.append(jnp.exp2(s - jnp.max(s, axis=0, keepdims=True)).astype(_bf16))
    vwt = jnp.concatenate([vwt_ref[0, 0, blk0 + c] for c in range(wlen // Q_BLOCK)], axis=1)
    yts = []
    for r in range(4):
        o_cmp = _dot(vct, p_cmp[r].astype(_bf16))[0:HEAD_DIM]
        win = _dot(vwt, e_win[r])
        o_win = win[0:HEAD_DIM] * (1.0 / win[HEAD_DIM:2 * HEAD_DIM])
        yts.append(gates[3 * r:3 * r + 1] * o_cmp + gates[3 * r + 2:3 * r + 3] * o_win)
    yp_ref[:, 0:LANES] = jnp.concatenate(yts[0:2], axis=0).T
    yp_ref[:, LANES:2 * LANES] = jnp.concatenate(yts[2:4], axis=0).T

    imp_hi = imp.astype(_bf16)
    rest = imp - imp_hi.astype(_f32)
    imp_mid = rest.astype(_bf16)
    imp_lo = (rest - imp_mid.astype(_f32)).astype(_bf16)
    st = stt_ref[...]
    p_slc = _dot(st, imp_hi) + (_dot(st, imp_mid) + _dot(st, imp_lo))
    selj = lax.broadcasted_iota(jnp.int32, (LANES, 1), 0)
    blk_t = qpos // SEL_BLOCK
    forced = (selj == 0) | (selj == blk_t) | (selj == blk_t - 1)
    score = jnp.where(forced, TAKEN, jnp.where(selj <= blk_t, p_slc, NEG_INF))
    seljf = selj.astype(_f32)
    picked = forced
    for _ in range(SEL_TOPK - 3):
        m = jnp.max(score, axis=0, keepdims=True)
        first = jnp.min(jnp.where(score == m, seljf, float(LANES)), axis=0, keepdims=True)
        hit = seljf == first
        picked = picked | hit
        score = jnp.where(hit, TAKEN, score)
    mb_ref[0] = jnp.where(picked, 0.0, NEG_INF).T.astype(_bf16)


def _nsa_cw_stage(nq_r, kc_d, vc_t, kw_d, vw_t, proj, stencil_t, batch, seq):
    n = nq_r.shape[0]
    nblk = seq // Q_BLOCK
    nc = seq // CMP_STRIDE
    qrow = lambda b, g, i: b * nblk + i
    return pl.pallas_call(
        _nsa_cw_kernel,
        grid=(batch, 2, nblk),
        in_specs=[pl.BlockSpec((Q_BLOCK, 2 * LANES), lambda b, g, i: (qrow(b, g, i), g)),
                  pl.BlockSpec((1, 1, nc, LANES), lambda b, g, i: (b, g, 0, 0)),
                  pl.BlockSpec((1, 1, LANES, nc), lambda b, g, i: (b, g, 0, 0)),
                  pl.BlockSpec((1, 1, seq, LANES), lambda b, g, i: (g, b, 0, 0)),
                  pl.BlockSpec((1, 1, nblk, LANES, LANES), lambda b, g, i: (g, b, 0, 0, 0)),
                  pl.BlockSpec((Q_BLOCK, LANES), lambda b, g, i: (qrow(b, g, i), CB_NG + g)),
                  pl.BlockSpec((LANES, nc), lambda b, g, i: (0, 0))],
        out_specs=[pl.BlockSpec((Q_BLOCK, 2 * LANES), lambda b, g, i: (qrow(b, g, i), g)),
                   pl.BlockSpec((1, Q_BLOCK, LANES), lambda b, g, i: (g, qrow(b, g, i), 0))],
        out_shape=[jax.ShapeDtypeStruct((n, 4 * LANES), _f32),
                   jax.ShapeDtypeStruct((2, n, LANES), _bf16)],
        compiler_params=_cparams(("arbitrary", "arbitrary", "arbitrary")),
        name="nsa_cmp_win_select",
    )(nq_r, kc_d, vc_t, kw_d.reshape(2, batch, seq, LANES), vw_t.reshape(2, batch, nblk, LANES, LANES), proj, stencil_t)


def _nsa_sel_kernel(q_ref, mb_ref, ka_ref, vat_ref, ng_ref, yp_ref, o_ref, qs_ref, m_ref, acc_ref, s_ref, p_ref, alpha_ref):
    tk = SEL_TK
    t0 = pl.program_id(2) * SEL_Q
    mb = mb_ref[0]
    for r in range(4):
        qs_ref[r * SEL_Q:(r + 1) * SEL_Q, 0:LANES] = _head_q(q_ref, r)
        qs_ref[r * SEL_Q:(r + 1) * SEL_Q, LANES:2 * LANES] = mb
    m_ref[...] = jnp.full(m_ref.shape, NEG_INF, _f32)
    acc_ref[...] = jnp.zeros(acc_ref.shape, _f32)
    p_ref[...] = jnp.zeros(p_ref.shape, _bf16)
    alpha_ref[...] = jnp.ones(alpha_ref.shape, _f32)
    qpos = t0 + _lane_iota((1, 4 * SEL_Q)) % SEL_Q

    def scores(kt):
        return _dot_t(ka_ref[0, 0, pl.ds(pl.multiple_of(kt * tk, tk), tk), :], qs_ref[...])

    def softmax_step(s):
        m_old = m_ref[...]
        m_new = jnp.maximum(m_old, jnp.max(s, axis=0, keepdims=True))
        m_ref[...] = m_new
        return jnp.exp2(s - m_new).astype(_bf16), jnp.exp2(m_old - m_new)

    def accumulate(kt, alpha, p):
        acc_ref[...] = alpha * acc_ref[...] + _dot(vat_ref[0, 0, kt], p)

    def trip(kt, carry):
        accumulate(jnp.maximum(kt - 1, 0), alpha_ref[...], p_ref[...])
        s = s_ref[...]
        s_ref[...] = scores(kt + 1)
        p, alpha = softmax_step(s)
        p_ref[...] = p
        alpha_ref[...] = alpha
        return carry

    n_full = t0 // tk
    s_ref[...] = scores(0)

    def two_trips(kp, carry):
        trip(2 * kp, carry)
        return trip(2 * kp + 1, carry)

    lax.fori_loop(0, n_full // 2, two_trips, 0)

    @pl.when(n_full % 2 == 1)
    def _():
        trip(n_full - 1, 0)

    accumulate(jnp.maximum(n_full - 1, 0), alpha_ref[...], p_ref[...])
    kpos = n_full * tk + lax.broadcasted_iota(jnp.int32, (tk, 1), 0)
    p, alpha = softmax_step(jnp.where(kpos <= qpos, s_ref[...], NEG_INF))
    accumulate(n_full, alpha, p)
    acc = acc_ref[...]
    o = acc[0:HEAD_DIM] * (1.0 / acc[HEAD_DIM:2 * HEAD_DIM])
    gates = jax.nn.sigmoid(ng_ref[...].astype(_f32).T[0:16, :])
    ys = [gates[3 * r + 1:3 * r + 2] * o[:, r * SEL_Q:(r + 1) * SEL_Q] for r in range(4)]
    o_ref[:, 0:LANES] = (yp_ref[:, 0:LANES] + jnp.concatenate(ys[0:2], axis=0).T).astype(o_ref.dtype)
    o_ref[:, LANES:2 * LANES] = (yp_ref[:, LANES:2 * LANES] + jnp.concatenate(ys[2:4], axis=0).T).astype(o_ref.dtype)


def _nsa_sel_stage(nq_r, mbias, k_aug, vs_t, proj, ypart, batch, seq):
    n = nq_r.shape[0]
    nblk = seq // SEL_Q
    qrow = lambda b, g, i: b * nblk + i
    return pl.pallas_call(
        _nsa_sel_kernel,
        grid=(batch, 2, nblk),
        in_specs=[pl.BlockSpec((SEL_Q, 2 * LANES), lambda b, g, i: (qrow(b, g, i), g)),
                  pl.BlockSpec((1, SEL_Q, LANES), lambda b, g, i: (g, qrow(b, g, i), 0)),
                  pl.BlockSpec((1, 1, seq, 2 * LANES), lambda b, g, i: (g, b, 0, 0)),
                  pl.BlockSpec((1, 1, seq // SEL_TK, LANES, SEL_TK), lambda b, g, i: (g, b, 0, 0, 0)),
                  pl.BlockSpec((SEL_Q, LANES), lambda b, g, i: (qrow(b, g, i), CB_NG + g)),
                  pl.BlockSpec((SEL_Q, 2 * LANES), lambda b, g, i: (qrow(b, g, i), g))],
        out_specs=pl.BlockSpec((SEL_Q, 2 * LANES), lambda b, g, i: (qrow(b, g, i), g)),
        out_shape=jax.ShapeDtypeStruct((n, 4 * LANES), _bf16),
        scratch_shapes=[pltpu.VMEM((4 * SEL_Q, 2 * LANES), _bf16),
                        pltpu.VMEM((1, 4 * SEL_Q), _f32),
                        pltpu.VMEM((LANES, 4 * SEL_Q), _f32),
                        pltpu.VMEM((SEL_TK, 4 * SEL_Q), _f32),
                        pltpu.VMEM((SEL_TK, 4 * SEL_Q), _bf16),
                        pltpu.VMEM((1, 4 * SEL_Q), _f32)],
        compiler_params=_cparams(("arbitrary", "arbitrary", "arbitrary")),
        name="nsa_selected",
    )(nq_r, mbias, k_aug.reshape(2, batch, seq, 2 * LANES), vs_t.reshape(2, batch, seq // SEL_TK, LANES, SEL_TK), proj, ypart)


def _sb_kernel(q_ref, k_ref, v_ref, o_ref, qs_ref, tail_ref, acc_ref):
    i = pl.program_id(1)
    h0 = _half0()
    heads = 2 * (q_ref.shape[1] // LANES)
    for h in range(heads):
        q = q_ref[:, (h // 2) * LANES:(h // 2 + 1) * LANES]
        keep = h0 if h % 2 == 0 else jnp.logical_not(h0)
        qs_ref[h] = jnp.where(keep, q, jnp.zeros_like(q)) * QK_SCALE
    tail_ref[...] = jnp.zeros(tail_ref.shape, _f32)
    acc_ref[...] = jnp.zeros(acc_ref.shape, _f32)
    rloc = lax.broadcasted_iota(jnp.int32, (Q_BLOCK, Q_BLOCK), 0)
    cloc = lax.broadcasted_iota(jnp.int32, (Q_BLOCK, Q_BLOCK), 1)
    later = jnp.where(rloc > cloc, 1.0, 0.0).astype(_bf16)

    def key_block(j, diagonal):
        k0 = pl.multiple_of(j * Q_BLOCK, Q_BLOCK)
        past = cloc < rloc
        cols = [slice((h // 2) * LANES, (h // 2 + 1) * LANES) for h in range(heads)]
        zs = [_dot_t(qs_ref[h], k_ref[0, pl.ds(k0, Q_BLOCK), cols[h]]) for h in range(heads)]
        log_beta, log_keep = [], []
        for z in zs:
            sp = jnp.maximum(z, 0.0) + jnp.log(1.0 + jnp.exp(-jnp.abs(z)))
            log_beta.append(z - sp)
            log_keep.append(jnp.where(past, -sp, 0.0) if diagonal else -sp)
        inner = []
        for lk in log_keep:
            hi = lk.astype(_bf16)
            lo = (lk - hi.astype(_f32)).astype(_bf16)
            inner.append(_dot(hi, later) + _dot(lo, later))
        probs = []
        for h in range(heads):
            a = jnp.exp(log_beta[h] + inner[h] + tail_ref[h])
            probs.append((jnp.where(past, a, 0.0) if diagonal else a).astype(_bf16))
        worst = jnp.full((Q_BLOCK, 1), -jnp.inf, _f32)
        for h in range(heads):
            acc_ref[h] = acc_ref[h] + _dot(probs[h], v_ref[0, pl.ds(k0, Q_BLOCK), cols[h]])
            tail = tail_ref[h] + jnp.sum(log_keep[h], axis=-1, keepdims=True)
            tail_ref[h] = tail
            worst = jnp.maximum(worst, tail)
        return jnp.max(worst)

    def cond(c):
        j, worst_tail = c
        return (j >= 0) & (worst_tail > SB_TAIL_CUTOFF)

    def body(c):
        j, _ = c
        return j - 1, key_block(j, False)

    lax.while_loop(cond, body, (i - 1, key_block(i, True)))
    for p in range(heads // 2):
        o_ref[:, p * LANES:(p + 1) * LANES] = jnp.where(h0, acc_ref[2 * p], acc_ref[2 * p + 1]).astype(o_ref.dtype)


def _sb_stage(proj, batch, seq):
    n = proj.shape[0]
    nblk = seq // Q_BLOCK
    width = 4 * LANES
    proj3 = proj.reshape(batch, seq, PROJ_W)
    return pl.pallas_call(
        _sb_kernel,
        grid=(batch, nblk),
        in_specs=[pl.BlockSpec((Q_BLOCK, width), lambda b, i: (b * nblk + i, CB_SBQ // 4)),
                  pl.BlockSpec((1, seq, width), lambda b, i: (b, 0, CB_SBK // 4)),
                  pl.BlockSpec((1, seq, width), lambda b, i: (b, 0, CB_SBV // 4))],
        out_specs=pl.BlockSpec((Q_BLOCK, width), lambda b, i: (b * nblk + i, 0)),
        out_shape=jax.ShapeDtypeStruct((n, width), _bf16),
        scratch_shapes=[pltpu.VMEM((8, Q_BLOCK, LANES), _bf16),
                        pltpu.VMEM((8, Q_BLOCK, 1), _f32),
                        pltpu.VMEM((8, Q_BLOCK, LANES), _f32)],
        compiler_params=_cparams(("arbitrary", "arbitrary")),
        name="stick_breaking",
    )(proj, proj3, proj3)


def _layer_norm(x, g, b):
    mu = jnp.mean(x, axis=-1, keepdims=True)
    xc = x - mu
    var = jnp.mean(xc * xc, axis=-1, keepdims=True)
    return xc * lax.rsqrt(var + LN_EPS) * g + b


def _merge_kernel(x_ref, ysb_ref, yns_ref, mg0_ref, mg1_ref, wsb_ref, wns_ref, wo_ref, g_ref, b_ref,
                  wr_ref, br_ref, h_ref, idx_ref, gw_ref):
    m0 = jax.nn.sigmoid(mg0_ref[...].astype(_f32))
    m1 = jax.nn.sigmoid(mg1_ref[...].astype(_f32))
    merged = m0 * _dot(ysb_ref[...], wsb_ref[...]) + m1 * _dot(yns_ref[...], wns_ref[...])
    pre = DEEPNORM_ALPHA * x_ref[...] + _dot(merged.astype(_bf16), wo_ref[...])
    h = _layer_norm(pre, g_ref[...], b_ref[...])
    tm = h.shape[0]
    for s in range(SLAB):
        h_ref[pl.ds(s, tm, stride=SLAB), :] = h[:, s * LANES:(s + 1) * LANES]
    h_hi = h.astype(_bf16)
    h_lo = (h - h_hi.astype(_f32)).astype(_bf16)
    w = wr_ref[...]
    w_hi = w.astype(_bf16)
    w_lo = (w - w_hi.astype(_f32)).astype(_bf16)
    logits = (_dot(h_hi, w_hi) + (_dot(h_hi, w_lo) + _dot(h_lo, w_hi))) + br_ref[...]
    lane = _lane_iota((1, LANES))
    lanef = lane.astype(_f32)
    lg = jnp.where(lane < N_EXPERTS, logits, TAKEN)
    vals, idxs = [], []
    for _ in range(TOP_K):
        m = jnp.max(lg, axis=-1, keepdims=True)
        first = jnp.min(jnp.where(lg == m, lanef, float(LANES)), axis=-1, keepdims=True)
        vals.append(m)
        idxs.append(first)
        lg = jnp.where(lanef == first, TAKEN, lg)
    es = [jnp.exp(v - vals[0]) for v in vals]
    inv = 1.0 / (es[0] + es[1] + es[2] + es[3])
    idx_t = jnp.zeros(lg.shape, _f32)
    gw_t = jnp.zeros(lg.shape, _f32)
    for k in range(TOP_K):
        idx_t = jnp.where(lane == k, idxs[k], idx_t)
        gw_t = jnp.where(lane == k, es[k] * inv, gw_t)
    idx_ref[...] = idx_t[:, :TOP_K].astype(jnp.int32)
    gw_ref[...] = gw_t[:, :TOP_K]


def _merge_stage(x2, y_sb, y_nsa, proj, wsb, wns, wo, g1, b1, wr, br):
    n = x2.shape[0]
    tm = 512
    row = lambda w: pl.BlockSpec((tm, w), lambda i: (i, 0))
    full = lambda a: pl.BlockSpec(a.shape, lambda i: (0,) * a.ndim)
    return pl.pallas_call(
        _merge_kernel,
        grid=(n // tm,),
        in_specs=[row(D_MODEL), row(4 * LANES), row(4 * LANES),
                  pl.BlockSpec((tm, D_MODEL), lambda i: (i, CB_MG // 8)),
                  pl.BlockSpec((tm, D_MODEL), lambda i: (i, CB_MG // 8 + 1)),
                  full(wsb), full(wns), full(wo), full(g1), full(b1), full(wr), full(br)],
        out_specs=[pl.BlockSpec((tm * SLAB, LANES), lambda i: (i, 0)), row(TOP_K), row(TOP_K)],
        out_shape=[jax.ShapeDtypeStruct((n * SLAB, LANES), _f32),
                   jax.ShapeDtypeStruct((n, TOP_K), jnp.int32),
                   jax.ShapeDtypeStruct((n, TOP_K), _f32)],
        compiler_params=_cparams(("arbitrary",)),
        name="merge_ln1_router",
    )(x2, y_sb, y_nsa, proj, proj, wsb, wns, wo, g1, b1, wr, br)


def _prep_w_in(w):
    main = w[:, :CB_NG * LANES]
    ng = w[:, CB_NG * LANES:CB_NG * LANES + 24]
    mg = w[:, CB_NG * LANES + 24:]
    pad = jnp.zeros((w.shape[0], LANES - 12), w.dtype)
    return jnp.concatenate([main, ng[:, :12], pad, ng[:, 12:], pad, mg], axis=1).astype(_bf16)


def _rope_tables(seq):
    half = HEAD_DIM // 2
    inv_freq = ROPE_THETA ** (-jnp.arange(half, dtype=_f32) / half)
    ang = jnp.arange(seq, dtype=_f32)[:, None] * inv_freq[None, :]
    cos = jnp.cos(ang)
    sin = jnp.sin(ang)
    cos128 = jnp.concatenate([cos, cos, cos, cos], axis=1)
    sin128 = jnp.concatenate([-sin, sin, -sin, sin], axis=1)
    return cos128, sin128


def _stencil(nc):
    n = np.arange(nc)[:, None]
    j = np.arange(LANES)[None, :]
    ratio = SEL_BLOCK // CMP_STRIDE
    ok = (n >= ratio * j - 1) & (n <= ratio * j + ratio - 1) & (n < nc - 1)
    return jnp.asarray(ok.astype(np.float32).T, dtype=_bf16)


def _attention_half(x, w_in, cmp_pe_k, cmp_w1_k, cmp_w2_k, cmp_pe_v, cmp_w1_v, cmp_w2_v,
                    w_proj_sb, w_proj_nsa, w_out, ln1_g, ln1_b, w_router, b_router):
    batch, seq, _ = x.shape
    n = batch * seq
    x2 = x.reshape(n, D_MODEL)
    proj = _in_proj(x2, _prep_w_in(w_in))
    cos, sin_signed = _rope_tables(seq)
    nq_r, kc_r, vc_r, k_aug, vs_t, kw_d, vw_t = _rope_stage(proj, cos, sin_signed, seq)
    kc_d, vc_t = _compress_stage(kc_r, vc_r, _compress_weights(cmp_pe_k, cmp_w1_k, cmp_w2_k),
                                 _compress_weights(cmp_pe_v, cmp_w1_v, cmp_w2_v), batch, seq)
    ypart, mbias = _nsa_cw_stage(nq_r, kc_d, vc_t, kw_d, vw_t, proj, _stencil(seq // CMP_STRIDE), batch, seq)
    y_nsa = _nsa_sel_stage(nq_r, mbias, k_aug, vs_t, proj, ypart, batch, seq)
    y_sb = _sb_stage(proj, batch, seq)
    wr = jnp.pad(w_router.astype(_f32), ((0, 0), (0, LANES - N_EXPERTS)))
    br = jnp.pad(b_router.astype(_f32), (0, LANES - N_EXPERTS)).reshape(1, LANES)
    return _merge_stage(x2, y_sb, y_nsa, proj, w_proj_sb.astype(_bf16), w_proj_nsa.astype(_bf16),
                        w_out.astype(_bf16), ln1_g.reshape(1, -1), ln1_b.reshape(1, -1), wr, br)


def _row_copy(src, src_row, dst, dst_row, sem):
    return pltpu.make_async_copy(src.at[pl.ds(src_row * SLAB, SLAB)], dst.at[pl.ds(dst_row * SLAB, SLAB)], sem)


def _dispatch_kernel(dest_ref, h_ref, zeros_ref, buf_ref, sem, *, tokens):
    del zeros_ref

    def issue(t, c):
        for k in range(TOP_K):
            _row_copy(h_ref, t, buf_ref, dest_ref[t * TOP_K + k], sem).start()
        return c

    lax.fori_loop(0, tokens, issue, 0)
    for _ in range(TOP_K):
        pltpu.make_async_copy(h_ref, buf_ref.at[pl.ds(0, tokens * SLAB)], sem).wait()


def _dispatch_stage(dest, h_slab, buf_rows):
    n = h_slab.shape[0] // SLAB
    tokens = 256
    zeros = jnp.zeros((buf_rows * SLAB, LANES), _f32)
    return pl.pallas_call(
        functools.partial(_dispatch_kernel, tokens=tokens),
        grid=(n // tokens,),
        in_specs=[pl.BlockSpec((tokens * TOP_K,), lambda i: (i,), memory_space=pltpu.SMEM),
                  pl.BlockSpec((tokens * SLAB, LANES), lambda i: (i, 0)),
                  pl.BlockSpec(memory_space=pl.ANY)],
        out_specs=pl.BlockSpec(memory_space=pl.ANY),
        out_shape=jax.ShapeDtypeStruct(zeros.shape, _f32),
        scratch_shapes=[pltpu.SemaphoreType.DMA(())],
        input_output_aliases={2: 0},
        compiler_params=pltpu.CompilerParams(dimension_semantics=("arbitrary",), has_side_effects=True),
        name="moe_dispatch",
    )(dest, h_slab, zeros)


def _slab_load(ref, rows):
    return jnp.concatenate([ref[pl.ds(s, rows, stride=SLAB), :] for s in range(SLAB)], axis=1)


def _expert_kernel(ce_ref, used_ref, x_ref, wgu_ref, bgu_ref, wd_ref, bd_ref, o_ref):
    del ce_ref
    c = pl.program_id(0)

    @pl.when(c < used_ref[0])
    def _():
        x = _slab_load(x_ref, MOE_ROWS).astype(_bf16)
        gu = _dot(x, wgu_ref[0]) + bgu_ref[0]
        gate = jnp.minimum(gu[:, :D_MODEL], SWIGLU_LIMIT)
        up = jnp.clip(gu[:, D_MODEL:], -SWIGLU_LIMIT, SWIGLU_LIMIT)
        h = gate * jax.nn.sigmoid(SWIGLU_ALPHA * gate) * (up + 1.0)
        y = _dot(h.astype(_bf16), wd_ref[0]) + bd_ref[0]
        for s in range(SLAB):
            o_ref[pl.ds(s, MOE_ROWS, stride=SLAB), :] = y[:, s * LANES:(s + 1) * LANES]

    @pl.when(c >= used_ref[0])
    def _():
        o_ref[...] = jnp.zeros(o_ref.shape, o_ref.dtype)


def _expert_stage(chunk_expert, n_used, buf, wgu, bgu, wd, bd):
    n_chunks = chunk_expert.shape[0]
    rows = MOE_ROWS * SLAB
    grid_spec = pltpu.PrefetchScalarGridSpec(
        num_scalar_prefetch=2,
        grid=(n_chunks,),
        in_specs=[pl.BlockSpec((rows, LANES), lambda c, ce, nu: (c, 0)),
                  pl.BlockSpec((1, D_MODEL, 2 * D_MODEL), lambda c, ce, nu: (ce[c], 0, 0)),
                  pl.BlockSpec((1, 1, 2 * D_MODEL), lambda c, ce, nu: (ce[c], 0, 0)),
                  pl.BlockSpec((1, D_MODEL, D_MODEL), lambda c, ce, nu: (ce[c], 0, 0)),
                  pl.BlockSpec((1, 1, D_MODEL), lambda c, ce, nu: (ce[c], 0, 0))],
        out_specs=pl.BlockSpec((rows, LANES), lambda c, ce, nu: (c, 0)))
    return pl.pallas_call(
        _expert_kernel,
        grid_spec=grid_spec,
        out_shape=jax.ShapeDtypeStruct(buf.shape, _f32),
        compiler_params=_cparams(("arbitrary",)),
        name="moe_experts",
    )(chunk_expert, n_used, buf, wgu, bgu, wd, bd)


def _combine_kernel(dest_ref, dest_next_ref, gw_ref, h_ref, eo_ref, g_ref, b_ref, o_ref, gbuf, ysl, sems, *, tokens):
    i = pl.program_id(0)
    slot = i % 2
    count = tokens * TOP_K

    def gather(idx_ref, into):
        def issue(t, c):
            for k in range(TOP_K):
                j = t * TOP_K + k
                _row_copy(eo_ref, idx_ref[j], gbuf.at[into], j, sems.at[into]).start()
            return c
        lax.fori_loop(0, tokens, issue, 0)

    @pl.when(i == 0)
    def _():
        gather(dest_ref, 0)

    @pl.when(i + 1 < pl.num_programs(0))
    def _():
        gather(dest_next_ref, 1 - slot)

    pltpu.make_async_copy(eo_ref.at[pl.ds(0, count * SLAB)], gbuf.at[slot], sems.at[slot]).wait()
    rows = gbuf.at[slot]

    unroll = 4

    def token_group(tg, c):
        for u in range(unroll):
            t = tg * unroll + u
            acc = DEEPNORM_ALPHA * h_ref[pl.ds(pl.multiple_of(t * SLAB, SLAB), SLAB), :]
            for k in range(TOP_K):
                j = t * TOP_K + k
                acc = acc + gw_ref[j] * rows[pl.ds(pl.multiple_of(j * SLAB, SLAB), SLAB), :]
            ysl[pl.ds(pl.multiple_of(t * SLAB, SLAB), SLAB), :] = acc
        return c

    lax.fori_loop(0, tokens // unroll, token_group, 0)
    o_ref[...] = _layer_norm(_slab_load(ysl, tokens), g_ref[...], b_ref[...])


def _combine_stage(dest, gate_w, h_slab, expert_out, g2, b2):
    n = h_slab.shape[0] // SLAB
    tokens = 256
    steps = n // tokens
    smem = lambda: pl.BlockSpec((tokens * TOP_K,), lambda i: (i,), memory_space=pltpu.SMEM)
    return pl.pallas_call(
        functools.partial(_combine_kernel, tokens=tokens),
        grid=(steps,),
        in_specs=[smem(),
                  pl.BlockSpec((tokens * TOP_K,), lambda i: (jnp.minimum(i + 1, steps - 1),), memory_space=pltpu.SMEM),
                  smem(),
                  pl.BlockSpec((tokens * SLAB, LANES), lambda i: (i, 0)),
                  pl.BlockSpec(memory_space=pl.ANY),
                  pl.BlockSpec((1, D_MODEL), lambda i: (0, 0)),
                  pl.BlockSpec((1, D_MODEL), lambda i: (0, 0))],
        out_specs=pl.BlockSpec((tokens, D_MODEL), lambda i: (i, 0)),
        out_shape=jax.ShapeDtypeStruct((n, D_MODEL), _f32),
        scratch_shapes=[pltpu.VMEM((2, tokens * TOP_K * SLAB, LANES), _f32),
                        pltpu.VMEM((tokens * SLAB, LANES), _f32),
                        pltpu.SemaphoreType.DMA((2,))],
        compiler_params=_cparams(("arbitrary",)),
        name="moe_combine_ln2",
    )(dest, dest, gate_w, h_slab, expert_out, g2, b2)


def _dispatch_plan(top_idx):
    m = top_idx.size
    e_flat = top_idx.reshape(m)
    onehot = (e_flat[:, None] == jnp.arange(N_EXPERTS, dtype=jnp.int32)[None, :]).astype(jnp.int32)
    csum = jnp.cumsum(onehot, axis=0)
    counts = csum[-1]
    padded = (counts + MOE_ROWS - 1) // MOE_ROWS * MOE_ROWS
    pends = jnp.cumsum(padded)
    pstarts = pends - padded
    dest = jnp.sum(onehot * (csum - 1 + pstarts[None, :]), axis=1).astype(jnp.int32)
    n_chunks = m // MOE_ROWS + N_EXPERTS
    chunk_start = jnp.arange(n_chunks, dtype=jnp.int32) * MOE_ROWS
    chunk_expert = jnp.minimum(jnp.sum((chunk_start[:, None] >= pends[None, :]).astype(jnp.int32), axis=1), N_EXPERTS - 1)
    n_used = (pends[-1] // MOE_ROWS).astype(jnp.int32).reshape(1)
    return dest, chunk_expert, n_used, n_chunks * MOE_ROWS


def _moe_half(h_slab, top_idx, gate_w, w_gate_up, b_gate_up, w_down, b_down, ln2_g, ln2_b):
    dest, chunk_expert, n_used, buf_rows = _dispatch_plan(top_idx)
    buf = _dispatch_stage(dest, h_slab, buf_rows)
    expert_out = _expert_stage(chunk_expert, n_used, buf, w_gate_up.astype(_bf16),
                               b_gate_up.reshape(N_EXPERTS, 1, -1), w_down.astype(_bf16),
                               b_down.reshape(N_EXPERTS, 1, -1))
    return _combine_stage(dest, gate_w.reshape(-1), h_slab, expert_out, ln2_g.reshape(1, -1), ln2_b.reshape(1, -1))


def kernel(x, w_in, cmp_pe_k, cmp_w1_k, cmp_w2_k, cmp_pe_v, cmp_w1_v, cmp_w2_v, w_proj_sb, w_proj_nsa, w_out,
           ln1_g, ln1_b, w_router, b_router, w_gate_up, b_gate_up, w_down, b_down, ln2_g, ln2_b):
    assert w_in.shape[0] == 1, "single-layer block"
    batch, seq, _ = x.shape
    assert seq % 512 == 0 and seq // SEL_BLOCK <= LANES and seq >= WINDOW + Q_BLOCK
    h_slab, top_idx, gate_w = _attention_half(
        x, w_in[0], cmp_pe_k[0], cmp_w1_k[0], cmp_w2_k[0], cmp_pe_v[0], cmp_w1_v[0], cmp_w2_v[0],
        w_proj_sb[0], w_proj_nsa[0], w_out[0], ln1_g[0], ln1_b[0], w_router[0], b_router[0])
    out = _moe_half(h_slab, top_idx, gate_w, w_gate_up[0], b_gate_up[0], w_down[0], b_down[0], ln2_g[0], ln2_b[0])
    return out.reshape(batch, seq, D_MODEL)
```

```python
import functools

import numpy as np
import jax
import jax.numpy as jnp
from jax import lax
from jax.experimental import pallas as pl
from jax.experimental.pallas import tpu as pltpu

D_MODEL = 1024
HEAD_DIM = 64
LANES = 128
Q_BLOCK = 128
CMP_BLOCK = 32
CMP_STRIDE = 16
SEL_BLOCK = 64
SEL_TOPK = 16
WINDOW = 512
ROPE_THETA = 10000.0
N_EXPERTS = 32
TOP_K = 4
SWIGLU_LIMIT = 7.0
SWIGLU_ALPHA = 1.702
LN_EPS = 1e-5
NEG_INF = -1e30
TAKEN = -3e38
DEEPNORM_ALPHA = 2.0 ** 0.25
QK_SCALE = HEAD_DIM ** -0.5
LOG2E = 1.4426950408889634

CB_SBQ, CB_SBK, CB_SBV, CB_NQ = 0, 4, 8, 12
CB_KC, CB_VC, CB_KS, CB_VS, CB_KW, CB_VW = 16, 17, 18, 19, 20, 21
CB_NG = 22
CB_MG = 24
PROJ_W = 40 * LANES

SB_TAIL_CUTOFF = -110.0

SEL_Q = 512
SEL_TK = 512
MOE_ROWS = 512
SLAB = D_MODEL // LANES
VMEM_LIMIT = 56 * 1024 * 1024

_bf16 = jnp.bfloat16
_f32 = jnp.float32


def _cparams(sem):
    return pltpu.CompilerParams(dimension_semantics=sem, vmem_limit_bytes=VMEM_LIMIT)


def _dot_t(a, b):
    return lax.dot_general(a, b, (((1,), (1,)), ((), ())), preferred_element_type=_f32)


def _dot(a, b):
    return jnp.dot(a, b, preferred_element_type=_f32)


def _lane_iota(shape):
    return lax.broadcasted_iota(jnp.int32, shape, len(shape) - 1)


def _half0(shape=(1, LANES)):
    return _lane_iota(shape) < HEAD_DIM


def _in_proj_kernel(x_ref, w_ref, o_ref):
    o_ref[...] = _dot(x_ref[...].astype(_bf16), w_ref[...]).astype(o_ref.dtype)


def _in_proj(x2, w):
    n = x2.shape[0]
    tm, tn = 512, 1280
    return pl.pallas_call(
        _in_proj_kernel,
        grid=(PROJ_W // tn, n // tm),
        in_specs=[pl.BlockSpec((tm, D_MODEL), lambda j, i: (i, 0)),
                  pl.BlockSpec((D_MODEL, tn), lambda j, i: (0, j))],
        out_specs=pl.BlockSpec((tm, tn), lambda j, i: (i, j)),
        out_shape=jax.ShapeDtypeStruct((n, PROJ_W), _bf16),
        compiler_params=_cparams(("arbitrary", "arbitrary")),
        name="in_proj",
    )(x2, w)


def _rope(x, cos, sin_signed):
    first = (_lane_iota((1, LANES)) % HEAD_DIM) < (HEAD_DIM // 2)
    swapped = jnp.where(first, pltpu.roll(x, LANES - HEAD_DIM // 2, 1), pltpu.roll(x, HEAD_DIM // 2, 1))
    return x * cos + swapped * sin_signed


def _dup(x, g):
    other = pltpu.roll(x, HEAD_DIM, 1)
    h0 = _half0()
    return jnp.where(h0, x, other) if g == 0 else jnp.where(h0, other, x)


def _rope_kernel(nq_ref, kc_ref, vc_ref, ks_ref, vs_ref, kw_ref, vw_ref, cos_ref, sin_ref,
                 nq_o, kc_o, vc_o, ka_o, vs_o, kw_o, vw_o, *, blocks_per_seq):
    ts = cos_ref.shape[0]
    cos = cos_ref[...]
    sin = sin_ref[...]
    for c in range(4):
        sl = slice(c * LANES, (c + 1) * LANES)
        nq_o[:, sl] = (_rope(nq_ref[:, sl].astype(_f32), cos, sin) * (QK_SCALE * LOG2E)).astype(_bf16)
    kc_o[...] = _rope(kc_ref[...].astype(_f32), cos, sin).astype(_bf16)
    vc_o[...] = vc_ref[...]
    ks = _rope(ks_ref[...].astype(_f32), cos, sin)
    kw = _rope(kw_ref[...].astype(_f32), cos, sin)
    vs = vs_ref[...].astype(_f32)
    vw = vw_ref[...].astype(_f32)
    pos = (pl.program_id(0) % blocks_per_seq) * ts + lax.broadcasted_iota(jnp.int32, (ts, LANES), 0)
    onehot = jnp.where(pos // SEL_BLOCK == _lane_iota((ts, LANES)), 1.0, 0.0).astype(_bf16)
    for g in range(2):
        ka_o[g, :, 0:LANES] = _dup(ks, g).astype(_bf16)
        ka_o[g, :, LANES:2 * LANES] = onehot
        vsa = jnp.where(_half0(), _dup(vs, g), 1.0)
        for c in range(ts // SEL_TK):
            vs_o[g, c] = vsa[c * SEL_TK:(c + 1) * SEL_TK, :].T.astype(_bf16)
        kw_o[g] = _dup(kw, g).astype(_bf16)
        vwd = jnp.where(_half0(), _dup(vw, g), 1.0)
        for c in range(ts // LANES):
            vw_o[g, c] = vwd[c * LANES:(c + 1) * LANES, :].T.astype(_bf16)


def _rope_stage(proj, cos, sin_signed, seq):
    n = proj.shape[0]
    ts = 512
    bps = seq // ts
    col = lambda cb: pl.BlockSpec((ts, LANES), lambda i, cb=cb: (i, cb))
    tab = pl.BlockSpec((ts, LANES), lambda i: (i % bps, 0))
    grp = lambda w: pl.BlockSpec((2, ts, w), lambda i: (0, i, 0))
    return pl.pallas_call(
        functools.partial(_rope_kernel, blocks_per_seq=bps),
        grid=(n // ts,),
        in_specs=[pl.BlockSpec((ts, 4 * LANES), lambda i: (i, CB_NQ // 4)),
                  col(CB_KC), col(CB_VC), col(CB_KS), col(CB_VS), col(CB_KW), col(CB_VW), tab, tab],
        out_specs=[pl.BlockSpec((ts, 4 * LANES), lambda i: (i, 0)),
                   pl.BlockSpec((ts, LANES), lambda i: (i, 0)),
                   pl.BlockSpec((ts, LANES), lambda i: (i, 0)),
                   grp(2 * LANES),
                   pl.BlockSpec((2, ts // SEL_TK, LANES, SEL_TK), lambda i: (0, i, 0, 0)),
                   grp(LANES),
                   pl.BlockSpec((2, ts // LANES, LANES, LANES), lambda i: (0, i, 0, 0))],
        out_shape=[jax.ShapeDtypeStruct((n, 4 * LANES), _bf16),
                   jax.ShapeDtypeStruct((n, LANES), _bf16),
                   jax.ShapeDtypeStruct((n, LANES), _bf16),
                   jax.ShapeDtypeStruct((2, n, 2 * LANES), _bf16),
                   jax.ShapeDtypeStruct((2, n // SEL_TK, LANES, SEL_TK), _bf16),
                   jax.ShapeDtypeStruct((2, n, LANES), _bf16),
                   jax.ShapeDtypeStruct((2, n // LANES, LANES, LANES), _bf16)],
        compiler_params=_cparams(("arbitrary",)),
        name="rope_layout",
    )(proj, proj, proj, proj, proj, proj, proj, cos, sin_signed)


def _gelu_tanh(x):
    return 0.5 * x * (1.0 + jnp.tanh(0.7978845608028654 * (x + 0.044715 * (x * x * x))))


def _compress_one(x_ref, pe_t, pe_b, w_t, w_b, w2, out_ref, transposed):
    x = x_ref[0].astype(_f32)
    a = _dot((x + pe_t[...]).astype(_bf16), w_t[...])
    b = _dot((x + pe_b[...]).astype(_bf16), w_b[...])
    nc = a.shape[0]
    pre = a + pltpu.roll(b, nc - 1, 0)
    y = _dot(_gelu_tanh(pre).astype(_bf16), w2[...])
    for g in range(2):
        d = _dup(y, g)
        out_ref[0, g] = (d.T if transposed else d).astype(_bf16)


def _compress_kernel(k_ref, v_ref, kpt, kpb, kwt, kwb, kw2, vpt, vpb, vwt, vwb, vw2, ko_ref, vo_ref):
    _compress_one(k_ref, kpt, kpb, kwt, kwb, kw2, ko_ref, False)
    _compress_one(v_ref, vpt, vpb, vwt, vwb, vw2, vo_ref, True)


def _compress_weights(pe, w1, w2):
    half = CMP_BLOCK // 2
    eye = jnp.eye(2, dtype=_f32)
    outs = []
    for part in range(2):
        w = w1[part * half * HEAD_DIM:(part + 1) * half * HEAD_DIM].reshape(half, HEAD_DIM, HEAD_DIM)
        wbd = (w[:, None, :, None, :] * eye[None, :, None, :, None]).reshape(half * 2 * HEAD_DIM, 2 * HEAD_DIM)
        p = jnp.broadcast_to(pe[part * half:(part + 1) * half, None, :], (half, 2, HEAD_DIM)).reshape(1, -1)
        outs.append((p.astype(_f32), wbd.astype(_bf16)))
    w2bd = (w2[None, :, None, :] * eye[:, None, :, None]).reshape(2 * HEAD_DIM, 2 * HEAD_DIM).astype(_bf16)
    (pt, wt), (pb, wb) = outs
    return pt, pb, wt, wb, w2bd


def _compress_stage(kc_r, vc_r, kparams, vparams, batch, seq):
    nc = seq // CMP_STRIDE
    width = CMP_STRIDE * LANES
    xs = pl.BlockSpec((1, nc, width), lambda b: (b, 0, 0))
    full = lambda a: pl.BlockSpec(a.shape, lambda b: (0,) * a.ndim)
    out = pl.BlockSpec((1, 2, nc, LANES), lambda b: (b, 0, 0, 0))
    weights = list(kparams) + list(vparams)
    return pl.pallas_call(
        _compress_kernel,
        grid=(batch,),
        in_specs=[xs, xs] + [full(a) for a in weights],
        out_specs=[out, pl.BlockSpec((1, 2, LANES, nc), lambda b: (b, 0, 0, 0))],
        out_shape=[jax.ShapeDtypeStruct((batch, 2, nc, LANES), _bf16),
                   jax.ShapeDtypeStruct((batch, 2, LANES, nc), _bf16)],
        compiler_params=_cparams(("arbitrary",)),
        name="compress",
    )(kc_r.reshape(batch, nc, width), vc_r.reshape(batch, nc, width), *weights)


def _head_q(q_ref, r):
    q2 = q_ref[:, (r // 2) * LANES:(r // 2 + 1) * LANES]
    keep = _half0() if r % 2 == 0 else jnp.logical_not(_half0())
    return jnp.where(keep, q2, jnp.zeros_like(q2))


def _softmax_over_rows(s):
    m = jnp.max(s, axis=0, keepdims=True)
    e = jnp.exp2(s - m)
    l = jnp.sum(e, axis=0, keepdims=True)
    return e * jnp.where(m > 0.5 * NEG_INF, 1.0 / l, 0.0)


def _pair(even, odd):
    return jnp.where(_half0(), even, odd)


def _nsa_cw_kernel(q_ref, kc_ref, vct_ref, kw_ref, vwt_ref, ng_ref, stt_ref, yp_ref, mb_ref):
    t0 = pl.program_id(2) * Q_BLOCK
    qpos = t0 + _lane_iota((1, Q_BLOCK))
    gates = jax.nn.sigmoid(ng_ref[...].astype(_f32).T[0:16, :])
    kc = kc_ref[0, 0]
    vct = vct_ref[0, 0]
    nc = kc.shape[0]
    cend = lax.broadcasted_iota(jnp.int32, (nc, 1), 0) * CMP_STRIDE + (CMP_BLOCK - 1)
    cmask = cend <= qpos
    start = pl.multiple_of(jnp.maximum(t0 - WINDOW, 0), Q_BLOCK)
    wlen = WINDOW + Q_BLOCK
    kwin = kw_ref[0, 0, pl.ds(start, wlen), :]
    kpos = start + lax.broadcasted_iota(jnp.int32, (wlen, 1), 0)
    wmask = (kpos <= qpos) & (qpos - kpos < WINDOW)
    blk0 = start // Q_BLOCK

    qs = [_head_q(q_ref, r) for r in range(4)]
    s_cmp = [_dot_t(kc, q) for q in qs]
    s_win = [_dot_t(kwin, q) for q in qs]
    p_cmp = [_softmax_over_rows(jnp.where(cmask, s, NEG_INF)) for s in s_cmp]
    imp = (p_cmp[0] + p_cmp[1]) + (p_cmp[2] + p_cmp[3])
    e_win = []
    for s in s_win:
        s = jnp.where(wmask, s, NEG_INF)
        e_win.append(jnp.exp2(s - jnp.max(s, axis=0, keepdims=True)).astype(_bf16))
    vwt = jnp.concatenate([vwt_ref[0, 0, blk0 + c] for c in range(wlen // Q_BLOCK)], axis=1)
    yts = []
    for r in range(4):
        o_cmp = _dot(vct, p_cmp[r].astype(_bf16))[0:HEAD_DIM]
        win = _dot(vwt, e_win[r])
        o_win = win[0:HEAD_DIM] * (1.0 / win[HEAD_DIM:2 * HEAD_DIM])
        yts.append(gates[3 * r:3 * r + 1] * o_cmp + gates[3 * r + 2:3 * r + 3] * o_win)
    yp_ref[:, 0:LANES] = jnp.concatenate(yts[0:2], axis=0).T
    yp_ref[:, LANES:2 * LANES] = jnp.concatenate(yts[2:4], axis=0).T

    imp_hi = imp.astype(_bf16)
    rest = imp - imp_hi.astype(_f32)
    imp_mid = rest.astype(_bf16)
    imp_lo = (rest - imp_mid.astype(_f32)).astype(_bf16)
    st = stt_ref[...]
    p_slc = _dot(st, imp_hi) + (_dot(st, imp_mid) + _dot(st, imp_lo))
    selj = lax.broadcasted_iota(jnp.int32, (LANES, 1), 0)
    blk_t = qpos // SEL_BLOCK
    forced = (selj == 0) | (selj == blk_t) | (selj == blk_t - 1)
    score = jnp.where(forced, TAKEN, jnp.where(selj <= blk_t, p_slc, NEG_INF))
    seljf = selj.astype(_f32)
    picked = forced
    for _ in range(SEL_TOPK - 3):
        m = jnp.max(score, axis=0, keepdims=True)
        first = jnp.min(jnp.where(score == m, seljf, float(LANES)), axis=0, keepdims=True)
        hit = seljf == first
        picked = picked | hit
        score = jnp.where(hit, TAKEN, score)
    mb_ref[0] = jnp.where(picked, 0.0, NEG_INF).T.astype(_bf16)


def _nsa_cw_stage(nq_r, kc_d, vc_t, kw_d, vw_t, proj, stencil_t, batch, seq):
    n = nq_r.shape[0]
    nblk = seq // Q_BLOCK
    nc = seq // CMP_STRIDE
    qrow = lambda b, g, i: b * nblk + i
    return pl.pallas_call(
        _nsa_cw_kernel,
        grid=(batch, 2, nblk),
        in_specs=[pl.BlockSpec((Q_BLOCK, 2 * LANES), lambda b, g, i: (qrow(b, g, i), g)),
                  pl.BlockSpec((1, 1, nc, LANES), lambda b, g, i: (b, g, 0, 0)),
                  pl.BlockSpec((1, 1, LANES, nc), lambda b, g, i: (b, g, 0, 0)),
                  pl.BlockSpec((1, 1, seq, LANES), lambda b, g, i: (g, b, 0, 0)),
                  pl.BlockSpec((1, 1, nblk, LANES, LANES), lambda b, g, i: (g, b, 0, 0, 0)),
                  pl.BlockSpec((Q_BLOCK, LANES), lambda b, g, i: (qrow(b, g, i), CB_NG + g)),
                  pl.BlockSpec((LANES, nc), lambda b, g, i: (0, 0))],
        out_specs=[pl.BlockSpec((Q_BLOCK, 2 * LANES), lambda b, g, i: (qrow(b, g, i), g)),
                   pl.BlockSpec((1, Q_BLOCK, LANES), lambda b, g, i: (g, qrow(b, g, i), 0))],
        out_shape=[jax.ShapeDtypeStruct((n, 4 * LANES), _f32),
                   jax.ShapeDtypeStruct((2, n, LANES), _bf16)],
        compiler_params=_cparams(("arbitrary", "arbitrary", "arbitrary")),
        name="nsa_cmp_win_select",
    )(nq_r, kc_d, vc_t, kw_d.reshape(2, batch, seq, LANES), vw_t.reshape(2, batch, nblk, LANES, LANES), proj, stencil_t)


def _nsa_sel_kernel(q_ref, mb_ref, ka_ref, vat_ref, ng_ref, yp_ref, o_ref, qs_ref, m_ref, acc_ref, s_ref, p_ref, alpha_ref):
    tk = SEL_TK
    t0 = pl.program_id(2) * SEL_Q
    mb = mb_ref[0]
    for r in range(4):
        qs_ref[r * SEL_Q:(r + 1) * SEL_Q, 0:LANES] = _head_q(q_ref, r)
        qs_ref[r * SEL_Q:(r + 1) * SEL_Q, LANES:2 * LANES] = mb
    m_ref[...] = jnp.full(m_ref.shape, NEG_INF, _f32)
    acc_ref[...] = jnp.zeros(acc_ref.shape, _f32)
    p_ref[...] = jnp.zeros(p_ref.shape, _bf16)
    alpha_ref[...] = jnp.ones(alpha_ref.shape, _f32)
    qpos = t0 + _lane_iota((1, 4 * SEL_Q)) % SEL_Q

    def scores(kt):
        return _dot_t(ka_ref[0, 0, pl.ds(pl.multiple_of(kt * tk, tk), tk), :], qs_ref[...])

    def softmax_step(s):
        m_old = m_ref[...]
        m_new = jnp.maximum(m_old, jnp.max(s, axis=0, keepdims=True))
        m_ref[...] = m_new
        return jnp.exp2(s - m_new).astype(_bf16), jnp.exp2(m_old - m_new)

    def accumulate(kt, alpha, p):
        acc_ref[...] = alpha * acc_ref[...] + _dot(vat_ref[0, 0, kt], p)

    def trip(kt, carry):
        accumulate(jnp.maximum(kt - 1, 0), alpha_ref[...], p_ref[...])
        s = s_ref[...]
        s_ref[...] = scores(kt + 1)
        p, alpha = softmax_step(s)
        p_ref[...] = p
        alpha_ref[...] = alpha
        return carry

    n_full = t0 // tk
    s_ref[...] = scores(0)

    def two_trips(kp, carry):
        trip(2 * kp, carry)
        return trip(2 * kp + 1, carry)

    lax.fori_loop(0, n_full // 2, two_trips, 0)

    @pl.when(n_full % 2 == 1)
    def _():
        trip(n_full - 1, 0)

    accumulate(jnp.maximum(n_full - 1, 0), alpha_ref[...], p_ref[...])
    kpos = n_full * tk + lax.broadcasted_iota(jnp.int32, (tk, 1), 0)
    p, alpha = softmax_step(jnp.where(kpos <= qpos, s_ref[...], NEG_INF))
    accumulate(n_full, alpha, p)
    acc = acc_ref[...]
    o = acc[0:HEAD_DIM] * (1.0 / acc[HEAD_DIM:2 * HEAD_DIM])
    gates = jax.nn.sigmoid(ng_ref[...].astype(_f32).T[0:16, :])
    ys = [gates[3 * r + 1:3 * r + 2] * o[:, r * SEL_Q:(r + 1) * SEL_Q] for r in range(4)]
    o_ref[:, 0:LANES] = (yp_ref[:, 0:LANES] + jnp.concatenate(ys[0:2], axis=0).T).astype(o_ref.dtype)
    o_ref[:, LANES:2 * LANES] = (yp_ref[:, LANES:2 * LANES] + jnp.concatenate(ys[2:4], axis=0).T).astype(o_ref.dtype)


def _nsa_sel_stage(nq_r, mbias, k_aug, vs_t, proj, ypart, batch, seq):
    n = nq_r.shape[0]
    nblk = seq // SEL_Q
    qrow = lambda b, g, i: b * nblk + i
    return pl.pallas_call(
        _nsa_sel_kernel,
        grid=(batch, 2, nblk),
        in_specs=[pl.BlockSpec((SEL_Q, 2 * LANES), lambda b, g, i: (qrow(b, g, i), g)),
                  pl.BlockSpec((1, SEL_Q, LANES), lambda b, g, i: (g, qrow(b, g, i), 0)),
                  pl.BlockSpec((1, 1, seq, 2 * LANES), lambda b, g, i: (g, b, 0, 0)),
                  pl.BlockSpec((1, 1, seq // SEL_TK, LANES, SEL_TK), lambda b, g, i: (g, b, 0, 0, 0)),
                  pl.BlockSpec((SEL_Q, LANES), lambda b, g, i: (qrow(b, g, i), CB_NG + g)),
                  pl.BlockSpec((SEL_Q, 2 * LANES), lambda b, g, i: (qrow(b, g, i), g))],
        out_specs=pl.BlockSpec((SEL_Q, 2 * LANES), lambda b, g, i: (qrow(b, g, i), g)),
        out_shape=jax.ShapeDtypeStruct((n, 4 * LANES), _bf16),
        scratch_shapes=[pltpu.VMEM((4 * SEL_Q, 2 * LANES), _bf16),
                        pltpu.VMEM((1, 4 * SEL_Q), _f32),
                        pltpu.VMEM((LANES, 4 * SEL_Q), _f32),
                        pltpu.VMEM((SEL_TK, 4 * SEL_Q), _f32),
                        pltpu.VMEM((SEL_TK, 4 * SEL_Q), _bf16),
                        pltpu.VMEM((1, 4 * SEL_Q), _f32)],
        compiler_params=_cparams(("arbitrary", "arbitrary", "arbitrary")),
        name="nsa_selected",
    )(nq_r, mbias, k_aug.reshape(2, batch, seq, 2 * LANES), vs_t.reshape(2, batch, seq // SEL_TK, LANES, SEL_TK), proj, ypart)


def _sb_kernel(q_ref, k_ref, v_ref, o_ref, qs_ref, tail_ref, acc_ref):
    i = pl.program_id(1)
    h0 = _half0()
    heads = 2 * (q_ref.shape[1] // LANES)
    for h in range(heads):
        q = q_ref[:, (h // 2) * LANES:(h // 2 + 1) * LANES]
        keep = h0 if h % 2 == 0 else jnp.logical_not(h0)
        qs_ref[h] = jnp.where(keep, q, jnp.zeros_like(q)) * QK_SCALE
    tail_ref[...] = jnp.zeros(tail_ref.shape, _f32)
    acc_ref[...] = jnp.zeros(acc_ref.shape, _f32)
    rloc = lax.broadcasted_iota(jnp.int32, (Q_BLOCK, Q_BLOCK), 0)
    cloc = lax.broadcasted_iota(jnp.int32, (Q_BLOCK, Q_BLOCK), 1)
    later = jnp.where(rloc > cloc, 1.0, 0.0).astype(_bf16)

    def key_block(j, diagonal):
        k0 = pl.multiple_of(j * Q_BLOCK, Q_BLOCK)
        past = cloc < rloc
        cols = [slice((h // 2) * LANES, (h // 2 + 1) * LANES) for h in range(heads)]
        zs = [_dot_t(qs_ref[h], k_ref[0, pl.ds(k0, Q_BLOCK), cols[h]]) for h in range(heads)]
        log_beta, log_keep = [], []
        for z in zs:
            sp = jnp.maximum(z, 0.0) + jnp.log(1.0 + jnp.exp(-jnp.abs(z)))
            log_beta.append(z - sp)
            log_keep.append(jnp.where(past, -sp, 0.0) if diagonal else -sp)
        inner = []
        for lk in log_keep:
            hi = lk.astype(_bf16)
            lo = (lk - hi.astype(_f32)).astype(_bf16)
            inner.append(_dot(hi, later) + _dot(lo, later))
        probs = []
        for h in range(heads):
            a = jnp.exp(log_beta[h] + inner[h] + tail_ref[h])
            probs.append((jnp.where(past, a, 0.0) if diagonal else a).astype(_bf16))
        worst = jnp.full((Q_BLOCK, 1), -jnp.inf, _f32)
        for h in range(heads):
            acc_ref[h] = acc_ref[h] + _dot(probs[h], v_ref[0, pl.ds(k0, Q_BLOCK), cols[h]])
            tail = tail_ref[h] + jnp.sum(log_keep[h], axis=-1, keepdims=True)
            tail_ref[h] = tail
            worst = jnp.maximum(worst, tail)
        return jnp.max(worst)

    def cond(c):
        j, worst_tail = c
        return (j >= 0) & (worst_tail > SB_TAIL_CUTOFF)

    def body(c):
        j, _ = c
        return j - 1, key_block(j, False)

    lax.while_loop(cond, body, (i - 1, key_block(i, True)))
    for p in range(heads // 2):
        o_ref[:, p * LANES:(p + 1) * LANES] = jnp.where(h0, acc_ref[2 * p], acc_ref[2 * p + 1]).astype(o_ref.dtype)


def _sb_stage(proj, batch, seq):
    n = proj.shape[0]
    nblk = seq // Q_BLOCK
    width = 4 * LANES
    proj3 = proj.reshape(batch, seq, PROJ_W)
    return pl.pallas_call(
        _sb_kernel,
        grid=(batch, nblk),
        in_specs=[pl.BlockSpec((Q_BLOCK, width), lambda b, i: (b * nblk + i, CB_SBQ // 4)),
                  pl.BlockSpec((1, seq, width), lambda b, i: (b, 0, CB_SBK // 4)),
                  pl.BlockSpec((1, seq, width), lambda b, i: (b, 0, CB_SBV // 4))],
        out_specs=pl.BlockSpec((Q_BLOCK, width), lambda b, i: (b * nblk + i, 0)),
        out_shape=jax.ShapeDtypeStruct((n, width), _bf16),
        scratch_shapes=[pltpu.VMEM((8, Q_BLOCK, LANES), _bf16),
                        pltpu.VMEM((8, Q_BLOCK, 1), _f32),
                        pltpu.VMEM((8, Q_BLOCK, LANES), _f32)],
        compiler_params=_cparams(("arbitrary", "arbitrary")),
        name="stick_breaking",
    )(proj, proj3, proj3)


def _layer_norm(x, g, b):
    mu = jnp.mean(x, axis=-1, keepdims=True)
    xc = x - mu
    var = jnp.mean(xc * xc, axis=-1, keepdims=True)
    return xc * lax.rsqrt(var + LN_EPS) * g + b


def _merge_kernel(x_ref, ysb_ref, yns_ref, mg0_ref, mg1_ref, wsb_ref, wns_ref, wo_ref, g_ref, b_ref,
                  wr_ref, br_ref, h_ref, idx_ref, gw_ref):
    m0 = jax.nn.sigmoid(mg0_ref[...].astype(_f32))
    m1 = jax.nn.sigmoid(mg1_ref[...].astype(_f32))
    merged = m0 * _dot(ysb_ref[...], wsb_ref[...]) + m1 * _dot(yns_ref[...], wns_ref[...])
    pre = DEEPNORM_ALPHA * x_ref[...] + _dot(merged.astype(_bf16), wo_ref[...])
    h = _layer_norm(pre, g_ref[...], b_ref[...])
    tm = h.shape[0]
    for s in range(SLAB):
        h_ref[pl.ds(s, tm, stride=SLAB), :] = h[:, s * LANES:(s + 1) * LANES]
    h_hi = h.astype(_bf16)
    h_lo = (h - h_hi.astype(_f32)).astype(_bf16)
    w = wr_ref[...]
    w_hi = w.astype(_bf16)
    w_lo = (w - w_hi.astype(_f32)).astype(_bf16)
    logits = (_dot(h_hi, w_hi) + (_dot(h_hi, w_lo) + _dot(h_lo, w_hi))) + br_ref[...]
    lane = _lane_iota((1, LANES))
    lanef = lane.astype(_f32)
    lg = jnp.where(lane < N_EXPERTS, logits, TAKEN)
    vals, idxs = [], []
    for _ in range(TOP_K):
        m = jnp.max(lg, axis=-1, keepdims=True)
        first = jnp.min(jnp.where(lg == m, lanef, float(LANES)), axis=-1, keepdims=True)
        vals.append(m)
        idxs.append(first)
        lg = jnp.where(lanef == first, TAKEN, lg)
    es = [jnp.exp(v - vals[0]) for v in vals]
    inv = 1.0 / (es[0] + es[1] + es[2] + es[3])
    idx_t = jnp.zeros(lg.shape, _f32)
    gw_t = jnp.zeros(lg.shape, _f32)
    for k in range(TOP_K):
        idx_t = jnp.where(lane == k, idxs[k], idx_t)
        gw_t = jnp.where(lane == k, es[k] * inv, gw_t)
    idx_ref[...] = idx_t[:, :TOP_K].astype(jnp.int32)
    gw_ref[...] = gw_t[:, :TOP_K]


def _merge_stage(x2, y_sb, y_nsa, proj, wsb, wns, wo, g1, b1, wr, br):
    n = x2.shape[0]
    tm = 512
    row = lambda w: pl.BlockSpec((tm, w), lambda i: (i, 0))
    full = lambda a: pl.BlockSpec(a.shape, lambda i: (0,) * a.ndim)
    return pl.pallas_call(
        _merge_kernel,
        grid=(n // tm,),
        in_specs=[row(D_MODEL), row(4 * LANES), row(4 * LANES),
                  pl.BlockSpec((tm, D_MODEL), lambda i: (i, CB_MG // 8)),
                  pl.BlockSpec((tm, D_MODEL), lambda i: (i, CB_MG // 8 + 1)),
                  full(wsb), full(wns), full(wo), full(g1), full(b1), full(wr), full(br)],
        out_specs=[pl.BlockSpec((tm * SLAB, LANES), lambda i: (i, 0)), row(TOP_K), row(TOP_K)],
        out_shape=[jax.ShapeDtypeStruct((n * SLAB, LANES), _f32),
                   jax.ShapeDtypeStruct((n, TOP_K), jnp.int32),
                   jax.ShapeDtypeStruct((n, TOP_K), _f32)],
        compiler_params=_cparams(("arbitrary",)),
        name="merge_ln1_router",
    )(x2, y_sb, y_nsa, proj, proj, wsb, wns, wo, g1, b1, wr, br)


def _prep_w_in(w):
    main = w[:, :CB_NG * LANES]
    ng = w[:, CB_NG * LANES:CB_NG * LANES + 24]
    mg = w[:, CB_NG * LANES + 24:]
    pad = jnp.zeros((w.shape[0], LANES - 12), w.dtype)
    return jnp.concatenate([main, ng[:, :12], pad, ng[:, 12:], pad, mg], axis=1).astype(_bf16)


def _rope_tables(seq):
    half = HEAD_DIM // 2
    inv_freq = ROPE_THETA ** (-jnp.arange(half, dtype=_f32) / half)
    ang = jnp.arange(seq, dtype=_f32)[:, None] * inv_freq[None, :]
    cos = jnp.cos(ang)
    sin = jnp.sin(ang)
    cos128 = jnp.concatenate([cos, cos, cos, cos], axis=1)
    sin128 = jnp.concatenate([-sin, sin, -sin, sin], axis=1)
    return cos128, sin128


def _stencil(nc):
    n = np.arange(nc)[:, None]
    j = np.arange(LANES)[None, :]
    ratio = SEL_BLOCK // CMP_STRIDE
    ok = (n >= ratio * j - 1) & (n <= ratio * j + ratio - 1) & (n < nc - 1)
    return jnp.asarray(ok.astype(np.float32).T, dtype=_bf16)


def _attention_half(x, w_in, cmp_pe_k, cmp_w1_k, cmp_w2_k, cmp_pe_v, cmp_w1_v, cmp_w2_v,
                    w_proj_sb, w_proj_nsa, w_out, ln1_g, ln1_b, w_router, b_router):
    batch, seq, _ = x.shape
    n = batch * seq
    x2 = x.reshape(n, D_MODEL)
    proj = _in_proj(x2, _prep_w_in(w_in))
    cos, sin_signed = _rope_tables(seq)
    nq_r, kc_r, vc_r, k_aug, vs_t, kw_d, vw_t = _rope_stage(proj, cos, sin_signed, seq)
    kc_d, vc_t = _compress_stage(kc_r, vc_r, _compress_weights(cmp_pe_k, cmp_w1_k, cmp_w2_k),
                                 _compress_weights(cmp_pe_v, cmp_w1_v, cmp_w2_v), batch, seq)
    ypart, mbias = _nsa_cw_stage(nq_r, kc_d, vc_t, kw_d, vw_t, proj, _stencil(seq // CMP_STRIDE), batch, seq)
    y_nsa = _nsa_sel_stage(nq_r, mbias, k_aug, vs_t, proj, ypart, batch, seq)
    y_sb = _sb_stage(proj, batch, seq)
    wr = jnp.pad(w_router.astype(_f32), ((0, 0), (0, LANES - N_EXPERTS)))
    br = jnp.pad(b_router.astype(_f32), (0, LANES - N_EXPERTS)).reshape(1, LANES)
    return _merge_stage(x2, y_sb, y_nsa, proj, w_proj_sb.astype(_bf16), w_proj_nsa.astype(_bf16),
                        w_out.astype(_bf16), ln1_g.reshape(1, -1), ln1_b.reshape(1, -1), wr, br)


def _row_copy(src, src_row, dst, dst_row, sem):
    return pltpu.make_async_copy(src.at[pl.ds(src_row * SLAB, SLAB)], dst.at[pl.ds(dst_row * SLAB, SLAB)], sem)


def _dispatch_kernel(pad_start_ref, pad_len_ref, dest_ref, h_hbm, buf_ref, hbuf, zslab, in_sems, row_sems, zero_sem,
                     *, tokens):
    i = pl.program_id(0)
    last = pl.num_programs(0) - 1
    slot = i % 2
    block_rows = tokens * SLAB

    def block_copy(step, s):
        return pltpu.make_async_copy(h_hbm.at[pl.ds(step * block_rows, block_rows)], hbuf.at[s], in_sems.at[s])

    def wait_rows(s):
        for _ in range(TOP_K):
            pltpu.make_async_copy(hbuf.at[s], buf_ref.at[pl.ds(0, block_rows)], row_sems.at[s]).wait()

    def padding_rows(act):
        def per_expert(e, c):
            def per_row(r, c2):
                act(_row_copy(zslab, 0, buf_ref, pad_start_ref[e] + r, zero_sem))
                return c2
            lax.fori_loop(0, pad_len_ref[e], per_row, 0)
            return c
        lax.fori_loop(0, N_EXPERTS, per_expert, 0)

    @pl.when(i == 0)
    def _():
        block_copy(0, 0).start()
        zslab[...] = jnp.zeros(zslab.shape, _f32)
        padding_rows(lambda cp: cp.start())

    @pl.when(i > 0)
    def _():
        wait_rows(1 - slot)

    @pl.when(i < last)
    def _():
        block_copy(i + 1, 1 - slot).start()

    block_copy(i, slot).wait()
    src = hbuf.at[slot]

    def issue(t, c):
        for k in range(TOP_K):
            _row_copy(src, t, buf_ref, dest_ref[t * TOP_K + k], row_sems.at[slot]).start()
        return c

    lax.fori_loop(0, tokens, issue, 0)

    @pl.when(i == last)
    def _():
        wait_rows(slot)
        padding_rows(lambda cp: cp.wait())


def _dispatch_stage(dest, pad_start, pad_len, h_slab, buf_rows):
    n = h_slab.shape[0] // SLAB
    tokens = 256
    grid_spec = pltpu.PrefetchScalarGridSpec(
        num_scalar_prefetch=2,
        grid=(n // tokens,),
        in_specs=[pl.BlockSpec((tokens * TOP_K,), lambda i, ps, pn: (i,), memory_space=pltpu.SMEM),
                  pl.BlockSpec(memory_space=pl.ANY)],
        out_specs=pl.BlockSpec(memory_space=pl.ANY),
        scratch_shapes=[pltpu.VMEM((2, tokens * SLAB, LANES), _f32),
                        pltpu.VMEM((SLAB, LANES), _f32),
                        pltpu.SemaphoreType.DMA((2,)),
                        pltpu.SemaphoreType.DMA((2,)),
                        pltpu.SemaphoreType.DMA(())])
    return pl.pallas_call(
        functools.partial(_dispatch_kernel, tokens=tokens),
        grid_spec=grid_spec,
        out_shape=jax.ShapeDtypeStruct((buf_rows * SLAB, LANES), _f32),
        compiler_params=pltpu.CompilerParams(dimension_semantics=("arbitrary",), has_side_effects=True),
        name="moe_dispatch",
    )(pad_start, pad_len, dest, h_slab)


def _slab_load(ref, rows):
    return jnp.concatenate([ref[pl.ds(s, rows, stride=SLAB), :] for s in range(SLAB)], axis=1)


def _expert_kernel(ce_ref, used_ref, x_ref, wgu_ref, bgu_ref, wd_ref, bd_ref, o_ref):
    del ce_ref
    c = pl.program_id(0)

    @pl.when(c < used_ref[0])
    def _():
        x = _slab_load(x_ref, MOE_ROWS).astype(_bf16)
        gu = _dot(x, wgu_ref[0]) + bgu_ref[0]
        gate = jnp.minimum(gu[:, :D_MODEL], SWIGLU_LIMIT)
        up = jnp.clip(gu[:, D_MODEL:], -SWIGLU_LIMIT, SWIGLU_LIMIT)
        h = gate * jax.nn.sigmoid(SWIGLU_ALPHA * gate) * (up + 1.0)
        y = _dot(h.astype(_bf16), wd_ref[0]) + bd_ref[0]
        for s in range(SLAB):
            o_ref[pl.ds(s, MOE_ROWS, stride=SLAB), :] = y[:, s * LANES:(s + 1) * LANES]

    @pl.when(c >= used_ref[0])
    def _():
        o_ref[...] = jnp.zeros(o_ref.shape, o_ref.dtype)


def _expert_stage(chunk_expert, n_used, buf, wgu, bgu, wd, bd):
    n_chunks = chunk_expert.shape[0]
    rows = MOE_ROWS * SLAB
    grid_spec = pltpu.PrefetchScalarGridSpec(
        num_scalar_prefetch=2,
        grid=(n_chunks,),
        in_specs=[pl.BlockSpec((rows, LANES), lambda c, ce, nu: (c, 0)),
                  pl.BlockSpec((1, D_MODEL, 2 * D_MODEL), lambda c, ce, nu: (ce[c], 0, 0)),
                  pl.BlockSpec((1, 1, 2 * D_MODEL), lambda c, ce, nu: (ce[c], 0, 0)),
                  pl.BlockSpec((1, D_MODEL, D_MODEL), lambda c, ce, nu: (ce[c], 0, 0)),
                  pl.BlockSpec((1, 1, D_MODEL), lambda c, ce, nu: (ce[c], 0, 0))],
        out_specs=pl.BlockSpec((rows, LANES), lambda c, ce, nu: (c, 0)))
    return pl.pallas_call(
        _expert_kernel,
        grid_spec=grid_spec,
        out_shape=jax.ShapeDtypeStruct(buf.shape, _f32),
        compiler_params=_cparams(("arbitrary",)),
        name="moe_experts",
    )(chunk_expert, n_used, buf, wgu, bgu, wd, bd)


def _combine_kernel(dest_ref, dest_next_ref, gw_ref, h_ref, eo_ref, g_ref, b_ref, o_ref, gbuf, ysl, sems, *, tokens):
    i = pl.program_id(0)
    slot = i % 2
    count = tokens * TOP_K

    def gather(idx_ref, into):
        def issue(t, c):
            for k in range(TOP_K):
                j = t * TOP_K + k
                _row_copy(eo_ref, idx_ref[j], gbuf.at[into], j, sems.at[into]).start()
            return c
        lax.fori_loop(0, tokens, issue, 0)

    @pl.when(i == 0)
    def _():
        gather(dest_ref, 0)

    @pl.when(i + 1 < pl.num_programs(0))
    def _():
        gather(dest_next_ref, 1 - slot)

    pltpu.make_async_copy(eo_ref.at[pl.ds(0, count * SLAB)], gbuf.at[slot], sems.at[slot]).wait()
    rows = gbuf.at[slot]

    unroll = 4

    def token_group(tg, c):
        for u in range(unroll):
            t = tg * unroll + u
            acc = DEEPNORM_ALPHA * h_ref[pl.ds(pl.multiple_of(t * SLAB, SLAB), SLAB), :]
            for k in range(TOP_K):
                j = t * TOP_K + k
                acc = acc + gw_ref[j] * rows[pl.ds(pl.multiple_of(j * SLAB, SLAB), SLAB), :]
            ysl[pl.ds(pl.multiple_of(t * SLAB, SLAB), SLAB), :] = acc
        return c

    lax.fori_loop(0, tokens // unroll, token_group, 0)
    o_ref[...] = _layer_norm(_slab_load(ysl, tokens), g_ref[...], b_ref[...])


def _combine_stage(dest, gate_w, h_slab, expert_out, g2, b2):
    n = h_slab.shape[0] // SLAB
    tokens = 256
    steps = n // tokens
    smem = lambda: pl.BlockSpec((tokens * TOP_K,), lambda i: (i,), memory_space=pltpu.SMEM)
    return pl.pallas_call(
        functools.partial(_combine_kernel, tokens=tokens),
        grid=(steps,),
        in_specs=[smem(),
                  pl.BlockSpec((tokens * TOP_K,), lambda i: (jnp.minimum(i + 1, steps - 1),), memory_space=pltpu.SMEM),
                  smem(),
                  pl.BlockSpec((tokens * SLAB, LANES), lambda i: (i, 0)),
                  pl.BlockSpec(memory_space=pl.ANY),
                  pl.BlockSpec((1, D_MODEL), lambda i: (0, 0)),
                  pl.BlockSpec((1, D_MODEL), lambda i: (0, 0))],
        out_specs=pl.BlockSpec((tokens, D_MODEL), lambda i: (i, 0)),
        out_shape=jax.ShapeDtypeStruct((n, D_MODEL), _f32),
        scratch_shapes=[pltpu.VMEM((2, tokens * TOP_K * SLAB, LANES), _f32),
                        pltpu.VMEM((tokens * SLAB, LANES), _f32),
                        pltpu.SemaphoreType.DMA((2,))],
        compiler_params=_cparams(("arbitrary",)),
        name="moe_combine_ln2",
    )(dest, dest, gate_w, h_slab, expert_out, g2, b2)


def _dispatch_plan(top_idx):
    m = top_idx.size
    e_flat = top_idx.reshape(m)
    onehot = (e_flat[:, None] == jnp.arange(N_EXPERTS, dtype=jnp.int32)[None, :]).astype(jnp.int32)
    csum = jnp.cumsum(onehot, axis=0)
    counts = csum[-1]
    padded = (counts + MOE_ROWS - 1) // MOE_ROWS * MOE_ROWS
    pends = jnp.cumsum(padded)
    pstarts = pends - padded
    dest = jnp.sum(onehot * (csum - 1 + pstarts[None, :]), axis=1).astype(jnp.int32)
    n_chunks = m // MOE_ROWS + N_EXPERTS
    chunk_start = jnp.arange(n_chunks, dtype=jnp.int32) * MOE_ROWS
    chunk_expert = jnp.minimum(jnp.sum((chunk_start[:, None] >= pends[None, :]).astype(jnp.int32), axis=1), N_EXPERTS - 1)
    n_used = (pends[-1] // MOE_ROWS).astype(jnp.int32).reshape(1)
    pad_start = (pstarts + counts).astype(jnp.int32)
    pad_len = (padded - counts).astype(jnp.int32)
    return dest, chunk_expert, n_used, pad_start, pad_len, n_chunks * MOE_ROWS


def _moe_half(h_slab, top_idx, gate_w, w_gate_up, b_gate_up, w_down, b_down, ln2_g, ln2_b):
    dest, chunk_expert, n_used, pad_start, pad_len, buf_rows = _dispatch_plan(top_idx)
    buf = _dispatch_stage(dest, pad_start, pad_len, h_slab, buf_rows)
    expert_out = _expert_stage(chunk_expert, n_used, buf, w_gate_up.astype(_bf16),
                               b_gate_up.reshape(N_EXPERTS, 1, -1), w_down.astype(_bf16),
                               b_down.reshape(N_EXPERTS, 1, -1))
    return _combine_stage(dest, gate_w.reshape(-1), h_slab, expert_out, ln2_g.reshape(1, -1), ln2_b.reshape(1, -1))


def kernel(x, w_in, cmp_pe_k, cmp_w1_k, cmp_w2_k, cmp_pe_v, cmp_w1_v, cmp_w2_v, w_proj_sb, w_proj_nsa, w_out,
           ln1_g, ln1_b, w_router, b_router, w_gate_up, b_gate_up, w_down, b_down, ln2_g, ln2_b):
    assert w_in.shape[0] == 1, "single-layer block"
    batch, seq, _ = x.shape
    assert seq % 512 == 0 and seq // SEL_BLOCK <= LANES and seq >= WINDOW + Q_BLOCK
    h_slab, top_idx, gate_w = _attention_half(
        x, w_in[0], cmp_pe_k[0], cmp_w1_k[0], cmp_w2_k[0], cmp_pe_v[0], cmp_w1_v[0], cmp_w2_v[0],
        w_proj_sb[0], w_proj_nsa[0], w_out[0], ln1_g[0], ln1_b[0], w_router[0], b_router[0])
    out = _moe_half(h_slab, top_idx, gate_w, w_gate_up[0], b_gate_up[0], w_down[0], b_down[0], ln2_g[0], ln2_b[0])
    return out.reshape(batch, seq, D_MODEL)
```

```python
import functools

import numpy as np
import jax
import jax.numpy as jnp
from jax import lax
from jax.experimental import pallas as pl
from jax.experimental.pallas import tpu as pltpu

D_MODEL = 1024
HEAD_DIM = 64
LANES = 128
Q_BLOCK = 128
CMP_BLOCK = 32
CMP_STRIDE = 16
SEL_BLOCK = 64
SEL_TOPK = 16
WINDOW = 512
ROPE_THETA = 10000.0
N_EXPERTS = 32
TOP_K = 4
SWIGLU_LIMIT = 7.0
SWIGLU_ALPHA = 1.702
LN_EPS = 1e-5
NEG_INF = -1e30
TAKEN = -3e38
DEEPNORM_ALPHA = 2.0 ** 0.25
QK_SCALE = HEAD_DIM ** -0.5
LOG2E = 1.4426950408889634

CB_SBQ, CB_SBK, CB_SBV, CB_NQ = 0, 4, 8, 12
CB_KC, CB_VC, CB_KS, CB_VS, CB_KW, CB_VW = 16, 17, 18, 19, 20, 21
CB_NG = 22
CB_MG = 24
PROJ_W = 40 * LANES

SB_TAIL_CUTOFF = -110.0

SEL_Q = 512
SEL_TK = 512
MOE_ROWS = 512
SLAB = D_MODEL // LANES
VMEM_LIMIT = 56 * 1024 * 1024

_bf16 = jnp.bfloat16
_f32 = jnp.float32


def _cparams(sem):
    return pltpu.CompilerParams(dimension_semantics=sem, vmem_limit_bytes=VMEM_LIMIT)


def _dot_t(a, b):
    return lax.dot_general(a, b, (((1,), (1,)), ((), ())), preferred_element_type=_f32)


def _dot(a, b):
    return jnp.dot(a, b, preferred_element_type=_f32)


def _lane_iota(shape):
    return lax.broadcasted_iota(jnp.int32, shape, len(shape) - 1)


def _half0(shape=(1, LANES)):
    return _lane_iota(shape) < HEAD_DIM


def _in_proj_kernel(x_ref, w_ref, o_ref):
    o_ref[...] = _dot(x_ref[...].astype(_bf16), w_ref[...]).astype(o_ref.dtype)


def _in_proj(x2, w):
    n = x2.shape[0]
    tm, tn = 512, 1280
    return pl.pallas_call(
        _in_proj_kernel,
        grid=(PROJ_W // tn, n // tm),
        in_specs=[pl.BlockSpec((tm, D_MODEL), lambda j, i: (i, 0)),
                  pl.BlockSpec((D_MODEL, tn), lambda j, i: (0, j))],
        out_specs=pl.BlockSpec((tm, tn), lambda j, i: (i, j)),
        out_shape=jax.ShapeDtypeStruct((n, PROJ_W), _bf16),
        compiler_params=_cparams(("arbitrary", "arbitrary")),
        name="in_proj",
    )(x2, w)


def _rope(x, cos, sin_signed):
    first = (_lane_iota((1, LANES)) % HEAD_DIM) < (HEAD_DIM // 2)
    swapped = jnp.where(first, pltpu.roll(x, LANES - HEAD_DIM // 2, 1), pltpu.roll(x, HEAD_DIM // 2, 1))
    return x * cos + swapped * sin_signed


def _dup(x, g):
    other = pltpu.roll(x, HEAD_DIM, 1)
    h0 = _half0()
    return jnp.where(h0, x, other) if g == 0 else jnp.where(h0, other, x)


def _rope_kernel(nq_ref, kc_ref, vc_ref, ks_ref, vs_ref, kw_ref, vw_ref, cos_ref, sin_ref,
                 nq_o, kc_o, vc_o, ka_o, vs_o, kw_o, vw_o, *, blocks_per_seq):
    ts = cos_ref.shape[0]
    cos = cos_ref[...]
    sin = sin_ref[...]
    for c in range(4):
        sl = slice(c * LANES, (c + 1) * LANES)
        nq_o[:, sl] = (_rope(nq_ref[:, sl].astype(_f32), cos, sin) * (QK_SCALE * LOG2E)).astype(_bf16)
    kc_o[...] = _rope(kc_ref[...].astype(_f32), cos, sin).astype(_bf16)
    vc_o[...] = vc_ref[...]
    ks = _rope(ks_ref[...].astype(_f32), cos, sin)
    kw = _rope(kw_ref[...].astype(_f32), cos, sin)
    vs = vs_ref[...].astype(_f32)
    vw = vw_ref[...].astype(_f32)
    pos = (pl.program_id(0) % blocks_per_seq) * ts + lax.broadcasted_iota(jnp.int32, (ts, LANES), 0)
    lane = _lane_iota((ts, LANES))
    onehot = jnp.where((pos // SEL_BLOCK) % HEAD_DIM + HEAD_DIM == lane, 1.0, 0.0)
    for g in range(2):
        ka_o[g] = jnp.where(_half0(), _dup(ks, g), onehot).astype(_bf16)
        vsa = jnp.where(_half0(), _dup(vs, g), 1.0)
        for c in range(ts // SEL_TK):
            vs_o[g, c] = vsa[c * SEL_TK:(c + 1) * SEL_TK, :].T.astype(_bf16)
        kw_o[g] = _dup(kw, g).astype(_bf16)
        vwd = jnp.where(_half0(), _dup(vw, g), 1.0)
        for c in range(ts // LANES):
            vw_o[g, c] = vwd[c * LANES:(c + 1) * LANES, :].T.astype(_bf16)


def _rope_stage(proj, cos, sin_signed, seq):
    n = proj.shape[0]
    ts = 512
    bps = seq // ts
    col = lambda cb: pl.BlockSpec((ts, LANES), lambda i, cb=cb: (i, cb))
    tab = pl.BlockSpec((ts, LANES), lambda i: (i % bps, 0))
    grp = lambda w: pl.BlockSpec((2, ts, w), lambda i: (0, i, 0))
    return pl.pallas_call(
        functools.partial(_rope_kernel, blocks_per_seq=bps),
        grid=(n // ts,),
        in_specs=[pl.BlockSpec((ts, 4 * LANES), lambda i: (i, CB_NQ // 4)),
                  col(CB_KC), col(CB_VC), col(CB_KS), col(CB_VS), col(CB_KW), col(CB_VW), tab, tab],
        out_specs=[pl.BlockSpec((ts, 4 * LANES), lambda i: (i, 0)),
                   pl.BlockSpec((ts, LANES), lambda i: (i, 0)),
                   pl.BlockSpec((ts, LANES), lambda i: (i, 0)),
                   grp(LANES),
                   pl.BlockSpec((2, ts // SEL_TK, LANES, SEL_TK), lambda i: (0, i, 0, 0)),
                   grp(LANES),
                   pl.BlockSpec((2, ts // LANES, LANES, LANES), lambda i: (0, i, 0, 0))],
        out_shape=[jax.ShapeDtypeStruct((n, 4 * LANES), _bf16),
                   jax.ShapeDtypeStruct((n, LANES), _bf16),
                   jax.ShapeDtypeStruct((n, LANES), _bf16),
                   jax.ShapeDtypeStruct((2, n, LANES), _bf16),
                   jax.ShapeDtypeStruct((2, n // SEL_TK, LANES, SEL_TK), _bf16),
                   jax.ShapeDtypeStruct((2, n, LANES), _bf16),
                   jax.ShapeDtypeStruct((2, n // LANES, LANES, LANES), _bf16)],
        compiler_params=_cparams(("arbitrary",)),
        name="rope_layout",
    )(proj, proj, proj, proj, proj, proj, proj, cos, sin_signed)


def _gelu_tanh(x):
    return 0.5 * x * (1.0 + jnp.tanh(0.7978845608028654 * (x + 0.044715 * (x * x * x))))


def _compress_one(x_ref, pe_t, pe_b, w_t, w_b, w2, out_ref, transposed):
    x = x_ref[0].astype(_f32)
    a = _dot((x + pe_t[...]).astype(_bf16), w_t[...])
    b = _dot((x + pe_b[...]).astype(_bf16), w_b[...])
    nc = a.shape[0]
    pre = a + pltpu.roll(b, nc - 1, 0)
    y = _dot(_gelu_tanh(pre).astype(_bf16), w2[...])
    for g in range(2):
        d = _dup(y, g)
        out_ref[0, g] = (d.T if transposed else d).astype(_bf16)


def _compress_kernel(k_ref, v_ref, kpt, kpb, kwt, kwb, kw2, vpt, vpb, vwt, vwb, vw2, ko_ref, vo_ref):
    _compress_one(k_ref, kpt, kpb, kwt, kwb, kw2, ko_ref, False)
    _compress_one(v_ref, vpt, vpb, vwt, vwb, vw2, vo_ref, True)


def _compress_weights(pe, w1, w2):
    half = CMP_BLOCK // 2
    eye = jnp.eye(2, dtype=_f32)
    outs = []
    for part in range(2):
        w = w1[part * half * HEAD_DIM:(part + 1) * half * HEAD_DIM].reshape(half, HEAD_DIM, HEAD_DIM)
        wbd = (w[:, None, :, None, :] * eye[None, :, None, :, None]).reshape(half * 2 * HEAD_DIM, 2 * HEAD_DIM)
        p = jnp.broadcast_to(pe[part * half:(part + 1) * half, None, :], (half, 2, HEAD_DIM)).reshape(1, -1)
        outs.append((p.astype(_f32), wbd.astype(_bf16)))
    w2bd = (w2[None, :, None, :] * eye[:, None, :, None]).reshape(2 * HEAD_DIM, 2 * HEAD_DIM).astype(_bf16)
    (pt, wt), (pb, wb) = outs
    return pt, pb, wt, wb, w2bd


def _compress_stage(kc_r, vc_r, kparams, vparams, batch, seq):
    nc = seq // CMP_STRIDE
    width = CMP_STRIDE * LANES
    xs = pl.BlockSpec((1, nc, width), lambda b: (b, 0, 0))
    full = lambda a: pl.BlockSpec(a.shape, lambda b: (0,) * a.ndim)
    out = pl.BlockSpec((1, 2, nc, LANES), lambda b: (b, 0, 0, 0))
    weights = list(kparams) + list(vparams)
    return pl.pallas_call(
        _compress_kernel,
        grid=(batch,),
        in_specs=[xs, xs] + [full(a) for a in weights],
        out_specs=[out, pl.BlockSpec((1, 2, LANES, nc), lambda b: (b, 0, 0, 0))],
        out_shape=[jax.ShapeDtypeStruct((batch, 2, nc, LANES), _bf16),
                   jax.ShapeDtypeStruct((batch, 2, LANES, nc), _bf16)],
        compiler_params=_cparams(("arbitrary",)),
        name="compress",
    )(kc_r.reshape(batch, nc, width), vc_r.reshape(batch, nc, width), *weights)


def _head_q(q_ref, r):
    q2 = q_ref[:, (r // 2) * LANES:(r // 2 + 1) * LANES]
    keep = _half0() if r % 2 == 0 else jnp.logical_not(_half0())
    return jnp.where(keep, q2, jnp.zeros_like(q2))


def _softmax_over_rows(s):
    m = jnp.max(s, axis=0, keepdims=True)
    e = jnp.exp2(s - m)
    l = jnp.sum(e, axis=0, keepdims=True)
    return e * jnp.where(m > 0.5 * NEG_INF, 1.0 / l, 0.0)


def _pair(even, odd):
    return jnp.where(_half0(), even, odd)


def _nsa_cw_kernel(q_ref, kc_ref, vct_ref, kw_ref, vwt_ref, ng_ref, stt_ref, yp_ref, mb_ref):
    t0 = pl.program_id(2) * Q_BLOCK
    qpos = t0 + _lane_iota((1, Q_BLOCK))
    gates = jax.nn.sigmoid(ng_ref[...].astype(_f32).T[0:16, :])
    kc = kc_ref[0, 0]
    vct = vct_ref[0, 0]
    nc = kc.shape[0]
    cend = lax.broadcasted_iota(jnp.int32, (nc, 1), 0) * CMP_STRIDE + (CMP_BLOCK - 1)
    cmask = cend <= qpos
    start = pl.multiple_of(jnp.maximum(t0 - WINDOW, 0), Q_BLOCK)
    wlen = WINDOW + Q_BLOCK
    kwin = kw_ref[0, 0, pl.ds(start, wlen), :]
    kpos = start + lax.broadcasted_iota(jnp.int32, (wlen, 1), 0)
    wmask = (kpos <= qpos) & (qpos - kpos < WINDOW)
    blk0 = start // Q_BLOCK

    qs = [_head_q(q_ref, r) for r in range(4)]
    s_cmp = [_dot_t(kc, q) for q in qs]
    s_win = [_dot_t(kwin, q) for q in qs]
    p_cmp = [_softmax_over_rows(jnp.where(cmask, s, NEG_INF)) for s in s_cmp]
    imp = (p_cmp[0] + p_cmp[1]) + (p_cmp[2] + p_cmp[3])
    e_win = []
    for s in s_win:
        s = jnp.where(wmask, s, NEG_INF)
        e_win.append(jnp.exp2(s - jnp.max(s, axis=0, keepdims=True)).astype(_bf16))
    vwt = jnp.concatenate([vwt_ref[0, 0, blk0 + c] for c in range(wlen // Q_BLOCK)], axis=1)
    yts = []
    for r in range(4):
        o_cmp = _dot(vct, p_cmp[r].astype(_bf16))[0:HEAD_DIM]
        win = _dot(vwt, e_win[r])
        o_win = win[0:HEAD_DIM] * (1.0 / win[HEAD_DIM:2 * HEAD_DIM])
        yts.append(gates[3 * r:3 * r + 1] * o_cmp + gates[3 * r + 2:3 * r + 3] * o_win)
    yp_ref[:, 0:LANES] = jnp.concatenate(yts[0:2], axis=0).T
    yp_ref[:, LANES:2 * LANES] = jnp.concatenate(yts[2:4], axis=0).T

    imp_hi = imp.astype(_bf16)
    rest = imp - imp_hi.astype(_f32)
    imp_mid = rest.astype(_bf16)
    imp_lo = (rest - imp_mid.astype(_f32)).astype(_bf16)
    st = stt_ref[...]
    p_slc = _dot(st, imp_hi) + (_dot(st, imp_mid) + _dot(st, imp_lo))
    selj = lax.broadcasted_iota(jnp.int32, (LANES, 1), 0)
    blk_t = qpos // SEL_BLOCK
    forced = (selj == 0) | (selj == blk_t) | (selj == blk_t - 1)
    score = jnp.where(forced, TAKEN, jnp.where(selj <= blk_t, p_slc, NEG_INF))
    seljf = selj.astype(_f32)
    picked = forced
    for _ in range(SEL_TOPK - 3):
        m = jnp.max(score, axis=0, keepdims=True)
        first = jnp.min(jnp.where(score == m, seljf, float(LANES)), axis=0, keepdims=True)
        hit = seljf == first
        picked = picked | hit
        score = jnp.where(hit, TAKEN, score)
    mb_ref[0] = jnp.where(picked, 0.0, NEG_INF).T.astype(_bf16)


def _nsa_cw_stage(nq_r, kc_d, vc_t, kw_d, vw_t, proj, stencil_t, batch, seq):
    n = nq_r.shape[0]
    nblk = seq // Q_BLOCK
    nc = seq // CMP_STRIDE
    qrow = lambda b, g, i: b * nblk + i
    return pl.pallas_call(
        _nsa_cw_kernel,
        grid=(batch, 2, nblk),
        in_specs=[pl.BlockSpec((Q_BLOCK, 2 * LANES), lambda b, g, i: (qrow(b, g, i), g)),
                  pl.BlockSpec((1, 1, nc, LANES), lambda b, g, i: (b, g, 0, 0)),
                  pl.BlockSpec((1, 1, LANES, nc), lambda b, g, i: (b, g, 0, 0)),
                  pl.BlockSpec((1, 1, seq, LANES), lambda b, g, i: (g, b, 0, 0)),
                  pl.BlockSpec((1, 1, nblk, LANES, LANES), lambda b, g, i: (g, b, 0, 0, 0)),
                  pl.BlockSpec((Q_BLOCK, LANES), lambda b, g, i: (qrow(b, g, i), CB_NG + g)),
                  pl.BlockSpec((LANES, nc), lambda b, g, i: (0, 0))],
        out_specs=[pl.BlockSpec((Q_BLOCK, 2 * LANES), lambda b, g, i: (qrow(b, g, i), g)),
                   pl.BlockSpec((1, Q_BLOCK, LANES), lambda b, g, i: (g, qrow(b, g, i), 0))],
        out_shape=[jax.ShapeDtypeStruct((n, 4 * LANES), _f32),
                   jax.ShapeDtypeStruct((2, n, LANES), _bf16)],
        compiler_params=_cparams(("arbitrary", "arbitrary", "arbitrary")),
        name="nsa_cmp_win_select",
    )(nq_r, kc_d, vc_t, kw_d.reshape(2, batch, seq, LANES), vw_t.reshape(2, batch, nblk, LANES, LANES), proj, stencil_t)


def _nsa_sel_kernel(q_ref, mb_ref, ka_ref, vat_ref, ng_ref, yp_ref, o_ref, qs_ref, m_ref, acc_ref, s_ref, p_ref, alpha_ref):
    tk = SEL_TK
    t0 = pl.program_id(2) * SEL_Q
    h0 = _half0()
    mb = mb_ref[0].astype(_f32)
    bias = [pltpu.roll(mb, HEAD_DIM, 1), mb]
    for r in range(4):
        q = q_ref[:, (r // 2) * LANES:(r // 2 + 1) * LANES].astype(_f32)
        if r % 2 == 1:
            q = pltpu.roll(q, HEAD_DIM, 1)
        for v in range(2):
            qs_ref[v, r * SEL_Q:(r + 1) * SEL_Q, :] = jnp.where(h0, q, bias[v]).astype(_bf16)
    m_ref[...] = jnp.full(m_ref.shape, NEG_INF, _f32)
    acc_ref[...] = jnp.zeros(acc_ref.shape, _f32)
    p_ref[...] = jnp.zeros(p_ref.shape, _bf16)
    alpha_ref[...] = jnp.ones(alpha_ref.shape, _f32)
    qpos = t0 + _lane_iota((1, 4 * SEL_Q)) % SEL_Q

    def scores(kt):
        version = (kt * (tk // SEL_BLOCK)) // HEAD_DIM
        return _dot_t(ka_ref[0, 0, pl.ds(pl.multiple_of(kt * tk, tk), tk), :], qs_ref[version])

    def softmax_step(s):
        m_old = m_ref[...]
        m_new = jnp.maximum(m_old, jnp.max(s, axis=0, keepdims=True))
        m_ref[...] = m_new
        return jnp.exp2(s - m_new).astype(_bf16), jnp.exp2(m_old - m_new)

    def accumulate(kt, alpha, p):
        acc_ref[...] = alpha * acc_ref[...] + _dot(vat_ref[0, 0, kt], p)

    def trip(kt, carry):
        accumulate(jnp.maximum(kt - 1, 0), alpha_ref[...], p_ref[...])
        s = s_ref[...]
        s_ref[...] = scores(kt + 1)
        p, alpha = softmax_step(s)
        p_ref[...] = p
        alpha_ref[...] = alpha
        return carry

    n_full = t0 // tk
    s_ref[...] = scores(0)

    def two_trips(kp, carry):
        trip(2 * kp, carry)
        return trip(2 * kp + 1, carry)

    lax.fori_loop(0, n_full // 2, two_trips, 0)

    @pl.when(n_full % 2 == 1)
    def _():
        trip(n_full - 1, 0)

    accumulate(jnp.maximum(n_full - 1, 0), alpha_ref[...], p_ref[...])
    kpos = n_full * tk + lax.broadcasted_iota(jnp.int32, (tk, 1), 0)
    p, alpha = softmax_step(jnp.where(kpos <= qpos, s_ref[...], NEG_INF))
    accumulate(n_full, alpha, p)
    acc = acc_ref[...]
    o = acc[0:HEAD_DIM] * (1.0 / acc[HEAD_DIM:2 * HEAD_DIM])
    gates = jax.nn.sigmoid(ng_ref[...].astype(_f32).T[0:16, :])
    ys = [gates[3 * r + 1:3 * r + 2] * o[:, r * SEL_Q:(r + 1) * SEL_Q] for r in range(4)]
    o_ref[:, 0:LANES] = (yp_ref[:, 0:LANES] + jnp.concatenate(ys[0:2], axis=0).T).astype(o_ref.dtype)
    o_ref[:, LANES:2 * LANES] = (yp_ref[:, LANES:2 * LANES] + jnp.concatenate(ys[2:4], axis=0).T).astype(o_ref.dtype)


def _nsa_sel_stage(nq_r, mbias, k_aug, vs_t, proj, ypart, batch, seq):
    n = nq_r.shape[0]
    nblk = seq // SEL_Q
    qrow = lambda b, g, i: b * nblk + i
    return pl.pallas_call(
        _nsa_sel_kernel,
        grid=(batch, 2, nblk),
        in_specs=[pl.BlockSpec((SEL_Q, 2 * LANES), lambda b, g, i: (qrow(b, g, i), g)),
                  pl.BlockSpec((1, SEL_Q, LANES), lambda b, g, i: (g, qrow(b, g, i), 0)),
                  pl.BlockSpec((1, 1, seq, LANES), lambda b, g, i: (g, b, 0, 0)),
                  pl.BlockSpec((1, 1, seq // SEL_TK, LANES, SEL_TK), lambda b, g, i: (g, b, 0, 0, 0)),
                  pl.BlockSpec((SEL_Q, LANES), lambda b, g, i: (qrow(b, g, i), CB_NG + g)),
                  pl.BlockSpec((SEL_Q, 2 * LANES), lambda b, g, i: (qrow(b, g, i), g))],
        out_specs=pl.BlockSpec((SEL_Q, 2 * LANES), lambda b, g, i: (qrow(b, g, i), g)),
        out_shape=jax.ShapeDtypeStruct((n, 4 * LANES), _bf16),
        scratch_shapes=[pltpu.VMEM((2, 4 * SEL_Q, LANES), _bf16),
                        pltpu.VMEM((1, 4 * SEL_Q), _f32),
                        pltpu.VMEM((LANES, 4 * SEL_Q), _f32),
                        pltpu.VMEM((SEL_TK, 4 * SEL_Q), _f32),
                        pltpu.VMEM((SEL_TK, 4 * SEL_Q), _bf16),
                        pltpu.VMEM((1, 4 * SEL_Q), _f32)],
        compiler_params=_cparams(("arbitrary", "arbitrary", "arbitrary")),
        name="nsa_selected",
    )(nq_r, mbias, k_aug.reshape(2, batch, seq, LANES), vs_t.reshape(2, batch, seq // SEL_TK, LANES, SEL_TK), proj, ypart)


def _sb_kernel(q_ref, k_ref, v_ref, o_ref, qs_ref, tail_ref, acc_ref):
    i = pl.program_id(1)
    h0 = _half0()
    heads = 2 * (q_ref.shape[1] // LANES)
    for h in range(heads):
        q = q_ref[:, (h // 2) * LANES:(h // 2 + 1) * LANES]
        keep = h0 if h % 2 == 0 else jnp.logical_not(h0)
        qs_ref[h] = jnp.where(keep, q, jnp.zeros_like(q)) * QK_SCALE
    tail_ref[...] = jnp.zeros(tail_ref.shape, _f32)
    acc_ref[...] = jnp.zeros(acc_ref.shape, _f32)
    rloc = lax.broadcasted_iota(jnp.int32, (Q_BLOCK, Q_BLOCK), 0)
    cloc = lax.broadcasted_iota(jnp.int32, (Q_BLOCK, Q_BLOCK), 1)
    later = jnp.where(rloc > cloc, 1.0, 0.0).astype(_bf16)

    def key_block(j, diagonal):
        k0 = pl.multiple_of(j * Q_BLOCK, Q_BLOCK)
        past = cloc < rloc
        cols = [slice((h // 2) * LANES, (h // 2 + 1) * LANES) for h in range(heads)]
        zs = [_dot_t(qs_ref[h], k_ref[0, pl.ds(k0, Q_BLOCK), cols[h]]) for h in range(heads)]
        log_beta, log_keep = [], []
        for z in zs:
            sp = jnp.maximum(z, 0.0) + jnp.log(1.0 + jnp.exp(-jnp.abs(z)))
            log_beta.append(z - sp)
            log_keep.append(jnp.where(past, -sp, 0.0) if diagonal else -sp)
        inner = []
        for lk in log_keep:
            hi = lk.astype(_bf16)
            lo = (lk - hi.astype(_f32)).astype(_bf16)
            inner.append(_dot(hi, later) + _dot(lo, later))
        probs = []
        for h in range(heads):
            a = jnp.exp(log_beta[h] + inner[h] + tail_ref[h])
            probs.append((jnp.where(past, a, 0.0) if diagonal else a).astype(_bf16))
        worst = jnp.full((Q_BLOCK, 1), -jnp.inf, _f32)
        for h in range(heads):
            acc_ref[h] = acc_ref[h] + _dot(probs[h], v_ref[0, pl.ds(k0, Q_BLOCK), cols[h]])
            tail = tail_ref[h] + jnp.sum(log_keep[h], axis=-1, keepdims=True)
            tail_ref[h] = tail
            worst = jnp.maximum(worst, tail)
        return jnp.max(worst)

    def cond(c):
        j, worst_tail = c
        return (j >= 0) & (worst_tail > SB_TAIL_CUTOFF)

    def body(c):
        j, _ = c
        return j - 1, key_block(j, False)

    lax.while_loop(cond, body, (i - 1, key_block(i, True)))
    for p in range(heads // 2):
        o_ref[:, p * LANES:(p + 1) * LANES] = jnp.where(h0, acc_ref[2 * p], acc_ref[2 * p + 1]).astype(o_ref.dtype)


def _sb_stage(proj, batch, seq):
    n = proj.shape[0]
    nblk = seq // Q_BLOCK
    width = 4 * LANES
    proj3 = proj.reshape(batch, seq, PROJ_W)
    return pl.pallas_call(
        _sb_kernel,
        grid=(batch, nblk),
        in_specs=[pl.BlockSpec((Q_BLOCK, width), lambda b, i: (b * nblk + i, CB_SBQ // 4)),
                  pl.BlockSpec((1, seq, width), lambda b, i: (b, 0, CB_SBK // 4)),
                  pl.BlockSpec((1, seq, width), lambda b, i: (b, 0, CB_SBV // 4))],
        out_specs=pl.BlockSpec((Q_BLOCK, width), lambda b, i: (b * nblk + i, 0)),
        out_shape=jax.ShapeDtypeStruct((n, width), _bf16),
        scratch_shapes=[pltpu.VMEM((8, Q_BLOCK, LANES), _bf16),
                        pltpu.VMEM((8, Q_BLOCK, 1), _f32),
                        pltpu.VMEM((8, Q_BLOCK, LANES), _f32)],
        compiler_params=_cparams(("arbitrary", "arbitrary")),
        name="stick_breaking",
    )(proj, proj3, proj3)


def _layer_norm(x, g, b):
    mu = jnp.mean(x, axis=-1, keepdims=True)
    xc = x - mu
    var = jnp.mean(xc * xc, axis=-1, keepdims=True)
    return xc * lax.rsqrt(var + LN_EPS) * g + b


def _merge_kernel(x_ref, ysb_ref, yns_ref, mg0_ref, mg1_ref, wsb_ref, wns_ref, wo_ref, g_ref, b_ref,
                  wr_ref, br_ref, h_ref, idx_ref, gw_ref):
    m0 = jax.nn.sigmoid(mg0_ref[...].astype(_f32))
    m1 = jax.nn.sigmoid(mg1_ref[...].astype(_f32))
    merged = m0 * _dot(ysb_ref[...], wsb_ref[...]) + m1 * _dot(yns_ref[...], wns_ref[...])
    pre = DEEPNORM_ALPHA * x_ref[...] + _dot(merged.astype(_bf16), wo_ref[...])
    h = _layer_norm(pre, g_ref[...], b_ref[...])
    tm = h.shape[0]
    for s in range(SLAB):
        h_ref[pl.ds(s, tm, stride=SLAB), :] = h[:, s * LANES:(s + 1) * LANES]
    h_hi = h.astype(_bf16)
    h_lo = (h - h_hi.astype(_f32)).astype(_bf16)
    w = wr_ref[...]
    w_hi = w.astype(_bf16)
    w_lo = (w - w_hi.astype(_f32)).astype(_bf16)
    logits = (_dot(h_hi, w_hi) + (_dot(h_hi, w_lo) + _dot(h_lo, w_hi))) + br_ref[...]
    lane = _lane_iota((1, LANES))
    lanef = lane.astype(_f32)
    lg = jnp.where(lane < N_EXPERTS, logits, TAKEN)
    vals, idxs = [], []
    for _ in range(TOP_K):
        m = jnp.max(lg, axis=-1, keepdims=True)
        first = jnp.min(jnp.where(lg == m, lanef, float(LANES)), axis=-1, keepdims=True)
        vals.append(m)
        idxs.append(first)
        lg = jnp.where(lanef == first, TAKEN, lg)
    es = [jnp.exp(v - vals[0]) for v in vals]
    inv = 1.0 / (es[0] + es[1] + es[2] + es[3])
    idx_t = jnp.zeros(lg.shape, _f32)
    gw_t = jnp.zeros(lg.shape, _f32)
    for k in range(TOP_K):
        idx_t = jnp.where(lane == k, idxs[k], idx_t)
        gw_t = jnp.where(lane == k, es[k] * inv, gw_t)
    idx_ref[...] = idx_t[:, :TOP_K].astype(jnp.int32)
    gw_ref[...] = gw_t[:, :TOP_K]


def _merge_stage(x2, y_sb, y_nsa, proj, wsb, wns, wo, g1, b1, wr, br):
    n = x2.shape[0]
    tm = 512
    row = lambda w: pl.BlockSpec((tm, w), lambda i: (i, 0))
    full = lambda a: pl.BlockSpec(a.shape, lambda i: (0,) * a.ndim)
    return pl.pallas_call(
        _merge_kernel,
        grid=(n // tm,),
        in_specs=[row(D_MODEL), row(4 * LANES), row(4 * LANES),
                  pl.BlockSpec((tm, D_MODEL), lambda i: (i, CB_MG // 8)),
                  pl.BlockSpec((tm, D_MODEL), lambda i: (i, CB_MG // 8 + 1)),
                  full(wsb), full(wns), full(wo), full(g1), full(b1), full(wr), full(br)],
        out_specs=[pl.BlockSpec((tm * SLAB, LANES), lambda i: (i, 0)), row(TOP_K), row(TOP_K)],
        out_shape=[jax.ShapeDtypeStruct((n * SLAB, LANES), _f32),
                   jax.ShapeDtypeStruct((n, TOP_K), jnp.int32),
                   jax.ShapeDtypeStruct((n, TOP_K), _f32)],
        compiler_params=_cparams(("arbitrary",)),
        name="merge_ln1_router",
    )(x2, y_sb, y_nsa, proj, proj, wsb, wns, wo, g1, b1, wr, br)


def _prep_w_in(w):
    main = w[:, :CB_NG * LANES]
    ng = w[:, CB_NG * LANES:CB_NG * LANES + 24]
    mg = w[:, CB_NG * LANES + 24:]
    pad = jnp.zeros((w.shape[0], LANES - 12), w.dtype)
    return jnp.concatenate([main, ng[:, :12], pad, ng[:, 12:], pad, mg], axis=1).astype(_bf16)


def _rope_tables(seq):
    half = HEAD_DIM // 2
    inv_freq = ROPE_THETA ** (-jnp.arange(half, dtype=_f32) / half)
    ang = jnp.arange(seq, dtype=_f32)[:, None] * inv_freq[None, :]
    cos = jnp.cos(ang)
    sin = jnp.sin(ang)
    cos128 = jnp.concatenate([cos, cos, cos, cos], axis=1)
    sin128 = jnp.concatenate([-sin, sin, -sin, sin], axis=1)
    return cos128, sin128


def _stencil(nc):
    n = np.arange(nc)[:, None]
    j = np.arange(LANES)[None, :]
    ratio = SEL_BLOCK // CMP_STRIDE
    ok = (n >= ratio * j - 1) & (n <= ratio * j + ratio - 1) & (n < nc - 1)
    return jnp.asarray(ok.astype(np.float32).T, dtype=_bf16)


def _attention_half(x, w_in, cmp_pe_k, cmp_w1_k, cmp_w2_k, cmp_pe_v, cmp_w1_v, cmp_w2_v,
                    w_proj_sb, w_proj_nsa, w_out, ln1_g, ln1_b, w_router, b_router):
    batch, seq, _ = x.shape
    n = batch * seq
    x2 = x.reshape(n, D_MODEL)
    proj = _in_proj(x2, _prep_w_in(w_in))
    cos, sin_signed = _rope_tables(seq)
    nq_r, kc_r, vc_r, k_aug, vs_t, kw_d, vw_t = _rope_stage(proj, cos, sin_signed, seq)
    kc_d, vc_t = _compress_stage(kc_r, vc_r, _compress_weights(cmp_pe_k, cmp_w1_k, cmp_w2_k),
                                 _compress_weights(cmp_pe_v, cmp_w1_v, cmp_w2_v), batch, seq)
    ypart, mbias = _nsa_cw_stage(nq_r, kc_d, vc_t, kw_d, vw_t, proj, _stencil(seq // CMP_STRIDE), batch, seq)
    y_nsa = _nsa_sel_stage(nq_r, mbias, k_aug, vs_t, proj, ypart, batch, seq)
    y_sb = _sb_stage(proj, batch, seq)
    wr = jnp.pad(w_router.astype(_f32), ((0, 0), (0, LANES - N_EXPERTS)))
    br = jnp.pad(b_router.astype(_f32), (0, LANES - N_EXPERTS)).reshape(1, LANES)
    return _merge_stage(x2, y_sb, y_nsa, proj, w_proj_sb.astype(_bf16), w_proj_nsa.astype(_bf16),
                        w_out.astype(_bf16), ln1_g.reshape(1, -1), ln1_b.reshape(1, -1), wr, br)


def _row_copy(src, src_row, dst, dst_row, sem):
    return pltpu.make_async_copy(src.at[pl.ds(src_row * SLAB, SLAB)], dst.at[pl.ds(dst_row * SLAB, SLAB)], sem)


def _dispatch_kernel(pad_start_ref, pad_len_ref, dest_ref, h_hbm, buf_ref, hbuf, zslab, in_sems, row_sems, zero_sem,
                     *, tokens):
    i = pl.program_id(0)
    last = pl.num_programs(0) - 1
    slot = i % 2
    block_rows = tokens * SLAB

    def block_copy(step, s):
        return pltpu.make_async_copy(h_hbm.at[pl.ds(step * block_rows, block_rows)], hbuf.at[s], in_sems.at[s])

    def wait_rows(s):
        for _ in range(TOP_K):
            pltpu.make_async_copy(hbuf.at[s], buf_ref.at[pl.ds(0, block_rows)], row_sems.at[s]).wait()

    def padding_rows(act):
        def per_expert(e, c):
            def per_row(r, c2):
                act(_row_copy(zslab, 0, buf_ref, pad_start_ref[e] + r, zero_sem))
                return c2
            lax.fori_loop(0, pad_len_ref[e], per_row, 0)
            return c
        lax.fori_loop(0, N_EXPERTS, per_expert, 0)

    @pl.when(i == 0)
    def _():
        block_copy(0, 0).start()
        zslab[...] = jnp.zeros(zslab.shape, _f32)
        padding_rows(lambda cp: cp.start())

    @pl.when(i > 0)
    def _():
        wait_rows(1 - slot)

    @pl.when(i < last)
    def _():
        block_copy(i + 1, 1 - slot).start()

    block_copy(i, slot).wait()
    src = hbuf.at[slot]

    def issue(t, c):
        for k in range(TOP_K):
            _row_copy(src, t, buf_ref, dest_ref[t * TOP_K + k], row_sems.at[slot]).start()
        return c

    lax.fori_loop(0, tokens, issue, 0)

    @pl.when(i == last)
    def _():
        wait_rows(slot)
        padding_rows(lambda cp: cp.wait())


def _dispatch_stage(dest, pad_start, pad_len, h_slab, buf_rows):
    n = h_slab.shape[0] // SLAB
    tokens = 256
    grid_spec = pltpu.PrefetchScalarGridSpec(
        num_scalar_prefetch=2,
        grid=(n // tokens,),
        in_specs=[pl.BlockSpec((tokens * TOP_K,), lambda i, ps, pn: (i,), memory_space=pltpu.SMEM),
                  pl.BlockSpec(memory_space=pl.ANY)],
        out_specs=pl.BlockSpec(memory_space=pl.ANY),
        scratch_shapes=[pltpu.VMEM((2, tokens * SLAB, LANES), _f32),
                        pltpu.VMEM((SLAB, LANES), _f32),
                        pltpu.SemaphoreType.DMA((2,)),
                        pltpu.SemaphoreType.DMA((2,)),
                        pltpu.SemaphoreType.DMA(())])
    return pl.pallas_call(
        functools.partial(_dispatch_kernel, tokens=tokens),
        grid_spec=grid_spec,
        out_shape=jax.ShapeDtypeStruct((buf_rows * SLAB, LANES), _f32),
        compiler_params=pltpu.CompilerParams(dimension_semantics=("arbitrary",), has_side_effects=True),
        name="moe_dispatch",
    )(pad_start, pad_len, dest, h_slab)


def _slab_load(ref, rows):
    return jnp.concatenate([ref[pl.ds(s, rows, stride=SLAB), :] for s in range(SLAB)], axis=1)


def _expert_kernel(ce_ref, used_ref, x_ref, wgu_ref, bgu_ref, wd_ref, bd_ref, o_ref):
    del ce_ref
    c = pl.program_id(0)

    @pl.when(c < used_ref[0])
    def _():
        x = _slab_load(x_ref, MOE_ROWS).astype(_bf16)
        gu = _dot(x, wgu_ref[0]) + bgu_ref[0]
        gate = jnp.minimum(gu[:, :D_MODEL], SWIGLU_LIMIT)
        up = jnp.clip(gu[:, D_MODEL:], -SWIGLU_LIMIT, SWIGLU_LIMIT)
        h = gate * jax.nn.sigmoid(SWIGLU_ALPHA * gate) * (up + 1.0)
        y = _dot(h.astype(_bf16), wd_ref[0]) + bd_ref[0]
        for s in range(SLAB):
            o_ref[pl.ds(s, MOE_ROWS, stride=SLAB), :] = y[:, s * LANES:(s + 1) * LANES]

    @pl.when(c >= used_ref[0])
    def _():
        o_ref[...] = jnp.zeros(o_ref.shape, o_ref.dtype)


def _expert_stage(chunk_expert, n_used, buf, wgu, bgu, wd, bd):
    n_chunks = chunk_expert.shape[0]
    rows = MOE_ROWS * SLAB
    grid_spec = pltpu.PrefetchScalarGridSpec(
        num_scalar_prefetch=2,
        grid=(n_chunks,),
        in_specs=[pl.BlockSpec((rows, LANES), lambda c, ce, nu: (c, 0)),
                  pl.BlockSpec((1, D_MODEL, 2 * D_MODEL), lambda c, ce, nu: (ce[c], 0, 0)),
                  pl.BlockSpec((1, 1, 2 * D_MODEL), lambda c, ce, nu: (ce[c], 0, 0)),
                  pl.BlockSpec((1, D_MODEL, D_MODEL), lambda c, ce, nu: (ce[c], 0, 0)),
                  pl.BlockSpec((1, 1, D_MODEL), lambda c, ce, nu: (ce[c], 0, 0))],
        out_specs=pl.BlockSpec((rows, LANES), lambda c, ce, nu: (c, 0)))
    return pl.pallas_call(
        _expert_kernel,
        grid_spec=grid_spec,
        out_shape=jax.ShapeDtypeStruct(buf.shape, _f32),
        compiler_params=_cparams(("arbitrary",)),
        name="moe_experts",
    )(chunk_expert, n_used, buf, wgu, bgu, wd, bd)


def _combine_kernel(dest_ref, dest_next_ref, gw_ref, h_ref, eo_ref, g_ref, b_ref, o_ref, gbuf, ysl, sems, *, tokens):
    i = pl.program_id(0)
    slot = i % 2
    count = tokens * TOP_K

    def gather(idx_ref, into):
        def issue(t, c):
            for k in range(TOP_K):
                j = t * TOP_K + k
                _row_copy(eo_ref, idx_ref[j], gbuf.at[into], j, sems.at[into]).start()
            return c
        lax.fori_loop(0, tokens, issue, 0)

    @pl.when(i == 0)
    def _():
        gather(dest_ref, 0)

    @pl.when(i + 1 < pl.num_programs(0))
    def _():
        gather(dest_next_ref, 1 - slot)

    pltpu.make_async_copy(eo_ref.at[pl.ds(0, count * SLAB)], gbuf.at[slot], sems.at[slot]).wait()
    rows = gbuf.at[slot]

    unroll = 4

    def token_group(tg, c):
        for u in range(unroll):
            t = tg * unroll + u
            acc = DEEPNORM_ALPHA * h_ref[pl.ds(pl.multiple_of(t * SLAB, SLAB), SLAB), :]
            for k in range(TOP_K):
                j = t * TOP_K + k
                acc = acc + gw_ref[j] * rows[pl.ds(pl.multiple_of(j * SLAB, SLAB), SLAB), :]
            ysl[pl.ds(pl.multiple_of(t * SLAB, SLAB), SLAB), :] = acc
        return c

    lax.fori_loop(0, tokens // unroll, token_group, 0)
    o_ref[...] = _layer_norm(_slab_load(ysl, tokens), g_ref[...], b_ref[...])


def _combine_stage(dest, gate_w, h_slab, expert_out, g2, b2):
    n = h_slab.shape[0] // SLAB
    tokens = 256
    steps = n // tokens
    smem = lambda: pl.BlockSpec((tokens * TOP_K,), lambda i: (i,), memory_space=pltpu.SMEM)
    return pl.pallas_call(
        functools.partial(_combine_kernel, tokens=tokens),
        grid=(steps,),
        in_specs=[smem(),
                  pl.BlockSpec((tokens * TOP_K,), lambda i: (jnp.minimum(i + 1, steps - 1),), memory_space=pltpu.SMEM),
                  smem(),
                  pl.BlockSpec((tokens * SLAB, LANES), lambda i: (i, 0)),
                  pl.BlockSpec(memory_space=pl.ANY),
                  pl.BlockSpec((1, D_MODEL), lambda i: (0, 0)),
                  pl.BlockSpec((1, D_MODEL), lambda i: (0, 0))],
        out_specs=pl.BlockSpec((tokens, D_MODEL), lambda i: (i, 0)),
        out_shape=jax.ShapeDtypeStruct((n, D_MODEL), _f32),
        scratch_shapes=[pltpu.VMEM((2, tokens * TOP_K * SLAB, LANES), _f32),
                        pltpu.VMEM((tokens * SLAB, LANES), _f32),
                        pltpu.SemaphoreType.DMA((2,))],
        compiler_params=_cparams(("arbitrary",)),
        name="moe_combine_ln2",
    )(dest, dest, gate_w, h_slab, expert_out, g2, b2)


def _dispatch_plan(top_idx):
    m = top_idx.size
    e_flat = top_idx.reshape(m)
    onehot = (e_flat[:, None] == jnp.arange(N_EXPERTS, dtype=jnp.int32)[None, :]).astype(jnp.int32)
    csum = jnp.cumsum(onehot, axis=0)
    counts = csum[-1]
    padded = (counts + MOE_ROWS - 1) // MOE_ROWS * MOE_ROWS
    pends = jnp.cumsum(padded)
    pstarts = pends - padded
    dest = jnp.sum(onehot * (csum - 1 + pstarts[None, :]), axis=1).astype(jnp.int32)
    n_chunks = m // MOE_ROWS + N_EXPERTS
    chunk_start = jnp.arange(n_chunks, dtype=jnp.int32) * MOE_ROWS
    chunk_expert = jnp.minimum(jnp.sum((chunk_start[:, None] >= pends[None, :]).astype(jnp.int32), axis=1), N_EXPERTS - 1)
    n_used = (pends[-1] // MOE_ROWS).astype(jnp.int32).reshape(1)
    pad_start = (pstarts + counts).astype(jnp.int32)
    pad_len = (padded - counts).astype(jnp.int32)
    return dest, chunk_expert, n_used, pad_start, pad_len, n_chunks * MOE_ROWS


def _moe_half(h_slab, top_idx, gate_w, w_gate_up, b_gate_up, w_down, b_down, ln2_g, ln2_b):
    dest, chunk_expert, n_used, pad_start, pad_len, buf_rows = _dispatch_plan(top_idx)
    buf = _dispatch_stage(dest, pad_start, pad_len, h_slab, buf_rows)
    expert_out = _expert_stage(chunk_expert, n_used, buf, w_gate_up.astype(_bf16),
                               b_gate_up.reshape(N_EXPERTS, 1, -1), w_down.astype(_bf16),
                               b_down.reshape(N_EXPERTS, 1, -1))
    return _combine_stage(dest, gate_w.reshape(-1), h_slab, expert_out, ln2_g.reshape(1, -1), ln2_b.reshape(1, -1))


def kernel(x, w_in, cmp_pe_k, cmp_w1_k, cmp_w2_k, cmp_pe_v, cmp_w1_v, cmp_w2_v, w_proj_sb, w_proj_nsa, w_out,
           ln1_g, ln1_b, w_router, b_router, w_gate_up, b_gate_up, w_down, b_down, ln2_g, ln2_b):
    assert w_in.shape[0] == 1, "single-layer block"
    batch, seq, _ = x.shape
    assert seq % 512 == 0 and seq // SEL_BLOCK <= LANES and seq >= WINDOW + Q_BLOCK
    h_slab, top_idx, gate_w = _attention_half(
        x, w_in[0], cmp_pe_k[0], cmp_w1_k[0], cmp_w2_k[0], cmp_pe_v[0], cmp_w1_v[0], cmp_w2_v[0],
        w_proj_sb[0], w_proj_nsa[0], w_out[0], ln1_g[0], ln1_b[0], w_router[0], b_router[0])
    out = _moe_half(h_slab, top_idx, gate_w, w_gate_up[0], b_gate_up[0], w_down[0], b_down[0], ln2_g[0], ln2_b[0])
    return out.reshape(batch, seq, D_MODEL)
```

```python
import functools

import numpy as np
import jax
import jax.numpy as jnp
from jax import lax
from jax.experimental import pallas as pl
from jax.experimental.pallas import tpu as pltpu

D_MODEL = 1024
HEAD_DIM = 64
LANES = 128
Q_BLOCK = 128
CMP_BLOCK = 32
CMP_STRIDE = 16
SEL_BLOCK = 64
SEL_TOPK = 16
WINDOW = 512
ROPE_THETA = 10000.0
N_EXPERTS = 32
TOP_K = 4
SWIGLU_LIMIT = 7.0
SWIGLU_ALPHA = 1.702
LN_EPS = 1e-5
NEG_INF = -1e30
TAKEN = -3e38
DEEPNORM_ALPHA = 2.0 ** 0.25
QK_SCALE = HEAD_DIM ** -0.5
LOG2E = 1.4426950408889634

CB_SBQ, CB_SBK, CB_SBV, CB_NQ = 0, 4, 8, 12
CB_KC, CB_VC, CB_KS, CB_VS, CB_KW, CB_VW = 16, 17, 18, 19, 20, 21
CB_NG = 22
CB_MG = 24
PROJ_W = 40 * LANES

SB_TAIL_CUTOFF = -110.0

SEL_Q = 512
SEL_TK = 512
MOE_ROWS = 512
SLAB = D_MODEL // LANES
VMEM_LIMIT = 56 * 1024 * 1024

_bf16 = jnp.bfloat16
_f32 = jnp.float32


def _cparams(sem):
    return pltpu.CompilerParams(dimension_semantics=sem, vmem_limit_bytes=VMEM_LIMIT)


def _dot_t(a, b):
    return lax.dot_general(a, b, (((1,), (1,)), ((), ())), preferred_element_type=_f32)


def _dot(a, b):
    return jnp.dot(a, b, preferred_element_type=_f32)


def _lane_iota(shape):
    return lax.broadcasted_iota(jnp.int32, shape, len(shape) - 1)


def _half0(shape=(1, LANES)):
    return _lane_iota(shape) < HEAD_DIM


def _in_proj_kernel(x_ref, w_ref, o_ref):
    o_ref[...] = _dot(x_ref[...].astype(_bf16), w_ref[...]).astype(o_ref.dtype)


def _in_proj(x2, w):
    n = x2.shape[0]
    tm, tn = 512, 1280
    return pl.pallas_call(
        _in_proj_kernel,
        grid=(PROJ_W // tn, n // tm),
        in_specs=[pl.BlockSpec((tm, D_MODEL), lambda j, i: (i, 0)),
                  pl.BlockSpec((D_MODEL, tn), lambda j, i: (0, j))],
        out_specs=pl.BlockSpec((tm, tn), lambda j, i: (i, j)),
        out_shape=jax.ShapeDtypeStruct((n, PROJ_W), _bf16),
        compiler_params=_cparams(("arbitrary", "arbitrary")),
        name="in_proj",
    )(x2, w)


def _rope(x, cos, sin_signed):
    first = (_lane_iota((1, LANES)) % HEAD_DIM) < (HEAD_DIM // 2)
    swapped = jnp.where(first, pltpu.roll(x, LANES - HEAD_DIM // 2, 1), pltpu.roll(x, HEAD_DIM // 2, 1))
    return x * cos + swapped * sin_signed


def _dup(x, g):
    other = pltpu.roll(x, HEAD_DIM, 1)
    h0 = _half0()
    return jnp.where(h0, x, other) if g == 0 else jnp.where(h0, other, x)


def _rope_kernel(nq_ref, kc_ref, vc_ref, ks_ref, vs_ref, kw_ref, vw_ref, cos_ref, sin_ref,
                 nq_o, kc_o, vc_o, ka_o, vs_o, kw_o, vw_o, *, blocks_per_seq):
    ts = cos_ref.shape[0]
    cos = cos_ref[...]
    sin = sin_ref[...]
    for c in range(4):
        sl = slice(c * LANES, (c + 1) * LANES)
        nq_o[:, sl] = (_rope(nq_ref[:, sl].astype(_f32), cos, sin) * (QK_SCALE * LOG2E)).astype(_bf16)
    kc_o[...] = _rope(kc_ref[...].astype(_f32), cos, sin).astype(_bf16)
    vc_o[...] = vc_ref[...]
    ks = _rope(ks_ref[...].astype(_f32), cos, sin)
    kw = _rope(kw_ref[...].astype(_f32), cos, sin)
    vs = vs_ref[...].astype(_f32)
    vw = vw_ref[...].astype(_f32)
    pos = (pl.program_id(0) % blocks_per_seq) * ts + lax.broadcasted_iota(jnp.int32, (ts, LANES), 0)
    lane = _lane_iota((ts, LANES))
    onehot = jnp.where((pos // SEL_BLOCK) % HEAD_DIM + HEAD_DIM == lane, 1.0, 0.0)
    for g in range(2):
        ka_o[g] = jnp.where(_half0(), _dup(ks, g), onehot).astype(_bf16)
        vsa = jnp.where(_half0(), _dup(vs, g), 1.0)
        for c in range(ts // SEL_TK):
            vs_o[g, c] = vsa[c * SEL_TK:(c + 1) * SEL_TK, :].T.astype(_bf16)
        kw_o[g] = _dup(kw, g).astype(_bf16)
        vwd = jnp.where(_half0(), _dup(vw, g), 1.0)
        for c in range(ts // LANES):
            vw_o[g, c] = vwd[c * LANES:(c + 1) * LANES, :].T.astype(_bf16)


def _rope_stage(proj, cos, sin_signed, seq):
    n = proj.shape[0]
    ts = 512
    bps = seq // ts
    col = lambda cb: pl.BlockSpec((ts, LANES), lambda i, cb=cb: (i, cb))
    tab = pl.BlockSpec((ts, LANES), lambda i: (i % bps, 0))
    grp = lambda w: pl.BlockSpec((2, ts, w), lambda i: (0, i, 0))
    return pl.pallas_call(
        functools.partial(_rope_kernel, blocks_per_seq=bps),
        grid=(n // ts,),
        in_specs=[pl.BlockSpec((ts, 4 * LANES), lambda i: (i, CB_NQ // 4)),
                  col(CB_KC), col(CB_VC), col(CB_KS), col(CB_VS), col(CB_KW), col(CB_VW), tab, tab],
        out_specs=[pl.BlockSpec((ts, 4 * LANES), lambda i: (i, 0)),
                   pl.BlockSpec((ts, LANES), lambda i: (i, 0)),
                   pl.BlockSpec((ts, LANES), lambda i: (i, 0)),
                   grp(LANES),
                   pl.BlockSpec((2, ts // SEL_TK, LANES, SEL_TK), lambda i: (0, i, 0, 0)),
                   grp(LANES),
                   pl.BlockSpec((2, ts // LANES, LANES, LANES), lambda i: (0, i, 0, 0))],
        out_shape=[jax.ShapeDtypeStruct((n, 4 * LANES), _bf16),
                   jax.ShapeDtypeStruct((n, LANES), _bf16),
                   jax.ShapeDtypeStruct((n, LANES), _bf16),
                   jax.ShapeDtypeStruct((2, n, LANES), _bf16),
                   jax.ShapeDtypeStruct((2, n // SEL_TK, LANES, SEL_TK), _bf16),
                   jax.ShapeDtypeStruct((2, n, LANES), _bf16),
                   jax.ShapeDtypeStruct((2, n // LANES, LANES, LANES), _bf16)],
        compiler_params=_cparams(("arbitrary",)),
        name="rope_layout",
    )(proj, proj, proj, proj, proj, proj, proj, cos, sin_signed)


def _gelu_tanh(x):
    return 0.5 * x * (1.0 + jnp.tanh(0.7978845608028654 * (x + 0.044715 * (x * x * x))))


def _compress_one(x_ref, pe_t, pe_b, w_t, w_b, w2, out_ref, transposed):
    x = x_ref[0].astype(_f32)
    a = _dot((x + pe_t[...]).astype(_bf16), w_t[...])
    b = _dot((x + pe_b[...]).astype(_bf16), w_b[...])
    nc = a.shape[0]
    pre = a + pltpu.roll(b, nc - 1, 0)
    y = _dot(_gelu_tanh(pre).astype(_bf16), w2[...])
    for g in range(2):
        d = _dup(y, g)
        out_ref[0, g] = (d.T if transposed else d).astype(_bf16)


def _compress_kernel(k_ref, v_ref, kpt, kpb, kwt, kwb, kw2, vpt, vpb, vwt, vwb, vw2, ko_ref, vo_ref):
    _compress_one(k_ref, kpt, kpb, kwt, kwb, kw2, ko_ref, False)
    _compress_one(v_ref, vpt, vpb, vwt, vwb, vw2, vo_ref, True)


def _compress_weights(pe, w1, w2):
    half = CMP_BLOCK // 2
    eye = jnp.eye(2, dtype=_f32)
    outs = []
    for part in range(2):
        w = w1[part * half * HEAD_DIM:(part + 1) * half * HEAD_DIM].reshape(half, HEAD_DIM, HEAD_DIM)
        wbd = (w[:, None, :, None, :] * eye[None, :, None, :, None]).reshape(half * 2 * HEAD_DIM, 2 * HEAD_DIM)
        p = jnp.broadcast_to(pe[part * half:(part + 1) * half, None, :], (half, 2, HEAD_DIM)).reshape(1, -1)
        outs.append((p.astype(_f32), wbd.astype(_bf16)))
    w2bd = (w2[None, :, None, :] * eye[:, None, :, None]).reshape(2 * HEAD_DIM, 2 * HEAD_DIM).astype(_bf16)
    (pt, wt), (pb, wb) = outs
    return pt, pb, wt, wb, w2bd


def _compress_stage(kc_r, vc_r, kparams, vparams, batch, seq):
    nc = seq // CMP_STRIDE
    width = CMP_STRIDE * LANES
    xs = pl.BlockSpec((1, nc, width), lambda b: (b, 0, 0))
    full = lambda a: pl.BlockSpec(a.shape, lambda b: (0,) * a.ndim)
    out = pl.BlockSpec((1, 2, nc, LANES), lambda b: (b, 0, 0, 0))
    weights = list(kparams) + list(vparams)
    return pl.pallas_call(
        _compress_kernel,
        grid=(batch,),
        in_specs=[xs, xs] + [full(a) for a in weights],
        out_specs=[out, pl.BlockSpec((1, 2, LANES, nc), lambda b: (b, 0, 0, 0))],
        out_shape=[jax.ShapeDtypeStruct((batch, 2, nc, LANES), _bf16),
                   jax.ShapeDtypeStruct((batch, 2, LANES, nc), _bf16)],
        compiler_params=_cparams(("arbitrary",)),
        name="compress",
    )(kc_r.reshape(batch, nc, width), vc_r.reshape(batch, nc, width), *weights)


def _head_q(q_ref, r):
    q2 = q_ref[:, (r // 2) * LANES:(r // 2 + 1) * LANES]
    keep = _half0() if r % 2 == 0 else jnp.logical_not(_half0())
    return jnp.where(keep, q2, jnp.zeros_like(q2))


def _softmax_over_rows(s):
    m = jnp.max(s, axis=0, keepdims=True)
    e = jnp.exp2(s - m)
    l = jnp.sum(e, axis=0, keepdims=True)
    return e * jnp.where(m > 0.5 * NEG_INF, 1.0 / l, 0.0)


def _pair(even, odd):
    return jnp.where(_half0(), even, odd)


def _nsa_cw_kernel(q_ref, kc_ref, vct_ref, kw_ref, vwt_ref, ng_ref, stt_ref, yp_ref, mb_ref):
    t0 = pl.program_id(2) * Q_BLOCK
    qpos = t0 + _lane_iota((1, Q_BLOCK))
    gates = jax.nn.sigmoid(ng_ref[...].astype(_f32).T[0:16, :])
    kc = kc_ref[0, 0]
    vct = vct_ref[0, 0]
    nc = kc.shape[0]
    cend = lax.broadcasted_iota(jnp.int32, (nc, 1), 0) * CMP_STRIDE + (CMP_BLOCK - 1)
    cmask = cend <= qpos
    start = pl.multiple_of(jnp.maximum(t0 - WINDOW, 0), Q_BLOCK)
    wlen = WINDOW + Q_BLOCK
    kwin = kw_ref[0, 0, pl.ds(start, wlen), :]
    kpos = start + lax.broadcasted_iota(jnp.int32, (wlen, 1), 0)
    wmask = (kpos <= qpos) & (qpos - kpos < WINDOW)
    blk0 = start // Q_BLOCK

    qs = [_head_q(q_ref, r) for r in range(4)]
    s_cmp = [_dot_t(kc, q) for q in qs]
    s_win = [_dot_t(kwin, q) for q in qs]
    p_cmp = [_softmax_over_rows(jnp.where(cmask, s, NEG_INF)) for s in s_cmp]
    imp = (p_cmp[0] + p_cmp[1]) + (p_cmp[2] + p_cmp[3])
    e_win = []
    for s in s_win:
        s = jnp.where(wmask, s, NEG_INF)
        e_win.append(jnp.exp2(s - jnp.max(s, axis=0, keepdims=True)).astype(_bf16))
    vwt = jnp.concatenate([vwt_ref[0, 0, blk0 + c] for c in range(wlen // Q_BLOCK)], axis=1)
    yts = []
    for r in range(4):
        o_cmp = _dot(vct, p_cmp[r].astype(_bf16))[0:HEAD_DIM]
        win = _dot(vwt, e_win[r])
        o_win = win[0:HEAD_DIM] * (1.0 / win[HEAD_DIM:2 * HEAD_DIM])
        yts.append(gates[3 * r:3 * r + 1] * o_cmp + gates[3 * r + 2:3 * r + 3] * o_win)
    yp_ref[:, 0:LANES] = jnp.concatenate(yts[0:2], axis=0).T
    yp_ref[:, LANES:2 * LANES] = jnp.concatenate(yts[2:4], axis=0).T

    imp_hi = imp.astype(_bf16)
    rest = imp - imp_hi.astype(_f32)
    imp_mid = rest.astype(_bf16)
    imp_lo = (rest - imp_mid.astype(_f32)).astype(_bf16)
    st = stt_ref[...]
    p_slc = _dot(st, imp_hi) + (_dot(st, imp_mid) + _dot(st, imp_lo))
    selj = lax.broadcasted_iota(jnp.int32, (LANES, 1), 0)
    blk_t = qpos // SEL_BLOCK
    forced = (selj == 0) | (selj == blk_t) | (selj == blk_t - 1)
    score = jnp.where(forced, TAKEN, jnp.where(selj <= blk_t, p_slc, NEG_INF))
    seljf = selj.astype(_f32)
    picked = forced
    for _ in range(SEL_TOPK - 3):
        m = jnp.max(score, axis=0, keepdims=True)
        first = jnp.min(jnp.where(score == m, seljf, float(LANES)), axis=0, keepdims=True)
        hit = seljf == first
        picked = picked | hit
        score = jnp.where(hit, TAKEN, score)
    mb_ref[0] = jnp.where(picked, 0.0, NEG_INF).T.astype(_bf16)


def _nsa_cw_stage(nq_r, kc_d, vc_t, kw_d, vw_t, proj, stencil_t, batch, seq):
    n = nq_r.shape[0]
    nblk = seq // Q_BLOCK
    nc = seq // CMP_STRIDE
    qrow = lambda b, g, i: b * nblk + i
    return pl.pallas_call(
        _nsa_cw_kernel,
        grid=(batch, 2, nblk),
        in_specs=[pl.BlockSpec((Q_BLOCK, 2 * LANES), lambda b, g, i: (qrow(b, g, i), g)),
                  pl.BlockSpec((1, 1, nc, LANES), lambda b, g, i: (b, g, 0, 0)),
                  pl.BlockSpec((1, 1, LANES, nc), lambda b, g, i: (b, g, 0, 0)),
                  pl.BlockSpec((1, 1, seq, LANES), lambda b, g, i: (g, b, 0, 0)),
                  pl.BlockSpec((1, 1, nblk, LANES, LANES), lambda b, g, i: (g, b, 0, 0, 0)),
                  pl.BlockSpec((Q_BLOCK, LANES), lambda b, g, i: (qrow(b, g, i), CB_NG + g)),
                  pl.BlockSpec((LANES, nc), lambda b, g, i: (0, 0))],
        out_specs=[pl.BlockSpec((Q_BLOCK, 2 * LANES), lambda b, g, i: (qrow(b, g, i), g)),
                   pl.BlockSpec((1, Q_BLOCK, LANES), lambda b, g, i: (g, qrow(b, g, i), 0))],
        out_shape=[jax.ShapeDtypeStruct((n, 4 * LANES), _f32),
                   jax.ShapeDtypeStruct((2, n, LANES), _bf16)],
        compiler_params=_cparams(("arbitrary", "arbitrary", "arbitrary")),
        name="nsa_cmp_win_select",
    )(nq_r, kc_d, vc_t, kw_d.reshape(2, batch, seq, LANES), vw_t.reshape(2, batch, nblk, LANES, LANES), proj, stencil_t)


def _nsa_sel_kernel(q_ref, mb_ref, ka_ref, vat_ref, ng_ref, yp_ref, o_ref, qs_ref, m_ref, acc_ref, s_ref, p_ref, alpha_ref):
    tk = SEL_TK
    t0 = pl.program_id(2) * SEL_Q
    h0 = _half0()
    mb = mb_ref[0].astype(_f32)
    bias = [pltpu.roll(mb, HEAD_DIM, 1), mb]
    for r in range(4):
        q = q_ref[:, (r // 2) * LANES:(r // 2 + 1) * LANES].astype(_f32)
        if r % 2 == 1:
            q = pltpu.roll(q, HEAD_DIM, 1)
        for v in range(2):
            qs_ref[v, r * SEL_Q:(r + 1) * SEL_Q, :] = jnp.where(h0, q, bias[v]).astype(_bf16)
    m_ref[...] = jnp.full(m_ref.shape, NEG_INF, _f32)
    acc_ref[...] = jnp.zeros(acc_ref.shape, _f32)
    p_ref[...] = jnp.zeros(p_ref.shape, _bf16)
    alpha_ref[...] = jnp.ones(alpha_ref.shape, _f32)
    qpos = t0 + _lane_iota((1, 4 * SEL_Q)) % SEL_Q

    def scores(kt):
        version = (kt * (tk // SEL_BLOCK)) // HEAD_DIM
        return _dot_t(ka_ref[0, 0, pl.ds(pl.multiple_of(kt * tk, tk), tk), :], qs_ref[version])

    def softmax_step(s):
        m_old = m_ref[...]
        m_new = jnp.maximum(m_old, jnp.max(s, axis=0, keepdims=True))
        m_ref[...] = m_new
        return jnp.exp2(s - m_new).astype(_bf16), jnp.exp2(m_old - m_new)

    def accumulate(kt, alpha, p):
        acc_ref[...] = alpha * acc_ref[...] + _dot(vat_ref[0, 0, kt], p)

    def trip(kt, carry):
        accumulate(jnp.maximum(kt - 1, 0), alpha_ref[...], p_ref[...])
        s = s_ref[...]
        s_ref[...] = scores(kt + 1)
        p, alpha = softmax_step(s)
        p_ref[...] = p
        alpha_ref[...] = alpha
        return carry

    n_full = t0 // tk
    s_ref[...] = scores(0)

    def two_trips(kp, carry):
        trip(2 * kp, carry)
        return trip(2 * kp + 1, carry)

    lax.fori_loop(0, n_full // 2, two_trips, 0)

    @pl.when(n_full % 2 == 1)
    def _():
        trip(n_full - 1, 0)

    accumulate(jnp.maximum(n_full - 1, 0), alpha_ref[...], p_ref[...])
    kpos = n_full * tk + lax.broadcasted_iota(jnp.int32, (tk, 1), 0)
    p, alpha = softmax_step(jnp.where(kpos <= qpos, s_ref[...], NEG_INF))
    accumulate(n_full, alpha, p)
    acc = acc_ref[...]
    o = acc[0:HEAD_DIM] * (1.0 / acc[HEAD_DIM:2 * HEAD_DIM])
    gates = jax.nn.sigmoid(ng_ref[...].astype(_f32).T[0:16, :])
    ys = [gates[3 * r + 1:3 * r + 2] * o[:, r * SEL_Q:(r + 1) * SEL_Q] for r in range(4)]
    o_ref[:, 0:LANES] = (yp_ref[:, 0:LANES] + jnp.concatenate(ys[0:2], axis=0).T).astype(o_ref.dtype)
    o_ref[:, LANES:2 * LANES] = (yp_ref[:, LANES:2 * LANES] + jnp.concatenate(ys[2:4], axis=0).T).astype(o_ref.dtype)


def _nsa_sel_stage(nq_r, mbias, k_aug, vs_t, proj, ypart, batch, seq):
    n = nq_r.shape[0]
    nblk = seq // SEL_Q
    qrow = lambda b, g, i: b * nblk + i
    return pl.pallas_call(
        _nsa_sel_kernel,
        grid=(batch, 2, nblk),
        in_specs=[pl.BlockSpec((SEL_Q, 2 * LANES), lambda b, g, i: (qrow(b, g, i), g)),
                  pl.BlockSpec((1, SEL_Q, LANES), lambda b, g, i: (g, qrow(b, g, i), 0)),
                  pl.BlockSpec((1, 1, seq, LANES), lambda b, g, i: (g, b, 0, 0)),
                  pl.BlockSpec((1, 1, seq // SEL_TK, LANES, SEL_TK), lambda b, g, i: (g, b, 0, 0, 0)),
                  pl.BlockSpec((SEL_Q, LANES), lambda b, g, i: (qrow(b, g, i), CB_NG + g)),
                  pl.BlockSpec((SEL_Q, 2 * LANES), lambda b, g, i: (qrow(b, g, i), g))],
        out_specs=pl.BlockSpec((SEL_Q, 2 * LANES), lambda b, g, i: (qrow(b, g, i), g)),
        out_shape=jax.ShapeDtypeStruct((n, 4 * LANES), _bf16),
        scratch_shapes=[pltpu.VMEM((2, 4 * SEL_Q, LANES), _bf16),
                        pltpu.VMEM((1, 4 * SEL_Q), _f32),
                        pltpu.VMEM((LANES, 4 * SEL_Q), _f32),
                        pltpu.VMEM((SEL_TK, 4 * SEL_Q), _f32),
                        pltpu.VMEM((SEL_TK, 4 * SEL_Q), _bf16),
                        pltpu.VMEM((1, 4 * SEL_Q), _f32)],
        compiler_params=_cparams(("arbitrary", "arbitrary", "arbitrary")),
        name="nsa_selected",
    )(nq_r, mbias, k_aug.reshape(2, batch, seq, LANES), vs_t.reshape(2, batch, seq // SEL_TK, LANES, SEL_TK), proj, ypart)


def _sb_kernel(q_ref, k_ref, v_ref, o_ref, qs_ref, tail_ref, acc_ref):
    i = pl.program_id(1)
    h0 = _half0()
    heads = 2 * (q_ref.shape[1] // LANES)
    for h in range(heads):
        q = q_ref[:, (h // 2) * LANES:(h // 2 + 1) * LANES]
        keep = h0 if h % 2 == 0 else jnp.logical_not(h0)
        qs_ref[h] = jnp.where(keep, q, jnp.zeros_like(q)) * QK_SCALE
    tail_ref[...] = jnp.zeros(tail_ref.shape, _f32)
    acc_ref[...] = jnp.zeros(acc_ref.shape, _f32)
    rloc = lax.broadcasted_iota(jnp.int32, (Q_BLOCK, Q_BLOCK), 0)
    cloc = lax.broadcasted_iota(jnp.int32, (Q_BLOCK, Q_BLOCK), 1)
    later = jnp.where(rloc > cloc, 1.0, 0.0).astype(_bf16)

    def key_block(j, diagonal):
        k0 = pl.multiple_of(j * Q_BLOCK, Q_BLOCK)
        past = cloc < rloc
        cols = [slice((h // 2) * LANES, (h // 2 + 1) * LANES) for h in range(heads)]
        zs = [_dot_t(qs_ref[h], k_ref[0, pl.ds(k0, Q_BLOCK), cols[h]]) for h in range(heads)]
        log_beta, log_keep = [], []
        for z in zs:
            sp = jnp.maximum(z, 0.0) + jnp.log(1.0 + jnp.exp(-jnp.abs(z)))
            log_beta.append(z - sp)
            log_keep.append(jnp.where(past, -sp, 0.0) if diagonal else -sp)
        inner = []
        for lk in log_keep:
            hi = lk.astype(_bf16)
            lo = (lk - hi.astype(_f32)).astype(_bf16)
            inner.append(_dot(hi, later) + _dot(lo, later))
        probs = []
        for h in range(heads):
            a = jnp.exp(log_beta[h] + inner[h] + tail_ref[h])
            probs.append((jnp.where(past, a, 0.0) if diagonal else a).astype(_bf16))
        worst = jnp.full((Q_BLOCK, 1), -jnp.inf, _f32)
        for h in range(heads):
            acc_ref[h] = acc_ref[h] + _dot(probs[h], v_ref[0, pl.ds(k0, Q_BLOCK), cols[h]])
            tail = tail_ref[h] + jnp.sum(log_keep[h], axis=-1, keepdims=True)
            tail_ref[h] = tail
            worst = jnp.maximum(worst, tail)
        return jnp.max(worst)

    def cond(c):
        j, worst_tail = c
        return (j >= 0) & (worst_tail > SB_TAIL_CUTOFF)

    def body(c):
        j, _ = c
        return j - 1, key_block(j, False)

    lax.while_loop(cond, body, (i - 1, key_block(i, True)))
    for p in range(heads // 2):
        o_ref[:, p * LANES:(p + 1) * LANES] = jnp.where(h0, acc_ref[2 * p], acc_ref[2 * p + 1]).astype(o_ref.dtype)


def _sb_stage(proj, batch, seq):
    n = proj.shape[0]
    nblk = seq // Q_BLOCK
    width = 4 * LANES
    proj3 = proj.reshape(batch, seq, PROJ_W)
    return pl.pallas_call(
        _sb_kernel,
        grid=(batch, nblk),
        in_specs=[pl.BlockSpec((Q_BLOCK, width), lambda b, i: (b * nblk + i, CB_SBQ // 4)),
                  pl.BlockSpec((1, seq, width), lambda b, i: (b, 0, CB_SBK // 4)),
                  pl.BlockSpec((1, seq, width), lambda b, i: (b, 0, CB_SBV // 4))],
        out_specs=pl.BlockSpec((Q_BLOCK, width), lambda b, i: (b * nblk + i, 0)),
        out_shape=jax.ShapeDtypeStruct((n, width), _bf16),
        scratch_shapes=[pltpu.VMEM((8, Q_BLOCK, LANES), _bf16),
                        pltpu.VMEM((8, Q_BLOCK, 1), _f32),
                        pltpu.VMEM((8, Q_BLOCK, LANES), _f32)],
        compiler_params=_cparams(("arbitrary", "arbitrary")),
        name="stick_breaking",
    )(proj, proj3, proj3)


def _layer_norm(x, g, b):
    mu = jnp.mean(x, axis=-1, keepdims=True)
    xc = x - mu
    var = jnp.mean(xc * xc, axis=-1, keepdims=True)
    return xc * lax.rsqrt(var + LN_EPS) * g + b


def _merge_kernel(x_ref, ysb_ref, yns_ref, mg0_ref, mg1_ref, wsb_ref, wns_ref, wo_ref, g_ref, b_ref,
                  wr_ref, br_ref, h_ref, idx_ref, gw_ref):
    m0 = jax.nn.sigmoid(mg0_ref[...].astype(_f32))
    m1 = jax.nn.sigmoid(mg1_ref[...].astype(_f32))
    merged = m0 * _dot(ysb_ref[...], wsb_ref[...]) + m1 * _dot(yns_ref[...], wns_ref[...])
    pre = DEEPNORM_ALPHA * x_ref[...] + _dot(merged.astype(_bf16), wo_ref[...])
    h = _layer_norm(pre, g_ref[...], b_ref[...])
    tm = h.shape[0]
    for s in range(SLAB):
        h_ref[pl.ds(s, tm, stride=SLAB), :] = h[:, s * LANES:(s + 1) * LANES]
    h_hi = h.astype(_bf16)
    h_lo = (h - h_hi.astype(_f32)).astype(_bf16)
    w = wr_ref[...]
    w_hi = w.astype(_bf16)
    w_lo = (w - w_hi.astype(_f32)).astype(_bf16)
    logits = (_dot(h_hi, w_hi) + (_dot(h_hi, w_lo) + _dot(h_lo, w_hi))) + br_ref[...]
    lane = _lane_iota((1, LANES))
    lanef = lane.astype(_f32)
    lg = jnp.where(lane < N_EXPERTS, logits, TAKEN)
    vals, idxs = [], []
    for _ in range(TOP_K):
        m = jnp.max(lg, axis=-1, keepdims=True)
        first = jnp.min(jnp.where(lg == m, lanef, float(LANES)), axis=-1, keepdims=True)
        vals.append(m)
        idxs.append(first)
        lg = jnp.where(lanef == first, TAKEN, lg)
    es = [jnp.exp(v - vals[0]) for v in vals]
    inv = 1.0 / (es[0] + es[1] + es[2] + es[3])
    idx_t = jnp.zeros(lg.shape, _f32)
    gw_t = jnp.zeros(lg.shape, _f32)
    for k in range(TOP_K):
        idx_t = jnp.where(lane == k, idxs[k], idx_t)
        gw_t = jnp.where(lane == k, es[k] * inv, gw_t)
    idx_ref[...] = idx_t[:, :TOP_K].astype(jnp.int32)
    gw_ref[...] = gw_t[:, :TOP_K]


def _merge_stage(x2, y_sb, y_nsa, proj, wsb, wns, wo, g1, b1, wr, br):
    n = x2.shape[0]
    tm = 512
    row = lambda w: pl.BlockSpec((tm, w), lambda i: (i, 0))
    full = lambda a: pl.BlockSpec(a.shape, lambda i: (0,) * a.ndim)
    return pl.pallas_call(
        _merge_kernel,
        grid=(n // tm,),
        in_specs=[row(D_MODEL), row(4 * LANES), row(4 * LANES),
                  pl.BlockSpec((tm, D_MODEL), lambda i: (i, CB_MG // 8)),
                  pl.BlockSpec((tm, D_MODEL), lambda i: (i, CB_MG // 8 + 1)),
                  full(wsb), full(wns), full(wo), full(g1), full(b1), full(wr), full(br)],
        out_specs=[pl.BlockSpec((tm * SLAB, LANES), lambda i: (i, 0)), row(TOP_K), row(TOP_K)],
        out_shape=[jax.ShapeDtypeStruct((n * SLAB, LANES), _f32),
                   jax.ShapeDtypeStruct((n, TOP_K), jnp.int32),
                   jax.ShapeDtypeStruct((n, TOP_K), _f32)],
        compiler_params=_cparams(("arbitrary",)),
        name="merge_ln1_router",
    )(x2, y_sb, y_nsa, proj, proj, wsb, wns, wo, g1, b1, wr, br)


def _prep_w_in(w):
    main = w[:, :CB_NG * LANES]
    ng = w[:, CB_NG * LANES:CB_NG * LANES + 24]
    mg = w[:, CB_NG * LANES + 24:]
    pad = jnp.zeros((w.shape[0], LANES - 12), w.dtype)
    return jnp.concatenate([main, ng[:, :12], pad, ng[:, 12:], pad, mg], axis=1).astype(_bf16)


def _rope_tables(seq):
    half = HEAD_DIM // 2
    inv_freq = ROPE_THETA ** (-jnp.arange(half, dtype=_f32) / half)
    ang = jnp.arange(seq, dtype=_f32)[:, None] * inv_freq[None, :]
    cos = jnp.cos(ang)
    sin = jnp.sin(ang)
    cos128 = jnp.concatenate([cos, cos, cos, cos], axis=1)
    sin128 = jnp.concatenate([-sin, sin, -sin, sin], axis=1)
    return cos128, sin128


def _stencil(nc):
    n = np.arange(nc)[:, None]
    j = np.arange(LANES)[None, :]
    ratio = SEL_BLOCK // CMP_STRIDE
    ok = (n >= ratio * j - 1) & (n <= ratio * j + ratio - 1) & (n < nc - 1)
    return jnp.asarray(ok.astype(np.float32).T, dtype=_bf16)


def _attention_half(x, w_in, cmp_pe_k, cmp_w1_k, cmp_w2_k, cmp_pe_v, cmp_w1_v, cmp_w2_v,
                    w_proj_sb, w_proj_nsa, w_out, ln1_g, ln1_b, w_router, b_router):
    batch, seq, _ = x.shape
    n = batch * seq
    x2 = x.reshape(n, D_MODEL)
    proj = _in_proj(x2, _prep_w_in(w_in))
    cos, sin_signed = _rope_tables(seq)
    nq_r, kc_r, vc_r, k_aug, vs_t, kw_d, vw_t = _rope_stage(proj, cos, sin_signed, seq)
    kc_d, vc_t = _compress_stage(kc_r, vc_r, _compress_weights(cmp_pe_k, cmp_w1_k, cmp_w2_k),
                                 _compress_weights(cmp_pe_v, cmp_w1_v, cmp_w2_v), batch, seq)
    ypart, mbias = _nsa_cw_stage(nq_r, kc_d, vc_t, kw_d, vw_t, proj, _stencil(seq // CMP_STRIDE), batch, seq)
    y_nsa = _nsa_sel_stage(nq_r, mbias, k_aug, vs_t, proj, ypart, batch, seq)
    y_sb = _sb_stage(proj, batch, seq)
    wr = jnp.pad(w_router.astype(_f32), ((0, 0), (0, LANES - N_EXPERTS)))
    br = jnp.pad(b_router.astype(_f32), (0, LANES - N_EXPERTS)).reshape(1, LANES)
    return _merge_stage(x2, y_sb, y_nsa, proj, w_proj_sb.astype(_bf16), w_proj_nsa.astype(_bf16),
                        w_out.astype(_bf16), ln1_g.reshape(1, -1), ln1_b.reshape(1, -1), wr, br)


def _row_copy(src, src_row, dst, dst_row, sem):
    return pltpu.make_async_copy(src.at[pl.ds(src_row * SLAB, SLAB)], dst.at[pl.ds(dst_row * SLAB, SLAB)], sem)


def _dispatch_kernel(pad_start_ref, pad_len_ref, dest_ref, h_hbm, buf_ref, hbuf, zslab, in_sems, row_sems, zero_sem,
                     *, tokens):
    i = pl.program_id(0)
    last = pl.num_programs(0) - 1
    slot = i % 2
    block_rows = tokens * SLAB

    def block_copy(step, s):
        return pltpu.make_async_copy(h_hbm.at[pl.ds(step * block_rows, block_rows)], hbuf.at[s], in_sems.at[s])

    def wait_rows(s):
        for _ in range(TOP_K):
            pltpu.make_async_copy(hbuf.at[s], buf_ref.at[pl.ds(0, block_rows)], row_sems.at[s]).wait()

    def padding_rows(act):
        def per_expert(e, c):
            def per_row(r, c2):
                act(_row_copy(zslab, 0, buf_ref, pad_start_ref[e] + r, zero_sem))
                return c2
            lax.fori_loop(0, pad_len_ref[e], per_row, 0)
            return c
        lax.fori_loop(0, N_EXPERTS, per_expert, 0)

    @pl.when(i == 0)
    def _():
        block_copy(0, 0).start()
        zslab[...] = jnp.zeros(zslab.shape, _f32)
        padding_rows(lambda cp: cp.start())

    @pl.when(i > 0)
    def _():
        wait_rows(1 - slot)

    @pl.when(i < last)
    def _():
        block_copy(i + 1, 1 - slot).start()

    block_copy(i, slot).wait()
    src = hbuf.at[slot]

    def issue(t, c):
        for k in range(TOP_K):
            _row_copy(src, t, buf_ref, dest_ref[t * TOP_K + k], row_sems.at[slot]).start(priority=k % 2)
        return c

    lax.fori_loop(0, tokens, issue, 0)

    @pl.when(i == last)
    def _():
        wait_rows(slot)
        padding_rows(lambda cp: cp.wait())


def _dispatch_stage(dest, pad_start, pad_len, h_slab, buf_rows):
    n = h_slab.shape[0] // SLAB
    tokens = 256
    grid_spec = pltpu.PrefetchScalarGridSpec(
        num_scalar_prefetch=2,
        grid=(n // tokens,),
        in_specs=[pl.BlockSpec((tokens * TOP_K,), lambda i, ps, pn: (i,), memory_space=pltpu.SMEM),
                  pl.BlockSpec(memory_space=pl.ANY)],
        out_specs=pl.BlockSpec(memory_space=pl.ANY),
        scratch_shapes=[pltpu.VMEM((2, tokens * SLAB, LANES), _f32),
                        pltpu.VMEM((SLAB, LANES), _f32),
                        pltpu.SemaphoreType.DMA((2,)),
                        pltpu.SemaphoreType.DMA((2,)),
                        pltpu.SemaphoreType.DMA(())])
    return pl.pallas_call(
        functools.partial(_dispatch_kernel, tokens=tokens),
        grid_spec=grid_spec,
        out_shape=jax.ShapeDtypeStruct((buf_rows * SLAB, LANES), _f32),
        compiler_params=pltpu.CompilerParams(dimension_semantics=("arbitrary",), has_side_effects=True),
        name="moe_dispatch",
    )(pad_start, pad_len, dest, h_slab)


def _slab_load(ref, rows):
    return jnp.concatenate([ref[pl.ds(s, rows, stride=SLAB), :] for s in range(SLAB)], axis=1)


def _expert_kernel(ce_ref, used_ref, x_ref, wgu_ref, bgu_ref, wd_ref, bd_ref, o_ref):
    del ce_ref
    c = pl.program_id(0)

    @pl.when(c < used_ref[0])
    def _():
        x = _slab_load(x_ref, MOE_ROWS).astype(_bf16)
        gu = _dot(x, wgu_ref[0]) + bgu_ref[0]
        gate = jnp.minimum(gu[:, :D_MODEL], SWIGLU_LIMIT)
        up = jnp.clip(gu[:, D_MODEL:], -SWIGLU_LIMIT, SWIGLU_LIMIT)
        h = gate * jax.nn.sigmoid(SWIGLU_ALPHA * gate) * (up + 1.0)
        y = _dot(h.astype(_bf16), wd_ref[0]) + bd_ref[0]
        for s in range(SLAB):
            o_ref[pl.ds(s, MOE_ROWS, stride=SLAB), :] = y[:, s * LANES:(s + 1) * LANES]

    @pl.when(c >= used_ref[0])
    def _():
        o_ref[...] = jnp.zeros(o_ref.shape, o_ref.dtype)


def _expert_stage(chunk_expert, n_used, buf, wgu, bgu, wd, bd):
    n_chunks = chunk_expert.shape[0]
    rows = MOE_ROWS * SLAB
    grid_spec = pltpu.PrefetchScalarGridSpec(
        num_scalar_prefetch=2,
        grid=(n_chunks,),
        in_specs=[pl.BlockSpec((rows, LANES), lambda c, ce, nu: (jnp.minimum(c, nu[0] - 1), 0)),
                  pl.BlockSpec((1, D_MODEL, 2 * D_MODEL), lambda c, ce, nu: (ce[c], 0, 0)),
                  pl.BlockSpec((1, 1, 2 * D_MODEL), lambda c, ce, nu: (ce[c], 0, 0)),
                  pl.BlockSpec((1, D_MODEL, D_MODEL), lambda c, ce, nu: (ce[c], 0, 0)),
                  pl.BlockSpec((1, 1, D_MODEL), lambda c, ce, nu: (ce[c], 0, 0))],
        out_specs=pl.BlockSpec((rows, LANES), lambda c, ce, nu: (c, 0)))
    return pl.pallas_call(
        _expert_kernel,
        grid_spec=grid_spec,
        out_shape=jax.ShapeDtypeStruct(buf.shape, _f32),
        compiler_params=_cparams(("arbitrary",)),
        name="moe_experts",
    )(chunk_expert, n_used, buf, wgu, bgu, wd, bd)


def _combine_kernel(dest_ref, dest_next_ref, gw_ref, h_ref, eo_ref, g_ref, b_ref, o_ref, gbuf, ysl, sems, *, tokens):
    i = pl.program_id(0)
    slot = i % 2
    count = tokens * TOP_K

    def gather(idx_ref, into):
        def issue(t, c):
            for k in range(TOP_K):
                j = t * TOP_K + k
                _row_copy(eo_ref, idx_ref[j], gbuf.at[into], j, sems.at[into]).start(priority=k % 2)
            return c
        lax.fori_loop(0, tokens, issue, 0)

    @pl.when(i == 0)
    def _():
        gather(dest_ref, 0)

    @pl.when(i + 1 < pl.num_programs(0))
    def _():
        gather(dest_next_ref, 1 - slot)

    pltpu.make_async_copy(eo_ref.at[pl.ds(0, count * SLAB)], gbuf.at[slot], sems.at[slot]).wait()
    rows = gbuf.at[slot]

    unroll = 4

    def token_group(tg, c):
        for u in range(unroll):
            t = tg * unroll + u
            acc = DEEPNORM_ALPHA * h_ref[pl.ds(pl.multiple_of(t * SLAB, SLAB), SLAB), :]
            for k in range(TOP_K):
                j = t * TOP_K + k
                acc = acc + gw_ref[j] * rows[pl.ds(pl.multiple_of(j * SLAB, SLAB), SLAB), :]
            ysl[pl.ds(pl.multiple_of(t * SLAB, SLAB), SLAB), :] = acc
        return c

    lax.fori_loop(0, tokens // unroll, token_group, 0)
    o_ref[...] = _layer_norm(_slab_load(ysl, tokens), g_ref[...], b_ref[...])


def _combine_stage(dest, gate_w, h_slab, expert_out, g2, b2):
    n = h_slab.shape[0] // SLAB
    tokens = 256
    steps = n // tokens
    smem = lambda: pl.BlockSpec((tokens * TOP_K,), lambda i: (i,), memory_space=pltpu.SMEM)
    return pl.pallas_call(
        functools.partial(_combine_kernel, tokens=tokens),
        grid=(steps,),
        in_specs=[smem(),
                  pl.BlockSpec((tokens * TOP_K,), lambda i: (jnp.minimum(i + 1, steps - 1),), memory_space=pltpu.SMEM),
                  smem(),
                  pl.BlockSpec((tokens * SLAB, LANES), lambda i: (i, 0)),
                  pl.BlockSpec(memory_space=pl.ANY),
                  pl.BlockSpec((1, D_MODEL), lambda i: (0, 0)),
                  pl.BlockSpec((1, D_MODEL), lambda i: (0, 0))],
        out_specs=pl.BlockSpec((tokens, D_MODEL), lambda i: (i, 0)),
        out_shape=jax.ShapeDtypeStruct((n, D_MODEL), _f32),
        scratch_shapes=[pltpu.VMEM((2, tokens * TOP_K * SLAB, LANES), _f32),
                        pltpu.VMEM((tokens * SLAB, LANES), _f32),
                        pltpu.SemaphoreType.DMA((2,))],
        compiler_params=_cparams(("arbitrary",)),
        name="moe_combine_ln2",
    )(dest, dest, gate_w, h_slab, expert_out, g2, b2)


def _dispatch_plan(top_idx):
    m = top_idx.size
    e_flat = top_idx.reshape(m)
    onehot = (e_flat[:, None] == jnp.arange(N_EXPERTS, dtype=jnp.int32)[None, :]).astype(jnp.int32)
    csum = jnp.cumsum(onehot, axis=0)
    counts = csum[-1]
    padded = (counts + MOE_ROWS - 1) // MOE_ROWS * MOE_ROWS
    pends = jnp.cumsum(padded)
    pstarts = pends - padded
    dest = jnp.sum(onehot * (csum - 1 + pstarts[None, :]), axis=1).astype(jnp.int32)
    n_chunks = m // MOE_ROWS + N_EXPERTS
    chunk_start = jnp.arange(n_chunks, dtype=jnp.int32) * MOE_ROWS
    chunk_expert = jnp.minimum(jnp.sum((chunk_start[:, None] >= pends[None, :]).astype(jnp.int32), axis=1), N_EXPERTS - 1)
    n_used = (pends[-1] // MOE_ROWS).astype(jnp.int32).reshape(1)
    pad_start = (pstarts + counts).astype(jnp.int32)
    pad_len = (padded - counts).astype(jnp.int32)
    return dest, chunk_expert, n_used, pad_start, pad_len, n_chunks * MOE_ROWS


def _moe_half(h_slab, top_idx, gate_w, w_gate_up, b_gate_up, w_down, b_down, ln2_g, ln2_b):
    dest, chunk_expert, n_used, pad_start, pad_len, buf_rows = _dispatch_plan(top_idx)
    buf = _dispatch_stage(dest, pad_start, pad_len, h_slab, buf_rows)
    expert_out = _expert_stage(chunk_expert, n_used, buf, w_gate_up.astype(_bf16),
                               b_gate_up.reshape(N_EXPERTS, 1, -1), w_down.astype(_bf16),
                               b_down.reshape(N_EXPERTS, 1, -1))
    return _combine_stage(dest, gate_w.reshape(-1), h_slab, expert_out, ln2_g.reshape(1, -1), ln2_b.reshape(1, -1))


def kernel(x, w_in, cmp_pe_k, cmp_w1_k, cmp_w2_k, cmp_pe_v, cmp_w1_v, cmp_w2_v, w_proj_sb, w_proj_nsa, w_out,
           ln1_g, ln1_b, w_router, b_router, w_gate_up, b_gate_up, w_down, b_down, ln2_g, ln2_b):
    assert w_in.shape[0] == 1, "single-layer block"
    batch, seq, _ = x.shape
    assert seq % 512 == 0 and seq // SEL_BLOCK <= LANES and seq >= WINDOW + Q_BLOCK
    h_slab, top_idx, gate_w = _attention_half(
        x, w_in[0], cmp_pe_k[0], cmp_w1_k[0], cmp_w2_k[0], cmp_pe_v[0], cmp_w1_v[0], cmp_w2_v[0],
        w_proj_sb[0], w_proj_nsa[0], w_out[0], ln1_g[0], ln1_b[0], w_router[0], b_router[0])
    out = _moe_half(h_slab, top_idx, gate_w, w_gate_up[0], b_gate_up[0], w_down[0], b_down[0], ln2_g[0], ln2_b[0])
    return out.reshape(batch, seq, D_MODEL)
```

```python
import functools

import numpy as np
import jax
import jax.numpy as jnp
from jax import lax
from jax.experimental import pallas as pl
from jax.experimental.pallas import tpu as pltpu

D_MODEL = 1024
HEAD_DIM = 64
LANES = 128
Q_BLOCK = 128
CMP_BLOCK = 32
CMP_STRIDE = 16
SEL_BLOCK = 64
SEL_TOPK = 16
WINDOW = 512
ROPE_THETA = 10000.0
N_EXPERTS = 32
TOP_K = 4
SWIGLU_LIMIT = 7.0
SWIGLU_ALPHA = 1.702
LN_EPS = 1e-5
NEG_INF = -1e30
TAKEN = -3e38
DEEPNORM_ALPHA = 2.0 ** 0.25
QK_SCALE = HEAD_DIM ** -0.5
LOG2E = 1.4426950408889634

CB_SBQ, CB_SBK, CB_SBV, CB_NQ = 0, 4, 8, 12
CB_KC, CB_VC, CB_KS, CB_VS, CB_KW, CB_VW = 16, 17, 18, 19, 20, 21
CB_NG = 22
CB_MG = 24
PROJ_W = 40 * LANES

SB_TAIL_CUTOFF = -110.0

SEL_Q = 512
SEL_TK = 512
MOE_ROWS = 512
SLAB = D_MODEL // LANES
VMEM_LIMIT = 56 * 1024 * 1024

_bf16 = jnp.bfloat16
_f32 = jnp.float32


def _cparams(sem):
    return pltpu.CompilerParams(dimension_semantics=sem, vmem_limit_bytes=VMEM_LIMIT)


def _dot_t(a, b):
    return lax.dot_general(a, b, (((1,), (1,)), ((), ())), preferred_element_type=_f32)


def _dot(a, b):
    return jnp.dot(a, b, preferred_element_type=_f32)


def _lane_iota(shape):
    return lax.broadcasted_iota(jnp.int32, shape, len(shape) - 1)


def _half0(shape=(1, LANES)):
    return _lane_iota(shape) < HEAD_DIM


def _in_proj_kernel(x_ref, w_ref, o_ref):
    o_ref[...] = _dot(x_ref[...].astype(_bf16), w_ref[...]).astype(o_ref.dtype)


def _in_proj(x2, w):
    n = x2.shape[0]
    tm, tn = 512, 1280
    return pl.pallas_call(
        _in_proj_kernel,
        grid=(PROJ_W // tn, n // tm),
        in_specs=[pl.BlockSpec((tm, D_MODEL), lambda j, i: (i, 0)),
                  pl.BlockSpec((D_MODEL, tn), lambda j, i: (0, j))],
        out_specs=pl.BlockSpec((tm, tn), lambda j, i: (i, j)),
        out_shape=jax.ShapeDtypeStruct((n, PROJ_W), _bf16),
        compiler_params=_cparams(("arbitrary", "arbitrary")),
        name="in_proj",
    )(x2, w)


def _rope(x, cos, sin_signed):
    first = (_lane_iota((1, LANES)) % HEAD_DIM) < (HEAD_DIM // 2)
    swapped = jnp.where(first, pltpu.roll(x, LANES - HEAD_DIM // 2, 1), pltpu.roll(x, HEAD_DIM // 2, 1))
    return x * cos + swapped * sin_signed


def _dup(x, g):
    other = pltpu.roll(x, HEAD_DIM, 1)
    h0 = _half0()
    return jnp.where(h0, x, other) if g == 0 else jnp.where(h0, other, x)


def _rope_kernel(nq_ref, kc_ref, vc_ref, ks_ref, vs_ref, kw_ref, vw_ref, cos_ref, sin_ref,
                 nq_o, kc_o, vc_o, ka_o, vs_o, kw_o, vw_o, *, blocks_per_seq):
    ts = cos_ref.shape[0]
    cos = cos_ref[...]
    sin = sin_ref[...]
    for c in range(4):
        sl = slice(c * LANES, (c + 1) * LANES)
        nq_o[:, sl] = (_rope(nq_ref[:, sl].astype(_f32), cos, sin) * (QK_SCALE * LOG2E)).astype(_bf16)
    kc_o[...] = _rope(kc_ref[...].astype(_f32), cos, sin).astype(_bf16)
    vc_o[...] = vc_ref[...]
    ks = _rope(ks_ref[...].astype(_f32), cos, sin)
    kw = _rope(kw_ref[...].astype(_f32), cos, sin)
    vs = vs_ref[...].astype(_f32)
    vw = vw_ref[...].astype(_f32)
    pos = (pl.program_id(0) % blocks_per_seq) * ts + lax.broadcasted_iota(jnp.int32, (ts, LANES), 0)
    lane = _lane_iota((ts, LANES))
    onehot = jnp.where((pos // SEL_BLOCK) % HEAD_DIM + HEAD_DIM == lane, 1.0, 0.0)
    for g in range(2):
        ka_o[g] = jnp.where(_half0(), _dup(ks, g), onehot).astype(_bf16)
        vsa = jnp.where(_half0(), _dup(vs, g), 1.0)
        for c in range(ts // SEL_TK):
            vs_o[g, c] = vsa[c * SEL_TK:(c + 1) * SEL_TK, :].T.astype(_bf16)
        kw_o[g] = _dup(kw, g).astype(_bf16)
        vwd = jnp.where(_half0(), _dup(vw, g), 1.0)
        for c in range(ts // LANES):
            vw_o[g, c] = vwd[c * LANES:(c + 1) * LANES, :].T.astype(_bf16)


def _rope_stage(proj, cos, sin_signed, seq):
    n = proj.shape[0]
    ts = 512
    bps = seq // ts
    col = lambda cb: pl.BlockSpec((ts, LANES), lambda i, cb=cb: (i, cb))
    tab = pl.BlockSpec((ts, LANES), lambda i: (i % bps, 0))
    grp = lambda w: pl.BlockSpec((2, ts, w), lambda i: (0, i, 0))
    return pl.pallas_call(
        functools.partial(_rope_kernel, blocks_per_seq=bps),
        grid=(n // ts,),
        in_specs=[pl.BlockSpec((ts, 4 * LANES), lambda i: (i, CB_NQ // 4)),
                  col(CB_KC), col(CB_VC), col(CB_KS), col(CB_VS), col(CB_KW), col(CB_VW), tab, tab],
        out_specs=[pl.BlockSpec((ts, 4 * LANES), lambda i: (i, 0)),
                   pl.BlockSpec((ts, LANES), lambda i: (i, 0)),
                   pl.BlockSpec((ts, LANES), lambda i: (i, 0)),
                   grp(LANES),
                   pl.BlockSpec((2, ts // SEL_TK, LANES, SEL_TK), lambda i: (0, i, 0, 0)),
                   grp(LANES),
                   pl.BlockSpec((2, ts // LANES, LANES, LANES), lambda i: (0, i, 0, 0))],
        out_shape=[jax.ShapeDtypeStruct((n, 4 * LANES), _bf16),
                   jax.ShapeDtypeStruct((n, LANES), _bf16),
                   jax.ShapeDtypeStruct((n, LANES), _bf16),
                   jax.ShapeDtypeStruct((2, n, LANES), _bf16),
                   jax.ShapeDtypeStruct((2, n // SEL_TK, LANES, SEL_TK), _bf16),
                   jax.ShapeDtypeStruct((2, n, LANES), _bf16),
                   jax.ShapeDtypeStruct((2, n // LANES, LANES, LANES), _bf16)],
        compiler_params=_cparams(("arbitrary",)),
        name="rope_layout",
    )(proj, proj, proj, proj, proj, proj, proj, cos, sin_signed)


def _gelu_tanh(x):
    return 0.5 * x * (1.0 + jnp.tanh(0.7978845608028654 * (x + 0.044715 * (x * x * x))))


def _compress_one(x_ref, pe_t, pe_b, w_t, w_b, w2, out_ref, transposed):
    x = x_ref[0].astype(_f32)
    a = _dot((x + pe_t[...]).astype(_bf16), w_t[...])
    b = _dot((x + pe_b[...]).astype(_bf16), w_b[...])
    nc = a.shape[0]
    pre = a + pltpu.roll(b, nc - 1, 0)
    y = _dot(_gelu_tanh(pre).astype(_bf16), w2[...])
    for g in range(2):
        d = _dup(y, g)
        out_ref[0, g] = (d.T if transposed else d).astype(_bf16)


def _compress_kernel(k_ref, v_ref, kpt, kpb, kwt, kwb, kw2, vpt, vpb, vwt, vwb, vw2, ko_ref, vo_ref):
    _compress_one(k_ref, kpt, kpb, kwt, kwb, kw2, ko_ref, False)
    _compress_one(v_ref, vpt, vpb, vwt, vwb, vw2, vo_ref, True)


def _compress_weights(pe, w1, w2):
    half = CMP_BLOCK // 2
    eye = jnp.eye(2, dtype=_f32)
    outs = []
    for part in range(2):
        w = w1[part * half * HEAD_DIM:(part + 1) * half * HEAD_DIM].reshape(half, HEAD_DIM, HEAD_DIM)
        wbd = (w[:, None, :, None, :] * eye[None, :, None, :, None]).reshape(half * 2 * HEAD_DIM, 2 * HEAD_DIM)
        p = jnp.broadcast_to(pe[part * half:(part + 1) * half, None, :], (half, 2, HEAD_DIM)).reshape(1, -1)
        outs.append((p.astype(_f32), wbd.astype(_bf16)))
    w2bd = (w2[None, :, None, :] * eye[:, None, :, None]).reshape(2 * HEAD_DIM, 2 * HEAD_DIM).astype(_bf16)
    (pt, wt), (pb, wb) = outs
    return pt, pb, wt, wb, w2bd


def _compress_stage(kc_r, vc_r, kparams, vparams, batch, seq):
    nc = seq // CMP_STRIDE
    width = CMP_STRIDE * LANES
    xs = pl.BlockSpec((1, nc, width), lambda b: (b, 0, 0))
    full = lambda a: pl.BlockSpec(a.shape, lambda b: (0,) * a.ndim)
    out = pl.BlockSpec((1, 2, nc, LANES), lambda b: (b, 0, 0, 0))
    weights = list(kparams) + list(vparams)
    return pl.pallas_call(
        _compress_kernel,
        grid=(batch,),
        in_specs=[xs, xs] + [full(a) for a in weights],
        out_specs=[out, pl.BlockSpec((1, 2, LANES, nc), lambda b: (b, 0, 0, 0))],
        out_shape=[jax.ShapeDtypeStruct((batch, 2, nc, LANES), _bf16),
                   jax.ShapeDtypeStruct((batch, 2, LANES, nc), _bf16)],
        compiler_params=_cparams(("arbitrary",)),
        name="compress",
    )(kc_r.reshape(batch, nc, width), vc_r.reshape(batch, nc, width), *weights)


def _head_q(q_ref, r):
    q2 = q_ref[:, (r // 2) * LANES:(r // 2 + 1) * LANES]
    keep = _half0() if r % 2 == 0 else jnp.logical_not(_half0())
    return jnp.where(keep, q2, jnp.zeros_like(q2))


def _softmax_over_rows(s):
    m = jnp.max(s, axis=0, keepdims=True)
    e = jnp.exp2(s - m)
    l = jnp.sum(e, axis=0, keepdims=True)
    return e * jnp.where(m > 0.5 * NEG_INF, 1.0 / l, 0.0)


def _pair(even, odd):
    return jnp.where(_half0(), even, odd)


def _nsa_cw_kernel(q_ref, kc_ref, vct_ref, kw_ref, vwt_ref, ng_ref, stt_ref, yp_ref, mb_ref):
    t0 = pl.program_id(2) * Q_BLOCK
    qpos = t0 + _lane_iota((1, Q_BLOCK))
    gates = jax.nn.sigmoid(ng_ref[...].astype(_f32).T[0:16, :])
    kc = kc_ref[0, 0]
    vct = vct_ref[0, 0]
    nc = kc.shape[0]
    cend = lax.broadcasted_iota(jnp.int32, (nc, 1), 0) * CMP_STRIDE + (CMP_BLOCK - 1)
    cmask = cend <= qpos
    start = pl.multiple_of(jnp.maximum(t0 - WINDOW, 0), Q_BLOCK)
    wlen = WINDOW + Q_BLOCK
    kwin = kw_ref[0, 0, pl.ds(start, wlen), :]
    kpos = start + lax.broadcasted_iota(jnp.int32, (wlen, 1), 0)
    wmask = (kpos <= qpos) & (qpos - kpos < WINDOW)
    blk0 = start // Q_BLOCK

    qs = [_head_q(q_ref, r) for r in range(4)]
    s_cmp = [_dot_t(kc, q) for q in qs]
    s_win = [_dot_t(kwin, q) for q in qs]
    p_cmp = [_softmax_over_rows(jnp.where(cmask, s, NEG_INF)) for s in s_cmp]
    imp = (p_cmp[0] + p_cmp[1]) + (p_cmp[2] + p_cmp[3])
    e_win = []
    for s in s_win:
        s = jnp.where(wmask, s, NEG_INF)
        e_win.append(jnp.exp2(s - jnp.max(s, axis=0, keepdims=True)).astype(_bf16))
    vwt = jnp.concatenate([vwt_ref[0, 0, blk0 + c] for c in range(wlen // Q_BLOCK)], axis=1)
    yts = []
    for r in range(4):
        o_cmp = _dot(vct, p_cmp[r].astype(_bf16))[0:HEAD_DIM]
        win = _dot(vwt, e_win[r])
        o_win = win[0:HEAD_DIM] * (1.0 / win[HEAD_DIM:2 * HEAD_DIM])
        yts.append(gates[3 * r:3 * r + 1] * o_cmp + gates[3 * r + 2:3 * r + 3] * o_win)
    yp_ref[:, 0:LANES] = jnp.concatenate(yts[0:2], axis=0).T
    yp_ref[:, LANES:2 * LANES] = jnp.concatenate(yts[2:4], axis=0).T

    imp_hi = imp.astype(_bf16)
    rest = imp - imp_hi.astype(_f32)
    imp_mid = rest.astype(_bf16)
    imp_lo = (rest - imp_mid.astype(_f32)).astype(_bf16)
    st = stt_ref[...]
    p_slc = _dot(st, imp_hi) + (_dot(st, imp_mid) + _dot(st, imp_lo))
    selj = lax.broadcasted_iota(jnp.int32, (LANES, 1), 0)
    blk_t = qpos // SEL_BLOCK
    forced = (selj == 0) | (selj == blk_t) | (selj == blk_t - 1)
    score = jnp.where(forced, TAKEN, jnp.where(selj <= blk_t, p_slc, NEG_INF))
    seljf = selj.astype(_f32)
    picked = forced
    for _ in range(SEL_TOPK - 3):
        m = jnp.max(score, axis=0, keepdims=True)
        first = jnp.min(jnp.where(score == m, seljf, float(LANES)), axis=0, keepdims=True)
        hit = seljf == first
        picked = picked | hit
        score = jnp.where(hit, TAKEN, score)
    mb_ref[0] = jnp.where(picked, 0.0, NEG_INF).T.astype(_bf16)


def _nsa_cw_stage(nq_r, kc_d, vc_t, kw_d, vw_t, proj, stencil_t, batch, seq):
    n = nq_r.shape[0]
    nblk = seq // Q_BLOCK
    nc = seq // CMP_STRIDE
    qrow = lambda b, g, i: b * nblk + i
    return pl.pallas_call(
        _nsa_cw_kernel,
        grid=(batch, 2, nblk),
        in_specs=[pl.BlockSpec((Q_BLOCK, 2 * LANES), lambda b, g, i: (qrow(b, g, i), g)),
                  pl.BlockSpec((1, 1, nc, LANES), lambda b, g, i: (b, g, 0, 0)),
                  pl.BlockSpec((1, 1, LANES, nc), lambda b, g, i: (b, g, 0, 0)),
                  pl.BlockSpec((1, 1, seq, LANES), lambda b, g, i: (g, b, 0, 0)),
                  pl.BlockSpec((1, 1, nblk, LANES, LANES), lambda b, g, i: (g, b, 0, 0, 0)),
                  pl.BlockSpec((Q_BLOCK, LANES), lambda b, g, i: (qrow(b, g, i), CB_NG + g)),
                  pl.BlockSpec((LANES, nc), lambda b, g, i: (0, 0))],
        out_specs=[pl.BlockSpec((Q_BLOCK, 2 * LANES), lambda b, g, i: (qrow(b, g, i), g)),
                   pl.BlockSpec((1, Q_BLOCK, LANES), lambda b, g, i: (g, qrow(b, g, i), 0))],
        out_shape=[jax.ShapeDtypeStruct((n, 4 * LANES), _f32),
                   jax.ShapeDtypeStruct((2, n, LANES), _bf16)],
        compiler_params=_cparams(("arbitrary", "arbitrary", "arbitrary")),
        name="nsa_cmp_win_select",
    )(nq_r, kc_d, vc_t, kw_d.reshape(2, batch, seq, LANES), vw_t.reshape(2, batch, nblk, LANES, LANES), proj, stencil_t)


def _nsa_sel_kernel(q_ref, mb_ref, ka_ref, vat_ref, ng_ref, yp_ref, o_ref, qs_ref, m_ref, acc_ref, s_ref, p_ref, alpha_ref):
    tk = SEL_TK
    t0 = pl.program_id(2) * SEL_Q
    h0 = _half0()
    mb = mb_ref[0].astype(_f32)
    bias = [pltpu.roll(mb, HEAD_DIM, 1), mb]
    for r in range(4):
        q = q_ref[:, (r // 2) * LANES:(r // 2 + 1) * LANES].astype(_f32)
        if r % 2 == 1:
            q = pltpu.roll(q, HEAD_DIM, 1)
        for v in range(2):
            qs_ref[v, r * SEL_Q:(r + 1) * SEL_Q, :] = jnp.where(h0, q, bias[v]).astype(_bf16)
    m_ref[...] = jnp.full(m_ref.shape, NEG_INF, _f32)
    acc_ref[...] = jnp.zeros(acc_ref.shape, _f32)
    p_ref[...] = jnp.zeros(p_ref.shape, _bf16)
    alpha_ref[...] = jnp.ones(alpha_ref.shape, _f32)
    qpos = t0 + _lane_iota((1, 4 * SEL_Q)) % SEL_Q

    def scores(kt):
        version = (kt * (tk // SEL_BLOCK)) // HEAD_DIM
        return _dot_t(ka_ref[0, 0, pl.ds(pl.multiple_of(kt * tk, tk), tk), :], qs_ref[version])

    def softmax_step(s):
        m_old = m_ref[...]
        m_new = jnp.maximum(m_old, jnp.max(s, axis=0, keepdims=True))
        m_ref[...] = m_new
        return jnp.exp2(s - m_new).astype(_bf16), jnp.exp2(m_old - m_new)

    def accumulate(kt, alpha, p):
        acc_ref[...] = alpha * acc_ref[...] + _dot(vat_ref[0, 0, kt], p)

    def trip(kt, carry):
        accumulate(jnp.maximum(kt - 1, 0), alpha_ref[...], p_ref[...])
        s = s_ref[...]
        s_ref[...] = scores(kt + 1)
        p, alpha = softmax_step(s)
        p_ref[...] = p
        alpha_ref[...] = alpha
        return carry

    n_full = t0 // tk
    s_ref[...] = scores(0)

    def two_trips(kp, carry):
        trip(2 * kp, carry)
        return trip(2 * kp + 1, carry)

    lax.fori_loop(0, n_full // 2, two_trips, 0)

    @pl.when(n_full % 2 == 1)
    def _():
        trip(n_full - 1, 0)

    accumulate(jnp.maximum(n_full - 1, 0), alpha_ref[...], p_ref[...])
    kpos = n_full * tk + lax.broadcasted_iota(jnp.int32, (tk, 1), 0)
    p, alpha = softmax_step(jnp.where(kpos <= qpos, s_ref[...], NEG_INF))
    accumulate(n_full, alpha, p)
    acc = acc_ref[...]
    o = acc[0:HEAD_DIM] * (1.0 / acc[HEAD_DIM:2 * HEAD_DIM])
    gates = jax.nn.sigmoid(ng_ref[...].astype(_f32).T[0:16, :])
    ys = [gates[3 * r + 1:3 * r + 2] * o[:, r * SEL_Q:(r + 1) * SEL_Q] for r in range(4)]
    o_ref[:, 0:LANES] = (yp_ref[:, 0:LANES] + jnp.concatenate(ys[0:2], axis=0).T).astype(o_ref.dtype)
    o_ref[:, LANES:2 * LANES] = (yp_ref[:, LANES:2 * LANES] + jnp.concatenate(ys[2:4], axis=0).T).astype(o_ref.dtype)


def _nsa_sel_stage(nq_r, mbias, k_aug, vs_t, proj, ypart, batch, seq):
    n = nq_r.shape[0]
    nblk = seq // SEL_Q
    qrow = lambda b, g, i: b * nblk + i
    return pl.pallas_call(
        _nsa_sel_kernel,
        grid=(batch, 2, nblk),
        in_specs=[pl.BlockSpec((SEL_Q, 2 * LANES), lambda b, g, i: (qrow(b, g, i), g)),
                  pl.BlockSpec((1, SEL_Q, LANES), lambda b, g, i: (g, qrow(b, g, i), 0)),
                  pl.BlockSpec((1, 1, seq, LANES), lambda b, g, i: (g, b, 0, 0)),
                  pl.BlockSpec((1, 1, seq // SEL_TK, LANES, SEL_TK), lambda b, g, i: (g, b, 0, 0, 0)),
                  pl.BlockSpec((SEL_Q, LANES), lambda b, g, i: (qrow(b, g, i), CB_NG + g)),
                  pl.BlockSpec((SEL_Q, 2 * LANES), lambda b, g, i: (qrow(b, g, i), g))],
        out_specs=pl.BlockSpec((SEL_Q, 2 * LANES), lambda b, g, i: (qrow(b, g, i), g)),
        out_shape=jax.ShapeDtypeStruct((n, 4 * LANES), _bf16),
        scratch_shapes=[pltpu.VMEM((2, 4 * SEL_Q, LANES), _bf16),
                        pltpu.VMEM((1, 4 * SEL_Q), _f32),
                        pltpu.VMEM((LANES, 4 * SEL_Q), _f32),
                        pltpu.VMEM((SEL_TK, 4 * SEL_Q), _f32),
                        pltpu.VMEM((SEL_TK, 4 * SEL_Q), _bf16),
                        pltpu.VMEM((1, 4 * SEL_Q), _f32)],
        compiler_params=_cparams(("arbitrary", "arbitrary", "arbitrary")),
        name="nsa_selected",
    )(nq_r, mbias, k_aug.reshape(2, batch, seq, LANES), vs_t.reshape(2, batch, seq // SEL_TK, LANES, SEL_TK), proj, ypart)


def _sb_kernel(q_ref, k_ref, v_ref, o_ref, qs_ref, tail_ref, acc_ref):
    i = pl.program_id(1)
    h0 = _half0()
    heads = 2 * (q_ref.shape[1] // LANES)
    for h in range(heads):
        q = q_ref[:, (h // 2) * LANES:(h // 2 + 1) * LANES]
        keep = h0 if h % 2 == 0 else jnp.logical_not(h0)
        qs_ref[h] = jnp.where(keep, q, jnp.zeros_like(q)) * QK_SCALE
    tail_ref[...] = jnp.zeros(tail_ref.shape, _f32)
    acc_ref[...] = jnp.zeros(acc_ref.shape, _f32)
    rloc = lax.broadcasted_iota(jnp.int32, (Q_BLOCK, Q_BLOCK), 0)
    cloc = lax.broadcasted_iota(jnp.int32, (Q_BLOCK, Q_BLOCK), 1)
    later = jnp.where(rloc > cloc, 1.0, 0.0).astype(_bf16)

    def key_block(j, diagonal):
        k0 = pl.multiple_of(j * Q_BLOCK, Q_BLOCK)
        past = cloc < rloc
        cols = [slice((h // 2) * LANES, (h // 2 + 1) * LANES) for h in range(heads)]
        zs = [_dot_t(qs_ref[h], k_ref[0, pl.ds(k0, Q_BLOCK), cols[h]]) for h in range(heads)]
        log_beta, log_keep = [], []
        for z in zs:
            sp = jnp.maximum(z, 0.0) + jnp.log(1.0 + jnp.exp(-jnp.abs(z)))
            log_beta.append(z - sp)
            log_keep.append(jnp.where(past, -sp, 0.0) if diagonal else -sp)
        inner = []
        for lk in log_keep:
            hi = lk.astype(_bf16)
            lo = (lk - hi.astype(_f32)).astype(_bf16)
            inner.append(_dot(hi, later) + _dot(lo, later))
        probs = []
        for h in range(heads):
            a = jnp.exp(log_beta[h] + inner[h] + tail_ref[h])
            probs.append((jnp.where(past, a, 0.0) if diagonal else a).astype(_bf16))
        worst = jnp.full((Q_BLOCK, 1), -jnp.inf, _f32)
        for h in range(heads):
            acc_ref[h] = acc_ref[h] + _dot(probs[h], v_ref[0, pl.ds(k0, Q_BLOCK), cols[h]])
            tail = tail_ref[h] + jnp.sum(log_keep[h], axis=-1, keepdims=True)
            tail_ref[h] = tail
            worst = jnp.maximum(worst, tail)
        return jnp.max(worst)

    def cond(c):
        j, worst_tail = c
        return (j >= 0) & (worst_tail > SB_TAIL_CUTOFF)

    def body(c):
        j, _ = c
        return j - 1, key_block(j, False)

    lax.while_loop(cond, body, (i - 1, key_block(i, True)))
    for p in range(heads // 2):
        o_ref[:, p * LANES:(p + 1) * LANES] = jnp.where(h0, acc_ref[2 * p], acc_ref[2 * p + 1]).astype(o_ref.dtype)


def _sb_stage(proj, batch, seq):
    n = proj.shape[0]
    nblk = seq // Q_BLOCK
    width = 4 * LANES
    proj3 = proj.reshape(batch, seq, PROJ_W)
    return pl.pallas_call(
        _sb_kernel,
        grid=(batch, nblk),
        in_specs=[pl.BlockSpec((Q_BLOCK, width), lambda b, i: (b * nblk + i, CB_SBQ // 4)),
                  pl.BlockSpec((1, seq, width), lambda b, i: (b, 0, CB_SBK // 4)),
                  pl.BlockSpec((1, seq, width), lambda b, i: (b, 0, CB_SBV // 4))],
        out_specs=pl.BlockSpec((Q_BLOCK, width), lambda b, i: (b * nblk + i, 0)),
        out_shape=jax.ShapeDtypeStruct((n, width), _bf16),
        scratch_shapes=[pltpu.VMEM((8, Q_BLOCK, LANES), _bf16),
                        pltpu.VMEM((8, Q_BLOCK, 1), _f32),
                        pltpu.VMEM((8, Q_BLOCK, LANES), _f32)],
        compiler_params=_cparams(("arbitrary", "arbitrary")),
        name="stick_breaking",
    )(proj, proj3, proj3)


def _layer_norm(x, g, b):
    mu = jnp.mean(x, axis=-1, keepdims=True)
    xc = x - mu
    var = jnp.mean(xc * xc, axis=-1, keepdims=True)
    return xc * lax.rsqrt(var + LN_EPS) * g + b


def _merge_kernel(x_ref, ysb_ref, yns_ref, mg0_ref, mg1_ref, wsb_ref, wns_ref, wo_ref, g_ref, b_ref,
                  wr_ref, br_ref, h_ref, idx_ref, gw_ref):
    m0 = jax.nn.sigmoid(mg0_ref[...].astype(_f32))
    m1 = jax.nn.sigmoid(mg1_ref[...].astype(_f32))
    merged = m0 * _dot(ysb_ref[...], wsb_ref[...]) + m1 * _dot(yns_ref[...], wns_ref[...])
    pre = DEEPNORM_ALPHA * x_ref[...] + _dot(merged.astype(_bf16), wo_ref[...])
    h = _layer_norm(pre, g_ref[...], b_ref[...])
    tm = h.shape[0]
    for s in range(SLAB):
        h_ref[pl.ds(s, tm, stride=SLAB), :] = h[:, s * LANES:(s + 1) * LANES]
    h_hi = h.astype(_bf16)
    h_lo = (h - h_hi.astype(_f32)).astype(_bf16)
    w = wr_ref[...]
    w_hi = w.astype(_bf16)
    w_lo = (w - w_hi.astype(_f32)).astype(_bf16)
    logits = (_dot(h_hi, w_hi) + (_dot(h_hi, w_lo) + _dot(h_lo, w_hi))) + br_ref[...]
    lane = _lane_iota((1, LANES))
    lanef = lane.astype(_f32)
    lg = jnp.where(lane < N_EXPERTS, logits, TAKEN)
    vals, idxs = [], []
    for _ in range(TOP_K):
        m = jnp.max(lg, axis=-1, keepdims=True)
        first = jnp.min(jnp.where(lg == m, lanef, float(LANES)), axis=-1, keepdims=True)
        vals.append(m)
        idxs.append(first)
        lg = jnp.where(lanef == first, TAKEN, lg)
    es = [jnp.exp(v - vals[0]) for v in vals]
    inv = 1.0 / (es[0] + es[1] + es[2] + es[3])
    idx_t = jnp.zeros(lg.shape, _f32)
    gw_t = jnp.zeros(lg.shape, _f32)
    for k in range(TOP_K):
        idx_t = jnp.where(lane == k, idxs[k], idx_t)
        gw_t = jnp.where(lane == k, es[k] * inv, gw_t)
    idx_ref[...] = idx_t[:, :TOP_K].astype(jnp.int32)
    gw_ref[...] = gw_t[:, :TOP_K]


def _merge_stage(x2, y_sb, y_nsa, proj, wsb, wns, wo, g1, b1, wr, br):
    n = x2.shape[0]
    tm = 512
    row = lambda w: pl.BlockSpec((tm, w), lambda i: (i, 0))
    full = lambda a: pl.BlockSpec(a.shape, lambda i: (0,) * a.ndim)
    return pl.pallas_call(
        _merge_kernel,
        grid=(n // tm,),
        in_specs=[row(D_MODEL), row(4 * LANES), row(4 * LANES),
                  pl.BlockSpec((tm, D_MODEL), lambda i: (i, CB_MG // 8)),
                  pl.BlockSpec((tm, D_MODEL), lambda i: (i, CB_MG // 8 + 1)),
                  full(wsb), full(wns), full(wo), full(g1), full(b1), full(wr), full(br)],
        out_specs=[pl.BlockSpec((tm * SLAB, LANES), lambda i: (i, 0)), row(TOP_K), row(TOP_K)],
        out_shape=[jax.ShapeDtypeStruct((n * SLAB, LANES), _f32),
                   jax.ShapeDtypeStruct((n, TOP_K), jnp.int32),
                   jax.ShapeDtypeStruct((n, TOP_K), _f32)],
        compiler_params=_cparams(("arbitrary",)),
        name="merge_ln1_router",
    )(x2, y_sb, y_nsa, proj, proj, wsb, wns, wo, g1, b1, wr, br)


def _prep_w_in(w):
    main = w[:, :CB_NG * LANES]
    ng = w[:, CB_NG * LANES:CB_NG * LANES + 24]
    mg = w[:, CB_NG * LANES + 24:]
    pad = jnp.zeros((w.shape[0], LANES - 12), w.dtype)
    return jnp.concatenate([main, ng[:, :12], pad, ng[:, 12:], pad, mg], axis=1).astype(_bf16)


def _rope_tables(seq):
    half = HEAD_DIM // 2
    inv_freq = ROPE_THETA ** (-jnp.arange(half, dtype=_f32) / half)
    ang = jnp.arange(seq, dtype=_f32)[:, None] * inv_freq[None, :]
    cos = jnp.cos(ang)
    sin = jnp.sin(ang)
    cos128 = jnp.concatenate([cos, cos, cos, cos], axis=1)
    sin128 = jnp.concatenate([-sin, sin, -sin, sin], axis=1)
    return cos128, sin128


def _stencil(nc):
    n = np.arange(nc)[:, None]
    j = np.arange(LANES)[None, :]
    ratio = SEL_BLOCK // CMP_STRIDE
    ok = (n >= ratio * j - 1) & (n <= ratio * j + ratio - 1) & (n < nc - 1)
    return jnp.asarray(ok.astype(np.float32).T, dtype=_bf16)


def _attention_half(x, w_in, cmp_pe_k, cmp_w1_k, cmp_w2_k, cmp_pe_v, cmp_w1_v, cmp_w2_v,
                    w_proj_sb, w_proj_nsa, w_out, ln1_g, ln1_b, w_router, b_router):
    batch, seq, _ = x.shape
    n = batch * seq
    x2 = x.reshape(n, D_MODEL)
    proj = _in_proj(x2, _prep_w_in(w_in))
    cos, sin_signed = _rope_tables(seq)
    nq_r, kc_r, vc_r, k_aug, vs_t, kw_d, vw_t = _rope_stage(proj, cos, sin_signed, seq)
    kc_d, vc_t = _compress_stage(kc_r, vc_r, _compress_weights(cmp_pe_k, cmp_w1_k, cmp_w2_k),
                                 _compress_weights(cmp_pe_v, cmp_w1_v, cmp_w2_v), batch, seq)
    ypart, mbias = _nsa_cw_stage(nq_r, kc_d, vc_t, kw_d, vw_t, proj, _stencil(seq // CMP_STRIDE), batch, seq)
    y_nsa = _nsa_sel_stage(nq_r, mbias, k_aug, vs_t, proj, ypart, batch, seq)
    y_sb = _sb_stage(proj, batch, seq)
    wr = jnp.pad(w_router.astype(_f32), ((0, 0), (0, LANES - N_EXPERTS)))
    br = jnp.pad(b_router.astype(_f32), (0, LANES - N_EXPERTS)).reshape(1, LANES)
    return _merge_stage(x2, y_sb, y_nsa, proj, w_proj_sb.astype(_bf16), w_proj_nsa.astype(_bf16),
                        w_out.astype(_bf16), ln1_g.reshape(1, -1), ln1_b.reshape(1, -1), wr, br)


def _row_copy(src, src_row, dst, dst_row, sem):
    return pltpu.make_async_copy(src.at[pl.ds(src_row * SLAB, SLAB)], dst.at[pl.ds(dst_row * SLAB, SLAB)], sem)


def _dispatch_kernel(pad_start_ref, pad_len_ref, dest_ref, h_hbm, buf_ref, hbuf, zslab, in_sems, row_sems, zero_sem,
                     *, tokens):
    i = pl.program_id(0)
    last = pl.num_programs(0) - 1
    slot = i % 2
    block_rows = tokens * SLAB

    def block_copy(step, s):
        return pltpu.make_async_copy(h_hbm.at[pl.ds(step * block_rows, block_rows)], hbuf.at[s], in_sems.at[s])

    def wait_rows(s):
        for _ in range(TOP_K):
            pltpu.make_async_copy(hbuf.at[s], buf_ref.at[pl.ds(0, block_rows)], row_sems.at[s]).wait()

    def padding_rows(act):
        def per_expert(e, c):
            def per_row(r, c2):
                act(_row_copy(zslab, 0, buf_ref, pad_start_ref[e] + r, zero_sem))
                return c2
            lax.fori_loop(0, pad_len_ref[e], per_row, 0)
            return c
        lax.fori_loop(0, N_EXPERTS, per_expert, 0)

    @pl.when(i == 0)
    def _():
        block_copy(0, 0).start()
        zslab[...] = jnp.zeros(zslab.shape, _f32)
        padding_rows(lambda cp: cp.start())

    @pl.when(i > 0)
    def _():
        wait_rows(1 - slot)

    @pl.when(i < last)
    def _():
        block_copy(i + 1, 1 - slot).start()

    block_copy(i, slot).wait()
    src = hbuf.at[slot]

    def issue(t, c):
        for k in range(TOP_K):
            _row_copy(src, t, buf_ref, dest_ref[t * TOP_K + k], row_sems.at[slot]).start(priority=k % 2)
        return c

    lax.fori_loop(0, tokens, issue, 0)

    @pl.when(i == last)
    def _():
        wait_rows(slot)
        padding_rows(lambda cp: cp.wait())


def _dispatch_stage(dest, pad_start, pad_len, h_slab, buf_rows):
    n = h_slab.shape[0] // SLAB
    tokens = 256
    grid_spec = pltpu.PrefetchScalarGridSpec(
        num_scalar_prefetch=2,
        grid=(n // tokens,),
        in_specs=[pl.BlockSpec((tokens * TOP_K,), lambda i, ps, pn: (i,), memory_space=pltpu.SMEM),
                  pl.BlockSpec(memory_space=pl.ANY)],
        out_specs=pl.BlockSpec(memory_space=pl.ANY),
        scratch_shapes=[pltpu.VMEM((2, tokens * SLAB, LANES), _f32),
                        pltpu.VMEM((SLAB, LANES), _f32),
                        pltpu.SemaphoreType.DMA((2,)),
                        pltpu.SemaphoreType.DMA((2,)),
                        pltpu.SemaphoreType.DMA(())])
    return pl.pallas_call(
        functools.partial(_dispatch_kernel, tokens=tokens),
        grid_spec=grid_spec,
        out_shape=jax.ShapeDtypeStruct((buf_rows * SLAB, LANES), _f32),
        compiler_params=pltpu.CompilerParams(dimension_semantics=("arbitrary",), has_side_effects=True),
        name="moe_dispatch",
    )(pad_start, pad_len, dest, h_slab)


def _slab_load(ref, rows):
    return jnp.concatenate([ref[pl.ds(s, rows, stride=SLAB), :] for s in range(SLAB)], axis=1)


def _expert_kernel(ce_ref, used_ref, x_ref, wgu_ref, bgu_ref, wd_ref, bd_ref, o_ref, wgu_bf, wd_bf):
    c = pl.program_id(0)
    new_expert = (c == 0) | (ce_ref[c] != ce_ref[jnp.maximum(c - 1, 0)])

    @pl.when(new_expert & (c < used_ref[0]))
    def _():
        wgu_bf[...] = wgu_ref[0].astype(_bf16)
        wd_bf[...] = wd_ref[0].astype(_bf16)

    @pl.when(c < used_ref[0])
    def _():
        x = _slab_load(x_ref, MOE_ROWS).astype(_bf16)
        gu = _dot(x, wgu_bf[...]) + bgu_ref[0]
        gate = jnp.minimum(gu[:, :D_MODEL], SWIGLU_LIMIT)
        up = jnp.clip(gu[:, D_MODEL:], -SWIGLU_LIMIT, SWIGLU_LIMIT)
        h = gate * jax.nn.sigmoid(SWIGLU_ALPHA * gate) * (up + 1.0)
        y = _dot(h.astype(_bf16), wd_bf[...]) + bd_ref[0]
        for s in range(SLAB):
            o_ref[pl.ds(s, MOE_ROWS, stride=SLAB), :] = y[:, s * LANES:(s + 1) * LANES]

    @pl.when(c >= used_ref[0])
    def _():
        o_ref[...] = jnp.zeros(o_ref.shape, o_ref.dtype)


def _expert_stage(chunk_expert, n_used, buf, wgu, bgu, wd, bd):
    n_chunks = chunk_expert.shape[0]
    rows = MOE_ROWS * SLAB
    grid_spec = pltpu.PrefetchScalarGridSpec(
        num_scalar_prefetch=2,
        grid=(n_chunks,),
        in_specs=[pl.BlockSpec((rows, LANES), lambda c, ce, nu: (jnp.minimum(c, nu[0] - 1), 0)),
                  pl.BlockSpec((1, D_MODEL, 2 * D_MODEL), lambda c, ce, nu: (ce[c], 0, 0)),
                  pl.BlockSpec((1, 1, 2 * D_MODEL), lambda c, ce, nu: (ce[c], 0, 0)),
                  pl.BlockSpec((1, D_MODEL, D_MODEL), lambda c, ce, nu: (ce[c], 0, 0)),
                  pl.BlockSpec((1, 1, D_MODEL), lambda c, ce, nu: (ce[c], 0, 0))],
        out_specs=pl.BlockSpec((rows, LANES), lambda c, ce, nu: (c, 0)),
        scratch_shapes=[pltpu.VMEM((D_MODEL, 2 * D_MODEL), _bf16), pltpu.VMEM((D_MODEL, D_MODEL), _bf16)])
    return pl.pallas_call(
        _expert_kernel,
        grid_spec=grid_spec,
        out_shape=jax.ShapeDtypeStruct(buf.shape, _f32),
        compiler_params=_cparams(("arbitrary",)),
        name="moe_experts",
    )(chunk_expert, n_used, buf, wgu, bgu, wd, bd)


def _combine_kernel(dest_ref, dest_next_ref, gw_ref, h_ref, eo_ref, g_ref, b_ref, o_ref, gbuf, ysl, sems, *, tokens):
    i = pl.program_id(0)
    slot = i % 2
    count = tokens * TOP_K

    def gather(idx_ref, into):
        def issue(t, c):
            for k in range(TOP_K):
                j = t * TOP_K + k
                _row_copy(eo_ref, idx_ref[j], gbuf.at[into], j, sems.at[into]).start(priority=k % 2)
            return c
        lax.fori_loop(0, tokens, issue, 0)

    @pl.when(i == 0)
    def _():
        gather(dest_ref, 0)

    @pl.when(i + 1 < pl.num_programs(0))
    def _():
        gather(dest_next_ref, 1 - slot)

    pltpu.make_async_copy(eo_ref.at[pl.ds(0, count * SLAB)], gbuf.at[slot], sems.at[slot]).wait()
    rows = gbuf.at[slot]

    unroll = 4

    def token_group(tg, c):
        for u in range(unroll):
            t = tg * unroll + u
            acc = DEEPNORM_ALPHA * h_ref[pl.ds(pl.multiple_of(t * SLAB, SLAB), SLAB), :]
            for k in range(TOP_K):
                j = t * TOP_K + k
                acc = acc + gw_ref[j] * rows[pl.ds(pl.multiple_of(j * SLAB, SLAB), SLAB), :]
            ysl[pl.ds(pl.multiple_of(t * SLAB, SLAB), SLAB), :] = acc
        return c

    lax.fori_loop(0, tokens // unroll, token_group, 0)
    o_ref[...] = _layer_norm(_slab_load(ysl, tokens), g_ref[...], b_ref[...])


def _combine_stage(dest, gate_w, h_slab, expert_out, g2, b2):
    n = h_slab.shape[0] // SLAB
    tokens = 256
    steps = n // tokens
    smem = lambda: pl.BlockSpec((tokens * TOP_K,), lambda i: (i,), memory_space=pltpu.SMEM)
    return pl.pallas_call(
        functools.partial(_combine_kernel, tokens=tokens),
        grid=(steps,),
        in_specs=[smem(),
                  pl.BlockSpec((tokens * TOP_K,), lambda i: (jnp.minimum(i + 1, steps - 1),), memory_space=pltpu.SMEM),
                  smem(),
                  pl.BlockSpec((tokens * SLAB, LANES), lambda i: (i, 0)),
                  pl.BlockSpec(memory_space=pl.ANY),
                  pl.BlockSpec((1, D_MODEL), lambda i: (0, 0)),
                  pl.BlockSpec((1, D_MODEL), lambda i: (0, 0))],
        out_specs=pl.BlockSpec((tokens, D_MODEL), lambda i: (i, 0)),
        out_shape=jax.ShapeDtypeStruct((n, D_MODEL), _f32),
        scratch_shapes=[pltpu.VMEM((2, tokens * TOP_K * SLAB, LANES), _f32),
                        pltpu.VMEM((tokens * SLAB, LANES), _f32),
                        pltpu.SemaphoreType.DMA((2,))],
        compiler_params=_cparams(("arbitrary",)),
        name="moe_combine_ln2",
    )(dest, dest, gate_w, h_slab, expert_out, g2, b2)


def _dispatch_plan(top_idx):
    m = top_idx.size
    e_flat = top_idx.reshape(m)
    onehot = (e_flat[:, None] == jnp.arange(N_EXPERTS, dtype=jnp.int32)[None, :]).astype(jnp.int32)
    csum = jnp.cumsum(onehot, axis=0)
    counts = csum[-1]
    padded = (counts + MOE_ROWS - 1) // MOE_ROWS * MOE_ROWS
    pends = jnp.cumsum(padded)
    pstarts = pends - padded
    dest = jnp.sum(onehot * (csum - 1 + pstarts[None, :]), axis=1).astype(jnp.int32)
    n_chunks = m // MOE_ROWS + N_EXPERTS
    chunk_start = jnp.arange(n_chunks, dtype=jnp.int32) * MOE_ROWS
    chunk_expert = jnp.minimum(jnp.sum((chunk_start[:, None] >= pends[None, :]).astype(jnp.int32), axis=1), N_EXPERTS - 1)
    n_used = (pends[-1] // MOE_ROWS).astype(jnp.int32).reshape(1)
    pad_start = (pstarts + counts).astype(jnp.int32)
    pad_len = (padded - counts).astype(jnp.int32)
    return dest, chunk_expert, n_used, pad_start, pad_len, n_chunks * MOE_ROWS


def _moe_half(h_slab, top_idx, gate_w, w_gate_up, b_gate_up, w_down, b_down, ln2_g, ln2_b):
    dest, chunk_expert, n_used, pad_start, pad_len, buf_rows = _dispatch_plan(top_idx)
    buf = _dispatch_stage(dest, pad_start, pad_len, h_slab, buf_rows)
    expert_out = _expert_stage(chunk_expert, n_used, buf, w_gate_up, b_gate_up.reshape(N_EXPERTS, 1, -1),
                               w_down, b_down.reshape(N_EXPERTS, 1, -1))
    return _combine_stage(dest, gate_w.reshape(-1), h_slab, expert_out, ln2_g.reshape(1, -1), ln2_b.reshape(1, -1))


def kernel(x, w_in, cmp_pe_k, cmp_w1_k, cmp_w2_k, cmp_pe_v, cmp_w1_v, cmp_w2_v, w_proj_sb, w_proj_nsa, w_out,
           ln1_g, ln1_b, w_router, b_router, w_gate_up, b_gate_up, w_down, b_down, ln2_g, ln2_b):
    assert w_in.shape[0] == 1, "single-layer block"
    batch, seq, _ = x.shape
    assert seq % 512 == 0 and seq // SEL_BLOCK <= LANES and seq >= WINDOW + Q_BLOCK
    h_slab, top_idx, gate_w = _attention_half(
        x, w_in[0], cmp_pe_k[0], cmp_w1_k[0], cmp_w2_k[0], cmp_pe_v[0], cmp_w1_v[0], cmp_w2_v[0],
        w_proj_sb[0], w_proj_nsa[0], w_out[0], ln1_g[0], ln1_b[0], w_router[0], b_router[0])
    out = _moe_half(h_slab, top_idx, gate_w, w_gate_up[0], b_gate_up[0], w_down[0], b_down[0], ln2_g[0], ln2_b[0])
    return out.reshape(batch, seq, D_MODEL)
```

```python
import functools

import numpy as np
import jax
import jax.numpy as jnp
from jax import lax
from jax.experimental import pallas as pl
from jax.experimental.pallas import tpu as pltpu

D_MODEL = 1024
HEAD_DIM = 64
LANES = 128
Q_BLOCK = 128
CMP_BLOCK = 32
CMP_STRIDE = 16
SEL_BLOCK = 64
SEL_TOPK = 16
WINDOW = 512
ROPE_THETA = 10000.0
N_EXPERTS = 32
TOP_K = 4
SWIGLU_LIMIT = 7.0
SWIGLU_ALPHA = 1.702
LN_EPS = 1e-5
NEG_INF = -1e30
TAKEN = -3e38
DEEPNORM_ALPHA = 2.0 ** 0.25
QK_SCALE = HEAD_DIM ** -0.5
LOG2E = 1.4426950408889634

CB_SBQ, CB_SBK, CB_SBV, CB_NQ = 0, 4, 8, 12
CB_KC, CB_VC, CB_KS, CB_VS, CB_KW, CB_VW = 16, 17, 18, 19, 20, 21
CB_NG = 22
CB_MG = 24
PROJ_W = 40 * LANES

SB_TAIL_CUTOFF = -110.0

SEL_Q = 512
SEL_TK = 512
MOE_ROWS = 512
SLAB = D_MODEL // LANES
ISSUE_UNROLL = 4
VMEM_LIMIT = 56 * 1024 * 1024

_bf16 = jnp.bfloat16
_f32 = jnp.float32


def _cparams(sem):
    return pltpu.CompilerParams(dimension_semantics=sem, vmem_limit_bytes=VMEM_LIMIT)


def _dot_t(a, b):
    return lax.dot_general(a, b, (((1,), (1,)), ((), ())), preferred_element_type=_f32)


def _dot(a, b):
    return jnp.dot(a, b, preferred_element_type=_f32)


def _lane_iota(shape):
    return lax.broadcasted_iota(jnp.int32, shape, len(shape) - 1)


def _half0(shape=(1, LANES)):
    return _lane_iota(shape) < HEAD_DIM


def _in_proj_kernel(x_ref, w_ref, o_ref):
    o_ref[...] = _dot(x_ref[...].astype(_bf16), w_ref[...]).astype(o_ref.dtype)


def _in_proj(x2, w):
    n = x2.shape[0]
    tm, tn = 512, 1280
    return pl.pallas_call(
        _in_proj_kernel,
        grid=(PROJ_W // tn, n // tm),
        in_specs=[pl.BlockSpec((tm, D_MODEL), lambda j, i: (i, 0)),
                  pl.BlockSpec((D_MODEL, tn), lambda j, i: (0, j))],
        out_specs=pl.BlockSpec((tm, tn), lambda j, i: (i, j)),
        out_shape=jax.ShapeDtypeStruct((n, PROJ_W), _bf16),
        compiler_params=_cparams(("arbitrary", "arbitrary")),
        name="in_proj",
    )(x2, w)


def _rope(x, cos, sin_signed):
    first = (_lane_iota((1, LANES)) % HEAD_DIM) < (HEAD_DIM // 2)
    swapped = jnp.where(first, pltpu.roll(x, LANES - HEAD_DIM // 2, 1), pltpu.roll(x, HEAD_DIM // 2, 1))
    return x * cos + swapped * sin_signed


def _dup(x, g):
    other = pltpu.roll(x, HEAD_DIM, 1)
    h0 = _half0()
    return jnp.where(h0, x, other) if g == 0 else jnp.where(h0, other, x)


def _rope_kernel(nq_ref, kc_ref, vc_ref, ks_ref, vs_ref, kw_ref, vw_ref, cos_ref, sin_ref,
                 nq_o, kc_o, vc_o, ka_o, vs_o, kw_o, vw_o, *, blocks_per_seq):
    ts = cos_ref.shape[0]
    cos = cos_ref[...]
    sin = sin_ref[...]
    for c in range(4):
        sl = slice(c * LANES, (c + 1) * LANES)
        nq_o[:, sl] = (_rope(nq_ref[:, sl].astype(_f32), cos, sin) * (QK_SCALE * LOG2E)).astype(_bf16)
    kc_o[...] = _rope(kc_ref[...].astype(_f32), cos, sin).astype(_bf16)
    vc_o[...] = vc_ref[...]
    ks = _rope(ks_ref[...].astype(_f32), cos, sin)
    kw = _rope(kw_ref[...].astype(_f32), cos, sin)
    vs = vs_ref[...].astype(_f32)
    vw = vw_ref[...].astype(_f32)
    pos = (pl.program_id(0) % blocks_per_seq) * ts + lax.broadcasted_iota(jnp.int32, (ts, LANES), 0)
    lane = _lane_iota((ts, LANES))
    onehot = jnp.where((pos // SEL_BLOCK) % HEAD_DIM + HEAD_DIM == lane, 1.0, 0.0)
    for g in range(2):
        ka_o[g] = jnp.where(_half0(), _dup(ks, g), onehot).astype(_bf16)
        vsa = jnp.where(_half0(), _dup(vs, g), 1.0)
        for c in range(ts // SEL_TK):
            vs_o[g, c] = vsa[c * SEL_TK:(c + 1) * SEL_TK, :].T.astype(_bf16)
        kw_o[g] = _dup(kw, g).astype(_bf16)
        vwd = jnp.where(_half0(), _dup(vw, g), 1.0)
        for c in range(ts // LANES):
            vw_o[g, c] = vwd[c * LANES:(c + 1) * LANES, :].T.astype(_bf16)


def _rope_stage(proj, cos, sin_signed, seq):
    n = proj.shape[0]
    ts = 512
    bps = seq // ts
    col = lambda cb: pl.BlockSpec((ts, LANES), lambda i, cb=cb: (i, cb))
    tab = pl.BlockSpec((ts, LANES), lambda i: (i % bps, 0))
    grp = lambda w: pl.BlockSpec((2, ts, w), lambda i: (0, i, 0))
    return pl.pallas_call(
        functools.partial(_rope_kernel, blocks_per_seq=bps),
        grid=(n // ts,),
        in_specs=[pl.BlockSpec((ts, 4 * LANES), lambda i: (i, CB_NQ // 4)),
                  col(CB_KC), col(CB_VC), col(CB_KS), col(CB_VS), col(CB_KW), col(CB_VW), tab, tab],
        out_specs=[pl.BlockSpec((ts, 4 * LANES), lambda i: (i, 0)),
                   pl.BlockSpec((ts, LANES), lambda i: (i, 0)),
                   pl.BlockSpec((ts, LANES), lambda i: (i, 0)),
                   grp(LANES),
                   pl.BlockSpec((2, ts // SEL_TK, LANES, SEL_TK), lambda i: (0, i, 0, 0)),
                   grp(LANES),
                   pl.BlockSpec((2, ts // LANES, LANES, LANES), lambda i: (0, i, 0, 0))],
        out_shape=[jax.ShapeDtypeStruct((n, 4 * LANES), _bf16),
                   jax.ShapeDtypeStruct((n, LANES), _bf16),
                   jax.ShapeDtypeStruct((n, LANES), _bf16),
                   jax.ShapeDtypeStruct((2, n, LANES), _bf16),
                   jax.ShapeDtypeStruct((2, n // SEL_TK, LANES, SEL_TK), _bf16),
                   jax.ShapeDtypeStruct((2, n, LANES), _bf16),
                   jax.ShapeDtypeStruct((2, n // LANES, LANES, LANES), _bf16)],
        compiler_params=_cparams(("arbitrary",)),
        name="rope_layout",
    )(proj, proj, proj, proj, proj, proj, proj, cos, sin_signed)


def _gelu_tanh(x):
    return 0.5 * x * (1.0 + jnp.tanh(0.7978845608028654 * (x + 0.044715 * (x * x * x))))


def _compress_one(x_ref, pe_t, pe_b, w_t, w_b, w2, out_ref, transposed):
    x = x_ref[0].astype(_f32)
    a = _dot((x + pe_t[...]).astype(_bf16), w_t[...])
    b = _dot((x + pe_b[...]).astype(_bf16), w_b[...])
    nc = a.shape[0]
    pre = a + pltpu.roll(b, nc - 1, 0)
    y = _dot(_gelu_tanh(pre).astype(_bf16), w2[...])
    for g in range(2):
        d = _dup(y, g)
        out_ref[0, g] = (d.T if transposed else d).astype(_bf16)


def _compress_kernel(k_ref, v_ref, kpt, kpb, kwt, kwb, kw2, vpt, vpb, vwt, vwb, vw2, ko_ref, vo_ref):
    _compress_one(k_ref, kpt, kpb, kwt, kwb, kw2, ko_ref, False)
    _compress_one(v_ref, vpt, vpb, vwt, vwb, vw2, vo_ref, True)


def _compress_weights(pe, w1, w2):
    half = CMP_BLOCK // 2
    eye = jnp.eye(2, dtype=_f32)
    outs = []
    for part in range(2):
        w = w1[part * half * HEAD_DIM:(part + 1) * half * HEAD_DIM].reshape(half, HEAD_DIM, HEAD_DIM)
        wbd = (w[:, None, :, None, :] * eye[None, :, None, :, None]).reshape(half * 2 * HEAD_DIM, 2 * HEAD_DIM)
        p = jnp.broadcast_to(pe[part * half:(part + 1) * half, None, :], (half, 2, HEAD_DIM)).reshape(1, -1)
        outs.append((p.astype(_f32), wbd.astype(_bf16)))
    w2bd = (w2[None, :, None, :] * eye[:, None, :, None]).reshape(2 * HEAD_DIM, 2 * HEAD_DIM).astype(_bf16)
    (pt, wt), (pb, wb) = outs
    return pt, pb, wt, wb, w2bd


def _compress_stage(kc_r, vc_r, kparams, vparams, batch, seq):
    nc = seq // CMP_STRIDE
    width = CMP_STRIDE * LANES
    xs = pl.BlockSpec((1, nc, width), lambda b: (b, 0, 0))
    full = lambda a: pl.BlockSpec(a.shape, lambda b: (0,) * a.ndim)
    out = pl.BlockSpec((1, 2, nc, LANES), lambda b: (b, 0, 0, 0))
    weights = list(kparams) + list(vparams)
    return pl.pallas_call(
        _compress_kernel,
        grid=(batch,),
        in_specs=[xs, xs] + [full(a) for a in weights],
        out_specs=[out, pl.BlockSpec((1, 2, LANES, nc), lambda b: (b, 0, 0, 0))],
        out_shape=[jax.ShapeDtypeStruct((batch, 2, nc, LANES), _bf16),
                   jax.ShapeDtypeStruct((batch, 2, LANES, nc), _bf16)],
        compiler_params=_cparams(("arbitrary",)),
        name="compress",
    )(kc_r.reshape(batch, nc, width), vc_r.reshape(batch, nc, width), *weights)


def _head_q(q_ref, r):
    q2 = q_ref[:, (r // 2) * LANES:(r // 2 + 1) * LANES]
    keep = _half0() if r % 2 == 0 else jnp.logical_not(_half0())
    return jnp.where(keep, q2, jnp.zeros_like(q2))


def _softmax_over_rows(s):
    m = jnp.max(s, axis=0, keepdims=True)
    e = jnp.exp2(s - m)
    l = jnp.sum(e, axis=0, keepdims=True)
    return e * jnp.where(m > 0.5 * NEG_INF, 1.0 / l, 0.0)


def _pair(even, odd):
    return jnp.where(_half0(), even, odd)


def _nsa_cw_kernel(q_ref, kc_ref, vct_ref, kw_ref, vwt_ref, ng_ref, stt_ref, yp_ref, mb_ref):
    t0 = pl.program_id(2) * Q_BLOCK
    qpos = t0 + _lane_iota((1, Q_BLOCK))
    gates = jax.nn.sigmoid(ng_ref[...].astype(_f32).T[0:16, :])
    kc = kc_ref[0, 0]
    vct = vct_ref[0, 0]
    nc = kc.shape[0]
    cend = lax.broadcasted_iota(jnp.int32, (nc, 1), 0) * CMP_STRIDE + (CMP_BLOCK - 1)
    cmask = cend <= qpos
    start = pl.multiple_of(jnp.maximum(t0 - WINDOW, 0), Q_BLOCK)
    wlen = WINDOW + Q_BLOCK
    kwin = kw_ref[0, 0, pl.ds(start, wlen), :]
    kpos = start + lax.broadcasted_iota(jnp.int32, (wlen, 1), 0)
    wmask = (kpos <= qpos) & (qpos - kpos < WINDOW)
    blk0 = start // Q_BLOCK

    qs = [_head_q(q_ref, r) for r in range(4)]
    s_cmp = [_dot_t(kc, q) for q in qs]
    s_win = [_dot_t(kwin, q) for q in qs]
    p_cmp = [_softmax_over_rows(jnp.where(cmask, s, NEG_INF)) for s in s_cmp]
    imp = (p_cmp[0] + p_cmp[1]) + (p_cmp[2] + p_cmp[3])
    e_win = []
    for s in s_win:
        s = jnp.where(wmask, s, NEG_INF)
        e_win.append(jnp.exp2(s - jnp.max(s, axis=0, keepdims=True)).astype(_bf16))
    vwt = jnp.concatenate([vwt_ref[0, 0, blk0 + c] for c in range(wlen // Q_BLOCK)], axis=1)
    yts = []
    for r in range(4):
        o_cmp = _dot(vct, p_cmp[r].astype(_bf16))[0:HEAD_DIM]
        win = _dot(vwt, e_win[r])
        o_win = win[0:HEAD_DIM] * (1.0 / win[HEAD_DIM:2 * HEAD_DIM])
        yts.append(gates[3 * r:3 * r + 1] * o_cmp + gates[3 * r + 2:3 * r + 3] * o_win)
    yp_ref[:, 0:LANES] = jnp.concatenate(yts[0:2], axis=0).T
    yp_ref[:, LANES:2 * LANES] = jnp.concatenate(yts[2:4], axis=0).T

    imp_hi = imp.astype(_bf16)
    rest = imp - imp_hi.astype(_f32)
    imp_mid = rest.astype(_bf16)
    imp_lo = (rest - imp_mid.astype(_f32)).astype(_bf16)
    st = stt_ref[...]
    p_slc = _dot(st, imp_hi) + (_dot(st, imp_mid) + _dot(st, imp_lo))
    selj = lax.broadcasted_iota(jnp.int32, (LANES, 1), 0)
    blk_t = qpos // SEL_BLOCK
    forced = (selj == 0) | (selj == blk_t) | (selj == blk_t - 1)
    score = jnp.where(forced, TAKEN, jnp.where(selj <= blk_t, p_slc, NEG_INF))
    seljf = selj.astype(_f32)
    picked = forced
    for _ in range(SEL_TOPK - 3):
        m = jnp.max(score, axis=0, keepdims=True)
        first = jnp.min(jnp.where(score == m, seljf, float(LANES)), axis=0, keepdims=True)
        hit = seljf == first
        picked = picked | hit
        score = jnp.where(hit, TAKEN, score)
    mb_ref[0] = jnp.where(picked, 0.0, NEG_INF).T.astype(_bf16)


def _nsa_cw_stage(nq_r, kc_d, vc_t, kw_d, vw_t, proj, stencil_t, batch, seq):
    n = nq_r.shape[0]
    nblk = seq // Q_BLOCK
    nc = seq // CMP_STRIDE
    qrow = lambda b, g, i: b * nblk + i
    return pl.pallas_call(
        _nsa_cw_kernel,
        grid=(batch, 2, nblk),
        in_specs=[pl.BlockSpec((Q_BLOCK, 2 * LANES), lambda b, g, i: (qrow(b, g, i), g)),
                  pl.BlockSpec((1, 1, nc, LANES), lambda b, g, i: (b, g, 0, 0)),
                  pl.BlockSpec((1, 1, LANES, nc), lambda b, g, i: (b, g, 0, 0)),
                  pl.BlockSpec((1, 1, seq, LANES), lambda b, g, i: (g, b, 0, 0)),
                  pl.BlockSpec((1, 1, nblk, LANES, LANES), lambda b, g, i: (g, b, 0, 0, 0)),
                  pl.BlockSpec((Q_BLOCK, LANES), lambda b, g, i: (qrow(b, g, i), CB_NG + g)),
                  pl.BlockSpec((LANES, nc), lambda b, g, i: (0, 0))],
        out_specs=[pl.BlockSpec((Q_BLOCK, 2 * LANES), lambda b, g, i: (qrow(b, g, i), g)),
                   pl.BlockSpec((1, Q_BLOCK, LANES), lambda b, g, i: (g, qrow(b, g, i), 0))],
        out_shape=[jax.ShapeDtypeStruct((n, 4 * LANES), _f32),
                   jax.ShapeDtypeStruct((2, n, LANES), _bf16)],
        compiler_params=_cparams(("arbitrary", "arbitrary", "arbitrary")),
        name="nsa_cmp_win_select",
    )(nq_r, kc_d, vc_t, kw_d.reshape(2, batch, seq, LANES), vw_t.reshape(2, batch, nblk, LANES, LANES), proj, stencil_t)


def _nsa_sel_kernel(q_ref, mb_ref, ka_ref, vat_ref, ng_ref, yp_ref, o_ref, qs_ref, m_ref, acc_ref, s_ref, p_ref, alpha_ref):
    tk = SEL_TK
    t0 = pl.program_id(2) * SEL_Q
    h0 = _half0()
    mb = mb_ref[0].astype(_f32)
    bias = [pltpu.roll(mb, HEAD_DIM, 1), mb]
    for r in range(4):
        q = q_ref[:, (r // 2) * LANES:(r // 2 + 1) * LANES].astype(_f32)
        if r % 2 == 1:
            q = pltpu.roll(q, HEAD_DIM, 1)
        for v in range(2):
            qs_ref[v, r * SEL_Q:(r + 1) * SEL_Q, :] = jnp.where(h0, q, bias[v]).astype(_bf16)
    m_ref[...] = jnp.full(m_ref.shape, NEG_INF, _f32)
    acc_ref[...] = jnp.zeros(acc_ref.shape, _f32)
    p_ref[...] = jnp.zeros(p_ref.shape, _bf16)
    alpha_ref[...] = jnp.ones(alpha_ref.shape, _f32)
    qpos = t0 + _lane_iota((1, 4 * SEL_Q)) % SEL_Q

    def scores(kt):
        version = (kt * (tk // SEL_BLOCK)) // HEAD_DIM
        return _dot_t(ka_ref[0, 0, pl.ds(pl.multiple_of(kt * tk, tk), tk), :], qs_ref[version])

    def softmax_step(s):
        m_old = m_ref[...]
        m_new = jnp.maximum(m_old, jnp.max(s, axis=0, keepdims=True))
        m_ref[...] = m_new
        return jnp.exp2(s - m_new).astype(_bf16), jnp.exp2(m_old - m_new)

    def accumulate(kt, alpha, p):
        acc_ref[...] = alpha * acc_ref[...] + _dot(vat_ref[0, 0, kt], p)

    def trip(kt, carry):
        accumulate(jnp.maximum(kt - 1, 0), alpha_ref[...], p_ref[...])
        s = s_ref[...]
        s_ref[...] = scores(kt + 1)
        p, alpha = softmax_step(s)
        p_ref[...] = p
        alpha_ref[...] = alpha
        return carry

    n_full = t0 // tk
    s_ref[...] = scores(0)

    def two_trips(kp, carry):
        trip(2 * kp, carry)
        return trip(2 * kp + 1, carry)

    lax.fori_loop(0, n_full // 2, two_trips, 0)

    @pl.when(n_full % 2 == 1)
    def _():
        trip(n_full - 1, 0)

    accumulate(jnp.maximum(n_full - 1, 0), alpha_ref[...], p_ref[...])
    kpos = n_full * tk + lax.broadcasted_iota(jnp.int32, (tk, 1), 0)
    p, alpha = softmax_step(jnp.where(kpos <= qpos, s_ref[...], NEG_INF))
    accumulate(n_full, alpha, p)
    acc = acc_ref[...]
    o = acc[0:HEAD_DIM] * (1.0 / acc[HEAD_DIM:2 * HEAD_DIM])
    gates = jax.nn.sigmoid(ng_ref[...].astype(_f32).T[0:16, :])
    ys = [gates[3 * r + 1:3 * r + 2] * o[:, r * SEL_Q:(r + 1) * SEL_Q] for r in range(4)]
    o_ref[:, 0:LANES] = (yp_ref[:, 0:LANES] + jnp.concatenate(ys[0:2], axis=0).T).astype(o_ref.dtype)
    o_ref[:, LANES:2 * LANES] = (yp_ref[:, LANES:2 * LANES] + jnp.concatenate(ys[2:4], axis=0).T).astype(o_ref.dtype)


def _nsa_sel_stage(nq_r, mbias, k_aug, vs_t, proj, ypart, batch, seq):
    n = nq_r.shape[0]
    nblk = seq // SEL_Q
    qrow = lambda b, g, i: b * nblk + i
    return pl.pallas_call(
        _nsa_sel_kernel,
        grid=(batch, 2, nblk),
        in_specs=[pl.BlockSpec((SEL_Q, 2 * LANES), lambda b, g, i: (qrow(b, g, i), g)),
                  pl.BlockSpec((1, SEL_Q, LANES), lambda b, g, i: (g, qrow(b, g, i), 0)),
                  pl.BlockSpec((1, 1, seq, LANES), lambda b, g, i: (g, b, 0, 0)),
                  pl.BlockSpec((1, 1, seq // SEL_TK, LANES, SEL_TK), lambda b, g, i: (g, b, 0, 0, 0)),
                  pl.BlockSpec((SEL_Q, LANES), lambda b, g, i: (qrow(b, g, i), CB_NG + g)),
                  pl.BlockSpec((SEL_Q, 2 * LANES), lambda b, g, i: (qrow(b, g, i), g))],
        out_specs=pl.BlockSpec((SEL_Q, 2 * LANES), lambda b, g, i: (qrow(b, g, i), g)),
        out_shape=jax.ShapeDtypeStruct((n, 4 * LANES), _bf16),
        scratch_shapes=[pltpu.VMEM((2, 4 * SEL_Q, LANES), _bf16),
                        pltpu.VMEM((1, 4 * SEL_Q), _f32),
                        pltpu.VMEM((LANES, 4 * SEL_Q), _f32),
                        pltpu.VMEM((SEL_TK, 4 * SEL_Q), _f32),
                        pltpu.VMEM((SEL_TK, 4 * SEL_Q), _bf16),
                        pltpu.VMEM((1, 4 * SEL_Q), _f32)],
        compiler_params=_cparams(("arbitrary", "arbitrary", "arbitrary")),
        name="nsa_selected",
    )(nq_r, mbias, k_aug.reshape(2, batch, seq, LANES), vs_t.reshape(2, batch, seq // SEL_TK, LANES, SEL_TK), proj, ypart)


def _sb_kernel(q_ref, k_ref, v_ref, o_ref, qs_ref, tail_ref, acc_ref):
    i = pl.program_id(1)
    h0 = _half0()
    heads = 2 * (q_ref.shape[1] // LANES)
    for h in range(heads):
        q = q_ref[:, (h // 2) * LANES:(h // 2 + 1) * LANES]
        keep = h0 if h % 2 == 0 else jnp.logical_not(h0)
        qs_ref[h] = jnp.where(keep, q, jnp.zeros_like(q)) * QK_SCALE
    tail_ref[...] = jnp.zeros(tail_ref.shape, _f32)
    acc_ref[...] = jnp.zeros(acc_ref.shape, _f32)
    rloc = lax.broadcasted_iota(jnp.int32, (Q_BLOCK, Q_BLOCK), 0)
    cloc = lax.broadcasted_iota(jnp.int32, (Q_BLOCK, Q_BLOCK), 1)
    later = jnp.where(rloc > cloc, 1.0, 0.0).astype(_bf16)

    def key_block(j, diagonal):
        k0 = pl.multiple_of(j * Q_BLOCK, Q_BLOCK)
        past = cloc < rloc
        cols = [slice((h // 2) * LANES, (h // 2 + 1) * LANES) for h in range(heads)]
        zs = [_dot_t(qs_ref[h], k_ref[0, pl.ds(k0, Q_BLOCK), cols[h]]) for h in range(heads)]
        log_beta, log_keep = [], []
        for z in zs:
            sp = jnp.maximum(z, 0.0) + jnp.log(1.0 + jnp.exp(-jnp.abs(z)))
            log_beta.append(z - sp)
            log_keep.append(jnp.where(past, -sp, 0.0) if diagonal else -sp)
        inner = []
        for lk in log_keep:
            hi = lk.astype(_bf16)
            lo = (lk - hi.astype(_f32)).astype(_bf16)
            inner.append(_dot(hi, later) + _dot(lo, later))
        probs = []
        for h in range(heads):
            a = jnp.exp(log_beta[h] + inner[h] + tail_ref[h])
            probs.append((jnp.where(past, a, 0.0) if diagonal else a).astype(_bf16))
        worst = jnp.full((Q_BLOCK, 1), -jnp.inf, _f32)
        for h in range(heads):
            acc_ref[h] = acc_ref[h] + _dot(probs[h], v_ref[0, pl.ds(k0, Q_BLOCK), cols[h]])
            tail = tail_ref[h] + jnp.sum(log_keep[h], axis=-1, keepdims=True)
            tail_ref[h] = tail
            worst = jnp.maximum(worst, tail)
        return jnp.max(worst)

    def cond(c):
        j, worst_tail = c
        return (j >= 0) & (worst_tail > SB_TAIL_CUTOFF)

    def body(c):
        j, _ = c
        return j - 1, key_block(j, False)

    lax.while_loop(cond, body, (i - 1, key_block(i, True)))
    for p in range(heads // 2):
        o_ref[:, p * LANES:(p + 1) * LANES] = jnp.where(h0, acc_ref[2 * p], acc_ref[2 * p + 1]).astype(o_ref.dtype)


def _sb_stage(proj, batch, seq):
    n = proj.shape[0]
    nblk = seq // Q_BLOCK
    width = 4 * LANES
    proj3 = proj.reshape(batch, seq, PROJ_W)
    return pl.pallas_call(
        _sb_kernel,
        grid=(batch, nblk),
        in_specs=[pl.BlockSpec((Q_BLOCK, width), lambda b, i: (b * nblk + i, CB_SBQ // 4)),
                  pl.BlockSpec((1, seq, width), lambda b, i: (b, 0, CB_SBK // 4)),
                  pl.BlockSpec((1, seq, width), lambda b, i: (b, 0, CB_SBV // 4))],
        out_specs=pl.BlockSpec((Q_BLOCK, width), lambda b, i: (b * nblk + i, 0)),
        out_shape=jax.ShapeDtypeStruct((n, width), _bf16),
        scratch_shapes=[pltpu.VMEM((8, Q_BLOCK, LANES), _bf16),
                        pltpu.VMEM((8, Q_BLOCK, 1), _f32),
                        pltpu.VMEM((8, Q_BLOCK, LANES), _f32)],
        compiler_params=_cparams(("arbitrary", "arbitrary")),
        name="stick_breaking",
    )(proj, proj3, proj3)


def _layer_norm(x, g, b):
    mu = jnp.mean(x, axis=-1, keepdims=True)
    xc = x - mu
    var = jnp.mean(xc * xc, axis=-1, keepdims=True)
    return xc * lax.rsqrt(var + LN_EPS) * g + b


def _merge_kernel(x_ref, ysb_ref, yns_ref, mg0_ref, mg1_ref, wsb_ref, wns_ref, wo_ref, g_ref, b_ref,
                  wr_ref, br_ref, h_ref, idx_ref, gw_ref):
    m0 = jax.nn.sigmoid(mg0_ref[...].astype(_f32))
    m1 = jax.nn.sigmoid(mg1_ref[...].astype(_f32))
    merged = m0 * _dot(ysb_ref[...], wsb_ref[...]) + m1 * _dot(yns_ref[...], wns_ref[...])
    pre = DEEPNORM_ALPHA * x_ref[...] + _dot(merged.astype(_bf16), wo_ref[...])
    h = _layer_norm(pre, g_ref[...], b_ref[...])
    tm = h.shape[0]
    for s in range(SLAB):
        h_ref[pl.ds(s, tm, stride=SLAB), :] = h[:, s * LANES:(s + 1) * LANES]
    h_hi = h.astype(_bf16)
    h_lo = (h - h_hi.astype(_f32)).astype(_bf16)
    w = wr_ref[...]
    w_hi = w.astype(_bf16)
    w_lo = (w - w_hi.astype(_f32)).astype(_bf16)
    logits = (_dot(h_hi, w_hi) + (_dot(h_hi, w_lo) + _dot(h_lo, w_hi))) + br_ref[...]
    lane = _lane_iota((1, LANES))
    lanef = lane.astype(_f32)
    lg = jnp.where(lane < N_EXPERTS, logits, TAKEN)
    vals, idxs = [], []
    for _ in range(TOP_K):
        m = jnp.max(lg, axis=-1, keepdims=True)
        first = jnp.min(jnp.where(lg == m, lanef, float(LANES)), axis=-1, keepdims=True)
        vals.append(m)
        idxs.append(first)
        lg = jnp.where(lanef == first, TAKEN, lg)
    es = [jnp.exp(v - vals[0]) for v in vals]
    inv = 1.0 / (es[0] + es[1] + es[2] + es[3])
    idx_t = jnp.zeros(lg.shape, _f32)
    gw_t = jnp.zeros(lg.shape, _f32)
    for k in range(TOP_K):
        idx_t = jnp.where(lane == k, idxs[k], idx_t)
        gw_t = jnp.where(lane == k, es[k] * inv, gw_t)
    idx_ref[...] = idx_t[:, :TOP_K].astype(jnp.int32)
    gw_ref[...] = gw_t[:, :TOP_K]


def _merge_stage(x2, y_sb, y_nsa, proj, wsb, wns, wo, g1, b1, wr, br):
    n = x2.shape[0]
    tm = 512
    row = lambda w: pl.BlockSpec((tm, w), lambda i: (i, 0))
    full = lambda a: pl.BlockSpec(a.shape, lambda i: (0,) * a.ndim)
    return pl.pallas_call(
        _merge_kernel,
        grid=(n // tm,),
        in_specs=[row(D_MODEL), row(4 * LANES), row(4 * LANES),
                  pl.BlockSpec((tm, D_MODEL), lambda i: (i, CB_MG // 8)),
                  pl.BlockSpec((tm, D_MODEL), lambda i: (i, CB_MG // 8 + 1)),
                  full(wsb), full(wns), full(wo), full(g1), full(b1), full(wr), full(br)],
        out_specs=[pl.BlockSpec((tm * SLAB, LANES), lambda i: (i, 0)), row(TOP_K), row(TOP_K)],
        out_shape=[jax.ShapeDtypeStruct((n * SLAB, LANES), _f32),
                   jax.ShapeDtypeStruct((n, TOP_K), jnp.int32),
                   jax.ShapeDtypeStruct((n, TOP_K), _f32)],
        compiler_params=_cparams(("arbitrary",)),
        name="merge_ln1_router",
    )(x2, y_sb, y_nsa, proj, proj, wsb, wns, wo, g1, b1, wr, br)


def _prep_w_in(w):
    main = w[:, :CB_NG * LANES]
    ng = w[:, CB_NG * LANES:CB_NG * LANES + 24]
    mg = w[:, CB_NG * LANES + 24:]
    pad = jnp.zeros((w.shape[0], LANES - 12), w.dtype)
    return jnp.concatenate([main, ng[:, :12], pad, ng[:, 12:], pad, mg], axis=1).astype(_bf16)


def _rope_tables(seq):
    half = HEAD_DIM // 2
    inv_freq = ROPE_THETA ** (-jnp.arange(half, dtype=_f32) / half)
    ang = jnp.arange(seq, dtype=_f32)[:, None] * inv_freq[None, :]
    cos = jnp.cos(ang)
    sin = jnp.sin(ang)
    cos128 = jnp.concatenate([cos, cos, cos, cos], axis=1)
    sin128 = jnp.concatenate([-sin, sin, -sin, sin], axis=1)
    return cos128, sin128


def _stencil(nc):
    n = np.arange(nc)[:, None]
    j = np.arange(LANES)[None, :]
    ratio = SEL_BLOCK // CMP_STRIDE
    ok = (n >= ratio * j - 1) & (n <= ratio * j + ratio - 1) & (n < nc - 1)
    return jnp.asarray(ok.astype(np.float32).T, dtype=_bf16)


def _attention_half(x, w_in, cmp_pe_k, cmp_w1_k, cmp_w2_k, cmp_pe_v, cmp_w1_v, cmp_w2_v,
                    w_proj_sb, w_proj_nsa, w_out, ln1_g, ln1_b, w_router, b_router):
    batch, seq, _ = x.shape
    n = batch * seq
    x2 = x.reshape(n, D_MODEL)
    proj = _in_proj(x2, _prep_w_in(w_in))
    cos, sin_signed = _rope_tables(seq)
    nq_r, kc_r, vc_r, k_aug, vs_t, kw_d, vw_t = _rope_stage(proj, cos, sin_signed, seq)
    kc_d, vc_t = _compress_stage(kc_r, vc_r, _compress_weights(cmp_pe_k, cmp_w1_k, cmp_w2_k),
                                 _compress_weights(cmp_pe_v, cmp_w1_v, cmp_w2_v), batch, seq)
    ypart, mbias = _nsa_cw_stage(nq_r, kc_d, vc_t, kw_d, vw_t, proj, _stencil(seq // CMP_STRIDE), batch, seq)
    y_nsa = _nsa_sel_stage(nq_r, mbias, k_aug, vs_t, proj, ypart, batch, seq)
    y_sb = _sb_stage(proj, batch, seq)
    wr = jnp.pad(w_router.astype(_f32), ((0, 0), (0, LANES - N_EXPERTS)))
    br = jnp.pad(b_router.astype(_f32), (0, LANES - N_EXPERTS)).reshape(1, LANES)
    return _merge_stage(x2, y_sb, y_nsa, proj, w_proj_sb.astype(_bf16), w_proj_nsa.astype(_bf16),
                        w_out.astype(_bf16), ln1_g.reshape(1, -1), ln1_b.reshape(1, -1), wr, br)


def _row_copy(src, src_row, dst, dst_row, sem):
    return pltpu.make_async_copy(src.at[pl.ds(src_row * SLAB, SLAB)], dst.at[pl.ds(dst_row * SLAB, SLAB)], sem)


def _dispatch_kernel(pad_start_ref, pad_len_ref, dest_ref, h_hbm, buf_ref, hbuf, zslab, in_sems, row_sems, zero_sem,
                     *, tokens):
    i = pl.program_id(0)
    last = pl.num_programs(0) - 1
    slot = i % 2
    block_rows = tokens * SLAB

    def block_copy(step, s):
        return pltpu.make_async_copy(h_hbm.at[pl.ds(step * block_rows, block_rows)], hbuf.at[s], in_sems.at[s])

    def wait_rows(s):
        for _ in range(TOP_K):
            pltpu.make_async_copy(hbuf.at[s], buf_ref.at[pl.ds(0, block_rows)], row_sems.at[s]).wait()

    def padding_rows(act):
        def per_expert(e, c):
            def per_row(r, c2):
                act(_row_copy(zslab, 0, buf_ref, pad_start_ref[e] + r, zero_sem))
                return c2
            lax.fori_loop(0, pad_len_ref[e], per_row, 0)
            return c
        lax.fori_loop(0, N_EXPERTS, per_expert, 0)

    @pl.when(i == 0)
    def _():
        block_copy(0, 0).start()
        zslab[...] = jnp.zeros(zslab.shape, _f32)
        padding_rows(lambda cp: cp.start())

    @pl.when(i > 0)
    def _():
        wait_rows(1 - slot)

    @pl.when(i < last)
    def _():
        block_copy(i + 1, 1 - slot).start()

    block_copy(i, slot).wait()
    src = hbuf.at[slot]

    def issue(tg, c):
        for u in range(ISSUE_UNROLL):
            t = tg * ISSUE_UNROLL + u
            for k in range(TOP_K):
                _row_copy(src, t, buf_ref, dest_ref[t * TOP_K + k], row_sems.at[slot]).start(priority=k % 2)
        return c

    lax.fori_loop(0, tokens // ISSUE_UNROLL, issue, 0)

    @pl.when(i == last)
    def _():
        wait_rows(slot)
        padding_rows(lambda cp: cp.wait())


def _dispatch_stage(dest, pad_start, pad_len, h_slab, buf_rows):
    n = h_slab.shape[0] // SLAB
    tokens = 256
    grid_spec = pltpu.PrefetchScalarGridSpec(
        num_scalar_prefetch=2,
        grid=(n // tokens,),
        in_specs=[pl.BlockSpec((tokens * TOP_K,), lambda i, ps, pn: (i,), memory_space=pltpu.SMEM),
                  pl.BlockSpec(memory_space=pl.ANY)],
        out_specs=pl.BlockSpec(memory_space=pl.ANY),
        scratch_shapes=[pltpu.VMEM((2, tokens * SLAB, LANES), _f32),
                        pltpu.VMEM((SLAB, LANES), _f32),
                        pltpu.SemaphoreType.DMA((2,)),
                        pltpu.SemaphoreType.DMA((2,)),
                        pltpu.SemaphoreType.DMA(())])
    return pl.pallas_call(
        functools.partial(_dispatch_kernel, tokens=tokens),
        grid_spec=grid_spec,
        out_shape=jax.ShapeDtypeStruct((buf_rows * SLAB, LANES), _f32),
        compiler_params=pltpu.CompilerParams(dimension_semantics=("arbitrary",), has_side_effects=True),
        name="moe_dispatch",
    )(pad_start, pad_len, dest, h_slab)


def _slab_load(ref, rows):
    return jnp.concatenate([ref[pl.ds(s, rows, stride=SLAB), :] for s in range(SLAB)], axis=1)


def _expert_kernel(ce_ref, used_ref, x_ref, wgu_ref, bgu_ref, wd_ref, bd_ref, o_ref, wgu_bf, wd_bf):
    c = pl.program_id(0)
    new_expert = (c == 0) | (ce_ref[c] != ce_ref[jnp.maximum(c - 1, 0)])

    @pl.when(new_expert & (c < used_ref[0]))
    def _():
        wgu_bf[...] = wgu_ref[0].astype(_bf16)
        wd_bf[...] = wd_ref[0].astype(_bf16)

    @pl.when(c < used_ref[0])
    def _():
        x = _slab_load(x_ref, MOE_ROWS).astype(_bf16)
        gu = _dot(x, wgu_bf[...]) + bgu_ref[0]
        gate = jnp.minimum(gu[:, :D_MODEL], SWIGLU_LIMIT)
        up = jnp.clip(gu[:, D_MODEL:], -SWIGLU_LIMIT, SWIGLU_LIMIT)
        h = gate * jax.nn.sigmoid(SWIGLU_ALPHA * gate) * (up + 1.0)
        y = _dot(h.astype(_bf16), wd_bf[...]) + bd_ref[0]
        for s in range(SLAB):
            o_ref[pl.ds(s, MOE_ROWS, stride=SLAB), :] = y[:, s * LANES:(s + 1) * LANES]

    @pl.when(c >= used_ref[0])
    def _():
        o_ref[...] = jnp.zeros(o_ref.shape, o_ref.dtype)


def _expert_stage(chunk_expert, n_used, buf, wgu, bgu, wd, bd):
    n_chunks = chunk_expert.shape[0]
    rows = MOE_ROWS * SLAB
    grid_spec = pltpu.PrefetchScalarGridSpec(
        num_scalar_prefetch=2,
        grid=(n_chunks,),
        in_specs=[pl.BlockSpec((rows, LANES), lambda c, ce, nu: (jnp.minimum(c, nu[0] - 1), 0)),
                  pl.BlockSpec((1, D_MODEL, 2 * D_MODEL), lambda c, ce, nu: (ce[c], 0, 0)),
                  pl.BlockSpec((1, 1, 2 * D_MODEL), lambda c, ce, nu: (ce[c], 0, 0)),
                  pl.BlockSpec((1, D_MODEL, D_MODEL), lambda c, ce, nu: (ce[c], 0, 0)),
                  pl.BlockSpec((1, 1, D_MODEL), lambda c, ce, nu: (ce[c], 0, 0))],
        out_specs=pl.BlockSpec((rows, LANES), lambda c, ce, nu: (c, 0)),
        scratch_shapes=[pltpu.VMEM((D_MODEL, 2 * D_MODEL), _bf16), pltpu.VMEM((D_MODEL, D_MODEL), _bf16)])
    return pl.pallas_call(
        _expert_kernel,
        grid_spec=grid_spec,
        out_shape=jax.ShapeDtypeStruct(buf.shape, _f32),
        compiler_params=_cparams(("arbitrary",)),
        name="moe_experts",
    )(chunk_expert, n_used, buf, wgu, bgu, wd, bd)


def _combine_kernel(dest_ref, dest_next_ref, gw_ref, h_ref, eo_ref, g_ref, b_ref, o_ref, gbuf, ysl, sems, *, tokens):
    i = pl.program_id(0)
    slot = i % 2
    count = tokens * TOP_K

    def gather(idx_ref, into):
        def issue(tg, c):
            for u in range(ISSUE_UNROLL * TOP_K):
                j = tg * (ISSUE_UNROLL * TOP_K) + u
                _row_copy(eo_ref, idx_ref[j], gbuf.at[into], j, sems.at[into]).start(priority=u % 2)
            return c
        lax.fori_loop(0, tokens // ISSUE_UNROLL, issue, 0)

    @pl.when(i == 0)
    def _():
        gather(dest_ref, 0)

    @pl.when(i + 1 < pl.num_programs(0))
    def _():
        gather(dest_next_ref, 1 - slot)

    pltpu.make_async_copy(eo_ref.at[pl.ds(0, count * SLAB)], gbuf.at[slot], sems.at[slot]).wait()
    rows = gbuf.at[slot]

    unroll = 4

    def token_group(tg, c):
        for u in range(unroll):
            t = tg * unroll + u
            acc = DEEPNORM_ALPHA * h_ref[pl.ds(pl.multiple_of(t * SLAB, SLAB), SLAB), :]
            for k in range(TOP_K):
                j = t * TOP_K + k
                acc = acc + gw_ref[j] * rows[pl.ds(pl.multiple_of(j * SLAB, SLAB), SLAB), :]
            ysl[pl.ds(pl.multiple_of(t * SLAB, SLAB), SLAB), :] = acc
        return c

    lax.fori_loop(0, tokens // unroll, token_group, 0)
    o_ref[...] = _layer_norm(_slab_load(ysl, tokens), g_ref[...], b_ref[...])


def _combine_stage(dest, gate_w, h_slab, expert_out, g2, b2):
    n = h_slab.shape[0] // SLAB
    tokens = 256
    steps = n // tokens
    smem = lambda: pl.BlockSpec((tokens * TOP_K,), lambda i: (i,), memory_space=pltpu.SMEM)
    return pl.pallas_call(
        functools.partial(_combine_kernel, tokens=tokens),
        grid=(steps,),
        in_specs=[smem(),
                  pl.BlockSpec((tokens * TOP_K,), lambda i: (jnp.minimum(i + 1, steps - 1),), memory_space=pltpu.SMEM),
                  smem(),
                  pl.BlockSpec((tokens * SLAB, LANES), lambda i: (i, 0)),
                  pl.BlockSpec(memory_space=pl.ANY),
                  pl.BlockSpec((1, D_MODEL), lambda i: (0, 0)),
                  pl.BlockSpec((1, D_MODEL), lambda i: (0, 0))],
        out_specs=pl.BlockSpec((tokens, D_MODEL), lambda i: (i, 0)),
        out_shape=jax.ShapeDtypeStruct((n, D_MODEL), _f32),
        scratch_shapes=[pltpu.VMEM((2, tokens * TOP_K * SLAB, LANES), _f32),
                        pltpu.VMEM((tokens * SLAB, LANES), _f32),
                        pltpu.SemaphoreType.DMA((2,))],
        compiler_params=_cparams(("arbitrary",)),
        name="moe_combine_ln2",
    )(dest, dest, gate_w, h_slab, expert_out, g2, b2)


def _dispatch_plan(top_idx):
    m = top_idx.size
    e_flat = top_idx.reshape(m)
    onehot = (e_flat[:, None] == jnp.arange(N_EXPERTS, dtype=jnp.int32)[None, :]).astype(jnp.int32)
    csum = jnp.cumsum(onehot, axis=0)
    counts = csum[-1]
    padded = (counts + MOE_ROWS - 1) // MOE_ROWS * MOE_ROWS
    pends = jnp.cumsum(padded)
    pstarts = pends - padded
    dest = jnp.sum(onehot * (csum - 1 + pstarts[None, :]), axis=1).astype(jnp.int32)
    n_chunks = m // MOE_ROWS + N_EXPERTS
    chunk_start = jnp.arange(n_chunks, dtype=jnp.int32) * MOE_ROWS
    chunk_expert = jnp.minimum(jnp.sum((chunk_start[:, None] >= pends[None, :]).astype(jnp.int32), axis=1), N_EXPERTS - 1)
    n_used = (pends[-1] // MOE_ROWS).astype(jnp.int32).reshape(1)
    pad_start = (pstarts + counts).astype(jnp.int32)
    pad_len = (padded - counts).astype(jnp.int32)
    return dest, chunk_expert, n_used, pad_start, pad_len, n_chunks * MOE_ROWS


def _moe_half(h_slab, top_idx, gate_w, w_gate_up, b_gate_up, w_down, b_down, ln2_g, ln2_b):
    dest, chunk_expert, n_used, pad_start, pad_len, buf_rows = _dispatch_plan(top_idx)
    buf = _dispatch_stage(dest, pad_start, pad_len, h_slab, buf_rows)
    expert_out = _expert_stage(chunk_expert, n_used, buf, w_gate_up, b_gate_up.reshape(N_EXPERTS, 1, -1),
                               w_down, b_down.reshape(N_EXPERTS, 1, -1))
    return _combine_stage(dest, gate_w.reshape(-1), h_slab, expert_out, ln2_g.reshape(1, -1), ln2_b.reshape(1, -1))


def kernel(x, w_in, cmp_pe_k, cmp_w1_k, cmp_w2_k, cmp_pe_v, cmp_w1_v, cmp_w2_v, w_proj_sb, w_proj_nsa, w_out,
           ln1_g, ln1_b, w_router, b_router, w_gate_up, b_gate_up, w_down, b_down, ln2_g, ln2_b):
    assert w_in.shape[0] == 1, "single-layer block"
    batch, seq, _ = x.shape
    assert seq % 512 == 0 and seq // SEL_BLOCK <= LANES and seq >= WINDOW + Q_BLOCK
    h_slab, top_idx, gate_w = _attention_half(
        x, w_in[0], cmp_pe_k[0], cmp_w1_k[0], cmp_w2_k[0], cmp_pe_v[0], cmp_w1_v[0], cmp_w2_v[0],
        w_proj_sb[0], w_proj_nsa[0], w_out[0], ln1_g[0], ln1_b[0], w_router[0], b_router[0])
    out = _moe_half(h_slab, top_idx, gate_w, w_gate_up[0], b_gate_up[0], w_down[0], b_down[0], ln2_g[0], ln2_b[0])
    return out.reshape(batch, seq, D_MODEL)
```

```python
import functools

import numpy as np
import jax
import jax.numpy as jnp
from jax import lax
from jax.experimental import pallas as pl
from jax.experimental.pallas import tpu as pltpu

D_MODEL = 1024
HEAD_DIM = 64
LANES = 128
Q_BLOCK = 128
CMP_BLOCK = 32
CMP_STRIDE = 16
SEL_BLOCK = 64
SEL_TOPK = 16
WINDOW = 512
ROPE_THETA = 10000.0
N_EXPERTS = 32
TOP_K = 4
SWIGLU_LIMIT = 7.0
SWIGLU_ALPHA = 1.702
LN_EPS = 1e-5
NEG_INF = -1e30
TAKEN = -3e38
DEEPNORM_ALPHA = 2.0 ** 0.25
QK_SCALE = HEAD_DIM ** -0.5
LOG2E = 1.4426950408889634

CB_SBQ, CB_SBK, CB_SBV, CB_NQ = 0, 4, 8, 12
CB_KC, CB_VC, CB_KS, CB_VS, CB_KW, CB_VW = 16, 17, 18, 19, 20, 21
CB_NG = 22
CB_MG = 24
PROJ_W = 40 * LANES

SB_TAIL_CUTOFF = -110.0

SEL_Q = 512
SEL_TK = 512
MOE_ROWS = 512
SLAB = D_MODEL // LANES
ISSUE_UNROLL = 4
VMEM_LIMIT = 56 * 1024 * 1024

_bf16 = jnp.bfloat16
_f32 = jnp.float32


def _cparams(sem):
    return pltpu.CompilerParams(dimension_semantics=sem, vmem_limit_bytes=VMEM_LIMIT)


def _dot_t(a, b):
    return lax.dot_general(a, b, (((1,), (1,)), ((), ())), preferred_element_type=_f32)


def _dot(a, b):
    return jnp.dot(a, b, preferred_element_type=_f32)


def _lane_iota(shape):
    return lax.broadcasted_iota(jnp.int32, shape, len(shape) - 1)


def _half0(shape=(1, LANES)):
    return _lane_iota(shape) < HEAD_DIM


def _in_proj_kernel(x_ref, w_ref, o_ref):
    o_ref[...] = _dot(x_ref[...].astype(_bf16), w_ref[...]).astype(o_ref.dtype)


def _in_proj(x2, w):
    n = x2.shape[0]
    tm, tn = 512, 1280
    return pl.pallas_call(
        _in_proj_kernel,
        grid=(PROJ_W // tn, n // tm),
        in_specs=[pl.BlockSpec((tm, D_MODEL), lambda j, i: (i, 0)),
                  pl.BlockSpec((D_MODEL, tn), lambda j, i: (0, j))],
        out_specs=pl.BlockSpec((tm, tn), lambda j, i: (i, j)),
        out_shape=jax.ShapeDtypeStruct((n, PROJ_W), _bf16),
        compiler_params=_cparams(("arbitrary", "arbitrary")),
        name="in_proj",
    )(x2, w)


def _rope(x, cos, sin_signed):
    first = (_lane_iota((1, LANES)) % HEAD_DIM) < (HEAD_DIM // 2)
    swapped = jnp.where(first, pltpu.roll(x, LANES - HEAD_DIM // 2, 1), pltpu.roll(x, HEAD_DIM // 2, 1))
    return x * cos + swapped * sin_signed


def _dup(x, g):
    other = pltpu.roll(x, HEAD_DIM, 1)
    h0 = _half0()
    return jnp.where(h0, x, other) if g == 0 else jnp.where(h0, other, x)


def _rope_kernel(nq_ref, kc_ref, vc_ref, ks_ref, vs_ref, kw_ref, vw_ref, cos_ref, sin_ref,
                 nq_o, kc_o, vc_o, ka_o, vs_o, kw_o, vw_o, *, blocks_per_seq):
    ts = cos_ref.shape[0]
    cos = cos_ref[...]
    sin = sin_ref[...]
    for c in range(4):
        sl = slice(c * LANES, (c + 1) * LANES)
        nq_o[:, sl] = (_rope(nq_ref[:, sl].astype(_f32), cos, sin) * (QK_SCALE * LOG2E)).astype(_bf16)
    kc_o[...] = _rope(kc_ref[...].astype(_f32), cos, sin).astype(_bf16)
    vc_o[...] = vc_ref[...]
    ks = _rope(ks_ref[...].astype(_f32), cos, sin)
    kw = _rope(kw_ref[...].astype(_f32), cos, sin)
    vs = vs_ref[...].astype(_f32)
    vw = vw_ref[...].astype(_f32)
    pos = (pl.program_id(0) % blocks_per_seq) * ts + lax.broadcasted_iota(jnp.int32, (ts, LANES), 0)
    lane = _lane_iota((ts, LANES))
    onehot = jnp.where((pos // SEL_BLOCK) % HEAD_DIM + HEAD_DIM == lane, 1.0, 0.0)
    for g in range(2):
        ka_o[g] = jnp.where(_half0(), _dup(ks, g), onehot).astype(_bf16)
        vsa = jnp.where(_half0(), _dup(vs, g), 1.0)
        for c in range(ts // SEL_TK):
            vs_o[g, c] = vsa[c * SEL_TK:(c + 1) * SEL_TK, :].T.astype(_bf16)
        kw_o[g] = _dup(kw, g).astype(_bf16)
        vwd = jnp.where(_half0(), _dup(vw, g), 1.0)
        for c in range(ts // LANES):
            vw_o[g, c] = vwd[c * LANES:(c + 1) * LANES, :].T.astype(_bf16)


def _rope_stage(proj, cos, sin_signed, seq):
    n = proj.shape[0]
    ts = 512
    bps = seq // ts
    col = lambda cb: pl.BlockSpec((ts, LANES), lambda i, cb=cb: (i, cb))
    tab = pl.BlockSpec((ts, LANES), lambda i: (i % bps, 0))
    grp = lambda w: pl.BlockSpec((2, ts, w), lambda i: (0, i, 0))
    return pl.pallas_call(
        functools.partial(_rope_kernel, blocks_per_seq=bps),
        grid=(n // ts,),
        in_specs=[pl.BlockSpec((ts, 4 * LANES), lambda i: (i, CB_NQ // 4)),
                  col(CB_KC), col(CB_VC), col(CB_KS), col(CB_VS), col(CB_KW), col(CB_VW), tab, tab],
        out_specs=[pl.BlockSpec((ts, 4 * LANES), lambda i: (i, 0)),
                   pl.BlockSpec((ts, LANES), lambda i: (i, 0)),
                   pl.BlockSpec((ts, LANES), lambda i: (i, 0)),
                   grp(LANES),
                   pl.BlockSpec((2, ts // SEL_TK, LANES, SEL_TK), lambda i: (0, i, 0, 0)),
                   grp(LANES),
                   pl.BlockSpec((2, ts // LANES, LANES, LANES), lambda i: (0, i, 0, 0))],
        out_shape=[jax.ShapeDtypeStruct((n, 4 * LANES), _bf16),
                   jax.ShapeDtypeStruct((n, LANES), _bf16),
                   jax.ShapeDtypeStruct((n, LANES), _bf16),
                   jax.ShapeDtypeStruct((2, n, LANES), _bf16),
                   jax.ShapeDtypeStruct((2, n // SEL_TK, LANES, SEL_TK), _bf16),
                   jax.ShapeDtypeStruct((2, n, LANES), _bf16),
                   jax.ShapeDtypeStruct((2, n // LANES, LANES, LANES), _bf16)],
        compiler_params=_cparams(("arbitrary",)),
        name="rope_layout",
    )(proj, proj, proj, proj, proj, proj, proj, cos, sin_signed)


def _gelu_tanh(x):
    return 0.5 * x * (1.0 + jnp.tanh(0.7978845608028654 * (x + 0.044715 * (x * x * x))))


def _compress_one(x_ref, pe_t, pe_b, w_t, w_b, w2, out_ref, transposed):
    x = x_ref[0].astype(_f32)
    a = _dot((x + pe_t[...]).astype(_bf16), w_t[...])
    b = _dot((x + pe_b[...]).astype(_bf16), w_b[...])
    nc = a.shape[0]
    pre = a + pltpu.roll(b, nc - 1, 0)
    y = _dot(_gelu_tanh(pre).astype(_bf16), w2[...])
    for g in range(2):
        d = _dup(y, g)
        out_ref[0, g] = (d.T if transposed else d).astype(_bf16)


def _compress_kernel(k_ref, v_ref, kpt, kpb, kwt, kwb, kw2, vpt, vpb, vwt, vwb, vw2, ko_ref, vo_ref):
    _compress_one(k_ref, kpt, kpb, kwt, kwb, kw2, ko_ref, False)
    _compress_one(v_ref, vpt, vpb, vwt, vwb, vw2, vo_ref, True)


def _compress_weights(pe, w1, w2):
    half = CMP_BLOCK // 2
    eye = jnp.eye(2, dtype=_f32)
    outs = []
    for part in range(2):
        w = w1[part * half * HEAD_DIM:(part + 1) * half * HEAD_DIM].reshape(half, HEAD_DIM, HEAD_DIM)
        wbd = (w[:, None, :, None, :] * eye[None, :, None, :, None]).reshape(half * 2 * HEAD_DIM, 2 * HEAD_DIM)
        p = jnp.broadcast_to(pe[part * half:(part + 1) * half, None, :], (half, 2, HEAD_DIM)).reshape(1, -1)
        outs.append((p.astype(_f32), wbd.astype(_bf16)))
    w2bd = (w2[None, :, None, :] * eye[:, None, :, None]).reshape(2 * HEAD_DIM, 2 * HEAD_DIM).astype(_bf16)
    (pt, wt), (pb, wb) = outs
    return pt, pb, wt, wb, w2bd


def _compress_stage(kc_r, vc_r, kparams, vparams, batch, seq):
    nc = seq // CMP_STRIDE
    width = CMP_STRIDE * LANES
    xs = pl.BlockSpec((1, nc, width), lambda b: (b, 0, 0))
    full = lambda a: pl.BlockSpec(a.shape, lambda b: (0,) * a.ndim)
    out = pl.BlockSpec((1, 2, nc, LANES), lambda b: (b, 0, 0, 0))
    weights = list(kparams) + list(vparams)
    return pl.pallas_call(
        _compress_kernel,
        grid=(batch,),
        in_specs=[xs, xs] + [full(a) for a in weights],
        out_specs=[out, pl.BlockSpec((1, 2, LANES, nc), lambda b: (b, 0, 0, 0))],
        out_shape=[jax.ShapeDtypeStruct((batch, 2, nc, LANES), _bf16),
                   jax.ShapeDtypeStruct((batch, 2, LANES, nc), _bf16)],
        compiler_params=_cparams(("arbitrary",)),
        name="compress",
    )(kc_r.reshape(batch, nc, width), vc_r.reshape(batch, nc, width), *weights)


def _head_q(q_ref, r):
    q2 = q_ref[:, (r // 2) * LANES:(r // 2 + 1) * LANES]
    keep = _half0() if r % 2 == 0 else jnp.logical_not(_half0())
    return jnp.where(keep, q2, jnp.zeros_like(q2))


def _softmax_over_rows(s):
    m = jnp.max(s, axis=0, keepdims=True)
    e = jnp.exp2(s - m)
    l = jnp.sum(e, axis=0, keepdims=True)
    return e * jnp.where(m > 0.5 * NEG_INF, 1.0 / l, 0.0)


def _pair(even, odd):
    return jnp.where(_half0(), even, odd)


def _nsa_cw_kernel(q_ref, kc_ref, vct_ref, kw_ref, vwt_ref, ng_ref, stt_ref, *rest, first_block):
    yp_ref, mb_ref = rest[-2:]
    t0 = (pl.program_id(2) + first_block) * Q_BLOCK
    qpos = t0 + _lane_iota((1, Q_BLOCK))
    gates = jax.nn.sigmoid(ng_ref[...].astype(_f32).T[0:16, :])
    kc = kc_ref[0, 0]
    vct = vct_ref[0, 0]
    nc = kc.shape[0]
    cend = lax.broadcasted_iota(jnp.int32, (nc, 1), 0) * CMP_STRIDE + (CMP_BLOCK - 1)
    cmask = cend <= qpos
    start = pl.multiple_of(jnp.maximum(t0 - WINDOW, 0), Q_BLOCK)
    wlen = WINDOW + Q_BLOCK
    kwin = kw_ref[0, 0, pl.ds(start, wlen), :]
    kpos = start + lax.broadcasted_iota(jnp.int32, (wlen, 1), 0)
    wmask = (kpos <= qpos) & (qpos - kpos < WINDOW)
    blk0 = start // Q_BLOCK

    qs = [_head_q(q_ref, r) for r in range(4)]
    s_cmp = [_dot_t(kc, q) for q in qs]
    s_win = [_dot_t(kwin, q) for q in qs]
    p_cmp = [_softmax_over_rows(jnp.where(cmask, s, NEG_INF)) for s in s_cmp]
    imp = (p_cmp[0] + p_cmp[1]) + (p_cmp[2] + p_cmp[3])
    e_win = []
    for s in s_win:
        s = jnp.where(wmask, s, NEG_INF)
        e_win.append(jnp.exp2(s - jnp.max(s, axis=0, keepdims=True)).astype(_bf16))
    vwt = jnp.concatenate([vwt_ref[0, 0, blk0 + c] for c in range(wlen // Q_BLOCK)], axis=1)
    yts = []
    for r in range(4):
        o_cmp = _dot(vct, p_cmp[r].astype(_bf16))[0:HEAD_DIM]
        win = _dot(vwt, e_win[r])
        o_win = win[0:HEAD_DIM] * (1.0 / win[HEAD_DIM:2 * HEAD_DIM])
        yts.append(gates[3 * r:3 * r + 1] * o_cmp + gates[3 * r + 2:3 * r + 3] * o_win)
    yp_ref[:, 0:LANES] = jnp.concatenate(yts[0:2], axis=0).T
    yp_ref[:, LANES:2 * LANES] = jnp.concatenate(yts[2:4], axis=0).T

    imp_hi = imp.astype(_bf16)
    rest = imp - imp_hi.astype(_f32)
    imp_mid = rest.astype(_bf16)
    imp_lo = (rest - imp_mid.astype(_f32)).astype(_bf16)
    st = stt_ref[...]
    p_slc = _dot(st, imp_hi) + (_dot(st, imp_mid) + _dot(st, imp_lo))
    selj = lax.broadcasted_iota(jnp.int32, (LANES, 1), 0)
    blk_t = qpos // SEL_BLOCK
    forced = (selj == 0) | (selj == blk_t) | (selj == blk_t - 1)
    score = jnp.where(forced, TAKEN, jnp.where(selj <= blk_t, p_slc, NEG_INF))
    seljf = selj.astype(_f32)
    picked = forced
    for _ in range(SEL_TOPK - 3):
        m = jnp.max(score, axis=0, keepdims=True)
        first = jnp.min(jnp.where(score == m, seljf, float(LANES)), axis=0, keepdims=True)
        hit = seljf == first
        picked = picked | hit
        score = jnp.where(hit, TAKEN, score)
    mb_ref[0] = jnp.where(picked, 0.0, NEG_INF).T.astype(_bf16)


def _nsa_cw_stage(nq_r, kc_d, vc_t, kw_d, vw_t, proj, stencil_t, batch, seq):
    n = nq_r.shape[0]
    nblk = seq // Q_BLOCK
    nc = seq // CMP_STRIDE
    per_block = Q_BLOCK // CMP_STRIDE
    splits = 4 if nblk % 4 == 0 and (nblk // 4 * per_block) % LANES == 0 else 1
    per = nblk // splits
    kw4 = kw_d.reshape(2, batch, seq, LANES)
    vw5 = vw_t.reshape(2, batch, nblk, LANES, LANES)
    outs = ()
    for k in range(splits):
        first = k * per
        nvis = min(nc, (first + per) * per_block)
        qrow = lambda b, g, i, first=first: b * nblk + first + i
        in_specs = [pl.BlockSpec((Q_BLOCK, 2 * LANES), lambda b, g, i, qrow=qrow: (qrow(b, g, i), g)),
                    pl.BlockSpec((1, 1, nvis, LANES), lambda b, g, i: (b, g, 0, 0)),
                    pl.BlockSpec((1, 1, LANES, nvis), lambda b, g, i: (b, g, 0, 0)),
                    pl.BlockSpec((1, 1, seq, LANES), lambda b, g, i: (g, b, 0, 0)),
                    pl.BlockSpec((1, 1, nblk, LANES, LANES), lambda b, g, i: (g, b, 0, 0, 0)),
                    pl.BlockSpec((Q_BLOCK, LANES), lambda b, g, i, qrow=qrow: (qrow(b, g, i), CB_NG + g)),
                    pl.BlockSpec((LANES, nvis), lambda b, g, i: (0, 0))]
        in_specs += [pl.BlockSpec(memory_space=pl.ANY)] * len(outs)
        outs = pl.pallas_call(
            functools.partial(_nsa_cw_kernel, first_block=first),
            grid=(batch, 2, per),
            in_specs=in_specs,
            out_specs=[pl.BlockSpec((Q_BLOCK, 2 * LANES), lambda b, g, i, qrow=qrow: (qrow(b, g, i), g)),
                       pl.BlockSpec((1, Q_BLOCK, LANES), lambda b, g, i, qrow=qrow: (g, qrow(b, g, i), 0))],
            out_shape=[jax.ShapeDtypeStruct((n, 4 * LANES), _f32),
                       jax.ShapeDtypeStruct((2, n, LANES), _bf16)],
            input_output_aliases={7: 0, 8: 1} if outs else {},
            compiler_params=_cparams(("arbitrary", "arbitrary", "arbitrary")),
            name="nsa_cmp_win_select",
        )(nq_r, kc_d, vc_t, kw4, vw5, proj, stencil_t, *outs)
    return outs


def _nsa_sel_kernel(q_ref, mb_ref, ka_ref, vat_ref, ng_ref, yp_ref, o_ref, qs_ref, m_ref, acc_ref, s_ref, p_ref, alpha_ref):
    tk = SEL_TK
    t0 = pl.program_id(2) * SEL_Q
    h0 = _half0()
    mb = mb_ref[0].astype(_f32)
    bias = [pltpu.roll(mb, HEAD_DIM, 1), mb]
    for r in range(4):
        q = q_ref[:, (r // 2) * LANES:(r // 2 + 1) * LANES].astype(_f32)
        if r % 2 == 1:
            q = pltpu.roll(q, HEAD_DIM, 1)
        for v in range(2):
            qs_ref[v, r * SEL_Q:(r + 1) * SEL_Q, :] = jnp.where(h0, q, bias[v]).astype(_bf16)
    m_ref[...] = jnp.full(m_ref.shape, NEG_INF, _f32)
    acc_ref[...] = jnp.zeros(acc_ref.shape, _f32)
    p_ref[...] = jnp.zeros(p_ref.shape, _bf16)
    alpha_ref[...] = jnp.ones(alpha_ref.shape, _f32)
    qpos = t0 + _lane_iota((1, 4 * SEL_Q)) % SEL_Q

    def scores(kt):
        version = (kt * (tk // SEL_BLOCK)) // HEAD_DIM
        return _dot_t(ka_ref[0, 0, pl.ds(pl.multiple_of(kt * tk, tk), tk), :], qs_ref[version])

    def softmax_step(s):
        m_old = m_ref[...]
        m_new = jnp.maximum(m_old, jnp.max(s, axis=0, keepdims=True))
        m_ref[...] = m_new
        return jnp.exp2(s - m_new).astype(_bf16), jnp.exp2(m_old - m_new)

    def accumulate(kt, alpha, p):
        acc_ref[...] = alpha * acc_ref[...] + _dot(vat_ref[0, 0, kt], p)

    def trip(kt, carry):
        accumulate(jnp.maximum(kt - 1, 0), alpha_ref[...], p_ref[...])
        s = s_ref[...]
        s_ref[...] = scores(kt + 1)
        p, alpha = softmax_step(s)
        p_ref[...] = p
        alpha_ref[...] = alpha
        return carry

    n_full = t0 // tk
    s_ref[...] = scores(0)

    def two_trips(kp, carry):
        trip(2 * kp, carry)
        return trip(2 * kp + 1, carry)

    lax.fori_loop(0, n_full // 2, two_trips, 0)

    @pl.when(n_full % 2 == 1)
    def _():
        trip(n_full - 1, 0)

    accumulate(jnp.maximum(n_full - 1, 0), alpha_ref[...], p_ref[...])
    kpos = n_full * tk + lax.broadcasted_iota(jnp.int32, (tk, 1), 0)
    p, alpha = softmax_step(jnp.where(kpos <= qpos, s_ref[...], NEG_INF))
    accumulate(n_full, alpha, p)
    acc = acc_ref[...]
    o = acc[0:HEAD_DIM] * (1.0 / acc[HEAD_DIM:2 * HEAD_DIM])
    gates = jax.nn.sigmoid(ng_ref[...].astype(_f32).T[0:16, :])
    ys = [gates[3 * r + 1:3 * r + 2] * o[:, r * SEL_Q:(r + 1) * SEL_Q] for r in range(4)]
    o_ref[:, 0:LANES] = (yp_ref[:, 0:LANES] + jnp.concatenate(ys[0:2], axis=0).T).astype(o_ref.dtype)
    o_ref[:, LANES:2 * LANES] = (yp_ref[:, LANES:2 * LANES] + jnp.concatenate(ys[2:4], axis=0).T).astype(o_ref.dtype)


def _nsa_sel_stage(nq_r, mbias, k_aug, vs_t, proj, ypart, batch, seq):
    n = nq_r.shape[0]
    nblk = seq // SEL_Q
    qrow = lambda b, g, i: b * nblk + i
    return pl.pallas_call(
        _nsa_sel_kernel,
        grid=(batch, 2, nblk),
        in_specs=[pl.BlockSpec((SEL_Q, 2 * LANES), lambda b, g, i: (qrow(b, g, i), g)),
                  pl.BlockSpec((1, SEL_Q, LANES), lambda b, g, i: (g, qrow(b, g, i), 0)),
                  pl.BlockSpec((1, 1, seq, LANES), lambda b, g, i: (g, b, 0, 0)),
                  pl.BlockSpec((1, 1, seq // SEL_TK, LANES, SEL_TK), lambda b, g, i: (g, b, 0, 0, 0)),
                  pl.BlockSpec((SEL_Q, LANES), lambda b, g, i: (qrow(b, g, i), CB_NG + g)),
                  pl.BlockSpec((SEL_Q, 2 * LANES), lambda b, g, i: (qrow(b, g, i), g))],
        out_specs=pl.BlockSpec((SEL_Q, 2 * LANES), lambda b, g, i: (qrow(b, g, i), g)),
        out_shape=jax.ShapeDtypeStruct((n, 4 * LANES), _bf16),
        scratch_shapes=[pltpu.VMEM((2, 4 * SEL_Q, LANES), _bf16),
                        pltpu.VMEM((1, 4 * SEL_Q), _f32),
                        pltpu.VMEM((LANES, 4 * SEL_Q), _f32),
                        pltpu.VMEM((SEL_TK, 4 * SEL_Q), _f32),
                        pltpu.VMEM((SEL_TK, 4 * SEL_Q), _bf16),
                        pltpu.VMEM((1, 4 * SEL_Q), _f32)],
        compiler_params=_cparams(("arbitrary", "arbitrary", "arbitrary")),
        name="nsa_selected",
    )(nq_r, mbias, k_aug.reshape(2, batch, seq, LANES), vs_t.reshape(2, batch, seq // SEL_TK, LANES, SEL_TK), proj, ypart)


def _sb_kernel(q_ref, k_ref, v_ref, o_ref, qs_ref, tail_ref, acc_ref):
    i = pl.program_id(1)
    h0 = _half0()
    heads = 2 * (q_ref.shape[1] // LANES)
    for h in range(heads):
        q = q_ref[:, (h // 2) * LANES:(h // 2 + 1) * LANES]
        keep = h0 if h % 2 == 0 else jnp.logical_not(h0)
        qs_ref[h] = jnp.where(keep, q, jnp.zeros_like(q)) * QK_SCALE
    tail_ref[...] = jnp.zeros(tail_ref.shape, _f32)
    acc_ref[...] = jnp.zeros(acc_ref.shape, _f32)
    rloc = lax.broadcasted_iota(jnp.int32, (Q_BLOCK, Q_BLOCK), 0)
    cloc = lax.broadcasted_iota(jnp.int32, (Q_BLOCK, Q_BLOCK), 1)
    later = jnp.where(rloc > cloc, 1.0, 0.0).astype(_bf16)

    def key_block(j, diagonal):
        k0 = pl.multiple_of(j * Q_BLOCK, Q_BLOCK)
        past = cloc < rloc
        cols = [slice((h // 2) * LANES, (h // 2 + 1) * LANES) for h in range(heads)]
        zs = [_dot_t(qs_ref[h], k_ref[0, pl.ds(k0, Q_BLOCK), cols[h]]) for h in range(heads)]
        log_beta, log_keep = [], []
        for z in zs:
            sp = jnp.maximum(z, 0.0) + jnp.log(1.0 + jnp.exp(-jnp.abs(z)))
            log_beta.append(z - sp)
            log_keep.append(jnp.where(past, -sp, 0.0) if diagonal else -sp)
        inner = []
        for lk in log_keep:
            hi = lk.astype(_bf16)
            lo = (lk - hi.astype(_f32)).astype(_bf16)
            inner.append(_dot(hi, later) + _dot(lo, later))
        probs = []
        for h in range(heads):
            a = jnp.exp(log_beta[h] + inner[h] + tail_ref[h])
            probs.append((jnp.where(past, a, 0.0) if diagonal else a).astype(_bf16))
        worst = jnp.full((Q_BLOCK, 1), -jnp.inf, _f32)
        for h in range(heads):
            acc_ref[h] = acc_ref[h] + _dot(probs[h], v_ref[0, pl.ds(k0, Q_BLOCK), cols[h]])
            tail = tail_ref[h] + jnp.sum(log_keep[h], axis=-1, keepdims=True)
            tail_ref[h] = tail
            worst = jnp.maximum(worst, tail)
        return jnp.max(worst)

    def cond(c):
        j, worst_tail = c
        return (j >= 0) & (worst_tail > SB_TAIL_CUTOFF)

    def body(c):
        j, _ = c
        return j - 1, key_block(j, False)

    lax.while_loop(cond, body, (i - 1, key_block(i, True)))
    for p in range(heads // 2):
        o_ref[:, p * LANES:(p + 1) * LANES] = jnp.where(h0, acc_ref[2 * p], acc_ref[2 * p + 1]).astype(o_ref.dtype)


def _sb_stage(proj, batch, seq):
    n = proj.shape[0]
    nblk = seq // Q_BLOCK
    width = 4 * LANES
    proj3 = proj.reshape(batch, seq, PROJ_W)
    return pl.pallas_call(
        _sb_kernel,
        grid=(batch, nblk),
        in_specs=[pl.BlockSpec((Q_BLOCK, width), lambda b, i: (b * nblk + i, CB_SBQ // 4)),
                  pl.BlockSpec((1, seq, width), lambda b, i: (b, 0, CB_SBK // 4)),
                  pl.BlockSpec((1, seq, width), lambda b, i: (b, 0, CB_SBV // 4))],
        out_specs=pl.BlockSpec((Q_BLOCK, width), lambda b, i: (b * nblk + i, 0)),
        out_shape=jax.ShapeDtypeStruct((n, width), _bf16),
        scratch_shapes=[pltpu.VMEM((8, Q_BLOCK, LANES), _bf16),
                        pltpu.VMEM((8, Q_BLOCK, 1), _f32),
                        pltpu.VMEM((8, Q_BLOCK, LANES), _f32)],
        compiler_params=_cparams(("arbitrary", "arbitrary")),
        name="stick_breaking",
    )(proj, proj3, proj3)


def _layer_norm(x, g, b):
    mu = jnp.mean(x, axis=-1, keepdims=True)
    xc = x - mu
    var = jnp.mean(xc * xc, axis=-1, keepdims=True)
    return xc * lax.rsqrt(var + LN_EPS) * g + b


def _merge_kernel(x_ref, ysb_ref, yns_ref, mg0_ref, mg1_ref, wsb_ref, wns_ref, wo_ref, g_ref, b_ref,
                  wr_ref, br_ref, h_ref, idx_ref, gw_ref):
    m0 = jax.nn.sigmoid(mg0_ref[...].astype(_f32))
    m1 = jax.nn.sigmoid(mg1_ref[...].astype(_f32))
    merged = m0 * _dot(ysb_ref[...], wsb_ref[...]) + m1 * _dot(yns_ref[...], wns_ref[...])
    pre = DEEPNORM_ALPHA * x_ref[...] + _dot(merged.astype(_bf16), wo_ref[...])
    h = _layer_norm(pre, g_ref[...], b_ref[...])
    tm = h.shape[0]
    for s in range(SLAB):
        h_ref[pl.ds(s, tm, stride=SLAB), :] = h[:, s * LANES:(s + 1) * LANES]
    h_hi = h.astype(_bf16)
    h_lo = (h - h_hi.astype(_f32)).astype(_bf16)
    w = wr_ref[...]
    w_hi = w.astype(_bf16)
    w_lo = (w - w_hi.astype(_f32)).astype(_bf16)
    logits = (_dot(h_hi, w_hi) + (_dot(h_hi, w_lo) + _dot(h_lo, w_hi))) + br_ref[...]
    lane = _lane_iota((1, LANES))
    lanef = lane.astype(_f32)
    lg = jnp.where(lane < N_EXPERTS, logits, TAKEN)
    vals, idxs = [], []
    for _ in range(TOP_K):
        m = jnp.max(lg, axis=-1, keepdims=True)
        first = jnp.min(jnp.where(lg == m, lanef, float(LANES)), axis=-1, keepdims=True)
        vals.append(m)
        idxs.append(first)
        lg = jnp.where(lanef == first, TAKEN, lg)
    es = [jnp.exp(v - vals[0]) for v in vals]
    inv = 1.0 / (es[0] + es[1] + es[2] + es[3])
    idx_t = jnp.zeros(lg.shape, _f32)
    gw_t = jnp.zeros(lg.shape, _f32)
    for k in range(TOP_K):
        idx_t = jnp.where(lane == k, idxs[k], idx_t)
        gw_t = jnp.where(lane == k, es[k] * inv, gw_t)
    idx_ref[...] = idx_t[:, :TOP_K].astype(jnp.int32)
    gw_ref[...] = gw_t[:, :TOP_K]


def _merge_stage(x2, y_sb, y_nsa, proj, wsb, wns, wo, g1, b1, wr, br):
    n = x2.shape[0]
    tm = 512
    row = lambda w: pl.BlockSpec((tm, w), lambda i: (i, 0))
    full = lambda a: pl.BlockSpec(a.shape, lambda i: (0,) * a.ndim)
    return pl.pallas_call(
        _merge_kernel,
        grid=(n // tm,),
        in_specs=[row(D_MODEL), row(4 * LANES), row(4 * LANES),
                  pl.BlockSpec((tm, D_MODEL), lambda i: (i, CB_MG // 8)),
                  pl.BlockSpec((tm, D_MODEL), lambda i: (i, CB_MG // 8 + 1)),
                  full(wsb), full(wns), full(wo), full(g1), full(b1), full(wr), full(br)],
        out_specs=[pl.BlockSpec((tm * SLAB, LANES), lambda i: (i, 0)), row(TOP_K), row(TOP_K)],
        out_shape=[jax.ShapeDtypeStruct((n * SLAB, LANES), _f32),
                   jax.ShapeDtypeStruct((n, TOP_K), jnp.int32),
                   jax.ShapeDtypeStruct((n, TOP_K), _f32)],
        compiler_params=_cparams(("arbitrary",)),
        name="merge_ln1_router",
    )(x2, y_sb, y_nsa, proj, proj, wsb, wns, wo, g1, b1, wr, br)


def _prep_w_in(w):
    main = w[:, :CB_NG * LANES]
    ng = w[:, CB_NG * LANES:CB_NG * LANES + 24]
    mg = w[:, CB_NG * LANES + 24:]
    pad = jnp.zeros((w.shape[0], LANES - 12), w.dtype)
    return jnp.concatenate([main, ng[:, :12], pad, ng[:, 12:], pad, mg], axis=1).astype(_bf16)


def _rope_tables(seq):
    half = HEAD_DIM // 2
    inv_freq = ROPE_THETA ** (-jnp.arange(half, dtype=_f32) / half)
    ang = jnp.arange(seq, dtype=_f32)[:, None] * inv_freq[None, :]
    cos = jnp.cos(ang)
    sin = jnp.sin(ang)
    cos128 = jnp.concatenate([cos, cos, cos, cos], axis=1)
    sin128 = jnp.concatenate([-sin, sin, -sin, sin], axis=1)
    return cos128, sin128


def _stencil(nc):
    n = np.arange(nc)[:, None]
    j = np.arange(LANES)[None, :]
    ratio = SEL_BLOCK // CMP_STRIDE
    ok = (n >= ratio * j - 1) & (n <= ratio * j + ratio - 1) & (n < nc - 1)
    return jnp.asarray(ok.astype(np.float32).T, dtype=_bf16)


def _attention_half(x, w_in, cmp_pe_k, cmp_w1_k, cmp_w2_k, cmp_pe_v, cmp_w1_v, cmp_w2_v,
                    w_proj_sb, w_proj_nsa, w_out, ln1_g, ln1_b, w_router, b_router):
    batch, seq, _ = x.shape
    n = batch * seq
    x2 = x.reshape(n, D_MODEL)
    proj = _in_proj(x2, _prep_w_in(w_in))
    cos, sin_signed = _rope_tables(seq)
    nq_r, kc_r, vc_r, k_aug, vs_t, kw_d, vw_t = _rope_stage(proj, cos, sin_signed, seq)
    kc_d, vc_t = _compress_stage(kc_r, vc_r, _compress_weights(cmp_pe_k, cmp_w1_k, cmp_w2_k),
                                 _compress_weights(cmp_pe_v, cmp_w1_v, cmp_w2_v), batch, seq)
    ypart, mbias = _nsa_cw_stage(nq_r, kc_d, vc_t, kw_d, vw_t, proj, _stencil(seq // CMP_STRIDE), batch, seq)
    y_nsa = _nsa_sel_stage(nq_r, mbias, k_aug, vs_t, proj, ypart, batch, seq)
    y_sb = _sb_stage(proj, batch, seq)
    wr = jnp.pad(w_router.astype(_f32), ((0, 0), (0, LANES - N_EXPERTS)))
    br = jnp.pad(b_router.astype(_f32), (0, LANES - N_EXPERTS)).reshape(1, LANES)
    return _merge_stage(x2, y_sb, y_nsa, proj, w_proj_sb.astype(_bf16), w_proj_nsa.astype(_bf16),
                        w_out.astype(_bf16), ln1_g.reshape(1, -1), ln1_b.reshape(1, -1), wr, br)


def _row_copy(src, src_row, dst, dst_row, sem):
    return pltpu.make_async_copy(src.at[pl.ds(src_row * SLAB, SLAB)], dst.at[pl.ds(dst_row * SLAB, SLAB)], sem)


def _dispatch_kernel(pad_start_ref, pad_len_ref, dest_ref, h_hbm, buf_ref, hbuf, zslab, in_sems, row_sems, zero_sem,
                     *, tokens):
    i = pl.program_id(0)
    last = pl.num_programs(0) - 1
    slot = i % 2
    block_rows = tokens * SLAB

    def block_copy(step, s):
        return pltpu.make_async_copy(h_hbm.at[pl.ds(step * block_rows, block_rows)], hbuf.at[s], in_sems.at[s])

    def wait_rows(s):
        for _ in range(TOP_K):
            pltpu.make_async_copy(hbuf.at[s], buf_ref.at[pl.ds(0, block_rows)], row_sems.at[s]).wait()

    def padding_rows(act):
        def per_expert(e, c):
            def per_row(r, c2):
                act(_row_copy(zslab, 0, buf_ref, pad_start_ref[e] + r, zero_sem))
                return c2
            lax.fori_loop(0, pad_len_ref[e], per_row, 0)
            return c
        lax.fori_loop(0, N_EXPERTS, per_expert, 0)

    @pl.when(i == 0)
    def _():
        block_copy(0, 0).start()
        zslab[...] = jnp.zeros(zslab.shape, _f32)
        padding_rows(lambda cp: cp.start())

    @pl.when(i > 0)
    def _():
        wait_rows(1 - slot)

    @pl.when(i < last)
    def _():
        block_copy(i + 1, 1 - slot).start()

    block_copy(i, slot).wait()
    src = hbuf.at[slot]

    def issue(tg, c):
        for u in range(ISSUE_UNROLL):
            t = tg * ISSUE_UNROLL + u
            for k in range(TOP_K):
                _row_copy(src, t, buf_ref, dest_ref[t * TOP_K + k], row_sems.at[slot]).start(priority=k % 2)
        return c

    lax.fori_loop(0, tokens // ISSUE_UNROLL, issue, 0)

    @pl.when(i == last)
    def _():
        wait_rows(slot)
        padding_rows(lambda cp: cp.wait())


def _dispatch_stage(dest, pad_start, pad_len, h_slab, buf_rows):
    n = h_slab.shape[0] // SLAB
    tokens = 256
    grid_spec = pltpu.PrefetchScalarGridSpec(
        num_scalar_prefetch=2,
        grid=(n // tokens,),
        in_specs=[pl.BlockSpec((tokens * TOP_K,), lambda i, ps, pn: (i,), memory_space=pltpu.SMEM),
                  pl.BlockSpec(memory_space=pl.ANY)],
        out_specs=pl.BlockSpec(memory_space=pl.ANY),
        scratch_shapes=[pltpu.VMEM((2, tokens * SLAB, LANES), _f32),
                        pltpu.VMEM((SLAB, LANES), _f32),
                        pltpu.SemaphoreType.DMA((2,)),
                        pltpu.SemaphoreType.DMA((2,)),
                        pltpu.SemaphoreType.DMA(())])
    return pl.pallas_call(
        functools.partial(_dispatch_kernel, tokens=tokens),
        grid_spec=grid_spec,
        out_shape=jax.ShapeDtypeStruct((buf_rows * SLAB, LANES), _f32),
        compiler_params=pltpu.CompilerParams(dimension_semantics=("arbitrary",), has_side_effects=True),
        name="moe_dispatch",
    )(pad_start, pad_len, dest, h_slab)


def _slab_load(ref, rows):
    return jnp.concatenate([ref[pl.ds(s, rows, stride=SLAB), :] for s in range(SLAB)], axis=1)


def _expert_kernel(ce_ref, used_ref, x_ref, wgu_ref, bgu_ref, wd_ref, bd_ref, o_ref, wgu_bf, wd_bf):
    c = pl.program_id(0)
    new_expert = (c == 0) | (ce_ref[c] != ce_ref[jnp.maximum(c - 1, 0)])

    @pl.when(new_expert & (c < used_ref[0]))
    def _():
        wgu_bf[...] = wgu_ref[0].astype(_bf16)
        wd_bf[...] = wd_ref[0].astype(_bf16)

    @pl.when(c < used_ref[0])
    def _():
        x = _slab_load(x_ref, MOE_ROWS).astype(_bf16)
        gu = _dot(x, wgu_bf[...]) + bgu_ref[0]
        gate = jnp.minimum(gu[:, :D_MODEL], SWIGLU_LIMIT)
        up = jnp.clip(gu[:, D_MODEL:], -SWIGLU_LIMIT, SWIGLU_LIMIT)
        h = gate * jax.nn.sigmoid(SWIGLU_ALPHA * gate) * (up + 1.0)
        y = _dot(h.astype(_bf16), wd_bf[...]) + bd_ref[0]
        for s in range(SLAB):
            o_ref[pl.ds(s, MOE_ROWS, stride=SLAB), :] = y[:, s * LANES:(s + 1) * LANES]

    @pl.when(c >= used_ref[0])
    def _():
        o_ref[...] = jnp.zeros(o_ref.shape, o_ref.dtype)


def _expert_stage(chunk_expert, n_used, buf, wgu, bgu, wd, bd):
    n_chunks = chunk_expert.shape[0]
    rows = MOE_ROWS * SLAB
    grid_spec = pltpu.PrefetchScalarGridSpec(
        num_scalar_prefetch=2,
        grid=(n_chunks,),
        in_specs=[pl.BlockSpec((rows, LANES), lambda c, ce, nu: (jnp.minimum(c, nu[0] - 1), 0)),
                  pl.BlockSpec((1, D_MODEL, 2 * D_MODEL), lambda c, ce, nu: (ce[c], 0, 0)),
                  pl.BlockSpec((1, 1, 2 * D_MODEL), lambda c, ce, nu: (ce[c], 0, 0)),
                  pl.BlockSpec((1, D_MODEL, D_MODEL), lambda c, ce, nu: (ce[c], 0, 0)),
                  pl.BlockSpec((1, 1, D_MODEL), lambda c, ce, nu: (ce[c], 0, 0))],
        out_specs=pl.BlockSpec((rows, LANES), lambda c, ce, nu: (c, 0)),
        scratch_shapes=[pltpu.VMEM((D_MODEL, 2 * D_MODEL), _bf16), pltpu.VMEM((D_MODEL, D_MODEL), _bf16)])
    return pl.pallas_call(
        _expert_kernel,
        grid_spec=grid_spec,
        out_shape=jax.ShapeDtypeStruct(buf.shape, _f32),
        compiler_params=_cparams(("arbitrary",)),
        name="moe_experts",
    )(chunk_expert, n_used, buf, wgu, bgu, wd, bd)


def _combine_kernel(dest_ref, dest_next_ref, gw_ref, h_ref, eo_ref, g_ref, b_ref, o_ref, gbuf, ysl, sems, *, tokens):
    i = pl.program_id(0)
    slot = i % 2
    count = tokens * TOP_K

    def gather(idx_ref, into):
        def issue(tg, c):
            for u in range(ISSUE_UNROLL * TOP_K):
                j = tg * (ISSUE_UNROLL * TOP_K) + u
                _row_copy(eo_ref, idx_ref[j], gbuf.at[into], j, sems.at[into]).start(priority=u % 2)
            return c
        lax.fori_loop(0, tokens // ISSUE_UNROLL, issue, 0)

    @pl.when(i == 0)
    def _():
        gather(dest_ref, 0)

    @pl.when(i + 1 < pl.num_programs(0))
    def _():
        gather(dest_next_ref, 1 - slot)

    pltpu.make_async_copy(eo_ref.at[pl.ds(0, count * SLAB)], gbuf.at[slot], sems.at[slot]).wait()
    rows = gbuf.at[slot]

    unroll = 4

    def token_group(tg, c):
        for u in range(unroll):
            t = tg * unroll + u
            acc = DEEPNORM_ALPHA * h_ref[pl.ds(pl.multiple_of(t * SLAB, SLAB), SLAB), :]
            for k in range(TOP_K):
                j = t * TOP_K + k
                acc = acc + gw_ref[j] * rows[pl.ds(pl.multiple_of(j * SLAB, SLAB), SLAB), :]
            ysl[pl.ds(pl.multiple_of(t * SLAB, SLAB), SLAB), :] = acc
        return c

    lax.fori_loop(0, tokens // unroll, token_group, 0)
    o_ref[...] = _layer_norm(_slab_load(ysl, tokens), g_ref[...], b_ref[...])


def _combine_stage(dest, gate_w, h_slab, expert_out, g2, b2):
    n = h_slab.shape[0] // SLAB
    tokens = 256
    steps = n // tokens
    smem = lambda: pl.BlockSpec((tokens * TOP_K,), lambda i: (i,), memory_space=pltpu.SMEM)
    return pl.pallas_call(
        functools.partial(_combine_kernel, tokens=tokens),
        grid=(steps,),
        in_specs=[smem(),
                  pl.BlockSpec((tokens * TOP_K,), lambda i: (jnp.minimum(i + 1, steps - 1),), memory_space=pltpu.SMEM),
                  smem(),
                  pl.BlockSpec((tokens * SLAB, LANES), lambda i: (i, 0)),
                  pl.BlockSpec(memory_space=pl.ANY),
                  pl.BlockSpec((1, D_MODEL), lambda i: (0, 0)),
                  pl.BlockSpec((1, D_MODEL), lambda i: (0, 0))],
        out_specs=pl.BlockSpec((tokens, D_MODEL), lambda i: (i, 0)),
        out_shape=jax.ShapeDtypeStruct((n, D_MODEL), _f32),
        scratch_shapes=[pltpu.VMEM((2, tokens * TOP_K * SLAB, LANES), _f32),
                        pltpu.VMEM((tokens * SLAB, LANES), _f32),
                        pltpu.SemaphoreType.DMA((2,))],
        compiler_params=_cparams(("arbitrary",)),
        name="moe_combine_ln2",
    )(dest, dest, gate_w, h_slab, expert_out, g2, b2)


def _dispatch_plan(top_idx):
    m = top_idx.size
    e_flat = top_idx.reshape(m)
    onehot = (e_flat[:, None] == jnp.arange(N_EXPERTS, dtype=jnp.int32)[None, :]).astype(jnp.int32)
    csum = jnp.cumsum(onehot, axis=0)
    counts = csum[-1]
    padded = (counts + MOE_ROWS - 1) // MOE_ROWS * MOE_ROWS
    pends = jnp.cumsum(padded)
    pstarts = pends - padded
    dest = jnp.sum(onehot * (csum - 1 + pstarts[None, :]), axis=1).astype(jnp.int32)
    n_chunks = m // MOE_ROWS + N_EXPERTS
    chunk_start = jnp.arange(n_chunks, dtype=jnp.int32) * MOE_ROWS
    chunk_expert = jnp.minimum(jnp.sum((chunk_start[:, None] >= pends[None, :]).astype(jnp.int32), axis=1), N_EXPERTS - 1)
    n_used = (pends[-1] // MOE_ROWS).astype(jnp.int32).reshape(1)
    pad_start = (pstarts + counts).astype(jnp.int32)
    pad_len = (padded - counts).astype(jnp.int32)
    return dest, chunk_expert, n_used, pad_start, pad_len, n_chunks * MOE_ROWS


def _moe_half(h_slab, top_idx, gate_w, w_gate_up, b_gate_up, w_down, b_down, ln2_g, ln2_b):
    dest, chunk_expert, n_used, pad_start, pad_len, buf_rows = _dispatch_plan(top_idx)
    buf = _dispatch_stage(dest, pad_start, pad_len, h_slab, buf_rows)
    expert_out = _expert_stage(chunk_expert, n_used, buf, w_gate_up, b_gate_up.reshape(N_EXPERTS, 1, -1),
                               w_down, b_down.reshape(N_EXPERTS, 1, -1))
    return _combine_stage(dest, gate_w.reshape(-1), h_slab, expert_out, ln2_g.reshape(1, -1), ln2_b.reshape(1, -1))


def kernel(x, w_in, cmp_pe_k, cmp_w1_k, cmp_w2_k, cmp_pe_v, cmp_w1_v, cmp_w2_v, w_proj_sb, w_proj_nsa, w_out,
           ln1_g, ln1_b, w_router, b_router, w_gate_up, b_gate_up, w_down, b_down, ln2_g, ln2_b):
    assert w_in.shape[0] == 1, "single-layer block"
    batch, seq, _ = x.shape
    assert seq % 512 == 0 and seq // SEL_BLOCK <= LANES and seq >= WINDOW + Q_BLOCK
    h_slab, top_idx, gate_w = _attention_half(
        x, w_in[0], cmp_pe_k[0], cmp_w1_k[0], cmp_w2_k[0], cmp_pe_v[0], cmp_w1_v[0], cmp_w2_v[0],
        w_proj_sb[0], w_proj_nsa[0], w_out[0], ln1_g[0], ln1_b[0], w_router[0], b_router[0])
    out = _moe_half(h_slab, top_idx, gate_w, w_gate_up[0], b_gate_up[0], w_down[0], b_down[0], ln2_g[0], ln2_b[0])
    return out.reshape(batch, seq, D_MODEL)
```

```python
import functools

import numpy as np
import jax
import jax.numpy as jnp
from jax import lax
from jax.experimental import pallas as pl
from jax.experimental.pallas import tpu as pltpu

D_MODEL = 1024
HEAD_DIM = 64
LANES = 128
Q_BLOCK = 128
CMP_BLOCK = 32
CMP_STRIDE = 16
SEL_BLOCK = 64
SEL_TOPK = 16
WINDOW = 512
ROPE_THETA = 10000.0
N_EXPERTS = 32
TOP_K = 4
SWIGLU_LIMIT = 7.0
SWIGLU_ALPHA = 1.702
LN_EPS = 1e-5
NEG_INF = -1e30
TAKEN = -3e38
DEEPNORM_ALPHA = 2.0 ** 0.25
QK_SCALE = HEAD_DIM ** -0.5
LOG2E = 1.4426950408889634

CB_SBQ, CB_SBK, CB_SBV, CB_NQ = 0, 4, 8, 12
CB_KC, CB_VC, CB_KS, CB_VS, CB_KW, CB_VW = 16, 17, 18, 19, 20, 21
CB_NG = 22
CB_MG = 24
PROJ_W = 40 * LANES

SB_TAIL_CUTOFF = -110.0

SEL_Q = 512
SEL_TK = 512
MOE_ROWS = 512
SLAB = D_MODEL // LANES
ISSUE_UNROLL = 4
VMEM_LIMIT = 56 * 1024 * 1024

_bf16 = jnp.bfloat16
_f32 = jnp.float32


def _cparams(sem):
    return pltpu.CompilerParams(dimension_semantics=sem, vmem_limit_bytes=VMEM_LIMIT)


def _dot_t(a, b):
    return lax.dot_general(a, b, (((1,), (1,)), ((), ())), preferred_element_type=_f32)


def _dot(a, b):
    return jnp.dot(a, b, preferred_element_type=_f32)


def _lane_iota(shape):
    return lax.broadcasted_iota(jnp.int32, shape, len(shape) - 1)


def _half0(shape=(1, LANES)):
    return _lane_iota(shape) < HEAD_DIM


def _in_proj_kernel(x_ref, w_ref, o_ref):
    o_ref[...] = _dot(x_ref[...].astype(_bf16), w_ref[...]).astype(o_ref.dtype)


def _in_proj(x2, w):
    n = x2.shape[0]
    tm, tn = 512, 1280
    return pl.pallas_call(
        _in_proj_kernel,
        grid=(PROJ_W // tn, n // tm),
        in_specs=[pl.BlockSpec((tm, D_MODEL), lambda j, i: (i, 0)),
                  pl.BlockSpec((D_MODEL, tn), lambda j, i: (0, j))],
        out_specs=pl.BlockSpec((tm, tn), lambda j, i: (i, j)),
        out_shape=jax.ShapeDtypeStruct((n, PROJ_W), _bf16),
        compiler_params=_cparams(("arbitrary", "arbitrary")),
        name="in_proj",
    )(x2, w)


def _rope(x, cos, sin_signed):
    first = (_lane_iota((1, LANES)) % HEAD_DIM) < (HEAD_DIM // 2)
    swapped = jnp.where(first, pltpu.roll(x, LANES - HEAD_DIM // 2, 1), pltpu.roll(x, HEAD_DIM // 2, 1))
    return x * cos + swapped * sin_signed


def _dup(x, g):
    other = pltpu.roll(x, HEAD_DIM, 1)
    h0 = _half0()
    return jnp.where(h0, x, other) if g == 0 else jnp.where(h0, other, x)


def _rope_kernel(nq_ref, kc_ref, vc_ref, ks_ref, vs_ref, kw_ref, vw_ref, cos_ref, sin_ref,
                 nq_o, kc_o, vc_o, ka_o, vs_o, kw_o, vw_o, *, blocks_per_seq):
    ts = cos_ref.shape[0]
    cos = cos_ref[...]
    sin = sin_ref[...]
    for c in range(4):
        sl = slice(c * LANES, (c + 1) * LANES)
        nq_o[:, sl] = (_rope(nq_ref[:, sl].astype(_f32), cos, sin) * (QK_SCALE * LOG2E)).astype(_bf16)
    kc_o[...] = _rope(kc_ref[...].astype(_f32), cos, sin).astype(_bf16)
    vc_o[...] = vc_ref[...]
    ks = _rope(ks_ref[...].astype(_f32), cos, sin)
    kw = _rope(kw_ref[...].astype(_f32), cos, sin)
    vs = vs_ref[...].astype(_f32)
    vw = vw_ref[...].astype(_f32)
    pos = (pl.program_id(0) % blocks_per_seq) * ts + lax.broadcasted_iota(jnp.int32, (ts, LANES), 0)
    lane = _lane_iota((ts, LANES))
    onehot = jnp.where((pos // SEL_BLOCK) % HEAD_DIM + HEAD_DIM == lane, 1.0, 0.0)
    for g in range(2):
        ka_o[g] = jnp.where(_half0(), _dup(ks, g), onehot).astype(_bf16)
        vsa = jnp.where(_half0(), _dup(vs, g), 1.0)
        for c in range(ts // SEL_TK):
            vs_o[g, c] = vsa[c * SEL_TK:(c + 1) * SEL_TK, :].T.astype(_bf16)
        kw_o[g] = _dup(kw, g).astype(_bf16)
        vwd = jnp.where(_half0(), _dup(vw, g), 1.0)
        for c in range(ts // LANES):
            vw_o[g, c] = vwd[c * LANES:(c + 1) * LANES, :].T.astype(_bf16)


def _rope_stage(proj, cos, sin_signed, seq):
    n = proj.shape[0]
    ts = 512
    bps = seq // ts
    col = lambda cb: pl.BlockSpec((ts, LANES), lambda i, cb=cb: (i, cb))
    tab = pl.BlockSpec((ts, LANES), lambda i: (i % bps, 0))
    grp = lambda w: pl.BlockSpec((2, ts, w), lambda i: (0, i, 0))
    return pl.pallas_call(
        functools.partial(_rope_kernel, blocks_per_seq=bps),
        grid=(n // ts,),
        in_specs=[pl.BlockSpec((ts, 4 * LANES), lambda i: (i, CB_NQ // 4)),
                  col(CB_KC), col(CB_VC), col(CB_KS), col(CB_VS), col(CB_KW), col(CB_VW), tab, tab],
        out_specs=[pl.BlockSpec((ts, 4 * LANES), lambda i: (i, 0)),
                   pl.BlockSpec((ts, LANES), lambda i: (i, 0)),
                   pl.BlockSpec((ts, LANES), lambda i: (i, 0)),
                   grp(LANES),
                   pl.BlockSpec((2, ts // SEL_TK, LANES, SEL_TK), lambda i: (0, i, 0, 0)),
                   grp(LANES),
                   pl.BlockSpec((2, ts // LANES, LANES, LANES), lambda i: (0, i, 0, 0))],
        out_shape=[jax.ShapeDtypeStruct((n, 4 * LANES), _bf16),
                   jax.ShapeDtypeStruct((n, LANES), _bf16),
                   jax.ShapeDtypeStruct((n, LANES), _bf16),
                   jax.ShapeDtypeStruct((2, n, LANES), _bf16),
                   jax.ShapeDtypeStruct((2, n // SEL_TK, LANES, SEL_TK), _bf16),
                   jax.ShapeDtypeStruct((2, n, LANES), _bf16),
                   jax.ShapeDtypeStruct((2, n // LANES, LANES, LANES), _bf16)],
        compiler_params=_cparams(("arbitrary",)),
        name="rope_layout",
    )(proj, proj, proj, proj, proj, proj, proj, cos, sin_signed)


def _gelu_tanh(x):
    return 0.5 * x * (1.0 + jnp.tanh(0.7978845608028654 * (x + 0.044715 * (x * x * x))))


def _compress_one(x_ref, pe_t, pe_b, w_t, w_b, w2, out_ref, transposed):
    x = x_ref[0].astype(_f32)
    a = _dot((x + pe_t[...]).astype(_bf16), w_t[...])
    b = _dot((x + pe_b[...]).astype(_bf16), w_b[...])
    nc = a.shape[0]
    pre = a + pltpu.roll(b, nc - 1, 0)
    y = _dot(_gelu_tanh(pre).astype(_bf16), w2[...])
    for g in range(2):
        d = _dup(y, g)
        out_ref[0, g] = (d.T if transposed else d).astype(_bf16)


def _compress_kernel(k_ref, v_ref, kpt, kpb, kwt, kwb, kw2, vpt, vpb, vwt, vwb, vw2, ko_ref, vo_ref):
    _compress_one(k_ref, kpt, kpb, kwt, kwb, kw2, ko_ref, False)
    _compress_one(v_ref, vpt, vpb, vwt, vwb, vw2, vo_ref, True)


def _compress_weights(pe, w1, w2):
    half = CMP_BLOCK // 2
    eye = jnp.eye(2, dtype=_f32)
    outs = []
    for part in range(2):
        w = w1[part * half * HEAD_DIM:(part + 1) * half * HEAD_DIM].reshape(half, HEAD_DIM, HEAD_DIM)
        wbd = (w[:, None, :, None, :] * eye[None, :, None, :, None]).reshape(half * 2 * HEAD_DIM, 2 * HEAD_DIM)
        p = jnp.broadcast_to(pe[part * half:(part + 1) * half, None, :], (half, 2, HEAD_DIM)).reshape(1, -1)
        outs.append((p.astype(_f32), wbd.astype(_bf16)))
    w2bd = (w2[None, :, None, :] * eye[:, None, :, None]).reshape(2 * HEAD_DIM, 2 * HEAD_DIM).astype(_bf16)
    (pt, wt), (pb, wb) = outs
    return pt, pb, wt, wb, w2bd


def _compress_stage(kc_r, vc_r, kparams, vparams, batch, seq):
    nc = seq // CMP_STRIDE
    width = CMP_STRIDE * LANES
    xs = pl.BlockSpec((1, nc, width), lambda b: (b, 0, 0))
    full = lambda a: pl.BlockSpec(a.shape, lambda b: (0,) * a.ndim)
    out = pl.BlockSpec((1, 2, nc, LANES), lambda b: (b, 0, 0, 0))
    weights = list(kparams) + list(vparams)
    return pl.pallas_call(
        _compress_kernel,
        grid=(batch,),
        in_specs=[xs, xs] + [full(a) for a in weights],
        out_specs=[out, pl.BlockSpec((1, 2, LANES, nc), lambda b: (b, 0, 0, 0))],
        out_shape=[jax.ShapeDtypeStruct((batch, 2, nc, LANES), _bf16),
                   jax.ShapeDtypeStruct((batch, 2, LANES, nc), _bf16)],
        compiler_params=_cparams(("arbitrary",)),
        name="compress",
    )(kc_r.reshape(batch, nc, width), vc_r.reshape(batch, nc, width), *weights)


def _head_q(q_ref, r):
    q2 = q_ref[:, (r // 2) * LANES:(r // 2 + 1) * LANES]
    keep = _half0() if r % 2 == 0 else jnp.logical_not(_half0())
    return jnp.where(keep, q2, jnp.zeros_like(q2))


def _softmax_over_rows(s):
    m = jnp.max(s, axis=0, keepdims=True)
    e = jnp.exp2(s - m)
    l = jnp.sum(e, axis=0, keepdims=True)
    return e * jnp.where(m > 0.5 * NEG_INF, 1.0 / l, 0.0)


def _pair(even, odd):
    return jnp.where(_half0(), even, odd)


def _nsa_cw_kernel(q_ref, kc_ref, vct_ref, kw_ref, vwt_ref, ng_ref, stt_ref, *rest, first_block, sel_rows):
    yp_ref, mb_ref = rest[-2:]
    t0 = (pl.program_id(2) + first_block) * Q_BLOCK
    qpos = t0 + _lane_iota((1, Q_BLOCK))
    gates = jax.nn.sigmoid(ng_ref[...].astype(_f32).T[0:16, :])
    kc = kc_ref[0, 0]
    vct = vct_ref[0, 0]
    nc = kc.shape[0]
    cend = lax.broadcasted_iota(jnp.int32, (nc, 1), 0) * CMP_STRIDE + (CMP_BLOCK - 1)
    cmask = cend <= qpos
    start = pl.multiple_of(jnp.maximum(t0 - WINDOW, 0), Q_BLOCK)
    wlen = WINDOW + Q_BLOCK
    kwin = kw_ref[0, 0, pl.ds(start, wlen), :]
    kpos = start + lax.broadcasted_iota(jnp.int32, (wlen, 1), 0)
    wmask = (kpos <= qpos) & (qpos - kpos < WINDOW)
    blk0 = start // Q_BLOCK

    qs = [_head_q(q_ref, r) for r in range(4)]
    s_cmp = [_dot_t(kc, q) for q in qs]
    s_win = [_dot_t(kwin, q) for q in qs]
    p_cmp = [_softmax_over_rows(jnp.where(cmask, s, NEG_INF)) for s in s_cmp]
    imp = (p_cmp[0] + p_cmp[1]) + (p_cmp[2] + p_cmp[3])
    e_win = []
    for s in s_win:
        s = jnp.where(wmask, s, NEG_INF)
        e_win.append(jnp.exp2(s - jnp.max(s, axis=0, keepdims=True)).astype(_bf16))
    vwt = jnp.concatenate([vwt_ref[0, 0, blk0 + c] for c in range(wlen // Q_BLOCK)], axis=1)
    yts = []
    for r in range(4):
        o_cmp = _dot(vct, p_cmp[r].astype(_bf16))[0:HEAD_DIM]
        win = _dot(vwt, e_win[r])
        o_win = win[0:HEAD_DIM] * (1.0 / win[HEAD_DIM:2 * HEAD_DIM])
        yts.append(gates[3 * r:3 * r + 1] * o_cmp + gates[3 * r + 2:3 * r + 3] * o_win)
    yp_ref[:, 0:LANES] = jnp.concatenate(yts[0:2], axis=0).T
    yp_ref[:, LANES:2 * LANES] = jnp.concatenate(yts[2:4], axis=0).T

    imp_hi = imp.astype(_bf16)
    rest = imp - imp_hi.astype(_f32)
    imp_mid = rest.astype(_bf16)
    imp_lo = (rest - imp_mid.astype(_f32)).astype(_bf16)
    st = stt_ref[0:sel_rows, :]
    p_slc = _dot(st, imp_hi) + (_dot(st, imp_mid) + _dot(st, imp_lo))
    selj = lax.broadcasted_iota(jnp.int32, (sel_rows, 1), 0)
    blk_t = qpos // SEL_BLOCK
    forced = (selj == 0) | (selj == blk_t) | (selj == blk_t - 1)
    score = jnp.where(forced, TAKEN, jnp.where(selj <= blk_t, p_slc, NEG_INF))
    seljf = selj.astype(_f32)
    picked = forced
    for _ in range(SEL_TOPK - 3):
        m = jnp.max(score, axis=0, keepdims=True)
        first = jnp.min(jnp.where(score == m, seljf, float(LANES)), axis=0, keepdims=True)
        hit = seljf == first
        picked = picked | hit
        score = jnp.where(hit, TAKEN, score)
    bias = jnp.where(picked, 0.0, NEG_INF)
    if sel_rows < LANES:
        bias = jnp.concatenate([bias, jnp.full((LANES - sel_rows, Q_BLOCK), NEG_INF, _f32)], axis=0)
    mb_ref[0] = bias.T.astype(_bf16)


def _nsa_cw_stage(nq_r, kc_d, vc_t, kw_d, vw_t, proj, stencil_t, batch, seq):
    n = nq_r.shape[0]
    nblk = seq // Q_BLOCK
    nc = seq // CMP_STRIDE
    per_block = Q_BLOCK // CMP_STRIDE
    splits = 4 if nblk % 4 == 0 and (nblk // 4 * per_block) % LANES == 0 else 1
    per = nblk // splits
    kw4 = kw_d.reshape(2, batch, seq, LANES)
    vw5 = vw_t.reshape(2, batch, nblk, LANES, LANES)
    outs = ()
    for k in range(splits):
        first = k * per
        nvis = min(nc, (first + per) * per_block)
        qrow = lambda b, g, i, first=first: b * nblk + first + i
        in_specs = [pl.BlockSpec((Q_BLOCK, 2 * LANES), lambda b, g, i, qrow=qrow: (qrow(b, g, i), g)),
                    pl.BlockSpec((1, 1, nvis, LANES), lambda b, g, i: (b, g, 0, 0)),
                    pl.BlockSpec((1, 1, LANES, nvis), lambda b, g, i: (b, g, 0, 0)),
                    pl.BlockSpec((1, 1, seq, LANES), lambda b, g, i: (g, b, 0, 0)),
                    pl.BlockSpec((1, 1, nblk, LANES, LANES), lambda b, g, i: (g, b, 0, 0, 0)),
                    pl.BlockSpec((Q_BLOCK, LANES), lambda b, g, i, qrow=qrow: (qrow(b, g, i), CB_NG + g)),
                    pl.BlockSpec((LANES, nvis), lambda b, g, i: (0, 0))]
        in_specs += [pl.BlockSpec(memory_space=pl.ANY)] * len(outs)
        outs = pl.pallas_call(
            functools.partial(_nsa_cw_kernel, first_block=first,
                              sel_rows=min(LANES, -(-(first + per) * Q_BLOCK // SEL_BLOCK // 8) * 8)),
            grid=(batch, 2, per),
            in_specs=in_specs,
            out_specs=[pl.BlockSpec((Q_BLOCK, 2 * LANES), lambda b, g, i, qrow=qrow: (qrow(b, g, i), g)),
                       pl.BlockSpec((1, Q_BLOCK, LANES), lambda b, g, i, qrow=qrow: (g, qrow(b, g, i), 0))],
            out_shape=[jax.ShapeDtypeStruct((n, 4 * LANES), _f32),
                       jax.ShapeDtypeStruct((2, n, LANES), _bf16)],
            input_output_aliases={7: 0, 8: 1} if outs else {},
            compiler_params=_cparams(("arbitrary", "arbitrary", "arbitrary")),
            name="nsa_cmp_win_select",
        )(nq_r, kc_d, vc_t, kw4, vw5, proj, stencil_t, *outs)
    return outs


def _nsa_sel_kernel(q_ref, mb_ref, ka_ref, vat_ref, ng_ref, yp_ref, o_ref, qs_ref, m_ref, acc_ref, s_ref, p_ref, alpha_ref):
    tk = SEL_TK
    t0 = pl.program_id(2) * SEL_Q
    h0 = _half0()
    mb = mb_ref[0].astype(_f32)
    bias = [pltpu.roll(mb, HEAD_DIM, 1), mb]
    for r in range(4):
        q = q_ref[:, (r // 2) * LANES:(r // 2 + 1) * LANES].astype(_f32)
        if r % 2 == 1:
            q = pltpu.roll(q, HEAD_DIM, 1)
        for v in range(2):
            qs_ref[v, r * SEL_Q:(r + 1) * SEL_Q, :] = jnp.where(h0, q, bias[v]).astype(_bf16)
    m_ref[...] = jnp.full(m_ref.shape, NEG_INF, _f32)
    acc_ref[...] = jnp.zeros(acc_ref.shape, _f32)
    p_ref[...] = jnp.zeros(p_ref.shape, _bf16)
    alpha_ref[...] = jnp.ones(alpha_ref.shape, _f32)
    qpos = t0 + _lane_iota((1, 4 * SEL_Q)) % SEL_Q

    def scores(kt):
        version = (kt * (tk // SEL_BLOCK)) // HEAD_DIM
        return _dot_t(ka_ref[0, 0, pl.ds(pl.multiple_of(kt * tk, tk), tk), :], qs_ref[version])

    def softmax_step(s):
        m_old = m_ref[...]
        m_new = jnp.maximum(m_old, jnp.max(s, axis=0, keepdims=True))
        m_ref[...] = m_new
        return jnp.exp2(s - m_new).astype(_bf16), jnp.exp2(m_old - m_new)

    def accumulate(kt, alpha, p):
        acc_ref[...] = alpha * acc_ref[...] + _dot(vat_ref[0, 0, kt], p)

    def trip(kt, carry):
        accumulate(jnp.maximum(kt - 1, 0), alpha_ref[...], p_ref[...])
        s = s_ref[...]
        s_ref[...] = scores(kt + 1)
        p, alpha = softmax_step(s)
        p_ref[...] = p
        alpha_ref[...] = alpha
        return carry

    n_full = t0 // tk
    s_ref[...] = scores(0)

    def two_trips(kp, carry):
        trip(2 * kp, carry)
        return trip(2 * kp + 1, carry)

    lax.fori_loop(0, n_full // 2, two_trips, 0)

    @pl.when(n_full % 2 == 1)
    def _():
        trip(n_full - 1, 0)

    accumulate(jnp.maximum(n_full - 1, 0), alpha_ref[...], p_ref[...])
    kpos = n_full * tk + lax.broadcasted_iota(jnp.int32, (tk, 1), 0)
    p, alpha = softmax_step(jnp.where(kpos <= qpos, s_ref[...], NEG_INF))
    accumulate(n_full, alpha, p)
    acc = acc_ref[...]
    o = acc[0:HEAD_DIM] * (1.0 / acc[HEAD_DIM:2 * HEAD_DIM])
    gates = jax.nn.sigmoid(ng_ref[...].astype(_f32).T[0:16, :])
    ys = [gates[3 * r + 1:3 * r + 2] * o[:, r * SEL_Q:(r + 1) * SEL_Q] for r in range(4)]
    o_ref[:, 0:LANES] = (yp_ref[:, 0:LANES] + jnp.concatenate(ys[0:2], axis=0).T).astype(o_ref.dtype)
    o_ref[:, LANES:2 * LANES] = (yp_ref[:, LANES:2 * LANES] + jnp.concatenate(ys[2:4], axis=0).T).astype(o_ref.dtype)


def _nsa_sel_stage(nq_r, mbias, k_aug, vs_t, proj, ypart, batch, seq):
    n = nq_r.shape[0]
    nblk = seq // SEL_Q
    qrow = lambda b, g, i: b * nblk + i
    return pl.pallas_call(
        _nsa_sel_kernel,
        grid=(batch, 2, nblk),
        in_specs=[pl.BlockSpec((SEL_Q, 2 * LANES), lambda b, g, i: (qrow(b, g, i), g)),
                  pl.BlockSpec((1, SEL_Q, LANES), lambda b, g, i: (g, qrow(b, g, i), 0)),
                  pl.BlockSpec((1, 1, seq, LANES), lambda b, g, i: (g, b, 0, 0)),
                  pl.BlockSpec((1, 1, seq // SEL_TK, LANES, SEL_TK), lambda b, g, i: (g, b, 0, 0, 0)),
                  pl.BlockSpec((SEL_Q, LANES), lambda b, g, i: (qrow(b, g, i), CB_NG + g)),
                  pl.BlockSpec((SEL_Q, 2 * LANES), lambda b, g, i: (qrow(b, g, i), g))],
        out_specs=pl.BlockSpec((SEL_Q, 2 * LANES), lambda b, g, i: (qrow(b, g, i), g)),
        out_shape=jax.ShapeDtypeStruct((n, 4 * LANES), _bf16),
        scratch_shapes=[pltpu.VMEM((2, 4 * SEL_Q, LANES), _bf16),
                        pltpu.VMEM((1, 4 * SEL_Q), _f32),
                        pltpu.VMEM((LANES, 4 * SEL_Q), _f32),
                        pltpu.VMEM((SEL_TK, 4 * SEL_Q), _f32),
                        pltpu.VMEM((SEL_TK, 4 * SEL_Q), _bf16),
                        pltpu.VMEM((1, 4 * SEL_Q), _f32)],
        compiler_params=_cparams(("arbitrary", "arbitrary", "arbitrary")),
        name="nsa_selected",
    )(nq_r, mbias, k_aug.reshape(2, batch, seq, LANES), vs_t.reshape(2, batch, seq // SEL_TK, LANES, SEL_TK), proj, ypart)


def _sb_kernel(q_ref, k_ref, v_ref, o_ref, qs_ref, tail_ref, acc_ref):
    i = pl.program_id(1)
    h0 = _half0()
    heads = 2 * (q_ref.shape[1] // LANES)
    for h in range(heads):
        q = q_ref[:, (h // 2) * LANES:(h // 2 + 1) * LANES]
        keep = h0 if h % 2 == 0 else jnp.logical_not(h0)
        qs_ref[h] = jnp.where(keep, q, jnp.zeros_like(q)) * QK_SCALE
    tail_ref[...] = jnp.zeros(tail_ref.shape, _f32)
    acc_ref[...] = jnp.zeros(acc_ref.shape, _f32)
    rloc = lax.broadcasted_iota(jnp.int32, (Q_BLOCK, Q_BLOCK), 0)
    cloc = lax.broadcasted_iota(jnp.int32, (Q_BLOCK, Q_BLOCK), 1)
    later = jnp.where(rloc > cloc, 1.0, 0.0).astype(_bf16)

    def key_block(j, diagonal):
        k0 = pl.multiple_of(j * Q_BLOCK, Q_BLOCK)
        past = cloc < rloc
        cols = [slice((h // 2) * LANES, (h // 2 + 1) * LANES) for h in range(heads)]
        zs = [_dot_t(qs_ref[h], k_ref[0, pl.ds(k0, Q_BLOCK), cols[h]]) for h in range(heads)]
        log_beta, log_keep = [], []
        for z in zs:
            sp = jnp.maximum(z, 0.0) + jnp.log(1.0 + jnp.exp(-jnp.abs(z)))
            log_beta.append(z - sp)
            log_keep.append(jnp.where(past, -sp, 0.0) if diagonal else -sp)
        inner = []
        for lk in log_keep:
            hi = lk.astype(_bf16)
            lo = (lk - hi.astype(_f32)).astype(_bf16)
            inner.append(_dot(hi, later) + _dot(lo, later))
        probs = []
        for h in range(heads):
            a = jnp.exp(log_beta[h] + inner[h] + tail_ref[h])
            probs.append((jnp.where(past, a, 0.0) if diagonal else a).astype(_bf16))
        worst = jnp.full((Q_BLOCK, 1), -jnp.inf, _f32)
        for h in range(heads):
            acc_ref[h] = acc_ref[h] + _dot(probs[h], v_ref[0, pl.ds(k0, Q_BLOCK), cols[h]])
            tail = tail_ref[h] + jnp.sum(log_keep[h], axis=-1, keepdims=True)
            tail_ref[h] = tail
            worst = jnp.maximum(worst, tail)
        return jnp.max(worst)

    def cond(c):
        j, worst_tail = c
        return (j >= 0) & (worst_tail > SB_TAIL_CUTOFF)

    def body(c):
        j, _ = c
        return j - 1, key_block(j, False)

    lax.while_loop(cond, body, (i - 1, key_block(i, True)))
    for p in range(heads // 2):
        o_ref[:, p * LANES:(p + 1) * LANES] = jnp.where(h0, acc_ref[2 * p], acc_ref[2 * p + 1]).astype(o_ref.dtype)


def _sb_stage(proj, batch, seq):
    n = proj.shape[0]
    nblk = seq // Q_BLOCK
    width = 4 * LANES
    proj3 = proj.reshape(batch, seq, PROJ_W)
    return pl.pallas_call(
        _sb_kernel,
        grid=(batch, nblk),
        in_specs=[pl.BlockSpec((Q_BLOCK, width), lambda b, i: (b * nblk + i, CB_SBQ // 4)),
                  pl.BlockSpec((1, seq, width), lambda b, i: (b, 0, CB_SBK // 4)),
                  pl.BlockSpec((1, seq, width), lambda b, i: (b, 0, CB_SBV // 4))],
        out_specs=pl.BlockSpec((Q_BLOCK, width), lambda b, i: (b * nblk + i, 0)),
        out_shape=jax.ShapeDtypeStruct((n, width), _bf16),
        scratch_shapes=[pltpu.VMEM((8, Q_BLOCK, LANES), _bf16),
                        pltpu.VMEM((8, Q_BLOCK, 1), _f32),
                        pltpu.VMEM((8, Q_BLOCK, LANES), _f32)],
        compiler_params=_cparams(("arbitrary", "arbitrary")),
        name="stick_breaking",
    )(proj, proj3, proj3)


def _layer_norm(x, g, b):
    mu = jnp.mean(x, axis=-1, keepdims=True)
    xc = x - mu
    var = jnp.mean(xc * xc, axis=-1, keepdims=True)
    return xc * lax.rsqrt(var + LN_EPS) * g + b


def _merge_kernel(x_ref, ysb_ref, yns_ref, mg0_ref, mg1_ref, wsb_ref, wns_ref, wo_ref, g_ref, b_ref,
                  wr_ref, br_ref, h_ref, idx_ref, gw_ref):
    m0 = jax.nn.sigmoid(mg0_ref[...].astype(_f32))
    m1 = jax.nn.sigmoid(mg1_ref[...].astype(_f32))
    merged = m0 * _dot(ysb_ref[...], wsb_ref[...]) + m1 * _dot(yns_ref[...], wns_ref[...])
    pre = DEEPNORM_ALPHA * x_ref[...] + _dot(merged.astype(_bf16), wo_ref[...])
    h = _layer_norm(pre, g_ref[...], b_ref[...])
    tm = h.shape[0]
    for s in range(SLAB):
        h_ref[pl.ds(s, tm, stride=SLAB), :] = h[:, s * LANES:(s + 1) * LANES]
    h_hi = h.astype(_bf16)
    h_lo = (h - h_hi.astype(_f32)).astype(_bf16)
    w = wr_ref[...]
    w_hi = w.astype(_bf16)
    w_lo = (w - w_hi.astype(_f32)).astype(_bf16)
    logits = (_dot(h_hi, w_hi) + (_dot(h_hi, w_lo) + _dot(h_lo, w_hi))) + br_ref[...]
    lane = _lane_iota((1, LANES))
    lanef = lane.astype(_f32)
    lg = jnp.where(lane < N_EXPERTS, logits, TAKEN)
    vals, idxs = [], []
    for _ in range(TOP_K):
        m = jnp.max(lg, axis=-1, keepdims=True)
        first = jnp.min(jnp.where(lg == m, lanef, float(LANES)), axis=-1, keepdims=True)
        vals.append(m)
        idxs.append(first)
        lg = jnp.where(lanef == first, TAKEN, lg)
    es = [jnp.exp(v - vals[0]) for v in vals]
    inv = 1.0 / (es[0] + es[1] + es[2] + es[3])
    idx_t = jnp.zeros(lg.shape, _f32)
    gw_t = jnp.zeros(lg.shape, _f32)
    for k in range(TOP_K):
        idx_t = jnp.where(lane == k, idxs[k], idx_t)
        gw_t = jnp.where(lane == k, es[k] * inv, gw_t)
    idx_ref[...] = idx_t[:, :TOP_K].astype(jnp.int32)
    gw_ref[...] = gw_t[:, :TOP_K]


def _merge_stage(x2, y_sb, y_nsa, proj, wsb, wns, wo, g1, b1, wr, br):
    n = x2.shape[0]
    tm = 512
    row = lambda w: pl.BlockSpec((tm, w), lambda i: (i, 0))
    full = lambda a: pl.BlockSpec(a.shape, lambda i: (0,) * a.ndim)
    return pl.pallas_call(
        _merge_kernel,
        grid=(n // tm,),
        in_specs=[row(D_MODEL), row(4 * LANES), row(4 * LANES),
                  pl.BlockSpec((tm, D_MODEL), lambda i: (i, CB_MG // 8)),
                  pl.BlockSpec((tm, D_MODEL), lambda i: (i, CB_MG // 8 + 1)),
                  full(wsb), full(wns), full(wo), full(g1), full(b1), full(wr), full(br)],
        out_specs=[pl.BlockSpec((tm * SLAB, LANES), lambda i: (i, 0)), row(TOP_K), row(TOP_K)],
        out_shape=[jax.ShapeDtypeStruct((n * SLAB, LANES), _f32),
                   jax.ShapeDtypeStruct((n, TOP_K), jnp.int32),
                   jax.ShapeDtypeStruct((n, TOP_K), _f32)],
        compiler_params=_cparams(("arbitrary",)),
        name="merge_ln1_router",
    )(x2, y_sb, y_nsa, proj, proj, wsb, wns, wo, g1, b1, wr, br)


def _prep_w_in(w):
    main = w[:, :CB_NG * LANES]
    ng = w[:, CB_NG * LANES:CB_NG * LANES + 24]
    mg = w[:, CB_NG * LANES + 24:]
    pad = jnp.zeros((w.shape[0], LANES - 12), w.dtype)
    return jnp.concatenate([main, ng[:, :12], pad, ng[:, 12:], pad, mg], axis=1).astype(_bf16)


def _rope_tables(seq):
    half = HEAD_DIM // 2
    inv_freq = ROPE_THETA ** (-jnp.arange(half, dtype=_f32) / half)
    ang = jnp.arange(seq, dtype=_f32)[:, None] * inv_freq[None, :]
    cos = jnp.cos(ang)
    sin = jnp.sin(ang)
    cos128 = jnp.concatenate([cos, cos, cos, cos], axis=1)
    sin128 = jnp.concatenate([-sin, sin, -sin, sin], axis=1)
    return cos128, sin128


def _stencil(nc):
    n = np.arange(nc)[:, None]
    j = np.arange(LANES)[None, :]
    ratio = SEL_BLOCK // CMP_STRIDE
    ok = (n >= ratio * j - 1) & (n <= ratio * j + ratio - 1) & (n < nc - 1)
    return jnp.asarray(ok.astype(np.float32).T, dtype=_bf16)


def _attention_half(x, w_in, cmp_pe_k, cmp_w1_k, cmp_w2_k, cmp_pe_v, cmp_w1_v, cmp_w2_v,
                    w_proj_sb, w_proj_nsa, w_out, ln1_g, ln1_b, w_router, b_router):
    batch, seq, _ = x.shape
    n = batch * seq
    x2 = x.reshape(n, D_MODEL)
    proj = _in_proj(x2, _prep_w_in(w_in))
    cos, sin_signed = _rope_tables(seq)
    nq_r, kc_r, vc_r, k_aug, vs_t, kw_d, vw_t = _rope_stage(proj, cos, sin_signed, seq)
    kc_d, vc_t = _compress_stage(kc_r, vc_r, _compress_weights(cmp_pe_k, cmp_w1_k, cmp_w2_k),
                                 _compress_weights(cmp_pe_v, cmp_w1_v, cmp_w2_v), batch, seq)
    ypart, mbias = _nsa_cw_stage(nq_r, kc_d, vc_t, kw_d, vw_t, proj, _stencil(seq // CMP_STRIDE), batch, seq)
    y_nsa = _nsa_sel_stage(nq_r, mbias, k_aug, vs_t, proj, ypart, batch, seq)
    y_sb = _sb_stage(proj, batch, seq)
    wr = jnp.pad(w_router.astype(_f32), ((0, 0), (0, LANES - N_EXPERTS)))
    br = jnp.pad(b_router.astype(_f32), (0, LANES - N_EXPERTS)).reshape(1, LANES)
    return _merge_stage(x2, y_sb, y_nsa, proj, w_proj_sb.astype(_bf16), w_proj_nsa.astype(_bf16),
                        w_out.astype(_bf16), ln1_g.reshape(1, -1), ln1_b.reshape(1, -1), wr, br)


def _row_copy(src, src_row, dst, dst_row, sem):
    return pltpu.make_async_copy(src.at[pl.ds(src_row * SLAB, SLAB)], dst.at[pl.ds(dst_row * SLAB, SLAB)], sem)


def _dispatch_kernel(pad_start_ref, pad_len_ref, dest_ref, h_hbm, buf_ref, hbuf, zslab, in_sems, row_sems, zero_sem,
                     *, tokens):
    i = pl.program_id(0)
    last = pl.num_programs(0) - 1
    slot = i % 2
    block_rows = tokens * SLAB

    def block_copy(step, s):
        return pltpu.make_async_copy(h_hbm.at[pl.ds(step * block_rows, block_rows)], hbuf.at[s], in_sems.at[s])

    def wait_rows(s):
        for _ in range(TOP_K):
            pltpu.make_async_copy(hbuf.at[s], buf_ref.at[pl.ds(0, block_rows)], row_sems.at[s]).wait()

    def padding_rows(act):
        def per_expert(e, c):
            def per_row(r, c2):
                act(_row_copy(zslab, 0, buf_ref, pad_start_ref[e] + r, zero_sem))
                return c2
            lax.fori_loop(0, pad_len_ref[e], per_row, 0)
            return c
        lax.fori_loop(0, N_EXPERTS, per_expert, 0)

    @pl.when(i == 0)
    def _():
        block_copy(0, 0).start()
        zslab[...] = jnp.zeros(zslab.shape, _f32)
        padding_rows(lambda cp: cp.start())

    @pl.when(i > 0)
    def _():
        wait_rows(1 - slot)

    @pl.when(i < last)
    def _():
        block_copy(i + 1, 1 - slot).start()

    block_copy(i, slot).wait()
    src = hbuf.at[slot]

    def issue(tg, c):
        for u in range(ISSUE_UNROLL):
            t = tg * ISSUE_UNROLL + u
            for k in range(TOP_K):
                _row_copy(src, t, buf_ref, dest_ref[t * TOP_K + k], row_sems.at[slot]).start(priority=k % 2)
        return c

    lax.fori_loop(0, tokens // ISSUE_UNROLL, issue, 0)

    @pl.when(i == last)
    def _():
        wait_rows(slot)
        padding_rows(lambda cp: cp.wait())


def _dispatch_stage(dest, pad_start, pad_len, h_slab, buf_rows):
    n = h_slab.shape[0] // SLAB
    tokens = 256
    grid_spec = pltpu.PrefetchScalarGridSpec(
        num_scalar_prefetch=2,
        grid=(n // tokens,),
        in_specs=[pl.BlockSpec((tokens * TOP_K,), lambda i, ps, pn: (i,), memory_space=pltpu.SMEM),
                  pl.BlockSpec(memory_space=pl.ANY)],
        out_specs=pl.BlockSpec(memory_space=pl.ANY),
        scratch_shapes=[pltpu.VMEM((2, tokens * SLAB, LANES), _f32),
                        pltpu.VMEM((SLAB, LANES), _f32),
                        pltpu.SemaphoreType.DMA((2,)),
                        pltpu.SemaphoreType.DMA((2,)),
                        pltpu.SemaphoreType.DMA(())])
    return pl.pallas_call(
        functools.partial(_dispatch_kernel, tokens=tokens),
        grid_spec=grid_spec,
        out_shape=jax.ShapeDtypeStruct((buf_rows * SLAB, LANES), _f32),
        compiler_params=pltpu.CompilerParams(dimension_semantics=("arbitrary",), has_side_effects=True),
        name="moe_dispatch",
    )(pad_start, pad_len, dest, h_slab)


def _slab_load(ref, rows):
    return jnp.concatenate([ref[pl.ds(s, rows, stride=SLAB), :] for s in range(SLAB)], axis=1)


def _expert_kernel(ce_ref, used_ref, x_ref, wgu_ref, bgu_ref, wd_ref, bd_ref, o_ref, wgu_bf, wd_bf):
    c = pl.program_id(0)
    new_expert = (c == 0) | (ce_ref[c] != ce_ref[jnp.maximum(c - 1, 0)])

    @pl.when(new_expert & (c < used_ref[0]))
    def _():
        wgu_bf[...] = wgu_ref[0].astype(_bf16)
        wd_bf[...] = wd_ref[0].astype(_bf16)

    @pl.when(c < used_ref[0])
    def _():
        x = _slab_load(x_ref, MOE_ROWS).astype(_bf16)
        gu = _dot(x, wgu_bf[...]) + bgu_ref[0]
        gate = jnp.minimum(gu[:, :D_MODEL], SWIGLU_LIMIT)
        up = jnp.clip(gu[:, D_MODEL:], -SWIGLU_LIMIT, SWIGLU_LIMIT)
        h = gate * jax.nn.sigmoid(SWIGLU_ALPHA * gate) * (up + 1.0)
        y = _dot(h.astype(_bf16), wd_bf[...]) + bd_ref[0]
        for s in range(SLAB):
            o_ref[pl.ds(s, MOE_ROWS, stride=SLAB), :] = y[:, s * LANES:(s + 1) * LANES]

    @pl.when(c >= used_ref[0])
    def _():
        o_ref[...] = jnp.zeros(o_ref.shape, o_ref.dtype)


def _expert_stage(chunk_expert, n_used, buf, wgu, bgu, wd, bd):
    n_chunks = chunk_expert.shape[0]
    rows = MOE_ROWS * SLAB
    grid_spec = pltpu.PrefetchScalarGridSpec(
        num_scalar_prefetch=2,
        grid=(n_chunks,),
        in_specs=[pl.BlockSpec((rows, LANES), lambda c, ce, nu: (jnp.minimum(c, nu[0] - 1), 0)),
                  pl.BlockSpec((1, D_MODEL, 2 * D_MODEL), lambda c, ce, nu: (ce[c], 0, 0)),
                  pl.BlockSpec((1, 1, 2 * D_MODEL), lambda c, ce, nu: (ce[c], 0, 0)),
                  pl.BlockSpec((1, D_MODEL, D_MODEL), lambda c, ce, nu: (ce[c], 0, 0)),
                  pl.BlockSpec((1, 1, D_MODEL), lambda c, ce, nu: (ce[c], 0, 0))],
        out_specs=pl.BlockSpec((rows, LANES), lambda c, ce, nu: (c, 0)),
        scratch_shapes=[pltpu.VMEM((D_MODEL, 2 * D_MODEL), _bf16), pltpu.VMEM((D_MODEL, D_MODEL), _bf16)])
    return pl.pallas_call(
        _expert_kernel,
        grid_spec=grid_spec,
        out_shape=jax.ShapeDtypeStruct(buf.shape, _f32),
        compiler_params=_cparams(("arbitrary",)),
        name="moe_experts",
    )(chunk_expert, n_used, buf, wgu, bgu, wd, bd)


def _combine_kernel(dest_ref, dest_next_ref, gw_ref, h_ref, eo_ref, g_ref, b_ref, o_ref, gbuf, ysl, sems, *, tokens):
    i = pl.program_id(0)
    slot = i % 2
    count = tokens * TOP_K

    def gather(idx_ref, into):
        def issue(tg, c):
            for u in range(ISSUE_UNROLL * TOP_K):
                j = tg * (ISSUE_UNROLL * TOP_K) + u
                _row_copy(eo_ref, idx_ref[j], gbuf.at[into], j, sems.at[into]).start(priority=u % 2)
            return c
        lax.fori_loop(0, tokens // ISSUE_UNROLL, issue, 0)

    @pl.when(i == 0)
    def _():
        gather(dest_ref, 0)

    @pl.when(i + 1 < pl.num_programs(0))
    def _():
        gather(dest_next_ref, 1 - slot)

    pltpu.make_async_copy(eo_ref.at[pl.ds(0, count * SLAB)], gbuf.at[slot], sems.at[slot]).wait()
    rows = gbuf.at[slot]

    unroll = 4

    def token_group(tg, c):
        for u in range(unroll):
            t = tg * unroll + u
            acc = DEEPNORM_ALPHA * h_ref[pl.ds(pl.multiple_of(t * SLAB, SLAB), SLAB), :]
            for k in range(TOP_K):
                j = t * TOP_K + k
                acc = acc + gw_ref[j] * rows[pl.ds(pl.multiple_of(j * SLAB, SLAB), SLAB), :]
            ysl[pl.ds(pl.multiple_of(t * SLAB, SLAB), SLAB), :] = acc
        return c

    lax.fori_loop(0, tokens // unroll, token_group, 0)
    o_ref[...] = _layer_norm(_slab_load(ysl, tokens), g_ref[...], b_ref[...])


def _combine_stage(dest, gate_w, h_slab, expert_out, g2, b2):
    n = h_slab.shape[0] // SLAB
    tokens = 256
    steps = n // tokens
    smem = lambda: pl.BlockSpec((tokens * TOP_K,), lambda i: (i,), memory_space=pltpu.SMEM)
    return pl.pallas_call(
        functools.partial(_combine_kernel, tokens=tokens),
        grid=(steps,),
        in_specs=[smem(),
                  pl.BlockSpec((tokens * TOP_K,), lambda i: (jnp.minimum(i + 1, steps - 1),), memory_space=pltpu.SMEM),
                  smem(),
                  pl.BlockSpec((tokens * SLAB, LANES), lambda i: (i, 0)),
                  pl.BlockSpec(memory_space=pl.ANY),
                  pl.BlockSpec((1, D_MODEL), lambda i: (0, 0)),
                  pl.BlockSpec((1, D_MODEL), lambda i: (0, 0))],
        out_specs=pl.BlockSpec((tokens, D_MODEL), lambda i: (i, 0)),
        out_shape=jax.ShapeDtypeStruct((n, D_MODEL), _f32),
        scratch_shapes=[pltpu.VMEM((2, tokens * TOP_K * SLAB, LANES), _f32),
                        pltpu.VMEM((tokens * SLAB, LANES), _f32),
                        pltpu.SemaphoreType.DMA((2,))],
        compiler_params=_cparams(("arbitrary",)),
        name="moe_combine_ln2",
    )(dest, dest, gate_w, h_slab, expert_out, g2, b2)


def _dispatch_plan(top_idx):
    m = top_idx.size
    e_flat = top_idx.reshape(m)
    onehot = (e_flat[:, None] == jnp.arange(N_EXPERTS, dtype=jnp.int32)[None, :]).astype(jnp.int32)
    csum = jnp.cumsum(onehot, axis=0)
    counts = csum[-1]
    padded = (counts + MOE_ROWS - 1) // MOE_ROWS * MOE_ROWS
    pends = jnp.cumsum(padded)
    pstarts = pends - padded
    dest = jnp.sum(onehot * (csum - 1 + pstarts[None, :]), axis=1).astype(jnp.int32)
    n_chunks = m // MOE_ROWS + N_EXPERTS
    chunk_start = jnp.arange(n_chunks, dtype=jnp.int32) * MOE_ROWS
    chunk_expert = jnp.minimum(jnp.sum((chunk_start[:, None] >= pends[None, :]).astype(jnp.int32), axis=1), N_EXPERTS - 1)
    n_used = (pends[-1] // MOE_ROWS).astype(jnp.int32).reshape(1)
    pad_start = (pstarts + counts).astype(jnp.int32)
    pad_len = (padded - counts).astype(jnp.int32)
    return dest, chunk_expert, n_used, pad_start, pad_len, n_chunks * MOE_ROWS


def _moe_half(h_slab, top_idx, gate_w, w_gate_up, b_gate_up, w_down, b_down, ln2_g, ln2_b):
    dest, chunk_expert, n_used, pad_start, pad_len, buf_rows = _dispatch_plan(top_idx)
    buf = _dispatch_stage(dest, pad_start, pad_len, h_slab, buf_rows)
    expert_out = _expert_stage(chunk_expert, n_used, buf, w_gate_up, b_gate_up.reshape(N_EXPERTS, 1, -1),
                               w_down, b_down.reshape(N_EXPERTS, 1, -1))
    return _combine_stage(dest, gate_w.reshape(-1), h_slab, expert_out, ln2_g.reshape(1, -1), ln2_b.reshape(1, -1))


def kernel(x, w_in, cmp_pe_k, cmp_w1_k, cmp_w2_k, cmp_pe_v, cmp_w1_v, cmp_w2_v, w_proj_sb, w_proj_nsa, w_out,
           ln1_g, ln1_b, w_router, b_router, w_gate_up, b_gate_up, w_down, b_down, ln2_g, ln2_b):
    assert w_in.shape[0] == 1, "single-layer block"
    batch, seq, _ = x.shape
    assert seq % 512 == 0 and seq // SEL_BLOCK <= LANES and seq >= WINDOW + Q_BLOCK
    h_slab, top_idx, gate_w = _attention_half(
        x, w_in[0], cmp_pe_k[0], cmp_w1_k[0], cmp_w2_k[0], cmp_pe_v[0], cmp_w1_v[0], cmp_w2_v[0],
        w_proj_sb[0], w_proj_nsa[0], w_out[0], ln1_g[0], ln1_b[0], w_router[0], b_router[0])
    out = _moe_half(h_slab, top_idx, gate_w, w_gate_up[0], b_gate_up[0], w_down[0], b_down[0], ln2_g[0], ln2_b[0])
    return out.reshape(batch, seq, D_MODEL)
```

```python
import functools

import numpy as np
import jax
import jax.numpy as jnp
from jax import lax
from jax.experimental import pallas as pl
from jax.experimental.pallas import tpu as pltpu

D_MODEL = 1024
HEAD_DIM = 64
LANES = 128
Q_BLOCK = 128
CMP_BLOCK = 32
CMP_STRIDE = 16
SEL_BLOCK = 64
SEL_TOPK = 16
WINDOW = 512
ROPE_THETA = 10000.0
N_EXPERTS = 32
TOP_K = 4
SWIGLU_LIMIT = 7.0
SWIGLU_ALPHA = 1.702
LN_EPS = 1e-5
NEG_INF = -1e30
TAKEN = -3e38
DEEPNORM_ALPHA = 2.0 ** 0.25
QK_SCALE = HEAD_DIM ** -0.5
LOG2E = 1.4426950408889634

CB_SBQ, CB_SBK, CB_SBV, CB_NQ = 0, 4, 8, 12
CB_KC, CB_VC, CB_KS, CB_VS, CB_KW, CB_VW = 16, 17, 18, 19, 20, 21
CB_NG = 22
CB_MG = 24
PROJ_W = 40 * LANES

SB_TAIL_CUTOFF = -110.0

SEL_Q = 512
SEL_TK = 512
MOE_ROWS = 512
SLAB = D_MODEL // LANES
ISSUE_UNROLL = 4
VMEM_LIMIT = 56 * 1024 * 1024

_bf16 = jnp.bfloat16
_f32 = jnp.float32


def _cparams(sem):
    return pltpu.CompilerParams(dimension_semantics=sem, vmem_limit_bytes=VMEM_LIMIT)


def _dot_t(a, b):
    return lax.dot_general(a, b, (((1,), (1,)), ((), ())), preferred_element_type=_f32)


def _dot(a, b):
    return jnp.dot(a, b, preferred_element_type=_f32)


def _lane_iota(shape):
    return lax.broadcasted_iota(jnp.int32, shape, len(shape) - 1)


def _half0(shape=(1, LANES)):
    return _lane_iota(shape) < HEAD_DIM


def _in_proj_kernel(x_ref, w_ref, o_ref):
    o_ref[...] = _dot(x_ref[...].astype(_bf16), w_ref[...]).astype(o_ref.dtype)


def _in_proj(x2, w):
    n = x2.shape[0]
    tm, tn = 512, 1280
    return pl.pallas_call(
        _in_proj_kernel,
        grid=(PROJ_W // tn, n // tm),
        in_specs=[pl.BlockSpec((tm, D_MODEL), lambda j, i: (i, 0)),
                  pl.BlockSpec((D_MODEL, tn), lambda j, i: (0, j))],
        out_specs=pl.BlockSpec((tm, tn), lambda j, i: (i, j)),
        out_shape=jax.ShapeDtypeStruct((n, PROJ_W), _bf16),
        compiler_params=_cparams(("arbitrary", "arbitrary")),
        name="in_proj",
    )(x2, w)


def _rope(x, cos, sin_signed):
    first = (_lane_iota((1, LANES)) % HEAD_DIM) < (HEAD_DIM // 2)
    swapped = jnp.where(first, pltpu.roll(x, LANES - HEAD_DIM // 2, 1), pltpu.roll(x, HEAD_DIM // 2, 1))
    return x * cos + swapped * sin_signed


def _dup(x, g):
    other = pltpu.roll(x, HEAD_DIM, 1)
    h0 = _half0()
    return jnp.where(h0, x, other) if g == 0 else jnp.where(h0, other, x)


def _rope_kernel(nq_ref, kc_ref, vc_ref, ks_ref, vs_ref, kw_ref, vw_ref, cos_ref, sin_ref,
                 nq_o, kc_o, vc_o, ka_o, vs_o, kw_o, vw_o, *, blocks_per_seq):
    ts = cos_ref.shape[0]
    cos = cos_ref[...]
    sin = sin_ref[...]
    for c in range(4):
        sl = slice(c * LANES, (c + 1) * LANES)
        nq_o[:, sl] = (_rope(nq_ref[:, sl].astype(_f32), cos, sin) * (QK_SCALE * LOG2E)).astype(_bf16)
    kc_o[...] = _rope(kc_ref[...].astype(_f32), cos, sin).astype(_bf16)
    vc_o[...] = vc_ref[...]
    ks = _rope(ks_ref[...].astype(_f32), cos, sin)
    kw = _rope(kw_ref[...].astype(_f32), cos, sin)
    vs = vs_ref[...].astype(_f32)
    vw = vw_ref[...].astype(_f32)
    pos = (pl.program_id(0) % blocks_per_seq) * ts + lax.broadcasted_iota(jnp.int32, (ts, LANES), 0)
    lane = _lane_iota((ts, LANES))
    onehot = jnp.where((pos // SEL_BLOCK) % HEAD_DIM + HEAD_DIM == lane, 1.0, 0.0)
    for g in range(2):
        ka_o[g] = jnp.where(_half0(), _dup(ks, g), onehot).astype(_bf16)
        vsa = jnp.where(_half0(), _dup(vs, g), 1.0)
        for c in range(ts // SEL_TK):
            vs_o[g, c] = vsa[c * SEL_TK:(c + 1) * SEL_TK, :].T.astype(_bf16)
        kw_o[g] = _dup(kw, g).astype(_bf16)
        vwd = jnp.where(_half0(), _dup(vw, g), 1.0)
        for c in range(ts // LANES):
            vw_o[g, c] = vwd[c * LANES:(c + 1) * LANES, :].T.astype(_bf16)


def _rope_stage(proj, cos, sin_signed, seq):
    n = proj.shape[0]
    ts = 512
    bps = seq // ts
    col = lambda cb: pl.BlockSpec((ts, LANES), lambda i, cb=cb: (i, cb))
    tab = pl.BlockSpec((ts, LANES), lambda i: (i % bps, 0))
    grp = lambda w: pl.BlockSpec((2, ts, w), lambda i: (0, i, 0))
    return pl.pallas_call(
        functools.partial(_rope_kernel, blocks_per_seq=bps),
        grid=(n // ts,),
        in_specs=[pl.BlockSpec((ts, 4 * LANES), lambda i: (i, CB_NQ // 4)),
                  col(CB_KC), col(CB_VC), col(CB_KS), col(CB_VS), col(CB_KW), col(CB_VW), tab, tab],
        out_specs=[pl.BlockSpec((ts, 4 * LANES), lambda i: (i, 0)),
                   pl.BlockSpec((ts, LANES), lambda i: (i, 0)),
                   pl.BlockSpec((ts, LANES), lambda i: (i, 0)),
                   grp(LANES),
                   pl.BlockSpec((2, ts // SEL_TK, LANES, SEL_TK), lambda i: (0, i, 0, 0)),
                   grp(LANES),
                   pl.BlockSpec((2, ts // LANES, LANES, LANES), lambda i: (0, i, 0, 0))],
        out_shape=[jax.ShapeDtypeStruct((n, 4 * LANES), _bf16),
                   jax.ShapeDtypeStruct((n, LANES), _bf16),
                   jax.ShapeDtypeStruct((n, LANES), _bf16),
                   jax.ShapeDtypeStruct((2, n, LANES), _bf16),
                   jax.ShapeDtypeStruct((2, n // SEL_TK, LANES, SEL_TK), _bf16),
                   jax.ShapeDtypeStruct((2, n, LANES), _bf16),
                   jax.ShapeDtypeStruct((2, n // LANES, LANES, LANES), _bf16)],
        compiler_params=_cparams(("arbitrary",)),
        name="rope_layout",
    )(proj, proj, proj, proj, proj, proj, proj, cos, sin_signed)


def _gelu_tanh(x):
    return 0.5 * x * (1.0 + jnp.tanh(0.7978845608028654 * (x + 0.044715 * (x * x * x))))


def _compress_one(x_ref, pe_t, pe_b, w_t, w_b, w2, out_ref, transposed):
    x = x_ref[0].astype(_f32)
    a = _dot((x + pe_t[...]).astype(_bf16), w_t[...])
    b = _dot((x + pe_b[...]).astype(_bf16), w_b[...])
    nc = a.shape[0]
    pre = a + pltpu.roll(b, nc - 1, 0)
    y = _dot(_gelu_tanh(pre).astype(_bf16), w2[...])
    for g in range(2):
        d = _dup(y, g)
        out_ref[0, g] = (d.T if transposed else d).astype(_bf16)


def _compress_kernel(k_ref, v_ref, kpt, kpb, kwt, kwb, kw2, vpt, vpb, vwt, vwb, vw2, ko_ref, vo_ref):
    _compress_one(k_ref, kpt, kpb, kwt, kwb, kw2, ko_ref, False)
    _compress_one(v_ref, vpt, vpb, vwt, vwb, vw2, vo_ref, True)


def _compress_weights(pe, w1, w2):
    half = CMP_BLOCK // 2
    eye = jnp.eye(2, dtype=_f32)
    outs = []
    for part in range(2):
        w = w1[part * half * HEAD_DIM:(part + 1) * half * HEAD_DIM].reshape(half, HEAD_DIM, HEAD_DIM)
        wbd = (w[:, None, :, None, :] * eye[None, :, None, :, None]).reshape(half * 2 * HEAD_DIM, 2 * HEAD_DIM)
        p = jnp.broadcast_to(pe[part * half:(part + 1) * half, None, :], (half, 2, HEAD_DIM)).reshape(1, -1)
        outs.append((p.astype(_f32), wbd.astype(_bf16)))
    w2bd = (w2[None, :, None, :] * eye[:, None, :, None]).reshape(2 * HEAD_DIM, 2 * HEAD_DIM).astype(_bf16)
    (pt, wt), (pb, wb) = outs
    return pt, pb, wt, wb, w2bd


def _compress_stage(kc_r, vc_r, kparams, vparams, batch, seq):
    nc = seq // CMP_STRIDE
    width = CMP_STRIDE * LANES
    xs = pl.BlockSpec((1, nc, width), lambda b: (b, 0, 0))
    full = lambda a: pl.BlockSpec(a.shape, lambda b: (0,) * a.ndim)
    out = pl.BlockSpec((1, 2, nc, LANES), lambda b: (b, 0, 0, 0))
    weights = list(kparams) + list(vparams)
    return pl.pallas_call(
        _compress_kernel,
        grid=(batch,),
        in_specs=[xs, xs] + [full(a) for a in weights],
        out_specs=[out, pl.BlockSpec((1, 2, LANES, nc), lambda b: (b, 0, 0, 0))],
        out_shape=[jax.ShapeDtypeStruct((batch, 2, nc, LANES), _bf16),
                   jax.ShapeDtypeStruct((batch, 2, LANES, nc), _bf16)],
        compiler_params=_cparams(("arbitrary",)),
        name="compress",
    )(kc_r.reshape(batch, nc, width), vc_r.reshape(batch, nc, width), *weights)


def _head_q(q_ref, r):
    q2 = q_ref[:, (r // 2) * LANES:(r // 2 + 1) * LANES]
    keep = _half0() if r % 2 == 0 else jnp.logical_not(_half0())
    return jnp.where(keep, q2, jnp.zeros_like(q2))


def _softmax_over_rows(s):
    m = jnp.max(s, axis=0, keepdims=True)
    e = jnp.exp2(s - m)
    l = jnp.sum(e, axis=0, keepdims=True)
    return e * jnp.where(m > 0.5 * NEG_INF, 1.0 / l, 0.0)


def _nsa_cw_kernel(q_ref, kc_ref, vct_ref, kw_ref, vwt_ref, ng_ref, stt_ref, *rest, first_block, sel_rows):
    yp_ref, mb_ref = rest[-2:]
    t0 = (pl.program_id(2) + first_block) * Q_BLOCK
    qpos = t0 + _lane_iota((1, Q_BLOCK))
    gates = jax.nn.sigmoid(ng_ref[...].astype(_f32).T[0:16, :])
    kc = kc_ref[0, 0]
    vct = vct_ref[0, 0]
    nc = kc.shape[0]
    cend = lax.broadcasted_iota(jnp.int32, (nc, 1), 0) * CMP_STRIDE + (CMP_BLOCK - 1)
    cmask = cend <= qpos
    start = pl.multiple_of(jnp.maximum(t0 - WINDOW, 0), Q_BLOCK)
    wlen = WINDOW + Q_BLOCK
    kwin = kw_ref[0, 0, pl.ds(start, wlen), :]
    kpos = start + lax.broadcasted_iota(jnp.int32, (wlen, 1), 0)
    wmask = (kpos <= qpos) & (qpos - kpos < WINDOW)
    blk0 = start // Q_BLOCK

    qs = [_head_q(q_ref, r) for r in range(4)]
    s_cmp = [_dot_t(kc, q) for q in qs]
    s_win = [_dot_t(kwin, q) for q in qs]
    p_cmp = [_softmax_over_rows(jnp.where(cmask, s, NEG_INF)) for s in s_cmp]
    imp = (p_cmp[0] + p_cmp[1]) + (p_cmp[2] + p_cmp[3])
    e_win = []
    for s in s_win:
        s = jnp.where(wmask, s, NEG_INF)
        e_win.append(jnp.exp2(s - jnp.max(s, axis=0, keepdims=True)).astype(_bf16))
    vwt = jnp.concatenate([vwt_ref[0, 0, blk0 + c] for c in range(wlen // Q_BLOCK)], axis=1)
    yts = []
    for r in range(4):
        o_cmp = _dot(vct, p_cmp[r].astype(_bf16))[0:HEAD_DIM]
        win = _dot(vwt, e_win[r])
        o_win = win[0:HEAD_DIM] * (1.0 / win[HEAD_DIM:2 * HEAD_DIM])
        yts.append(gates[3 * r:3 * r + 1] * o_cmp + gates[3 * r + 2:3 * r + 3] * o_win)
    yp_ref[:, 0:LANES] = jnp.concatenate(yts[0:2], axis=0).T
    yp_ref[:, LANES:2 * LANES] = jnp.concatenate(yts[2:4], axis=0).T

    imp_hi = imp.astype(_bf16)
    rest = imp - imp_hi.astype(_f32)
    imp_mid = rest.astype(_bf16)
    imp_lo = (rest - imp_mid.astype(_f32)).astype(_bf16)
    st = stt_ref[0:sel_rows, :]
    p_slc = _dot(st, imp_hi) + (_dot(st, imp_mid) + _dot(st, imp_lo))
    selj = lax.broadcasted_iota(jnp.int32, (sel_rows, 1), 0)
    blk_t = qpos // SEL_BLOCK
    forced = (selj == 0) | (selj == blk_t) | (selj == blk_t - 1)
    score = jnp.where(forced, TAKEN, jnp.where(selj <= blk_t, p_slc, NEG_INF))
    seljf = selj.astype(_f32)
    picked = forced
    for _ in range(SEL_TOPK - 3):
        m = jnp.max(score, axis=0, keepdims=True)
        first = jnp.min(jnp.where(score == m, seljf, float(LANES)), axis=0, keepdims=True)
        hit = seljf == first
        picked = picked | hit
        score = jnp.where(hit, TAKEN, score)
    bias = jnp.where(picked, 0.0, NEG_INF)
    if sel_rows < LANES:
        bias = jnp.concatenate([bias, jnp.full((LANES - sel_rows, Q_BLOCK), NEG_INF, _f32)], axis=0)
    mb_ref[0] = bias.T.astype(_bf16)


def _nsa_cw_stage(nq_r, kc_d, vc_t, kw_d, vw_t, proj, stencil_t, batch, seq):
    n = nq_r.shape[0]
    nblk = seq // Q_BLOCK
    nc = seq // CMP_STRIDE
    per_block = Q_BLOCK // CMP_STRIDE
    splits = 4 if nblk % 4 == 0 and (nblk // 4 * per_block) % LANES == 0 else 1
    per = nblk // splits
    kw4 = kw_d.reshape(2, batch, seq, LANES)
    vw5 = vw_t.reshape(2, batch, nblk, LANES, LANES)
    outs = ()
    for k in range(splits):
        first = k * per
        nvis = min(nc, (first + per) * per_block)
        qrow = lambda b, g, i, first=first: b * nblk + first + i
        in_specs = [pl.BlockSpec((Q_BLOCK, 2 * LANES), lambda b, g, i, qrow=qrow: (qrow(b, g, i), g)),
                    pl.BlockSpec((1, 1, nvis, LANES), lambda b, g, i: (b, g, 0, 0)),
                    pl.BlockSpec((1, 1, LANES, nvis), lambda b, g, i: (b, g, 0, 0)),
                    pl.BlockSpec((1, 1, seq, LANES), lambda b, g, i: (g, b, 0, 0)),
                    pl.BlockSpec((1, 1, nblk, LANES, LANES), lambda b, g, i: (g, b, 0, 0, 0)),
                    pl.BlockSpec((Q_BLOCK, LANES), lambda b, g, i, qrow=qrow: (qrow(b, g, i), CB_NG + g)),
                    pl.BlockSpec((LANES, nvis), lambda b, g, i: (0, 0))]
        in_specs += [pl.BlockSpec(memory_space=pl.ANY)] * len(outs)
        outs = pl.pallas_call(
            functools.partial(_nsa_cw_kernel, first_block=first,
                              sel_rows=min(LANES, -(-(first + per) * Q_BLOCK // SEL_BLOCK // 8) * 8)),
            grid=(batch, 2, per),
            in_specs=in_specs,
            out_specs=[pl.BlockSpec((Q_BLOCK, 2 * LANES), lambda b, g, i, qrow=qrow: (qrow(b, g, i), g)),
                       pl.BlockSpec((1, Q_BLOCK, LANES), lambda b, g, i, qrow=qrow: (g, qrow(b, g, i), 0))],
            out_shape=[jax.ShapeDtypeStruct((n, 4 * LANES), _f32),
                       jax.ShapeDtypeStruct((2, n, LANES), _bf16)],
            input_output_aliases={7: 0, 8: 1} if outs else {},
            compiler_params=_cparams(("arbitrary", "arbitrary", "arbitrary")),
            name="nsa_cmp_win_select",
        )(nq_r, kc_d, vc_t, kw4, vw5, proj, stencil_t, *outs)
    return outs


def _nsa_sel_kernel(q_ref, mb_ref, ka_ref, vat_ref, ng_ref, yp_ref, o_ref, qs_ref, m_ref, acc_ref, s_ref, p_ref, alpha_ref):
    tk = SEL_TK
    t0 = pl.program_id(2) * SEL_Q
    h0 = _half0()
    mb = mb_ref[0].astype(_f32)
    bias = [pltpu.roll(mb, HEAD_DIM, 1), mb]
    for r in range(4):
        q = q_ref[:, (r // 2) * LANES:(r // 2 + 1) * LANES].astype(_f32)
        if r % 2 == 1:
            q = pltpu.roll(q, HEAD_DIM, 1)
        for v in range(2):
            qs_ref[v, r * SEL_Q:(r + 1) * SEL_Q, :] = jnp.where(h0, q, bias[v]).astype(_bf16)
    m_ref[...] = jnp.full(m_ref.shape, NEG_INF, _f32)
    acc_ref[...] = jnp.zeros(acc_ref.shape, _f32)
    p_ref[...] = jnp.zeros(p_ref.shape, _bf16)
    alpha_ref[...] = jnp.ones(alpha_ref.shape, _f32)
    qpos = t0 + _lane_iota((1, 4 * SEL_Q)) % SEL_Q

    def scores(kt):
        version = (kt * (tk // SEL_BLOCK)) // HEAD_DIM
        return _dot_t(ka_ref[0, 0, pl.ds(pl.multiple_of(kt * tk, tk), tk), :], qs_ref[version])

    def softmax_step(s):
        m_old = m_ref[...]
        m_new = jnp.maximum(m_old, jnp.max(s, axis=0, keepdims=True))
        m_ref[...] = m_new
        return jnp.exp2(s - m_new).astype(_bf16), jnp.exp2(m_old - m_new)

    def accumulate(kt, alpha, p):
        acc_ref[...] = alpha * acc_ref[...] + _dot(vat_ref[0, 0, kt], p)

    def trip(kt, carry):
        accumulate(jnp.maximum(kt - 1, 0), alpha_ref[...], p_ref[...])
        s = s_ref[...]
        s_ref[...] = scores(kt + 1)
        p, alpha = softmax_step(s)
        p_ref[...] = p
        alpha_ref[...] = alpha
        return carry

    n_full = t0 // tk
    s_ref[...] = scores(0)

    def two_trips(kp, carry):
        trip(2 * kp, carry)
        return trip(2 * kp + 1, carry)

    lax.fori_loop(0, n_full // 2, two_trips, 0)

    @pl.when(n_full % 2 == 1)
    def _():
        trip(n_full - 1, 0)

    accumulate(jnp.maximum(n_full - 1, 0), alpha_ref[...], p_ref[...])
    kpos = n_full * tk + lax.broadcasted_iota(jnp.int32, (tk, 1), 0)
    p, alpha = softmax_step(jnp.where(kpos <= qpos, s_ref[...], NEG_INF))
    accumulate(n_full, alpha, p)
    acc = acc_ref[...]
    o = acc[0:HEAD_DIM] * (1.0 / acc[HEAD_DIM:2 * HEAD_DIM])
    gates = jax.nn.sigmoid(ng_ref[...].astype(_f32).T[0:16, :])
    ys = [gates[3 * r + 1:3 * r + 2] * o[:, r * SEL_Q:(r + 1) * SEL_Q] for r in range(4)]
    o_ref[:, 0:LANES] = (yp_ref[:, 0:LANES] + jnp.concatenate(ys[0:2], axis=0).T).astype(o_ref.dtype)
    o_ref[:, LANES:2 * LANES] = (yp_ref[:, LANES:2 * LANES] + jnp.concatenate(ys[2:4], axis=0).T).astype(o_ref.dtype)


def _nsa_sel_stage(nq_r, mbias, k_aug, vs_t, proj, ypart, batch, seq):
    n = nq_r.shape[0]
    nblk = seq // SEL_Q
    qrow = lambda b, g, i: b * nblk + i
    return pl.pallas_call(
        _nsa_sel_kernel,
        grid=(batch, 2, nblk),
        in_specs=[pl.BlockSpec((SEL_Q, 2 * LANES), lambda b, g, i: (qrow(b, g, i), g)),
                  pl.BlockSpec((1, SEL_Q, LANES), lambda b, g, i: (g, qrow(b, g, i), 0)),
                  pl.BlockSpec((1, 1, seq, LANES), lambda b, g, i: (g, b, 0, 0)),
                  pl.BlockSpec((1, 1, seq // SEL_TK, LANES, SEL_TK), lambda b, g, i: (g, b, 0, 0, 0)),
                  pl.BlockSpec((SEL_Q, LANES), lambda b, g, i: (qrow(b, g, i), CB_NG + g)),
                  pl.BlockSpec((SEL_Q, 2 * LANES), lambda b, g, i: (qrow(b, g, i), g))],
        out_specs=pl.BlockSpec((SEL_Q, 2 * LANES), lambda b, g, i: (qrow(b, g, i), g)),
        out_shape=jax.ShapeDtypeStruct((n, 4 * LANES), _bf16),
        scratch_shapes=[pltpu.VMEM((2, 4 * SEL_Q, LANES), _bf16),
                        pltpu.VMEM((1, 4 * SEL_Q), _f32),
                        pltpu.VMEM((LANES, 4 * SEL_Q), _f32),
                        pltpu.VMEM((SEL_TK, 4 * SEL_Q), _f32),
                        pltpu.VMEM((SEL_TK, 4 * SEL_Q), _bf16),
                        pltpu.VMEM((1, 4 * SEL_Q), _f32)],
        compiler_params=_cparams(("arbitrary", "arbitrary", "arbitrary")),
        name="nsa_selected",
    )(nq_r, mbias, k_aug.reshape(2, batch, seq, LANES), vs_t.reshape(2, batch, seq // SEL_TK, LANES, SEL_TK), proj, ypart)


def _sb_kernel(q_ref, k_ref, v_ref, o_ref, qs_ref, tail_ref, acc_ref):
    i = pl.program_id(1)
    h0 = _half0()
    heads = 2 * (q_ref.shape[1] // LANES)
    for h in range(heads):
        q = q_ref[:, (h // 2) * LANES:(h // 2 + 1) * LANES]
        keep = h0 if h % 2 == 0 else jnp.logical_not(h0)
        qs_ref[h] = jnp.where(keep, q, jnp.zeros_like(q)) * QK_SCALE
    tail_ref[...] = jnp.zeros(tail_ref.shape, _f32)
    acc_ref[...] = jnp.zeros(acc_ref.shape, _f32)
    rloc = lax.broadcasted_iota(jnp.int32, (Q_BLOCK, Q_BLOCK), 0)
    cloc = lax.broadcasted_iota(jnp.int32, (Q_BLOCK, Q_BLOCK), 1)
    later = jnp.where(rloc > cloc, 1.0, 0.0).astype(_bf16)

    def key_block(j, diagonal):
        k0 = pl.multiple_of(j * Q_BLOCK, Q_BLOCK)
        past = cloc < rloc
        cols = [slice((h // 2) * LANES, (h // 2 + 1) * LANES) for h in range(heads)]
        zs = [_dot_t(qs_ref[h], k_ref[0, pl.ds(k0, Q_BLOCK), cols[h]]) for h in range(heads)]
        log_beta, log_keep = [], []
        for z in zs:
            sp = jnp.maximum(z, 0.0) + jnp.log(1.0 + jnp.exp(-jnp.abs(z)))
            log_beta.append(z - sp)
            log_keep.append(jnp.where(past, -sp, 0.0) if diagonal else -sp)
        inner = []
        for lk in log_keep:
            hi = lk.astype(_bf16)
            lo = (lk - hi.astype(_f32)).astype(_bf16)
            inner.append(_dot(hi, later) + _dot(lo, later))
        probs = []
        for h in range(heads):
            a = jnp.exp(log_beta[h] + inner[h] + tail_ref[h])
            probs.append((jnp.where(past, a, 0.0) if diagonal else a).astype(_bf16))
        worst = jnp.full((Q_BLOCK, 1), -jnp.inf, _f32)
        for h in range(heads):
            acc_ref[h] = acc_ref[h] + _dot(probs[h], v_ref[0, pl.ds(k0, Q_BLOCK), cols[h]])
            tail = tail_ref[h] + jnp.sum(log_keep[h], axis=-1, keepdims=True)
            tail_ref[h] = tail
            worst = jnp.maximum(worst, tail)
        return jnp.max(worst)

    def cond(c):
        j, worst_tail = c
        return (j >= 0) & (worst_tail > SB_TAIL_CUTOFF)

    def body(c):
        j, _ = c
        return j - 1, key_block(j, False)

    lax.while_loop(cond, body, (i - 1, key_block(i, True)))
    for p in range(heads // 2):
        o_ref[:, p * LANES:(p + 1) * LANES] = jnp.where(h0, acc_ref[2 * p], acc_ref[2 * p + 1]).astype(o_ref.dtype)


def _sb_stage(proj, batch, seq):
    n = proj.shape[0]
    nblk = seq // Q_BLOCK
    width = 4 * LANES
    proj3 = proj.reshape(batch, seq, PROJ_W)
    return pl.pallas_call(
        _sb_kernel,
        grid=(batch, nblk),
        in_specs=[pl.BlockSpec((Q_BLOCK, width), lambda b, i: (b * nblk + i, CB_SBQ // 4)),
                  pl.BlockSpec((1, seq, width), lambda b, i: (b, 0, CB_SBK // 4)),
                  pl.BlockSpec((1, seq, width), lambda b, i: (b, 0, CB_SBV // 4))],
        out_specs=pl.BlockSpec((Q_BLOCK, width), lambda b, i: (b * nblk + i, 0)),
        out_shape=jax.ShapeDtypeStruct((n, width), _bf16),
        scratch_shapes=[pltpu.VMEM((8, Q_BLOCK, LANES), _bf16),
                        pltpu.VMEM((8, Q_BLOCK, 1), _f32),
                        pltpu.VMEM((8, Q_BLOCK, LANES), _f32)],
        compiler_params=_cparams(("arbitrary", "arbitrary")),
        name="stick_breaking",
    )(proj, proj3, proj3)


def _layer_norm(x, g, b):
    mu = jnp.mean(x, axis=-1, keepdims=True)
    xc = x - mu
    var = jnp.mean(xc * xc, axis=-1, keepdims=True)
    return xc * lax.rsqrt(var + LN_EPS) * g + b


def _merge_kernel(x_ref, ysb_ref, yns_ref, mg0_ref, mg1_ref, wsb_ref, wns_ref, wo_ref, g_ref, b_ref,
                  wr_ref, br_ref, h_ref, idx_ref, gw_ref):
    m0 = jax.nn.sigmoid(mg0_ref[...].astype(_f32))
    m1 = jax.nn.sigmoid(mg1_ref[...].astype(_f32))
    merged = m0 * _dot(ysb_ref[...], wsb_ref[...]) + m1 * _dot(yns_ref[...], wns_ref[...])
    pre = DEEPNORM_ALPHA * x_ref[...] + _dot(merged.astype(_bf16), wo_ref[...])
    h = _layer_norm(pre, g_ref[...], b_ref[...])
    tm = h.shape[0]
    for s in range(SLAB):
        h_ref[pl.ds(s, tm, stride=SLAB), :] = h[:, s * LANES:(s + 1) * LANES]
    h_hi = h.astype(_bf16)
    h_lo = (h - h_hi.astype(_f32)).astype(_bf16)
    w = wr_ref[...]
    w_hi = w.astype(_bf16)
    w_lo = (w - w_hi.astype(_f32)).astype(_bf16)
    logits = (_dot(h_hi, w_hi) + (_dot(h_hi, w_lo) + _dot(h_lo, w_hi))) + br_ref[...]
    lane = _lane_iota((1, LANES))
    lanef = lane.astype(_f32)
    lg = jnp.where(lane < N_EXPERTS, logits, TAKEN)
    vals, idxs = [], []
    for _ in range(TOP_K):
        m = jnp.max(lg, axis=-1, keepdims=True)
        first = jnp.min(jnp.where(lg == m, lanef, float(LANES)), axis=-1, keepdims=True)
        vals.append(m)
        idxs.append(first)
        lg = jnp.where(lanef == first, TAKEN, lg)
    es = [jnp.exp(v - vals[0]) for v in vals]
    inv = 1.0 / (es[0] + es[1] + es[2] + es[3])
    idx_t = jnp.zeros(lg.shape, _f32)
    gw_t = jnp.zeros(lg.shape, _f32)
    for k in range(TOP_K):
        idx_t = jnp.where(lane == k, idxs[k], idx_t)
        gw_t = jnp.where(lane == k, es[k] * inv, gw_t)
    idx_ref[...] = idx_t[:, :TOP_K].astype(jnp.int32)
    gw_ref[...] = gw_t[:, :TOP_K]


def _merge_stage(x2, y_sb, y_nsa, proj, wsb, wns, wo, g1, b1, wr, br):
    n = x2.shape[0]
    tm = 512
    row = lambda w: pl.BlockSpec((tm, w), lambda i: (i, 0))
    full = lambda a: pl.BlockSpec(a.shape, lambda i: (0,) * a.ndim)
    return pl.pallas_call(
        _merge_kernel,
        grid=(n // tm,),
        in_specs=[row(D_MODEL), row(4 * LANES), row(4 * LANES),
                  pl.BlockSpec((tm, D_MODEL), lambda i: (i, CB_MG // 8)),
                  pl.BlockSpec((tm, D_MODEL), lambda i: (i, CB_MG // 8 + 1)),
                  full(wsb), full(wns), full(wo), full(g1), full(b1), full(wr), full(br)],
        out_specs=[pl.BlockSpec((tm * SLAB, LANES), lambda i: (i, 0)), row(TOP_K), row(TOP_K)],
        out_shape=[jax.ShapeDtypeStruct((n * SLAB, LANES), _f32),
                   jax.ShapeDtypeStruct((n, TOP_K), jnp.int32),
                   jax.ShapeDtypeStruct((n, TOP_K), _f32)],
        compiler_params=_cparams(("arbitrary",)),
        name="merge_ln1_router",
    )(x2, y_sb, y_nsa, proj, proj, wsb, wns, wo, g1, b1, wr, br)


def _prep_w_in(w):
    main = w[:, :CB_NG * LANES]
    ng = w[:, CB_NG * LANES:CB_NG * LANES + 24]
    mg = w[:, CB_NG * LANES + 24:]
    pad = jnp.zeros((w.shape[0], LANES - 12), w.dtype)
    return jnp.concatenate([main, ng[:, :12], pad, ng[:, 12:], pad, mg], axis=1).astype(_bf16)


def _rope_tables(seq):
    half = HEAD_DIM // 2
    inv_freq = ROPE_THETA ** (-jnp.arange(half, dtype=_f32) / half)
    ang = jnp.arange(seq, dtype=_f32)[:, None] * inv_freq[None, :]
    cos = jnp.cos(ang)
    sin = jnp.sin(ang)
    cos128 = jnp.concatenate([cos, cos, cos, cos], axis=1)
    sin128 = jnp.concatenate([-sin, sin, -sin, sin], axis=1)
    return cos128, sin128


def _stencil(nc):
    n = np.arange(nc)[:, None]
    j = np.arange(LANES)[None, :]
    ratio = SEL_BLOCK // CMP_STRIDE
    ok = (n >= ratio * j - 1) & (n <= ratio * j + ratio - 1) & (n < nc - 1)
    return jnp.asarray(ok.astype(np.float32).T, dtype=_bf16)


def _attention_half(x, w_in, cmp_pe_k, cmp_w1_k, cmp_w2_k, cmp_pe_v, cmp_w1_v, cmp_w2_v,
                    w_proj_sb, w_proj_nsa, w_out, ln1_g, ln1_b, w_router, b_router):
    batch, seq, _ = x.shape
    n = batch * seq
    x2 = x.reshape(n, D_MODEL)
    proj = _in_proj(x2, _prep_w_in(w_in))
    cos, sin_signed = _rope_tables(seq)
    nq_r, kc_r, vc_r, k_aug, vs_t, kw_d, vw_t = _rope_stage(proj, cos, sin_signed, seq)
    kc_d, vc_t = _compress_stage(kc_r, vc_r, _compress_weights(cmp_pe_k, cmp_w1_k, cmp_w2_k),
                                 _compress_weights(cmp_pe_v, cmp_w1_v, cmp_w2_v), batch, seq)
    ypart, mbias = _nsa_cw_stage(nq_r, kc_d, vc_t, kw_d, vw_t, proj, _stencil(seq // CMP_STRIDE), batch, seq)
    y_nsa = _nsa_sel_stage(nq_r, mbias, k_aug, vs_t, proj, ypart, batch, seq)
    y_sb = _sb_stage(proj, batch, seq)
    wr = jnp.pad(w_router.astype(_f32), ((0, 0), (0, LANES - N_EXPERTS)))
    br = jnp.pad(b_router.astype(_f32), (0, LANES - N_EXPERTS)).reshape(1, LANES)
    return _merge_stage(x2, y_sb, y_nsa, proj, w_proj_sb.astype(_bf16), w_proj_nsa.astype(_bf16),
                        w_out.astype(_bf16), ln1_g.reshape(1, -1), ln1_b.reshape(1, -1), wr, br)


def _row_copy(src, src_row, dst, dst_row, sem):
    return pltpu.make_async_copy(src.at[pl.ds(src_row * SLAB, SLAB)], dst.at[pl.ds(dst_row * SLAB, SLAB)], sem)


def _dispatch_kernel(pad_start_ref, pad_len_ref, dest_ref, h_hbm, buf_ref, hbuf, zslab, in_sems, row_sems, zero_sem,
                     *, tokens):
    i = pl.program_id(0)
    last = pl.num_programs(0) - 1
    slot = i % 2
    block_rows = tokens * SLAB

    def block_copy(step, s):
        return pltpu.make_async_copy(h_hbm.at[pl.ds(step * block_rows, block_rows)], hbuf.at[s], in_sems.at[s])

    def wait_rows(s):
        for _ in range(TOP_K):
            pltpu.make_async_copy(hbuf.at[s], buf_ref.at[pl.ds(0, block_rows)], row_sems.at[s]).wait()

    def padding_rows(act):
        def per_expert(e, c):
            def per_row(r, c2):
                act(_row_copy(zslab, 0, buf_ref, pad_start_ref[e] + r, zero_sem))
                return c2
            lax.fori_loop(0, pad_len_ref[e], per_row, 0)
            return c
        lax.fori_loop(0, N_EXPERTS, per_expert, 0)

    @pl.when(i == 0)
    def _():
        block_copy(0, 0).start()
        zslab[...] = jnp.zeros(zslab.shape, _f32)
        padding_rows(lambda cp: cp.start())

    @pl.when(i > 0)
    def _():
        wait_rows(1 - slot)

    @pl.when(i < last)
    def _():
        block_copy(i + 1, 1 - slot).start()

    block_copy(i, slot).wait()
    src = hbuf.at[slot]

    def issue(tg, c):
        for u in range(ISSUE_UNROLL):
            t = tg * ISSUE_UNROLL + u
            for k in range(TOP_K):
                _row_copy(src, t, buf_ref, dest_ref[t * TOP_K + k], row_sems.at[slot]).start(priority=k % 2)
        return c

    lax.fori_loop(0, tokens // ISSUE_UNROLL, issue, 0)

    @pl.when(i == last)
    def _():
        wait_rows(slot)
        padding_rows(lambda cp: cp.wait())


def _dispatch_stage(dest, pad_start, pad_len, h_slab, buf_rows):
    n = h_slab.shape[0] // SLAB
    tokens = 256
    grid_spec = pltpu.PrefetchScalarGridSpec(
        num_scalar_prefetch=2,
        grid=(n // tokens,),
        in_specs=[pl.BlockSpec((tokens * TOP_K,), lambda i, ps, pn: (i,), memory_space=pltpu.SMEM),
                  pl.BlockSpec(memory_space=pl.ANY)],
        out_specs=pl.BlockSpec(memory_space=pl.ANY),
        scratch_shapes=[pltpu.VMEM((2, tokens * SLAB, LANES), _f32),
                        pltpu.VMEM((SLAB, LANES), _f32),
                        pltpu.SemaphoreType.DMA((2,)),
                        pltpu.SemaphoreType.DMA((2,)),
                        pltpu.SemaphoreType.DMA(())])
    return pl.pallas_call(
        functools.partial(_dispatch_kernel, tokens=tokens),
        grid_spec=grid_spec,
        out_shape=jax.ShapeDtypeStruct((buf_rows * SLAB, LANES), _f32),
        compiler_params=pltpu.CompilerParams(dimension_semantics=("arbitrary",), has_side_effects=True),
        name="moe_dispatch",
    )(pad_start, pad_len, dest, h_slab)


def _slab_load(ref, rows):
    return jnp.concatenate([ref[pl.ds(s, rows, stride=SLAB), :] for s in range(SLAB)], axis=1)


def _expert_kernel(ce_ref, used_ref, x_ref, wgu_ref, bgu_ref, wd_ref, bd_ref, o_ref, wgu_bf, wd_bf):
    c = pl.program_id(0)
    new_expert = (c == 0) | (ce_ref[c] != ce_ref[jnp.maximum(c - 1, 0)])

    @pl.when(new_expert & (c < used_ref[0]))
    def _():
        wgu_bf[...] = wgu_ref[0].astype(_bf16)
        wd_bf[...] = wd_ref[0].astype(_bf16)

    @pl.when(c < used_ref[0])
    def _():
        x = _slab_load(x_ref, MOE_ROWS).astype(_bf16)
        gu = _dot(x, wgu_bf[...]) + bgu_ref[0]
        gate = jnp.minimum(gu[:, :D_MODEL], SWIGLU_LIMIT)
        up = jnp.clip(gu[:, D_MODEL:], -SWIGLU_LIMIT, SWIGLU_LIMIT)
        h = gate * jax.nn.sigmoid(SWIGLU_ALPHA * gate) * (up + 1.0)
        y = _dot(h.astype(_bf16), wd_bf[...]) + bd_ref[0]
        for s in range(SLAB):
            o_ref[pl.ds(s, MOE_ROWS, stride=SLAB), :] = y[:, s * LANES:(s + 1) * LANES]

    @pl.when(c >= used_ref[0])
    def _():
        o_ref[...] = jnp.zeros(o_ref.shape, o_ref.dtype)


def _expert_stage(chunk_expert, n_used, buf, wgu, bgu, wd, bd):
    n_chunks = chunk_expert.shape[0]
    rows = MOE_ROWS * SLAB
    grid_spec = pltpu.PrefetchScalarGridSpec(
        num_scalar_prefetch=2,
        grid=(n_chunks,),
        in_specs=[pl.BlockSpec((rows, LANES), lambda c, ce, nu: (jnp.minimum(c, nu[0] - 1), 0)),
                  pl.BlockSpec((1, D_MODEL, 2 * D_MODEL), lambda c, ce, nu: (ce[c], 0, 0)),
                  pl.BlockSpec((1, 1, 2 * D_MODEL), lambda c, ce, nu: (ce[c], 0, 0)),
                  pl.BlockSpec((1, D_MODEL, D_MODEL), lambda c, ce, nu: (ce[c], 0, 0)),
                  pl.BlockSpec((1, 1, D_MODEL), lambda c, ce, nu: (ce[c], 0, 0))],
        out_specs=pl.BlockSpec((rows, LANES), lambda c, ce, nu: (c, 0)),
        scratch_shapes=[pltpu.VMEM((D_MODEL, 2 * D_MODEL), _bf16), pltpu.VMEM((D_MODEL, D_MODEL), _bf16)])
    return pl.pallas_call(
        _expert_kernel,
        grid_spec=grid_spec,
        out_shape=jax.ShapeDtypeStruct(buf.shape, _f32),
        compiler_params=_cparams(("arbitrary",)),
        name="moe_experts",
    )(chunk_expert, n_used, buf, wgu, bgu, wd, bd)


def _combine_kernel(dest_ref, dest_next_ref, gw_ref, h_ref, eo_ref, g_ref, b_ref, o_ref, gbuf, ysl, sems, *, tokens):
    i = pl.program_id(0)
    slot = i % 2
    count = tokens * TOP_K

    def gather(idx_ref, into):
        def issue(tg, c):
            for u in range(ISSUE_UNROLL * TOP_K):
                j = tg * (ISSUE_UNROLL * TOP_K) + u
                _row_copy(eo_ref, idx_ref[j], gbuf.at[into], j, sems.at[into]).start(priority=u % 2)
            return c
        lax.fori_loop(0, tokens // ISSUE_UNROLL, issue, 0)

    @pl.when(i == 0)
    def _():
        gather(dest_ref, 0)

    @pl.when(i + 1 < pl.num_programs(0))
    def _():
        gather(dest_next_ref, 1 - slot)

    pltpu.make_async_copy(eo_ref.at[pl.ds(0, count * SLAB)], gbuf.at[slot], sems.at[slot]).wait()
    rows = gbuf.at[slot]

    unroll = 4

    def token_group(tg, c):
        for u in range(unroll):
            t = tg * unroll + u
            acc = DEEPNORM_ALPHA * h_ref[pl.ds(pl.multiple_of(t * SLAB, SLAB), SLAB), :]
            for k in range(TOP_K):
                j = t * TOP_K + k
                acc = acc + gw_ref[j] * rows[pl.ds(pl.multiple_of(j * SLAB, SLAB), SLAB), :]
            ysl[pl.ds(pl.multiple_of(t * SLAB, SLAB), SLAB), :] = acc
        return c

    lax.fori_loop(0, tokens // unroll, token_group, 0)
    o_ref[...] = _layer_norm(_slab_load(ysl, tokens), g_ref[...], b_ref[...])


def _combine_stage(dest, gate_w, h_slab, expert_out, g2, b2):
    n = h_slab.shape[0] // SLAB
    tokens = 256
    steps = n // tokens
    smem = lambda: pl.BlockSpec((tokens * TOP_K,), lambda i: (i,), memory_space=pltpu.SMEM)
    return pl.pallas_call(
        functools.partial(_combine_kernel, tokens=tokens),
        grid=(steps,),
        in_specs=[smem(),
                  pl.BlockSpec((tokens * TOP_K,), lambda i: (jnp.minimum(i + 1, steps - 1),), memory_space=pltpu.SMEM),
                  smem(),
                  pl.BlockSpec((tokens * SLAB, LANES), lambda i: (i, 0)),
                  pl.BlockSpec(memory_space=pl.ANY),
                  pl.BlockSpec((1, D_MODEL), lambda i: (0, 0)),
                  pl.BlockSpec((1, D_MODEL), lambda i: (0, 0))],
        out_specs=pl.BlockSpec((tokens, D_MODEL), lambda i: (i, 0)),
        out_shape=jax.ShapeDtypeStruct((n, D_MODEL), _f32),
        scratch_shapes=[pltpu.VMEM((2, tokens * TOP_K * SLAB, LANES), _f32),
                        pltpu.VMEM((tokens * SLAB, LANES), _f32),
                        pltpu.SemaphoreType.DMA((2,))],
        compiler_params=_cparams(("arbitrary",)),
        name="moe_combine_ln2",
    )(dest, dest, gate_w, h_slab, expert_out, g2, b2)


def _dispatch_plan(top_idx):
    m = top_idx.size
    e_flat = top_idx.reshape(m)
    onehot = (e_flat[:, None] == jnp.arange(N_EXPERTS, dtype=jnp.int32)[None, :]).astype(jnp.int32)
    csum = jnp.cumsum(onehot, axis=0)
    counts = csum[-1]
    padded = (counts + MOE_ROWS - 1) // MOE_ROWS * MOE_ROWS
    pends = jnp.cumsum(padded)
    pstarts = pends - padded
    dest = jnp.sum(onehot * (csum - 1 + pstarts[None, :]), axis=1).astype(jnp.int32)
    n_chunks = m // MOE_ROWS + N_EXPERTS
    chunk_start = jnp.arange(n_chunks, dtype=jnp.int32) * MOE_ROWS
    chunk_expert = jnp.minimum(jnp.sum((chunk_start[:, None] >= pends[None, :]).astype(jnp.int32), axis=1), N_EXPERTS - 1)
    n_used = (pends[-1] // MOE_ROWS).astype(jnp.int32).reshape(1)
    pad_start = (pstarts + counts).astype(jnp.int32)
    pad_len = (padded - counts).astype(jnp.int32)
    return dest, chunk_expert, n_used, pad_start, pad_len, n_chunks * MOE_ROWS


def _moe_half(h_slab, top_idx, gate_w, w_gate_up, b_gate_up, w_down, b_down, ln2_g, ln2_b):
    dest, chunk_expert, n_used, pad_start, pad_len, buf_rows = _dispatch_plan(top_idx)
    buf = _dispatch_stage(dest, pad_start, pad_len, h_slab, buf_rows)
    expert_out = _expert_stage(chunk_expert, n_used, buf, w_gate_up, b_gate_up.reshape(N_EXPERTS, 1, -1),
                               w_down, b_down.reshape(N_EXPERTS, 1, -1))
    return _combine_stage(dest, gate_w.reshape(-1), h_slab, expert_out, ln2_g.reshape(1, -1), ln2_b.reshape(1, -1))


def kernel(x, w_in, cmp_pe_k, cmp_w1_k, cmp_w2_k, cmp_pe_v, cmp_w1_v, cmp_w2_v, w_proj_sb, w_proj_nsa, w_out,
           ln1_g, ln1_b, w_router, b_router, w_gate_up, b_gate_up, w_down, b_down, ln2_g, ln2_b):
    assert w_in.shape[0] == 1, "single-layer block"
    batch, seq, _ = x.shape
    assert seq % 512 == 0 and seq // SEL_BLOCK <= LANES and seq >= WINDOW + Q_BLOCK
    h_slab, top_idx, gate_w = _attention_half(
        x, w_in[0], cmp_pe_k[0], cmp_w1_k[0], cmp_w2_k[0], cmp_pe_v[0], cmp_w1_v[0], cmp_w2_v[0],
        w_proj_sb[0], w_proj_nsa[0], w_out[0], ln1_g[0], ln1_b[0], w_router[0], b_router[0])
    out = _moe_half(h_slab, top_idx, gate_w, w_gate_up[0], b_gate_up[0], w_down[0], b_down[0], ln2_g[0], ln2_b[0])
    return out.reshape(batch, seq, D_MODEL)
```

```python
import functools

import numpy as np
import jax
import jax.numpy as jnp
from jax import lax
from jax.experimental import pallas as pl
from jax.experimental.pallas import tpu as pltpu

D_MODEL = 1024
HEAD_DIM = 64
LANES = 128
Q_BLOCK = 128
CMP_BLOCK = 32
CMP_STRIDE = 16
SEL_BLOCK = 64
SEL_TOPK = 16
WINDOW = 512
ROPE_THETA = 10000.0
N_EXPERTS = 32
TOP_K = 4
SWIGLU_LIMIT = 7.0
SWIGLU_ALPHA = 1.702
LN_EPS = 1e-5
NEG_INF = -1e30
TAKEN = -3e38
DEEPNORM_ALPHA = 2.0 ** 0.25
QK_SCALE = HEAD_DIM ** -0.5
LOG2E = 1.4426950408889634

CB_SBQ, CB_SBK, CB_SBV, CB_NQ = 0, 4, 8, 12
CB_KC, CB_VC, CB_KS, CB_VS, CB_KW, CB_VW = 16, 17, 18, 19, 20, 21
CB_NG = 22
CB_MG = 24
PROJ_W = 40 * LANES

SB_TAIL_CUTOFF = -110.0

SEL_Q = 512
SEL_TK = 512
MOE_ROWS = 512
SLAB = D_MODEL // LANES
ISSUE_UNROLL = 8
VMEM_LIMIT = 56 * 1024 * 1024

_bf16 = jnp.bfloat16
_f32 = jnp.float32


def _cparams(sem):
    return pltpu.CompilerParams(dimension_semantics=sem, vmem_limit_bytes=VMEM_LIMIT)


def _dot_t(a, b):
    return lax.dot_general(a, b, (((1,), (1,)), ((), ())), preferred_element_type=_f32)


def _dot(a, b):
    return jnp.dot(a, b, preferred_element_type=_f32)


def _lane_iota(shape):
    return lax.broadcasted_iota(jnp.int32, shape, len(shape) - 1)


def _half0(shape=(1, LANES)):
    return _lane_iota(shape) < HEAD_DIM


def _in_proj_kernel(x_ref, w_ref, o_ref):
    o_ref[...] = _dot(x_ref[...].astype(_bf16), w_ref[...]).astype(o_ref.dtype)


def _in_proj(x2, w):
    n = x2.shape[0]
    tm, tn = 1024, 2560
    return pl.pallas_call(
        _in_proj_kernel,
        grid=(PROJ_W // tn, n // tm),
        in_specs=[pl.BlockSpec((tm, D_MODEL), lambda j, i: (i, 0)),
                  pl.BlockSpec((D_MODEL, tn), lambda j, i: (0, j))],
        out_specs=pl.BlockSpec((tm, tn), lambda j, i: (i, j)),
        out_shape=jax.ShapeDtypeStruct((n, PROJ_W), _bf16),
        compiler_params=_cparams(("arbitrary", "arbitrary")),
        name="in_proj",
    )(x2, w)


def _rope(x, cos, sin_signed):
    first = (_lane_iota((1, LANES)) % HEAD_DIM) < (HEAD_DIM // 2)
    swapped = jnp.where(first, pltpu.roll(x, LANES - HEAD_DIM // 2, 1), pltpu.roll(x, HEAD_DIM // 2, 1))
    return x * cos + swapped * sin_signed


def _dup(x, g):
    other = pltpu.roll(x, HEAD_DIM, 1)
    h0 = _half0()
    return jnp.where(h0, x, other) if g == 0 else jnp.where(h0, other, x)


def _rope_kernel(nq_ref, kc_ref, vc_ref, ks_ref, vs_ref, kw_ref, vw_ref, cos_ref, sin_ref,
                 nq_o, kc_o, vc_o, ka_o, vs_o, kw_o, vw_o, *, blocks_per_seq):
    ts = cos_ref.shape[0]
    cos = cos_ref[...]
    sin = sin_ref[...]
    for c in range(4):
        sl = slice(c * LANES, (c + 1) * LANES)
        nq_o[:, sl] = (_rope(nq_ref[:, sl].astype(_f32), cos, sin) * (QK_SCALE * LOG2E)).astype(_bf16)
    kc_o[...] = _rope(kc_ref[...].astype(_f32), cos, sin).astype(_bf16)
    vc_o[...] = vc_ref[...]
    ks = _rope(ks_ref[...].astype(_f32), cos, sin)
    kw = _rope(kw_ref[...].astype(_f32), cos, sin)
    vs = vs_ref[...].astype(_f32)
    vw = vw_ref[...].astype(_f32)
    pos = (pl.program_id(0) % blocks_per_seq) * ts + lax.broadcasted_iota(jnp.int32, (ts, LANES), 0)
    lane = _lane_iota((ts, LANES))
    onehot = jnp.where((pos // SEL_BLOCK) % HEAD_DIM + HEAD_DIM == lane, 1.0, 0.0)
    for g in range(2):
        ka_o[g] = jnp.where(_half0(), _dup(ks, g), onehot).astype(_bf16)
        vsa = jnp.where(_half0(), _dup(vs, g), 1.0)
        for c in range(ts // SEL_TK):
            vs_o[g, c] = vsa[c * SEL_TK:(c + 1) * SEL_TK, :].T.astype(_bf16)
        kw_o[g] = _dup(kw, g).astype(_bf16)
        vwd = jnp.where(_half0(), _dup(vw, g), 1.0)
        for c in range(ts // LANES):
            vw_o[g, c] = vwd[c * LANES:(c + 1) * LANES, :].T.astype(_bf16)


def _rope_stage(proj, cos, sin_signed, seq):
    n = proj.shape[0]
    ts = 512
    bps = seq // ts
    col = lambda cb: pl.BlockSpec((ts, LANES), lambda i, cb=cb: (i, cb))
    tab = pl.BlockSpec((ts, LANES), lambda i: (i % bps, 0))
    grp = lambda w: pl.BlockSpec((2, ts, w), lambda i: (0, i, 0))
    return pl.pallas_call(
        functools.partial(_rope_kernel, blocks_per_seq=bps),
        grid=(n // ts,),
        in_specs=[pl.BlockSpec((ts, 4 * LANES), lambda i: (i, CB_NQ // 4)),
                  col(CB_KC), col(CB_VC), col(CB_KS), col(CB_VS), col(CB_KW), col(CB_VW), tab, tab],
        out_specs=[pl.BlockSpec((ts, 4 * LANES), lambda i: (i, 0)),
                   pl.BlockSpec((ts, LANES), lambda i: (i, 0)),
                   pl.BlockSpec((ts, LANES), lambda i: (i, 0)),
                   grp(LANES),
                   pl.BlockSpec((2, ts // SEL_TK, LANES, SEL_TK), lambda i: (0, i, 0, 0)),
                   grp(LANES),
                   pl.BlockSpec((2, ts // LANES, LANES, LANES), lambda i: (0, i, 0, 0))],
        out_shape=[jax.ShapeDtypeStruct((n, 4 * LANES), _bf16),
                   jax.ShapeDtypeStruct((n, LANES), _bf16),
                   jax.ShapeDtypeStruct((n, LANES), _bf16),
                   jax.ShapeDtypeStruct((2, n, LANES), _bf16),
                   jax.ShapeDtypeStruct((2, n // SEL_TK, LANES, SEL_TK), _bf16),
                   jax.ShapeDtypeStruct((2, n, LANES), _bf16),
                   jax.ShapeDtypeStruct((2, n // LANES, LANES, LANES), _bf16)],
        compiler_params=_cparams(("arbitrary",)),
        name="rope_layout",
    )(proj, proj, proj, proj, proj, proj, proj, cos, sin_signed)


def _gelu_tanh(x):
    return 0.5 * x * (1.0 + jnp.tanh(0.7978845608028654 * (x + 0.044715 * (x * x * x))))


def _compress_one(x_ref, pe_t, pe_b, w_t, w_b, w2, out_ref, transposed):
    x = x_ref[0].astype(_f32)
    a = _dot((x + pe_t[...]).astype(_bf16), w_t[...])
    b = _dot((x + pe_b[...]).astype(_bf16), w_b[...])
    nc = a.shape[0]
    pre = a + pltpu.roll(b, nc - 1, 0)
    y = _dot(_gelu_tanh(pre).astype(_bf16), w2[...])
    for g in range(2):
        d = _dup(y, g)
        out_ref[0, g] = (d.T if transposed else d).astype(_bf16)


def _compress_kernel(k_ref, v_ref, kpt, kpb, kwt, kwb, kw2, vpt, vpb, vwt, vwb, vw2, ko_ref, vo_ref):
    _compress_one(k_ref, kpt, kpb, kwt, kwb, kw2, ko_ref, False)
    _compress_one(v_ref, vpt, vpb, vwt, vwb, vw2, vo_ref, True)


def _compress_weights(pe, w1, w2):
    half = CMP_BLOCK // 2
    eye = jnp.eye(2, dtype=_f32)
    outs = []
    for part in range(2):
        w = w1[part * half * HEAD_DIM:(part + 1) * half * HEAD_DIM].reshape(half, HEAD_DIM, HEAD_DIM)
        wbd = (w[:, None, :, None, :] * eye[None, :, None, :, None]).reshape(half * 2 * HEAD_DIM, 2 * HEAD_DIM)
        p = jnp.broadcast_to(pe[part * half:(part + 1) * half, None, :], (half, 2, HEAD_DIM)).reshape(1, -1)
        outs.append((p.astype(_f32), wbd.astype(_bf16)))
    w2bd = (w2[None, :, None, :] * eye[:, None, :, None]).reshape(2 * HEAD_DIM, 2 * HEAD_DIM).astype(_bf16)
    (pt, wt), (pb, wb) = outs
    return pt, pb, wt, wb, w2bd


def _compress_stage(kc_r, vc_r, kparams, vparams, batch, seq):
    nc = seq // CMP_STRIDE
    width = CMP_STRIDE * LANES
    xs = pl.BlockSpec((1, nc, width), lambda b: (b, 0, 0))
    full = lambda a: pl.BlockSpec(a.shape, lambda b: (0,) * a.ndim)
    out = pl.BlockSpec((1, 2, nc, LANES), lambda b: (b, 0, 0, 0))
    weights = list(kparams) + list(vparams)
    return pl.pallas_call(
        _compress_kernel,
        grid=(batch,),
        in_specs=[xs, xs] + [full(a) for a in weights],
        out_specs=[out, pl.BlockSpec((1, 2, LANES, nc), lambda b: (b, 0, 0, 0))],
        out_shape=[jax.ShapeDtypeStruct((batch, 2, nc, LANES), _bf16),
                   jax.ShapeDtypeStruct((batch, 2, LANES, nc), _bf16)],
        compiler_params=_cparams(("arbitrary",)),
        name="compress",
    )(kc_r.reshape(batch, nc, width), vc_r.reshape(batch, nc, width), *weights)


def _head_q(q_ref, r):
    q2 = q_ref[:, (r // 2) * LANES:(r // 2 + 1) * LANES]
    keep = _half0() if r % 2 == 0 else jnp.logical_not(_half0())
    return jnp.where(keep, q2, jnp.zeros_like(q2))


def _softmax_over_rows(s):
    m = jnp.max(s, axis=0, keepdims=True)
    e = jnp.exp2(s - m)
    l = jnp.sum(e, axis=0, keepdims=True)
    return e * jnp.where(m > 0.5 * NEG_INF, 1.0 / l, 0.0)


def _nsa_cw_kernel(q_ref, kc_ref, vct_ref, kw_ref, vwt_ref, ng_ref, stt_ref, *rest, first_block, sel_rows):
    yp_ref, mb_ref = rest[-2:]
    t0 = (pl.program_id(2) + first_block) * Q_BLOCK
    qpos = t0 + _lane_iota((1, Q_BLOCK))
    gates = jax.nn.sigmoid(ng_ref[...].astype(_f32).T[0:16, :])
    kc = kc_ref[0, 0]
    vct = vct_ref[0, 0]
    nc = kc.shape[0]
    cend = lax.broadcasted_iota(jnp.int32, (nc, 1), 0) * CMP_STRIDE + (CMP_BLOCK - 1)
    cmask = cend <= qpos
    start = pl.multiple_of(jnp.maximum(t0 - WINDOW, 0), Q_BLOCK)
    wlen = WINDOW + Q_BLOCK
    kwin = kw_ref[0, 0, pl.ds(start, wlen), :]
    kpos = start + lax.broadcasted_iota(jnp.int32, (wlen, 1), 0)
    wmask = (kpos <= qpos) & (qpos - kpos < WINDOW)
    blk0 = start // Q_BLOCK

    qs = [_head_q(q_ref, r) for r in range(4)]
    s_cmp = [_dot_t(kc, q) for q in qs]
    s_win = [_dot_t(kwin, q) for q in qs]
    p_cmp = [_softmax_over_rows(jnp.where(cmask, s, NEG_INF)) for s in s_cmp]
    imp = (p_cmp[0] + p_cmp[1]) + (p_cmp[2] + p_cmp[3])
    e_win = []
    for s in s_win:
        s = jnp.where(wmask, s, NEG_INF)
        e_win.append(jnp.exp2(s - jnp.max(s, axis=0, keepdims=True)).astype(_bf16))
    vwt = jnp.concatenate([vwt_ref[0, 0, blk0 + c] for c in range(wlen // Q_BLOCK)], axis=1)
    yts = []
    for r in range(4):
        o_cmp = _dot(vct, p_cmp[r].astype(_bf16))[0:HEAD_DIM]
        win = _dot(vwt, e_win[r])
        o_win = win[0:HEAD_DIM] * (1.0 / win[HEAD_DIM:2 * HEAD_DIM])
        yts.append(gates[3 * r:3 * r + 1] * o_cmp + gates[3 * r + 2:3 * r + 3] * o_win)
    yp_ref[:, 0:LANES] = jnp.concatenate(yts[0:2], axis=0).T
    yp_ref[:, LANES:2 * LANES] = jnp.concatenate(yts[2:4], axis=0).T

    imp_hi = imp.astype(_bf16)
    rest = imp - imp_hi.astype(_f32)
    imp_mid = rest.astype(_bf16)
    imp_lo = (rest - imp_mid.astype(_f32)).astype(_bf16)
    st = stt_ref[0:sel_rows, :]
    p_slc = _dot(st, imp_hi) + (_dot(st, imp_mid) + _dot(st, imp_lo))
    selj = lax.broadcasted_iota(jnp.int32, (sel_rows, 1), 0)
    blk_t = qpos // SEL_BLOCK
    forced = (selj == 0) | (selj == blk_t) | (selj == blk_t - 1)
    score = jnp.where(forced, TAKEN, jnp.where(selj <= blk_t, p_slc, NEG_INF))
    seljf = selj.astype(_f32)
    picked = forced
    for _ in range(SEL_TOPK - 3):
        m = jnp.max(score, axis=0, keepdims=True)
        first = jnp.min(jnp.where(score == m, seljf, float(LANES)), axis=0, keepdims=True)
        hit = seljf == first
        picked = picked | hit
        score = jnp.where(hit, TAKEN, score)
    bias = jnp.where(picked, 0.0, NEG_INF)
    if sel_rows < LANES:
        bias = jnp.concatenate([bias, jnp.full((LANES - sel_rows, Q_BLOCK), NEG_INF, _f32)], axis=0)
    mb_ref[0] = bias.T.astype(_bf16)


def _nsa_cw_stage(nq_r, kc_d, vc_t, kw_d, vw_t, proj, stencil_t, batch, seq):
    n = nq_r.shape[0]
    nblk = seq // Q_BLOCK
    nc = seq // CMP_STRIDE
    per_block = Q_BLOCK // CMP_STRIDE
    splits = 4 if nblk % 4 == 0 and (nblk // 4 * per_block) % LANES == 0 else 1
    per = nblk // splits
    kw4 = kw_d.reshape(2, batch, seq, LANES)
    vw5 = vw_t.reshape(2, batch, nblk, LANES, LANES)
    outs = ()
    for k in range(splits):
        first = k * per
        nvis = min(nc, (first + per) * per_block)
        qrow = lambda b, g, i, first=first: b * nblk + first + i
        in_specs = [pl.BlockSpec((Q_BLOCK, 2 * LANES), lambda b, g, i, qrow=qrow: (qrow(b, g, i), g)),
                    pl.BlockSpec((1, 1, nvis, LANES), lambda b, g, i: (b, g, 0, 0)),
                    pl.BlockSpec((1, 1, LANES, nvis), lambda b, g, i: (b, g, 0, 0)),
                    pl.BlockSpec((1, 1, seq, LANES), lambda b, g, i: (g, b, 0, 0)),
                    pl.BlockSpec((1, 1, nblk, LANES, LANES), lambda b, g, i: (g, b, 0, 0, 0)),
                    pl.BlockSpec((Q_BLOCK, LANES), lambda b, g, i, qrow=qrow: (qrow(b, g, i), CB_NG + g)),
                    pl.BlockSpec((LANES, nvis), lambda b, g, i: (0, 0))]
        in_specs += [pl.BlockSpec(memory_space=pl.ANY)] * len(outs)
        outs = pl.pallas_call(
            functools.partial(_nsa_cw_kernel, first_block=first,
                              sel_rows=min(LANES, -(-(first + per) * Q_BLOCK // SEL_BLOCK // 8) * 8)),
            grid=(batch, 2, per),
            in_specs=in_specs,
            out_specs=[pl.BlockSpec((Q_BLOCK, 2 * LANES), lambda b, g, i, qrow=qrow: (qrow(b, g, i), g)),
                       pl.BlockSpec((1, Q_BLOCK, LANES), lambda b, g, i, qrow=qrow: (g, qrow(b, g, i), 0))],
            out_shape=[jax.ShapeDtypeStruct((n, 4 * LANES), _f32),
                       jax.ShapeDtypeStruct((2, n, LANES), _bf16)],
            input_output_aliases={7: 0, 8: 1} if outs else {},
            compiler_params=_cparams(("arbitrary", "arbitrary", "arbitrary")),
            name="nsa_cmp_win_select",
        )(nq_r, kc_d, vc_t, kw4, vw5, proj, stencil_t, *outs)
    return outs


def _nsa_sel_kernel(q_ref, mb_ref, ka_ref, vat_ref, ng_ref, yp_ref, o_ref, qs_ref, m_ref, acc_ref, s_ref, p_ref, alpha_ref):
    tk = SEL_TK
    t0 = pl.program_id(2) * SEL_Q
    h0 = _half0()
    mb = mb_ref[0].astype(_f32)
    bias = [pltpu.roll(mb, HEAD_DIM, 1), mb]
    for r in range(4):
        q = q_ref[:, (r // 2) * LANES:(r // 2 + 1) * LANES].astype(_f32)
        if r % 2 == 1:
            q = pltpu.roll(q, HEAD_DIM, 1)
        for v in range(2):
            qs_ref[v, r * SEL_Q:(r + 1) * SEL_Q, :] = jnp.where(h0, q, bias[v]).astype(_bf16)
    m_ref[...] = jnp.full(m_ref.shape, NEG_INF, _f32)
    acc_ref[...] = jnp.zeros(acc_ref.shape, _f32)
    p_ref[...] = jnp.zeros(p_ref.shape, _bf16)
    alpha_ref[...] = jnp.ones(alpha_ref.shape, _f32)
    qpos = t0 + _lane_iota((1, 4 * SEL_Q)) % SEL_Q

    def scores(kt):
        version = (kt * (tk // SEL_BLOCK)) // HEAD_DIM
        return _dot_t(ka_ref[0, 0, pl.ds(pl.multiple_of(kt * tk, tk), tk), :], qs_ref[version])

    def softmax_step(s):
        m_old = m_ref[...]
        m_new = jnp.maximum(m_old, jnp.max(s, axis=0, keepdims=True))
        m_ref[...] = m_new
        return jnp.exp2(s - m_new).astype(_bf16), jnp.exp2(m_old - m_new)

    def accumulate(kt, alpha, p):
        acc_ref[...] = alpha * acc_ref[...] + _dot(vat_ref[0, 0, kt], p)

    def trip(kt, carry):
        accumulate(jnp.maximum(kt - 1, 0), alpha_ref[...], p_ref[...])
        s = s_ref[...]
        s_ref[...] = scores(kt + 1)
        p, alpha = softmax_step(s)
        p_ref[...] = p
        alpha_ref[...] = alpha
        return carry

    n_full = t0 // tk
    s_ref[...] = scores(0)

    def two_trips(kp, carry):
        trip(2 * kp, carry)
        return trip(2 * kp + 1, carry)

    lax.fori_loop(0, n_full // 2, two_trips, 0)

    @pl.when(n_full % 2 == 1)
    def _():
        trip(n_full - 1, 0)

    accumulate(jnp.maximum(n_full - 1, 0), alpha_ref[...], p_ref[...])
    kpos = n_full * tk + lax.broadcasted_iota(jnp.int32, (tk, 1), 0)
    p, alpha = softmax_step(jnp.where(kpos <= qpos, s_ref[...], NEG_INF))
    accumulate(n_full, alpha, p)
    acc = acc_ref[...]
    o = acc[0:HEAD_DIM] * (1.0 / acc[HEAD_DIM:2 * HEAD_DIM])
    gates = jax.nn.sigmoid(ng_ref[...].astype(_f32).T[0:16, :])
    ys = [gates[3 * r + 1:3 * r + 2] * o[:, r * SEL_Q:(r + 1) * SEL_Q] for r in range(4)]
    o_ref[:, 0:LANES] = (yp_ref[:, 0:LANES] + jnp.concatenate(ys[0:2], axis=0).T).astype(o_ref.dtype)
    o_ref[:, LANES:2 * LANES] = (yp_ref[:, LANES:2 * LANES] + jnp.concatenate(ys[2:4], axis=0).T).astype(o_ref.dtype)


def _nsa_sel_stage(nq_r, mbias, k_aug, vs_t, proj, ypart, batch, seq):
    n = nq_r.shape[0]
    nblk = seq // SEL_Q
    qrow = lambda b, g, i: b * nblk + i
    return pl.pallas_call(
        _nsa_sel_kernel,
        grid=(batch, 2, nblk),
        in_specs=[pl.BlockSpec((SEL_Q, 2 * LANES), lambda b, g, i: (qrow(b, g, i), g)),
                  pl.BlockSpec((1, SEL_Q, LANES), lambda b, g, i: (g, qrow(b, g, i), 0)),
                  pl.BlockSpec((1, 1, seq, LANES), lambda b, g, i: (g, b, 0, 0)),
                  pl.BlockSpec((1, 1, seq // SEL_TK, LANES, SEL_TK), lambda b, g, i: (g, b, 0, 0, 0)),
                  pl.BlockSpec((SEL_Q, LANES), lambda b, g, i: (qrow(b, g, i), CB_NG + g)),
                  pl.BlockSpec((SEL_Q, 2 * LANES), lambda b, g, i: (qrow(b, g, i), g))],
        out_specs=pl.BlockSpec((SEL_Q, 2 * LANES), lambda b, g, i: (qrow(b, g, i), g)),
        out_shape=jax.ShapeDtypeStruct((n, 4 * LANES), _bf16),
        scratch_shapes=[pltpu.VMEM((2, 4 * SEL_Q, LANES), _bf16),
                        pltpu.VMEM((1, 4 * SEL_Q), _f32),
                        pltpu.VMEM((LANES, 4 * SEL_Q), _f32),
                        pltpu.VMEM((SEL_TK, 4 * SEL_Q), _f32),
                        pltpu.VMEM((SEL_TK, 4 * SEL_Q), _bf16),
                        pltpu.VMEM((1, 4 * SEL_Q), _f32)],
        compiler_params=_cparams(("arbitrary", "arbitrary", "arbitrary")),
        name="nsa_selected",
    )(nq_r, mbias, k_aug.reshape(2, batch, seq, LANES), vs_t.reshape(2, batch, seq // SEL_TK, LANES, SEL_TK), proj, ypart)


def _sb_kernel(q_ref, k_ref, v_ref, o_ref, qs_ref, tail_ref, acc_ref):
    i = pl.program_id(1)
    h0 = _half0()
    heads = 2 * (q_ref.shape[1] // LANES)
    for h in range(heads):
        q = q_ref[:, (h // 2) * LANES:(h // 2 + 1) * LANES]
        keep = h0 if h % 2 == 0 else jnp.logical_not(h0)
        qs_ref[h] = jnp.where(keep, q, jnp.zeros_like(q)) * QK_SCALE
    tail_ref[...] = jnp.zeros(tail_ref.shape, _f32)
    acc_ref[...] = jnp.zeros(acc_ref.shape, _f32)
    rloc = lax.broadcasted_iota(jnp.int32, (Q_BLOCK, Q_BLOCK), 0)
    cloc = lax.broadcasted_iota(jnp.int32, (Q_BLOCK, Q_BLOCK), 1)
    later = jnp.where(rloc > cloc, 1.0, 0.0).astype(_bf16)

    def key_block(j, diagonal):
        k0 = pl.multiple_of(j * Q_BLOCK, Q_BLOCK)
        past = cloc < rloc
        cols = [slice((h // 2) * LANES, (h // 2 + 1) * LANES) for h in range(heads)]
        zs = [_dot_t(qs_ref[h], k_ref[0, pl.ds(k0, Q_BLOCK), cols[h]]) for h in range(heads)]
        log_beta, log_keep = [], []
        for z in zs:
            sp = jnp.maximum(z, 0.0) + jnp.log(1.0 + jnp.exp(-jnp.abs(z)))
            log_beta.append(z - sp)
            log_keep.append(jnp.where(past, -sp, 0.0) if diagonal else -sp)
        inner = []
        for lk in log_keep:
            hi = lk.astype(_bf16)
            lo = (lk - hi.astype(_f32)).astype(_bf16)
            inner.append(_dot(hi, later) + _dot(lo, later))
        probs = []
        for h in range(heads):
            a = jnp.exp(log_beta[h] + inner[h] + tail_ref[h])
            probs.append((jnp.where(past, a, 0.0) if diagonal else a).astype(_bf16))
        worst = jnp.full((Q_BLOCK, 1), -jnp.inf, _f32)
        for h in range(heads):
            acc_ref[h] = acc_ref[h] + _dot(probs[h], v_ref[0, pl.ds(k0, Q_BLOCK), cols[h]])
            tail = tail_ref[h] + jnp.sum(log_keep[h], axis=-1, keepdims=True)
            tail_ref[h] = tail
            worst = jnp.maximum(worst, tail)
        return jnp.max(worst)

    def cond(c):
        j, worst_tail = c
        return (j >= 0) & (worst_tail > SB_TAIL_CUTOFF)

    def body(c):
        j, _ = c
        return j - 1, key_block(j, False)

    lax.while_loop(cond, body, (i - 1, key_block(i, True)))
    for p in range(heads // 2):
        o_ref[:, p * LANES:(p + 1) * LANES] = jnp.where(h0, acc_ref[2 * p], acc_ref[2 * p + 1]).astype(o_ref.dtype)


def _sb_stage(proj, batch, seq):
    n = proj.shape[0]
    nblk = seq // Q_BLOCK
    width = 4 * LANES
    proj3 = proj.reshape(batch, seq, PROJ_W)
    return pl.pallas_call(
        _sb_kernel,
        grid=(batch, nblk),
        in_specs=[pl.BlockSpec((Q_BLOCK, width), lambda b, i: (b * nblk + i, CB_SBQ // 4)),
                  pl.BlockSpec((1, seq, width), lambda b, i: (b, 0, CB_SBK // 4)),
                  pl.BlockSpec((1, seq, width), lambda b, i: (b, 0, CB_SBV // 4))],
        out_specs=pl.BlockSpec((Q_BLOCK, width), lambda b, i: (b * nblk + i, 0)),
        out_shape=jax.ShapeDtypeStruct((n, width), _bf16),
        scratch_shapes=[pltpu.VMEM((8, Q_BLOCK, LANES), _bf16),
                        pltpu.VMEM((8, Q_BLOCK, 1), _f32),
                        pltpu.VMEM((8, Q_BLOCK, LANES), _f32)],
        compiler_params=_cparams(("arbitrary", "arbitrary")),
        name="stick_breaking",
    )(proj, proj3, proj3)


def _layer_norm(x, g, b):
    mu = jnp.mean(x, axis=-1, keepdims=True)
    xc = x - mu
    var = jnp.mean(xc * xc, axis=-1, keepdims=True)
    return xc * lax.rsqrt(var + LN_EPS) * g + b


def _merge_kernel(x_ref, ysb_ref, yns_ref, mg0_ref, mg1_ref, wsb_ref, wns_ref, wo_ref, g_ref, b_ref,
                  wr_ref, br_ref, h_ref, idx_ref, gw_ref):
    m0 = jax.nn.sigmoid(mg0_ref[...].astype(_f32))
    m1 = jax.nn.sigmoid(mg1_ref[...].astype(_f32))
    merged = m0 * _dot(ysb_ref[...], wsb_ref[...]) + m1 * _dot(yns_ref[...], wns_ref[...])
    pre = DEEPNORM_ALPHA * x_ref[...] + _dot(merged.astype(_bf16), wo_ref[...])
    h = _layer_norm(pre, g_ref[...], b_ref[...])
    tm = h.shape[0]
    for s in range(SLAB):
        h_ref[pl.ds(s, tm, stride=SLAB), :] = h[:, s * LANES:(s + 1) * LANES]
    h_hi = h.astype(_bf16)
    h_lo = (h - h_hi.astype(_f32)).astype(_bf16)
    w = wr_ref[...]
    w_hi = w.astype(_bf16)
    w_lo = (w - w_hi.astype(_f32)).astype(_bf16)
    logits = (_dot(h_hi, w_hi) + (_dot(h_hi, w_lo) + _dot(h_lo, w_hi))) + br_ref[...]
    lane = _lane_iota((1, LANES))
    lanef = lane.astype(_f32)
    lg = jnp.where(lane < N_EXPERTS, logits, TAKEN)
    vals, idxs = [], []
    for _ in range(TOP_K):
        m = jnp.max(lg, axis=-1, keepdims=True)
        first = jnp.min(jnp.where(lg == m, lanef, float(LANES)), axis=-1, keepdims=True)
        vals.append(m)
        idxs.append(first)
        lg = jnp.where(lanef == first, TAKEN, lg)
    es = [jnp.exp(v - vals[0]) for v in vals]
    inv = 1.0 / (es[0] + es[1] + es[2] + es[3])
    idx_t = jnp.zeros(lg.shape, _f32)
    gw_t = jnp.zeros(lg.shape, _f32)
    for k in range(TOP_K):
        idx_t = jnp.where(lane == k, idxs[k], idx_t)
        gw_t = jnp.where(lane == k, es[k] * inv, gw_t)
    idx_ref[...] = idx_t[:, :TOP_K].astype(jnp.int32)
    gw_ref[...] = gw_t[:, :TOP_K]


def _merge_stage(x2, y_sb, y_nsa, proj, wsb, wns, wo, g1, b1, wr, br):
    n = x2.shape[0]
    tm = 1024
    row = lambda w: pl.BlockSpec((tm, w), lambda i: (i, 0))
    full = lambda a: pl.BlockSpec(a.shape, lambda i: (0,) * a.ndim)
    return pl.pallas_call(
        _merge_kernel,
        grid=(n // tm,),
        in_specs=[row(D_MODEL), row(4 * LANES), row(4 * LANES),
                  pl.BlockSpec((tm, D_MODEL), lambda i: (i, CB_MG // 8)),
                  pl.BlockSpec((tm, D_MODEL), lambda i: (i, CB_MG // 8 + 1)),
                  full(wsb), full(wns), full(wo), full(g1), full(b1), full(wr), full(br)],
        out_specs=[pl.BlockSpec((tm * SLAB, LANES), lambda i: (i, 0)), row(TOP_K), row(TOP_K)],
        out_shape=[jax.ShapeDtypeStruct((n * SLAB, LANES), _f32),
                   jax.ShapeDtypeStruct((n, TOP_K), jnp.int32),
                   jax.ShapeDtypeStruct((n, TOP_K), _f32)],
        compiler_params=_cparams(("arbitrary",)),
        name="merge_ln1_router",
    )(x2, y_sb, y_nsa, proj, proj, wsb, wns, wo, g1, b1, wr, br)


def _prep_w_in(w):
    main = w[:, :CB_NG * LANES]
    ng = w[:, CB_NG * LANES:CB_NG * LANES + 24]
    mg = w[:, CB_NG * LANES + 24:]
    pad = jnp.zeros((w.shape[0], LANES - 12), w.dtype)
    return jnp.concatenate([main, ng[:, :12], pad, ng[:, 12:], pad, mg], axis=1).astype(_bf16)


def _rope_tables(seq):
    half = HEAD_DIM // 2
    inv_freq = ROPE_THETA ** (-jnp.arange(half, dtype=_f32) / half)
    ang = jnp.arange(seq, dtype=_f32)[:, None] * inv_freq[None, :]
    cos = jnp.cos(ang)
    sin = jnp.sin(ang)
    cos128 = jnp.concatenate([cos, cos, cos, cos], axis=1)
    sin128 = jnp.concatenate([-sin, sin, -sin, sin], axis=1)
    return cos128, sin128


def _stencil(nc):
    n = np.arange(nc)[:, None]
    j = np.arange(LANES)[None, :]
    ratio = SEL_BLOCK // CMP_STRIDE
    ok = (n >= ratio * j - 1) & (n <= ratio * j + ratio - 1) & (n < nc - 1)
    return jnp.asarray(ok.astype(np.float32).T, dtype=_bf16)


def _attention_half(x, w_in, cmp_pe_k, cmp_w1_k, cmp_w2_k, cmp_pe_v, cmp_w1_v, cmp_w2_v,
                    w_proj_sb, w_proj_nsa, w_out, ln1_g, ln1_b, w_router, b_router):
    batch, seq, _ = x.shape
    n = batch * seq
    x2 = x.reshape(n, D_MODEL)
    proj = _in_proj(x2, _prep_w_in(w_in))
    cos, sin_signed = _rope_tables(seq)
    nq_r, kc_r, vc_r, k_aug, vs_t, kw_d, vw_t = _rope_stage(proj, cos, sin_signed, seq)
    kc_d, vc_t = _compress_stage(kc_r, vc_r, _compress_weights(cmp_pe_k, cmp_w1_k, cmp_w2_k),
                                 _compress_weights(cmp_pe_v, cmp_w1_v, cmp_w2_v), batch, seq)
    ypart, mbias = _nsa_cw_stage(nq_r, kc_d, vc_t, kw_d, vw_t, proj, _stencil(seq // CMP_STRIDE), batch, seq)
    y_nsa = _nsa_sel_stage(nq_r, mbias, k_aug, vs_t, proj, ypart, batch, seq)
    y_sb = _sb_stage(proj, batch, seq)
    wr = jnp.pad(w_router.astype(_f32), ((0, 0), (0, LANES - N_EXPERTS)))
    br = jnp.pad(b_router.astype(_f32), (0, LANES - N_EXPERTS)).reshape(1, LANES)
    return _merge_stage(x2, y_sb, y_nsa, proj, w_proj_sb.astype(_bf16), w_proj_nsa.astype(_bf16),
                        w_out.astype(_bf16), ln1_g.reshape(1, -1), ln1_b.reshape(1, -1), wr, br)


def _row_copy(src, src_row, dst, dst_row, sem):
    return pltpu.make_async_copy(src.at[pl.ds(src_row * SLAB, SLAB)], dst.at[pl.ds(dst_row * SLAB, SLAB)], sem)


def _dispatch_kernel(pad_start_ref, pad_len_ref, dest_ref, h_hbm, buf_ref, hbuf, zslab, in_sems, row_sems, zero_sem,
                     *, tokens):
    i = pl.program_id(0)
    last = pl.num_programs(0) - 1
    slot = i % 2
    block_rows = tokens * SLAB

    def block_copy(step, s):
        return pltpu.make_async_copy(h_hbm.at[pl.ds(step * block_rows, block_rows)], hbuf.at[s], in_sems.at[s])

    def wait_rows(s):
        for _ in range(TOP_K):
            pltpu.make_async_copy(hbuf.at[s], buf_ref.at[pl.ds(0, block_rows)], row_sems.at[s]).wait()

    def padding_rows(act):
        def per_expert(e, c):
            def per_row(r, c2):
                act(_row_copy(zslab, 0, buf_ref, pad_start_ref[e] + r, zero_sem))
                return c2
            lax.fori_loop(0, pad_len_ref[e], per_row, 0)
            return c
        lax.fori_loop(0, N_EXPERTS, per_expert, 0)

    @pl.when(i == 0)
    def _():
        block_copy(0, 0).start()
        zslab[...] = jnp.zeros(zslab.shape, _f32)
        padding_rows(lambda cp: cp.start())

    @pl.when(i > 0)
    def _():
        wait_rows(1 - slot)

    @pl.when(i < last)
    def _():
        block_copy(i + 1, 1 - slot).start()

    block_copy(i, slot).wait()
    src = hbuf.at[slot]

    def issue(tg, c):
        for u in range(ISSUE_UNROLL):
            t = tg * ISSUE_UNROLL + u
            for k in range(TOP_K):
                _row_copy(src, t, buf_ref, dest_ref[t * TOP_K + k], row_sems.at[slot]).start(priority=k % 2)
        return c

    lax.fori_loop(0, tokens // ISSUE_UNROLL, issue, 0)

    @pl.when(i == last)
    def _():
        wait_rows(slot)
        padding_rows(lambda cp: cp.wait())


def _dispatch_stage(dest, pad_start, pad_len, h_slab, buf_rows):
    n = h_slab.shape[0] // SLAB
    tokens = 256
    grid_spec = pltpu.PrefetchScalarGridSpec(
        num_scalar_prefetch=2,
        grid=(n // tokens,),
        in_specs=[pl.BlockSpec((tokens * TOP_K,), lambda i, ps, pn: (i,), memory_space=pltpu.SMEM),
                  pl.BlockSpec(memory_space=pl.ANY)],
        out_specs=pl.BlockSpec(memory_space=pl.ANY),
        scratch_shapes=[pltpu.VMEM((2, tokens * SLAB, LANES), _f32),
                        pltpu.VMEM((SLAB, LANES), _f32),
                        pltpu.SemaphoreType.DMA((2,)),
                        pltpu.SemaphoreType.DMA((2,)),
                        pltpu.SemaphoreType.DMA(())])
    return pl.pallas_call(
        functools.partial(_dispatch_kernel, tokens=tokens),
        grid_spec=grid_spec,
        out_shape=jax.ShapeDtypeStruct((buf_rows * SLAB, LANES), _f32),
        compiler_params=pltpu.CompilerParams(dimension_semantics=("arbitrary",), has_side_effects=True),
        name="moe_dispatch",
    )(pad_start, pad_len, dest, h_slab)


def _slab_load(ref, rows):
    return jnp.concatenate([ref[pl.ds(s, rows, stride=SLAB), :] for s in range(SLAB)], axis=1)


def _expert_kernel(ce_ref, used_ref, x_ref, wgu_ref, bgu_ref, wd_ref, bd_ref, o_ref, wgu_bf, wd_bf):
    c = pl.program_id(0)
    new_expert = (c == 0) | (ce_ref[c] != ce_ref[jnp.maximum(c - 1, 0)])

    @pl.when(new_expert & (c < used_ref[0]))
    def _():
        wgu_bf[...] = wgu_ref[0].astype(_bf16)
        wd_bf[...] = wd_ref[0].astype(_bf16)

    @pl.when(c < used_ref[0])
    def _():
        x = _slab_load(x_ref, MOE_ROWS).astype(_bf16)
        gu = _dot(x, wgu_bf[...]) + bgu_ref[0]
        gate = jnp.minimum(gu[:, :D_MODEL], SWIGLU_LIMIT)
        up = jnp.clip(gu[:, D_MODEL:], -SWIGLU_LIMIT, SWIGLU_LIMIT)
        h = gate * jax.nn.sigmoid(SWIGLU_ALPHA * gate) * (up + 1.0)
        y = _dot(h.astype(_bf16), wd_bf[...]) + bd_ref[0]
        for s in range(SLAB):
            o_ref[pl.ds(s, MOE_ROWS, stride=SLAB), :] = y[:, s * LANES:(s + 1) * LANES]

    @pl.when(c >= used_ref[0])
    def _():
        o_ref[...] = jnp.zeros(o_ref.shape, o_ref.dtype)


def _expert_stage(chunk_expert, n_used, buf, wgu, bgu, wd, bd):
    n_chunks = chunk_expert.shape[0]
    rows = MOE_ROWS * SLAB
    grid_spec = pltpu.PrefetchScalarGridSpec(
        num_scalar_prefetch=2,
        grid=(n_chunks,),
        in_specs=[pl.BlockSpec((rows, LANES), lambda c, ce, nu: (jnp.minimum(c, nu[0] - 1), 0)),
                  pl.BlockSpec((1, D_MODEL, 2 * D_MODEL), lambda c, ce, nu: (ce[c], 0, 0)),
                  pl.BlockSpec((1, 1, 2 * D_MODEL), lambda c, ce, nu: (ce[c], 0, 0)),
                  pl.BlockSpec((1, D_MODEL, D_MODEL), lambda c, ce, nu: (ce[c], 0, 0)),
                  pl.BlockSpec((1, 1, D_MODEL), lambda c, ce, nu: (ce[c], 0, 0))],
        out_specs=pl.BlockSpec((rows, LANES), lambda c, ce, nu: (c, 0)),
        scratch_shapes=[pltpu.VMEM((D_MODEL, 2 * D_MODEL), _bf16), pltpu.VMEM((D_MODEL, D_MODEL), _bf16)])
    return pl.pallas_call(
        _expert_kernel,
        grid_spec=grid_spec,
        out_shape=jax.ShapeDtypeStruct(buf.shape, _f32),
        compiler_params=_cparams(("arbitrary",)),
        name="moe_experts",
    )(chunk_expert, n_used, buf, wgu, bgu, wd, bd)


def _combine_kernel(dest_ref, dest_next_ref, gw_ref, h_ref, eo_ref, g_ref, b_ref, o_ref, gbuf, ysl, sems, *, tokens):
    i = pl.program_id(0)
    slot = i % 2
    count = tokens * TOP_K

    def gather(idx_ref, into):
        def issue(tg, c):
            for u in range(ISSUE_UNROLL * TOP_K):
                j = tg * (ISSUE_UNROLL * TOP_K) + u
                _row_copy(eo_ref, idx_ref[j], gbuf.at[into], j, sems.at[into]).start(priority=u % 2)
            return c
        lax.fori_loop(0, tokens // ISSUE_UNROLL, issue, 0)

    @pl.when(i == 0)
    def _():
        gather(dest_ref, 0)

    @pl.when(i + 1 < pl.num_programs(0))
    def _():
        gather(dest_next_ref, 1 - slot)

    pltpu.make_async_copy(eo_ref.at[pl.ds(0, count * SLAB)], gbuf.at[slot], sems.at[slot]).wait()
    rows = gbuf.at[slot]

    unroll = 4

    def token_group(tg, c):
        for u in range(unroll):
            t = tg * unroll + u
            acc = DEEPNORM_ALPHA * h_ref[pl.ds(pl.multiple_of(t * SLAB, SLAB), SLAB), :]
            for k in range(TOP_K):
                j = t * TOP_K + k
                acc = acc + gw_ref[j] * rows[pl.ds(pl.multiple_of(j * SLAB, SLAB), SLAB), :]
            ysl[pl.ds(pl.multiple_of(t * SLAB, SLAB), SLAB), :] = acc
        return c

    lax.fori_loop(0, tokens // unroll, token_group, 0)
    o_ref[...] = _layer_norm(_slab_load(ysl, tokens), g_ref[...], b_ref[...])


def _combine_stage(dest, gate_w, h_slab, expert_out, g2, b2):
    n = h_slab.shape[0] // SLAB
    tokens = 256
    steps = n // tokens
    smem = lambda: pl.BlockSpec((tokens * TOP_K,), lambda i: (i,), memory_space=pltpu.SMEM)
    return pl.pallas_call(
        functools.partial(_combine_kernel, tokens=tokens),
        grid=(steps,),
        in_specs=[smem(),
                  pl.BlockSpec((tokens * TOP_K,), lambda i: (jnp.minimum(i + 1, steps - 1),), memory_space=pltpu.SMEM),
                  smem(),
                  pl.BlockSpec((tokens * SLAB, LANES), lambda i: (i, 0)),
                  pl.BlockSpec(memory_space=pl.ANY),
                  pl.BlockSpec((1, D_MODEL), lambda i: (0, 0)),
                  pl.BlockSpec((1, D_MODEL), lambda i: (0, 0))],
        out_specs=pl.BlockSpec((tokens, D_MODEL), lambda i: (i, 0)),
        out_shape=jax.ShapeDtypeStruct((n, D_MODEL), _f32),
        scratch_shapes=[pltpu.VMEM((2, tokens * TOP_K * SLAB, LANES), _f32),
                        pltpu.VMEM((tokens * SLAB, LANES), _f32),
                        pltpu.SemaphoreType.DMA((2,))],
        compiler_params=_cparams(("arbitrary",)),
        name="moe_combine_ln2",
    )(dest, dest, gate_w, h_slab, expert_out, g2, b2)


def _dispatch_plan(top_idx):
    m = top_idx.size
    e_flat = top_idx.reshape(m)
    onehot = (e_flat[:, None] == jnp.arange(N_EXPERTS, dtype=jnp.int32)[None, :]).astype(jnp.int32)
    csum = jnp.cumsum(onehot, axis=0)
    counts = csum[-1]
    padded = (counts + MOE_ROWS - 1) // MOE_ROWS * MOE_ROWS
    pends = jnp.cumsum(padded)
    pstarts = pends - padded
    dest = jnp.sum(onehot * (csum - 1 + pstarts[None, :]), axis=1).astype(jnp.int32)
    n_chunks = m // MOE_ROWS + N_EXPERTS
    chunk_start = jnp.arange(n_chunks, dtype=jnp.int32) * MOE_ROWS
    chunk_expert = jnp.minimum(jnp.sum((chunk_start[:, None] >= pends[None, :]).astype(jnp.int32), axis=1), N_EXPERTS - 1)
    n_used = (pends[-1] // MOE_ROWS).astype(jnp.int32).reshape(1)
    pad_start = (pstarts + counts).astype(jnp.int32)
    pad_len = (padded - counts).astype(jnp.int32)
    return dest, chunk_expert, n_used, pad_start, pad_len, n_chunks * MOE_ROWS


def _moe_half(h_slab, top_idx, gate_w, w_gate_up, b_gate_up, w_down, b_down, ln2_g, ln2_b):
    dest, chunk_expert, n_used, pad_start, pad_len, buf_rows = _dispatch_plan(top_idx)
    buf = _dispatch_stage(dest, pad_start, pad_len, h_slab, buf_rows)
    expert_out = _expert_stage(chunk_expert, n_used, buf, w_gate_up, b_gate_up.reshape(N_EXPERTS, 1, -1),
                               w_down, b_down.reshape(N_EXPERTS, 1, -1))
    return _combine_stage(dest, gate_w.reshape(-1), h_slab, expert_out, ln2_g.reshape(1, -1), ln2_b.reshape(1, -1))


def kernel(x, w_in, cmp_pe_k, cmp_w1_k, cmp_w2_k, cmp_pe_v, cmp_w1_v, cmp_w2_v, w_proj_sb, w_proj_nsa, w_out,
           ln1_g, ln1_b, w_router, b_router, w_gate_up, b_gate_up, w_down, b_down, ln2_g, ln2_b):
    assert w_in.shape[0] == 1, "single-layer block"
    batch, seq, _ = x.shape
    assert seq % 512 == 0 and seq // SEL_BLOCK <= LANES and seq >= WINDOW + Q_BLOCK
    h_slab, top_idx, gate_w = _attention_half(
        x, w_in[0], cmp_pe_k[0], cmp_w1_k[0], cmp_w2_k[0], cmp_pe_v[0], cmp_w1_v[0], cmp_w2_v[0],
        w_proj_sb[0], w_proj_nsa[0], w_out[0], ln1_g[0], ln1_b[0], w_router[0], b_router[0])
    out = _moe_half(h_slab, top_idx, gate_w, w_gate_up[0], b_gate_up[0], w_down[0], b_down[0], ln2_g[0], ln2_b[0])
    return out.reshape(batch, seq, D_MODEL)
```
